```python
import math
import jax, jax.numpy as jnp
from jax import lax
import numpy as np

D_MODEL = 1024
BATCH = 8
SEQ = 2048
DEPTH = 1
DEC_BATCH = 128
DEC_SEQ = 4
PAST_LEN = 8192
PAGE_SIZE = 128

MLA_HEADS = 8
Q_LORA = 384
KV_LORA = 256
NOPE_DIM = 128
ROPE_DIM = 64
V_DIM = 128
ROPE_THETA = 10000.0
Q_BLOCK = 128
SSM_HEADS = 32
SSM_HEAD_DIM = 64
D_INNER = SSM_HEADS * SSM_HEAD_DIM
SSM_GROUPS = 4
HEADS_PER_GROUP = SSM_HEADS // SSM_GROUPS
D_STATE = 128
CONV_W = 4
CONV_DIM = D_INNER + 2 * SSM_GROUPS * D_STATE
SSD_BLOCK = 128
N_MEM = 256
MEM_HEADS = 4
MEM_HEAD_DIM = 256
MEM_WIDTH = MEM_HEADS * MEM_HEAD_DIM
D_FF = 4 * D_MODEL
N_BRANCH = 3
ALPHA = (2 * DEPTH) ** 0.25
BETA = (8 * DEPTH) ** -0.25
LN_EPS = 1e-5
RMS_EPS = 1e-6
IN_SIZES = (Q_LORA, KV_LORA, ROPE_DIM, D_INNER, CONV_DIM, SSM_HEADS, MEM_WIDTH, N_BRANCH * D_MODEL)
D_IN = Q_LORA + KV_LORA + ROPE_DIM + D_INNER + CONV_DIM + SSM_HEADS + MEM_WIDTH + N_BRANCH * D_MODEL

kernel_name = "hybrid_mla_ssd_mem_decoder_step"

F32 = jnp.float32


def _split_in(proj):
    idx = np.cumsum(IN_SIZES)[:-1].tolist()
    return jnp.split(proj, idx, axis=-1)


def _rms_norm(x, g):
    xf = x.astype(F32)
    y = xf * lax.rsqrt(jnp.mean(xf * xf, -1, keepdims=True) + RMS_EPS)
    return (y * g.astype(F32)).astype(x.dtype)


def _layer_norm(x, g, b):
    xf = x.astype(F32)
    mu = jnp.mean(xf, -1, keepdims=True)
    var = jnp.mean(jnp.square(xf - mu), -1, keepdims=True)
    return ((xf - mu) * lax.rsqrt(var + LN_EPS) * g.astype(F32) + b.astype(F32)).astype(x.dtype)


def _rope(x, pos):
    half = ROPE_DIM // 2
    inv = ROPE_THETA ** (-jnp.arange(half, dtype=F32) / half)
    ang = pos.astype(F32)[:, None] * inv[None, :]
    ang = ang.reshape((1, pos.shape[0]) + (1,) * (x.ndim - 3) + (half,))
    cos, sin = jnp.cos(ang), jnp.sin(ang)
    xf = x.astype(F32)
    x1, x2 = xf[..., :half], xf[..., half:]
    return jnp.concatenate([x1 * cos - x2 * sin, x1 * sin + x2 * cos], -1).astype(x.dtype)


def _mla_queries(cq, q_norm_g, w_uq, w_uk, pos):
    q = jnp.einsum('blr,rhd->blhd', _rms_norm(cq, q_norm_g), w_uq)
    q_nope, q_rope = q[..., :NOPE_DIM], q[..., NOPE_DIM:]
    q_lat = jnp.einsum('blhd,chd->blhc', q_nope, w_uk)
    return q_lat, _rope(q_rope, pos)


def _mla_scores(q_lat, q_rope, ckv, krope):
    s = jnp.einsum('bqhc,bkc->bhqk', q_lat, ckv) + jnp.einsum('bqhr,bkr->bhqk', q_rope, krope)
    return s.astype(F32) * ((NOPE_DIM + ROPE_DIM) ** -0.5)


def _mla_prompt_attend(q_lat, q_rope, ckv, krope):
    b, L = q_lat.shape[:2]
    qb = min(Q_BLOCK, L)
    nblk = L // qb
    ql = q_lat.reshape(b, nblk, qb, MLA_HEADS, KV_LORA).swapaxes(0, 1)
    qr = q_rope.reshape(b, nblk, qb, MLA_HEADS, ROPE_DIM).swapaxes(0, 1)
    kpos = jnp.arange(L)

    def block(args):
        i, qlb, qrb = args
        s = _mla_scores(qlb, qrb, ckv, krope)
        qpos = i * qb + jnp.arange(qb)
        s = jnp.where(kpos[None, :] <= qpos[:, None], s, -jnp.inf)
        p = jax.nn.softmax(s, axis=-1).astype(ckv.dtype)
        return jnp.einsum('bhqk,bkc->bqhc', p, ckv)

    o = lax.map(block, (jnp.arange(nblk), ql, qr))
    return o.swapaxes(0, 1).reshape(b, L, MLA_HEADS, KV_LORA)


def _mla_sample_attend(q_lat, q_rope, ckv_new, krope_new, ckv_past, krope_past):
    T = q_lat.shape[1]
    s_past = _mla_scores(q_lat, q_rope, ckv_past, krope_past)
    s_new = _mla_scores(q_lat, q_rope, ckv_new, krope_new)
    s_new = jnp.where(jnp.tril(jnp.ones((T, T), bool)), s_new, -jnp.inf)
    p = jax.nn.softmax(jnp.concatenate([s_past, s_new], -1), axis=-1).astype(ckv_new.dtype)
    P = ckv_past.shape[1]
    return (jnp.einsum('bhqk,bkc->bqhc', p[..., :P], ckv_past)
            + jnp.einsum('bhqk,bkc->bqhc', p[..., P:], ckv_new))


def _mla_out(o_lat, w_uv):
    b, L = o_lat.shape[:2]
    return jnp.einsum('bqhc,chv->bqhv', o_lat, w_uv).reshape(b, L, MLA_HEADS * V_DIM)


def _causal_conv(xbc_ext, conv_w, conv_b, L):
    out = conv_b.astype(F32) + sum(conv_w[k].astype(F32) * xbc_ext[:, k:k + L].astype(F32) for k in range(CONV_W))
    return jax.nn.silu(out)


def _ssd(x, dt, a_coef, Bm, Cm, s0):
    b, L = x.shape[:2]
    q = math.gcd(L, SSD_BLOCK)
    nc = L // q

    def chunks(t):
        return t.reshape((b, nc, q) + t.shape[2:]).swapaxes(0, 1)

    tri = jnp.tril(jnp.ones((q, q), bool))[None, :, :, None, None]

    def step(S, inp):
        xc, dtc, Bc, Cc = inp
        cum = jnp.cumsum(dtc * a_coef, axis=1)
        seg = cum[:, :, None] - cum[:, None, :]
        decay = jnp.exp(jnp.where(tri, seg, -jnp.inf))
        cb = jnp.einsum('bign,bjgn->bijg', Cc, Bc)
        w = cb[..., None] * decay * dtc[:, None]
        y = jnp.einsum('bijgh,bjghp->bighp', w, xc)
        y = y + jnp.einsum('bign,bghpn->bighp', Cc, S) * jnp.exp(cum)[..., None]
        last = cum[:, -1]
        wb = dtc * jnp.exp(last[:, None] - cum)
        S = S * jnp.exp(last)[..., None, None] + jnp.einsum('bjgh,bjgn,bjghp->bghpn', wb, Bc, xc)
        return S, y

    S, ys = lax.scan(step, s0, (chunks(x), chunks(dt), chunks(Bm), chunks(Cm)))
    return ys.swapaxes(0, 1).reshape(x.shape), S


def _mamba(z, xbc_ext, dt_raw, s0, conv_w, conv_b, dt_bias, a_log, d_skip, ssm_norm_g):
    b, L = z.shape[:2]
    xbc = _causal_conv(xbc_ext, conv_w, conv_b, L)
    xs, Bm, Cm = jnp.split(xbc, [D_INNER, D_INNER + SSM_GROUPS * D_STATE], axis=-1)
    xs = xs.reshape(b, L, SSM_GROUPS, HEADS_PER_GROUP, SSM_HEAD_DIM)
    Bm = Bm.reshape(b, L, SSM_GROUPS, D_STATE)
    Cm = Cm.reshape(b, L, SSM_GROUPS, D_STATE)
    dt = jax.nn.softplus(dt_raw.astype(F32) + dt_bias.astype(F32)).reshape(b, L, SSM_GROUPS, HEADS_PER_GROUP)
    a_coef = -jnp.exp(a_log.astype(F32)).reshape(SSM_GROUPS, HEADS_PER_GROUP)
    y, S = _ssd(xs, dt, a_coef, Bm, Cm, s0)
    y = y + d_skip.astype(F32).reshape(SSM_GROUPS, HEADS_PER_GROUP)[..., None] * xs
    y = y.reshape(b, L, D_INNER) * jax.nn.silu(z.astype(F32))
    yg = y.reshape(b, L, SSM_GROUPS, D_INNER // SSM_GROUPS)
    yg = yg * lax.rsqrt(jnp.mean(yg * yg, -1, keepdims=True) + RMS_EPS)
    y = yg.reshape(b, L, D_INNER) * ssm_norm_g.astype(F32)
    return y.astype(z.dtype), S.reshape(b, SSM_HEADS, SSM_HEAD_DIM, D_STATE).astype(z.dtype)


def _mem_kv(mem, w_mem_k, w_mem_v):
    return jnp.einsum('bmd,dhe->bmhe', mem, w_mem_k), jnp.einsum('bmd,dhe->bmhe', mem, w_mem_v)


def _mem_attend(mq, mem_k, mem_v):
    b, L = mq.shape[:2]
    q = mq.reshape(b, L, MEM_HEADS, MEM_HEAD_DIM)
    s = jnp.einsum('blhe,bmhe->bhlm', q, mem_k).astype(F32) * (MEM_HEAD_DIM ** -0.5)
    p = jax.nn.softmax(s, axis=-1).astype(mem_v.dtype)
    return jnp.einsum('bhlm,bmhe->blhe', p, mem_v).reshape(b, L, MEM_WIDTH)


def _merge_ffn(x, o_mla, o_ssm, o_mem, g_raw, b_gate, w_o_mla, w_o_ssm, w_o_mem, w_out,
               ln1_g, ln1_b, w_up, b_up, w_down, b_down, ln2_g, ln2_b):
    g = jax.nn.sigmoid((g_raw + b_gate).astype(F32)).astype(x.dtype)
    g_a, g_b, g_m = jnp.split(g, N_BRANCH, axis=-1)
    m = g_a * (o_mla @ w_o_mla) + g_b * (o_ssm @ w_o_ssm) + g_m * (o_mem @ w_o_mem)
    x1 = _layer_norm(ALPHA * x + m @ w_out, ln1_g, ln1_b)
    h = jnp.square(jax.nn.relu(x1 @ w_up + b_up))
    return _layer_norm(ALPHA * x1 + h @ w_down + b_down, ln2_g, ln2_b)


def _normal(k, shape, scale):
    return jax.random.normal(k, shape, F32) * scale


def setup_inputs(seed: int = 0) -> dict:
    key = jax.random.key(seed)
    ks = jax.random.split(key, 40)
    n_pages = PAST_LEN // PAGE_SIZE
    n_used = DEC_BATCH * n_pages
    n_pool = n_used + n_used // 4
    page_table = jax.random.permutation(ks[0], n_pool)[:n_used].reshape(DEC_BATCH, n_pages).astype(jnp.int32)
    dt0 = jnp.exp(jax.random.uniform(ks[1], (SSM_HEADS,), F32, math.log(1e-3), math.log(1e-1)))
    return {
        "x_prompt": _normal(ks[2], (BATCH, SEQ, D_MODEL), 1.0),
        "x_sample": _normal(ks[3], (DEC_BATCH, DEC_SEQ, D_MODEL), 1.0),
        "mem_prompt": _normal(ks[4], (BATCH, N_MEM, D_MODEL), 1.0),
        "cache_ckv": _normal(ks[5], (n_pool, PAGE_SIZE, KV_LORA), 1.0),
        "cache_krope": _normal(ks[6], (n_pool, PAGE_SIZE, ROPE_DIM), 1.0),
        "page_table": page_table,
        "cache_mem_k": _normal(ks[7], (DEC_BATCH, N_MEM, MEM_HEADS, MEM_HEAD_DIM), 1.0),
        "cache_mem_v": _normal(ks[8], (DEC_BATCH, N_MEM, MEM_HEADS, MEM_HEAD_DIM), 1.0),
        "state_ssm": _normal(ks[9], (DEC_BATCH, SSM_HEADS, SSM_HEAD_DIM, D_STATE), 0.1),
        "state_conv": _normal(ks[10], (DEC_BATCH, CONV_W - 1, CONV_DIM), 1.0),
        "w_in": _normal(ks[11], (D_MODEL, D_IN), D_MODEL ** -0.5),
        "q_norm_g": 1.0 + _normal(ks[12], (Q_LORA,), 0.02),
        "w_uq": _normal(ks[13], (Q_LORA, MLA_HEADS, NOPE_DIM + ROPE_DIM), Q_LORA ** -0.5),
        "kv_norm_g": 1.0 + _normal(ks[14], (KV_LORA,), 0.02),
        "w_uk": _normal(ks[15], (KV_LORA, MLA_HEADS, NOPE_DIM), KV_LORA ** -0.5),
        "w_uv": _normal(ks[16], (KV_LORA, MLA_HEADS, V_DIM), KV_LORA ** -0.5),
        "conv_w": _normal(ks[17], (CONV_W, CONV_DIM), CONV_W ** -0.5),
        "conv_b": _normal(ks[18], (CONV_DIM,), 0.02),
        "dt_bias": dt0 + jnp.log(-jnp.expm1(-dt0)),
        "a_log": jnp.log(jax.random.uniform(ks[19], (SSM_HEADS,), F32, 1.0, 16.0)),
        "d_skip": 1.0 + _normal(ks[20], (SSM_HEADS,), 0.1),
        "ssm_norm_g": 1.0 + _normal(ks[21], (D_INNER,), 0.02),
        "w_mem_k": _normal(ks[22], (D_MODEL, MEM_HEADS, MEM_HEAD_DIM), D_MODEL ** -0.5),
        "w_mem_v": _normal(ks[23], (D_MODEL, MEM_HEADS, MEM_HEAD_DIM), D_MODEL ** -0.5),
        "b_gate": _normal(ks[24], (N_BRANCH * D_MODEL,), 0.02),
        "w_o_mla": _normal(ks[25], (MLA_HEADS * V_DIM, D_MODEL), (MLA_HEADS * V_DIM) ** -0.5),
        "w_o_ssm": _normal(ks[26], (D_INNER, D_MODEL), D_INNER ** -0.5),
        "w_o_mem": _normal(ks[27], (MEM_WIDTH, D_MODEL), MEM_WIDTH ** -0.5),
        "w_out": _normal(ks[28], (D_MODEL, D_MODEL), BETA * D_MODEL ** -0.5),
        "ln1_g": 1.0 + _normal(ks[29], (D_MODEL,), 0.02),
        "ln1_b": _normal(ks[30], (D_MODEL,), 0.02),
        "w_up": _normal(ks[31], (D_MODEL, D_FF), D_MODEL ** -0.5),
        "b_up": _normal(ks[32], (D_FF,), 0.02),
        "w_down": _normal(ks[33], (D_FF, D_MODEL), BETA * D_FF ** -0.5),
        "b_down": _normal(ks[34], (D_MODEL,), 0.02),
        "ln2_g": 1.0 + _normal(ks[35], (D_MODEL,), 0.02),
        "ln2_b": _normal(ks[36], (D_MODEL,), 0.02),
    }


def reference(x_prompt, x_sample, mem_prompt, cache_ckv, cache_krope, page_table, cache_mem_k, cache_mem_v,
              state_ssm, state_conv, w_in, q_norm_g, w_uq, kv_norm_g, w_uk, w_uv, conv_w, conv_b, dt_bias,
              a_log, d_skip, ssm_norm_g, w_mem_k, w_mem_v, b_gate, w_o_mla, w_o_ssm, w_o_mem, w_out,
              ln1_g, ln1_b, w_up, b_up, w_down, b_down, ln2_g, ln2_b):
    ssm_w = (conv_w, conv_b, dt_bias, a_log, d_skip, ssm_norm_g)
    merge_w = (b_gate, w_o_mla, w_o_ssm, w_o_mem, w_out, ln1_g, ln1_b, w_up, b_up, w_down, b_down, ln2_g, ln2_b)

    bp, L = x_prompt.shape[:2]
    pos_p = jnp.arange(L)
    cq, ckv_raw, kr_raw, z, xbc, dt_raw, mq, g_raw = _split_in(x_prompt @ w_in)
    ckv_p = _rms_norm(ckv_raw, kv_norm_g)
    krope_p = _rope(kr_raw, pos_p)
    q_lat, q_rope = _mla_queries(cq, q_norm_g, w_uq, w_uk, pos_p)
    o_mla = _mla_out(_mla_prompt_attend(q_lat, q_rope, ckv_p, krope_p), w_uv)
    xbc_ext = jnp.concatenate([jnp.zeros((bp, CONV_W - 1, CONV_DIM), xbc.dtype), xbc], axis=1)
    s0 = jnp.zeros((bp, SSM_GROUPS, HEADS_PER_GROUP, SSM_HEAD_DIM, D_STATE), F32)
    o_ssm, ssm_p = _mamba(z, xbc_ext, dt_raw, s0, *ssm_w)
    conv_p = xbc_ext[:, -(CONV_W - 1):]
    mem_k_p, mem_v_p = _mem_kv(mem_prompt, w_mem_k, w_mem_v)
    o_mem = _mem_attend(mq, mem_k_p, mem_v_p)
    y_prompt = _merge_ffn(x_prompt, o_mla, o_ssm, o_mem, g_raw, *merge_w)

    bs, T = x_sample.shape[:2]
    past = page_table.shape[1] * PAGE_SIZE
    pos_s = past + jnp.arange(T)
    cq, ckv_raw, kr_raw, z, xbc, dt_raw, mq, g_raw = _split_in(x_sample @ w_in)
    ckv_s = _rms_norm(ckv_raw, kv_norm_g)
    krope_s = _rope(kr_raw, pos_s)
    q_lat, q_rope = _mla_queries(cq, q_norm_g, w_uq, w_uk, pos_s)
    ckv_past = cache_ckv[page_table].reshape(bs, past, KV_LORA)
    krope_past = cache_krope[page_table].reshape(bs, past, ROPE_DIM)
    o_mla = _mla_out(_mla_sample_attend(q_lat, q_rope, ckv_s, krope_s, ckv_past, krope_past), w_uv)
    xbc_ext = jnp.concatenate([state_conv.astype(xbc.dtype), xbc], axis=1)
    s0 = state_ssm.astype(F32).reshape(bs, SSM_GROUPS, HEADS_PER_GROUP, SSM_HEAD_DIM, D_STATE)
    o_ssm, ssm_s = _mamba(z, xbc_ext, dt_raw, s0, *ssm_w)
    conv_s = xbc_ext[:, -(CONV_W - 1):]
    o_mem = _mem_attend(mq, cache_mem_k, cache_mem_v)
    y_sample = _merge_ffn(x_sample, o_mla, o_ssm, o_mem, g_raw, *merge_w)

    return (y_prompt, y_sample, ckv_p, krope_p, mem_k_p, mem_v_p, ssm_p, conv_p, ckv_s, krope_s, ssm_s, conv_s)
```

```python
import functools
import math

import jax
import jax.numpy as jnp
import numpy as np
from jax import lax
from jax.experimental import pallas as pl
from jax.experimental.pallas import tpu as pltpu

F32 = jnp.float32
BF16 = jnp.bfloat16

D_MODEL = 1024
MLA_HEADS = 8
Q_LORA = 384
KV_LORA = 256
NOPE_DIM = 128
ROPE_DIM = 64
V_DIM = 128
ROPE_THETA = 10000.0
QK_DIM = KV_LORA + ROPE_DIM
SSM_HEADS = 32
SSM_HEAD_DIM = 64
D_INNER = SSM_HEADS * SSM_HEAD_DIM
SSM_GROUPS = 4
GROUP_W = D_INNER // SSM_GROUPS
D_STATE = 128
CONV_W = 4
CONV_DIM = D_INNER + 2 * SSM_GROUPS * D_STATE
CHUNK = 128
N_MEM = 256
MEM_HEADS = 4
MEM_HEAD_DIM = 256
MEM_WIDTH = MEM_HEADS * MEM_HEAD_DIM
D_FF = 4 * D_MODEL
N_BRANCH = 3
DEPTH = 1
ALPHA = (2 * DEPTH) ** 0.25
LN_EPS = 1e-5
RMS_EPS = 1e-6
PAGE_SIZE = 128
SCORE_SCALE = (NOPE_DIM + ROPE_DIM) ** -0.5
NEG_BIG = -1e30

C_CQ = 0
C_SMALL = C_CQ + Q_LORA
C_Z = C_SMALL + 512
C_XBC = C_Z + D_INNER
C_MQ = C_XBC + CONV_DIM
C_G = C_MQ + MEM_WIDTH
C_END = C_G + N_BRANCH * D_MODEL


def _cp(sem, vmem_mb):
    return pltpu.CompilerParams(dimension_semantics=sem, vmem_limit_bytes=vmem_mb << 20)


def _resident(shape):
    nd = len(shape)
    return pl.BlockSpec(shape, lambda *_: (0,) * nd, pipeline_mode=pl.Buffered(1))


def _dot(a, b):
    return jnp.dot(a, b, preferred_element_type=F32)


def _dot_nt(a, b):
    return lax.dot_general(a, b, (((1,), (1,)), ((), ())), preferred_element_type=F32)


def _rms(v, g, eps):
    return v * lax.rsqrt(jnp.mean(v * v, axis=-1, keepdims=True) + eps) * g


def _layer_norm(v, g, b):
    mu = jnp.mean(v, axis=-1, keepdims=True)
    d = v - mu
    var = jnp.mean(d * d, axis=-1, keepdims=True)
    return d * lax.rsqrt(var + LN_EPS) * g + b


def _in_proj_kernel(x_ref, w_ref, cs_ref, gq_ref, gkv_ref, bg_ref,
                    cqn_ref, ckv_ref, kr_ref, kcat_ref, dt_ref, z_ref, xbc_ref, mq_ref, g_ref):
    xb = x_ref[...].astype(BF16)

    def mm(a, b):
        return _dot(xb, w_ref[:, a:b])

    cqn_ref[...] = _rms(mm(C_CQ, C_SMALL), gq_ref[...], RMS_EPS).astype(BF16)

    small = mm(C_SMALL, C_Z)
    ckv = _rms(small[:, :KV_LORA], gkv_ref[...], RMS_EPS)
    ckv_ref[...] = ckv
    kcat_ref[:, 0:KV_LORA] = ckv.astype(BF16)
    a = small[:, 256:384]
    b = pltpu.roll(a, ROPE_DIM // 2, 1)
    cs = cs_ref[...]
    ro = a * cs[:, :128] + b * cs[:, 128:]
    kr_ref[...] = ro[:, :ROPE_DIM]
    kcat_ref[:, KV_LORA:QK_DIM] = ro[:, :ROPE_DIM].astype(BF16)
    dt_ref[...] = small[:, 384:512]

    for c in range(0, D_INNER, 1024):
        z_ref[:, c:c + 1024] = mm(C_Z + c, C_Z + c + 1024).astype(BF16)
    for c in range(0, CONV_DIM, 1024):
        xbc_ref[:, c:c + 1024] = mm(C_XBC + c, C_XBC + c + 1024).astype(BF16)
    mq_ref[...] = mm(C_MQ, C_G).astype(BF16)
    for c in range(0, N_BRANCH * D_MODEL, 1024):
        gr = mm(C_G + c, C_G + c + 1024) + bg_ref[:, c:c + 1024]
        g_ref[:, c:c + 1024] = jax.nn.sigmoid(gr).astype(BF16)


def _in_proj(x2d, w_all, cs_tab, gq, gkv, bg, tm):
    m = x2d.shape[0]
    ncs = cs_tab.shape[0] // tm
    widths = [(Q_LORA, BF16), (KV_LORA, F32), (ROPE_DIM, F32), (QK_DIM, BF16), (128, F32),
              (D_INNER, BF16), (CONV_DIM, BF16), (MEM_WIDTH, BF16), (N_BRANCH * D_MODEL, BF16)]

    def row(n):
        return pl.BlockSpec((tm, n), lambda i: (i, 0))

    return pl.pallas_call(
        _in_proj_kernel,
        grid=(m // tm,),
        in_specs=[row(D_MODEL), _resident(w_all.shape),
                  pl.BlockSpec((tm, 256), lambda i: (i % ncs, 0)),
                  _resident(gq.shape), _resident(gkv.shape), _resident(bg.shape)],
        out_specs=[row(n) for n, _ in widths],
        out_shape=[jax.ShapeDtypeStruct((m, n), d) for n, d in widths],
        compiler_params=_cp(("arbitrary",), 56),
        name="in_proj",
    )(x2d, w_all, cs_tab, gq, gkv, bg)


def _q_prep_kernel(c_ref, wq_ref, wuk_ref, cs_ref, q_ref):
    c = c_ref[...]
    cs = cs_ref[...]
    for h in range(MLA_HEADS):
        qn = _dot(c, wq_ref[:, h * NOPE_DIM:(h + 1) * NOPE_DIM]).astype(BF16)
        ql = _dot(qn, wuk_ref[h]) * SCORE_SCALE
        q_ref[h, :, 0:KV_LORA] = ql.astype(BF16)
    r0 = MLA_HEADS * NOPE_DIM
    r1 = r0 + MLA_HEADS * ROPE_DIM
    for p in range(MLA_HEADS // 2):
        a = _dot(c, wq_ref[:, r0 + p * 128:r0 + (p + 1) * 128])
        b = _dot(c, wq_ref[:, r1 + p * 128:r1 + (p + 1) * 128])
        ro = (a * cs[:, :128] + b * cs[:, 128:]) * SCORE_SCALE
        q_ref[2 * p, :, KV_LORA:QK_DIM] = ro[:, :ROPE_DIM].astype(BF16)
        q_ref[2 * p + 1, :, KV_LORA:QK_DIM] = pltpu.roll(ro, ROPE_DIM, 1)[:, :ROPE_DIM].astype(BF16)


def _q_prep(cqn, wq, wuk_t, cs_tab, tm):
    m = cqn.shape[0]
    ncs = cs_tab.shape[0] // tm
    return pl.pallas_call(
        _q_prep_kernel,
        grid=(m // tm,),
        in_specs=[pl.BlockSpec((tm, Q_LORA), lambda i: (i, 0)), _resident(wq.shape), _resident(wuk_t.shape),
                  pl.BlockSpec((tm, 256), lambda i: (i % ncs, 0))],
        out_specs=pl.BlockSpec((MLA_HEADS, tm, QK_DIM), lambda i: (0, i, 0)),
        out_shape=jax.ShapeDtypeStruct((MLA_HEADS, m, QK_DIM), BF16),
        compiler_params=_cp(("arbitrary",), 32),
        name="q_prep",
    )(cqn, wq, wuk_t, cs_tab)


def _prompt_attn_kernel(q_ref, k_ref, wuv_ref, o_ref, m_sc, l_sc, acc_sc, *, tq):
    i = pl.program_id(1)
    m_sc[...] = jnp.full(m_sc.shape, NEG_BIG, F32)
    l_sc[...] = jnp.zeros(l_sc.shape, F32)
    acc_sc[...] = jnp.zeros(acc_sc.shape, F32)

    def kv_block(j, masked):
        k = k_ref[pl.ds(pl.multiple_of(j * tq, tq), tq), :]
        kc = k[:, :KV_LORA]
        kr = k[:, KV_LORA:]
        if masked:
            row = lax.broadcasted_iota(jnp.int32, (tq, tq), 0)
            col = lax.broadcasted_iota(jnp.int32, (tq, tq), 1)
            keep = col <= row
        for h in range(MLA_HEADS):
            q = q_ref[h]
            s = _dot_nt(q[:, :KV_LORA], kc) + _dot_nt(q[:, KV_LORA:], kr)
            if masked:
                s = jnp.where(keep, s, NEG_BIG)
            m_prev = m_sc[h]
            m_new = jnp.maximum(m_prev, jnp.max(s, axis=1, keepdims=True))
            alpha = jnp.exp(m_prev - m_new)
            p = jnp.exp(s - m_new)
            l_sc[h] = alpha * l_sc[h] + jnp.sum(p, axis=1, keepdims=True)
            acc_sc[h] = alpha * acc_sc[h] + _dot(p.astype(BF16), kc)
            m_sc[h] = m_new

    def body(j, carry):
        kv_block(j, False)
        return carry

    lax.fori_loop(0, i, body, 0)
    kv_block(i, True)

    for h in range(MLA_HEADS):
        o = (acc_sc[h] / l_sc[h]).astype(BF16)
        o_ref[:, h * V_DIM:(h + 1) * V_DIM] = _dot(o, wuv_ref[h]).astype(BF16)


def _prompt_attn(q, kcat, wuv, batch, seq, tq):
    nq = seq // tq
    kern = functools.partial(_prompt_attn_kernel, tq=tq)
    return pl.pallas_call(
        kern,
        grid=(batch, nq),
        in_specs=[pl.BlockSpec((MLA_HEADS, tq, QK_DIM), lambda b, i: (0, b * nq + i, 0)),
                  pl.BlockSpec((seq, QK_DIM), lambda b, i: (b, 0)),
                  _resident(wuv.shape)],
        out_specs=pl.BlockSpec((tq, MLA_HEADS * V_DIM), lambda b, i: (b * nq + i, 0)),
        out_shape=jax.ShapeDtypeStruct((batch * seq, MLA_HEADS * V_DIM), BF16),
        scratch_shapes=[pltpu.VMEM((MLA_HEADS, tq, 1), F32), pltpu.VMEM((MLA_HEADS, tq, 1), F32),
                        pltpu.VMEM((MLA_HEADS, tq, KV_LORA), F32)],
        compiler_params=_cp(("arbitrary", "arbitrary"), 32),
        name="prompt_attn",
    )(q, kcat, wuv)


def _decode_attn_kernel(pt_ref, q_ref, kn_ref, ckv_hbm, kr_hbm, o_ref, kbuf, rbuf, sem, *, n_pages, n_new, kv_chunk):
    b = pl.program_id(0)
    nb = pl.num_programs(0)
    slot = b % 2

    def page_copies(page, p, sl):
        rows = pl.ds(pl.multiple_of(p * PAGE_SIZE, PAGE_SIZE), PAGE_SIZE)
        return (pltpu.make_async_copy(ckv_hbm.at[page], kbuf.at[sl, rows, :], sem.at[0, sl]),
                pltpu.make_async_copy(kr_hbm.at[page], rbuf.at[sl, rows, :], sem.at[1, sl]))

    def issue(bi, sl):
        def body(p, carry):
            for cp in page_copies(pt_ref[bi * n_pages + p], p, sl):
                cp.start()
            return carry
        lax.fori_loop(0, n_pages, body, 0)

    @pl.when(b == 0)
    def _():
        issue(0, 0)

    @pl.when(b + 1 < nb)
    def _():
        issue(b + 1, 1 - slot)

    def wait_body(p, carry):
        for cp in page_copies(0, p, slot):
            cp.wait()
        return carry
    lax.fori_loop(0, n_pages, wait_body, 0)

    q = q_ref[0]
    rows = q.shape[0]
    qc = q[:, :KV_LORA]
    qr = q[:, KV_LORA:]
    qf = q.astype(F32)
    kn = kn_ref[0]
    tok = lax.broadcasted_iota(jnp.int32, (rows, 1), 0) % n_new

    s_new = []
    for j in range(n_new):
        sj = jnp.sum(qf * kn[j:j + 1, :], axis=1, keepdims=True)
        s_new.append(jnp.where(tok >= j, sj, NEG_BIG))
    m0 = s_new[0]
    for j in range(1, n_new):
        m0 = jnp.maximum(m0, s_new[j])

    def chunk(ci, carry):
        m_prev, l_prev, acc = carry
        r = pl.ds(pl.multiple_of(ci * kv_chunk, kv_chunk), kv_chunk)
        kc = kbuf[slot, r, :].astype(BF16)
        rc = rbuf[slot, r, :].astype(BF16)
        s = _dot_nt(qc, kc) + _dot_nt(qr, rc)
        m_new = jnp.maximum(m_prev, jnp.max(s, axis=1, keepdims=True))
        alpha = jnp.exp(m_prev - m_new)
        p = jnp.exp(s - m_new)
        l_new = alpha * l_prev + jnp.sum(p, axis=1, keepdims=True)
        acc = alpha * acc + _dot(p.astype(BF16), kc)
        return m_new, l_new, acc

    n_chunks = (n_pages * PAGE_SIZE) // kv_chunk
    m, l, acc = lax.fori_loop(0, n_chunks, chunk,
                              (m0, jnp.zeros((rows, 1), F32), jnp.zeros((rows, KV_LORA), F32)))
    for j in range(n_new):
        pj = jnp.exp(s_new[j] - m)
        l = l + pj
        acc = acc + pj * kn[j:j + 1, :KV_LORA]
    o_ref[0] = (acc / l).astype(BF16)


def _decode_attn(page_table, q, k_new, cache_ckv, cache_krope):
    nb, n_pages = page_table.shape
    rows = q.shape[1]
    n_new = k_new.shape[1]
    past = n_pages * PAGE_SIZE
    kern = functools.partial(_decode_attn_kernel, n_pages=n_pages, n_new=n_new, kv_chunk=1024)
    grid_spec = pltpu.PrefetchScalarGridSpec(
        num_scalar_prefetch=1,
        grid=(nb,),
        in_specs=[pl.BlockSpec((1, rows, QK_DIM), lambda b, pt: (b, 0, 0)),
                  pl.BlockSpec((1, n_new, QK_DIM), lambda b, pt: (b, 0, 0)),
                  pl.BlockSpec(memory_space=pl.ANY),
                  pl.BlockSpec(memory_space=pl.ANY)],
        out_specs=pl.BlockSpec((1, rows, KV_LORA), lambda b, pt: (b, 0, 0)),
        scratch_shapes=[pltpu.VMEM((2, past, KV_LORA), F32), pltpu.VMEM((2, past, ROPE_DIM), F32),
                        pltpu.SemaphoreType.DMA((2, 2))],
    )
    return pl.pallas_call(
        kern,
        grid_spec=grid_spec,
        out_shape=jax.ShapeDtypeStruct((nb, rows, KV_LORA), BF16),
        compiler_params=_cp(("arbitrary",), 40),
        name="decode_attn",
    )(page_table.reshape(-1), q, k_new, cache_ckv, cache_krope)


def _uv_proj_kernel(o_ref, w_ref, out_ref):
    out_ref[...] = _dot(o_ref[0], w_ref[0]).astype(BF16)


def _uv_proj(o_lat, wuv):
    m = o_lat.shape[1]
    return pl.pallas_call(
        _uv_proj_kernel,
        grid=(MLA_HEADS,),
        in_specs=[pl.BlockSpec((1, m, KV_LORA), lambda h: (h, 0, 0)),
                  pl.BlockSpec((1, KV_LORA, V_DIM), lambda h: (h, 0, 0))],
        out_specs=pl.BlockSpec((m, V_DIM), lambda h: (0, h)),
        out_shape=jax.ShapeDtypeStruct((m, MLA_HEADS * V_DIM), BF16),
        compiler_params=_cp(("arbitrary",), 16),
        name="uv_proj",
    )(o_lat, wuv)


def _split_dot(v, e, terms, left=True):
    out = None
    r = v
    for _ in range(terms):
        hi = r.astype(BF16)
        d = _dot(hi, e) if left else _dot(e, hi)
        out = d if out is None else out + d
        r = r - hi.astype(F32)
    return out


def _ssd_kernel(xbc_ref, z_ref, dt_ref, dtt_ref, tail_ref, s0_ref, cw_ref, cb_ref, dtb_ref, dtbt_ref,
                alog_ref, alogt_ref, dskip_ref, ng_ref, e64_ref, e128_ref,
                y_ref, sout_ref, cout_ref, ext_sc, st_sc, y_sc, *, t_valid):
    c = pl.program_id(1)
    n_pairs = SSM_HEADS // 2
    pairs_per_group = n_pairs // SSM_GROUPS

    @pl.when(c == 0)
    def _():
        ext_sc[0:8, :] = tail_ref[0]
        for q in range(n_pairs):
            g, pr = divmod(q, pairs_per_group)
            st_sc[g, :, pr * 128:(pr + 1) * 128] = s0_ref[0, q * 128:(q + 1) * 128, :].T

    if t_valid < CHUNK:
        ext_sc[8:8 + CHUNK, :] = jnp.zeros((CHUNK, CONV_DIM), F32)
    ext_sc[8:8 + t_valid, :] = xbc_ref[0].astype(F32)
    conv = cb_ref[...] + cw_ref[0:1, :] * ext_sc[5:5 + CHUNK, :]
    for k in range(1, CONV_W):
        conv = conv + cw_ref[k:k + 1, :] * ext_sc[5 + k:5 + k + CHUNK, :]
    tail = ext_sc[8 + t_valid - (CONV_W - 1):8 + t_valid, :]
    cout_ref[0] = tail
    ext_sc[8 - (CONV_W - 1):8, :] = tail
    xc = conv * jax.nn.sigmoid(conv)
    xs = xc[:, :D_INNER]
    bm = xc[:, D_INNER:D_INNER + SSM_GROUPS * D_STATE]
    cm = xc[:, D_INNER + SSM_GROUPS * D_STATE:]

    rowi = lax.broadcasted_iota(jnp.int32, (CHUNK, CHUNK), 0)
    coli = lax.broadcasted_iota(jnp.int32, (CHUNK, CHUNK), 1)
    lower = coli <= rowi
    tri = jnp.where(lower, 1.0, 0.0).astype(BF16)
    tri_t = jnp.where(rowi <= coli, 1.0, 0.0).astype(BF16)

    dt = jax.nn.softplus(dt_ref[0] + dtb_ref[...])
    dtt = jax.nn.softplus(dtt_ref[0] + dtbt_ref[...])
    if t_valid < CHUNK:
        dt = jnp.where(rowi < t_valid, dt, 0.0)
        dtt = jnp.where(lax.broadcasted_iota(jnp.int32, (SSM_HEADS, CHUNK), 1) < t_valid, dtt, 0.0)
    da = dt * (-jnp.exp(alog_ref[...]))
    dat = dtt * (-jnp.exp(alogt_ref[...]))
    cum = _split_dot(da, tri, 3, left=False)
    cum_t = _split_dot(dat, tri_t, 3)
    last = cum[CHUNK - 1:CHUNK, :]
    e64 = e64_ref[...]
    ec_e = _split_dot(jnp.exp(cum), e64, 2)
    wb_e = _split_dot(dt * jnp.exp(last - cum), e64, 2)
    dec_e = _split_dot(jnp.broadcast_to(jnp.exp(last), (16, CHUNK)), e64, 2)[0:1, :]
    cum_e = _split_dot(cum, e128_ref[...], 3)

    lane = coli
    for g in range(SSM_GROUPS):
        cg = cm[:, g * D_STATE:(g + 1) * D_STATE].astype(BF16)
        bg32 = bm[:, g * D_STATE:(g + 1) * D_STATE]
        cb = _dot_nt(cg, bg32.astype(BF16))
        st = st_sc[g]
        gs = slice(g * GROUP_W, (g + 1) * GROUP_W)
        y_int = _dot(cg, st.astype(BF16)) * ec_e[:, gs]
        for pr in range(pairs_per_group):
            q = g * pairs_per_group + pr
            ws = []
            for h in (2 * q, 2 * q + 1):
                seg = cum_e[:, h * 128:(h + 1) * 128] - cum_t[h:h + 1, :]
                decay = jnp.exp(jnp.where(lower, seg, NEG_BIG)) * dtt[h:h + 1, :]
                ws.append((cb * decay).astype(BF16))
            w_pair = jnp.concatenate(ws, axis=1)
            xp = xs[:, q * 128:(q + 1) * 128]
            x_bd = jnp.concatenate([jnp.where(lane < SSM_HEAD_DIM, xp, 0.0).astype(BF16),
                                    jnp.where(lane >= SSM_HEAD_DIM, xp, 0.0).astype(BF16)], axis=0)
            y_sc[:, q * 128:(q + 1) * 128] = (_dot(w_pair, x_bd) + y_int[:, pr * 128:(pr + 1) * 128]
                                              + dskip_ref[:, q * 128:(q + 1) * 128] * xp)
        xw = (xs[:, gs] * wb_e[:, gs]).astype(BF16)
        st_sc[g] = st * dec_e[:, gs] + _dot(bg32.T.astype(BF16), xw)

    for g in range(SSM_GROUPS):
        gs = slice(g * GROUP_W, (g + 1) * GROUP_W)
        zf = z_ref[0, :, gs].astype(F32)
        yv = y_sc[0:t_valid, gs] * (zf * jax.nn.sigmoid(zf))
        y_ref[0, :, gs] = _rms(yv, ng_ref[:, gs], RMS_EPS).astype(y_ref.dtype)

    for q in range(n_pairs):
        g, pr = divmod(q, pairs_per_group)
        sout_ref[0, q * 128:(q + 1) * 128, :] = st_sc[g, :, pr * 128:(pr + 1) * 128].T


def _ssd(xbc, z, dt, dtt, tail0, s0, consts, n_chunks, t_valid, y_dtype):
    batch = s0.shape[0]
    kern = functools.partial(_ssd_kernel, t_valid=t_valid)

    def step(n):
        return pl.BlockSpec((1, t_valid, n), lambda b, c: (b * n_chunks + c, 0, 0))

    def per_batch(r, n):
        return pl.BlockSpec((1, r, n), lambda b, c: (b, 0, 0))

    return pl.pallas_call(
        kern,
        grid=(batch, n_chunks),
        in_specs=[step(CONV_DIM), step(D_INNER),
                  pl.BlockSpec((1, CHUNK, 128), lambda b, c: (b * n_chunks + c, 0, 0)),
                  pl.BlockSpec((1, SSM_HEADS, CHUNK), lambda b, c: (b * n_chunks + c, 0, 0)),
                  per_batch(8, CONV_DIM), per_batch(D_INNER, D_STATE)]
                 + [_resident(a.shape) for a in consts],
        out_specs=[step(D_INNER), per_batch(D_INNER, D_STATE), per_batch(CONV_W - 1, CONV_DIM)],
        out_shape=[jax.ShapeDtypeStruct((batch * n_chunks, t_valid, D_INNER), y_dtype),
                   jax.ShapeDtypeStruct((batch, D_INNER, D_STATE), F32),
                   jax.ShapeDtypeStruct((batch, CONV_W - 1, CONV_DIM), F32)],
        scratch_shapes=[pltpu.VMEM((8 + CHUNK, CONV_DIM), F32),
                        pltpu.VMEM((SSM_GROUPS, D_STATE, GROUP_W), F32),
                        pltpu.VMEM((CHUNK, D_INNER), F32)],
        compiler_params=_cp(("arbitrary", "arbitrary"), 48),
        name="ssd",
    )(xbc, z, dt, dtt, tail0, s0, *consts)


def _mem_kv_kernel(m_ref, w_ref, k_ref, v_ref):
    mb = m_ref[...].astype(BF16)
    k_ref[...] = _dot(mb, w_ref[:, :MEM_WIDTH])
    v_ref[...] = _dot(mb, w_ref[:, MEM_WIDTH:])


def _mem_kv(mem2d, w_kv, tm):
    m = mem2d.shape[0]
    return pl.pallas_call(
        _mem_kv_kernel,
        grid=(m // tm,),
        in_specs=[pl.BlockSpec((tm, D_MODEL), lambda i: (i, 0)), _resident(w_kv.shape)],
        out_specs=[pl.BlockSpec((tm, MEM_WIDTH), lambda i: (i, 0))] * 2,
        out_shape=[jax.ShapeDtypeStruct((m, MEM_WIDTH), F32)] * 2,
        compiler_params=_cp(("arbitrary",), 32),
        name="mem_kv",
    )(mem2d, w_kv)


def _mem_attn_kernel(q_ref, k_ref, v_ref, o_ref, *, nb):
    scale = MEM_HEAD_DIM ** -0.5
    for bb in range(nb):
        for h in range(MEM_HEADS):
            hs = slice(h * MEM_HEAD_DIM, (h + 1) * MEM_HEAD_DIM)
            kh = k_ref[bb, :, hs].astype(BF16)
            vh = v_ref[bb, :, hs].astype(BF16)
            s = _dot_nt(q_ref[bb, :, hs], kh) * scale
            p = jnp.exp(s - jnp.max(s, axis=1, keepdims=True))
            l = jnp.sum(p, axis=1, keepdims=True)
            o_ref[bb, :, hs] = (_dot(p.astype(BF16), vh) / l).astype(BF16)


def _mem_attn(mq, mem_k, mem_v, nb, tq):
    batch, rows, _ = mq.shape
    kern = functools.partial(_mem_attn_kernel, nb=nb)
    kv_spec = pl.BlockSpec((nb, N_MEM, MEM_WIDTH), lambda b, i: (b, 0, 0))
    return pl.pallas_call(
        kern,
        grid=(batch // nb, rows // tq),
        in_specs=[pl.BlockSpec((nb, tq, MEM_WIDTH), lambda b, i: (b, i, 0)), kv_spec, kv_spec],
        out_specs=pl.BlockSpec((nb, tq, MEM_WIDTH), lambda b, i: (b, i, 0)),
        out_shape=jax.ShapeDtypeStruct((batch, rows, MEM_WIDTH), BF16),
        compiler_params=_cp(("arbitrary", "arbitrary"), 40),
        name="mem_attn",
    )(mq, mem_k, mem_v)


def _merge_ffn_kernel(x_ref, oa_ref, ob_ref, om_ref, g_ref, woa_ref, wob_ref, wom_ref, wout_ref,
                      ln1g_ref, ln1b_ref, wup_ref, bup_ref, wdown_ref, bdown_ref, ln2g_ref, ln2b_ref, y_ref):
    x = x_ref[...]
    m = g_ref[:, 0:D_MODEL].astype(F32) * _dot(oa_ref[...], woa_ref[...])
    m = m + g_ref[:, D_MODEL:2 * D_MODEL].astype(F32) * _dot(ob_ref[...], wob_ref[...])
    m = m + g_ref[:, 2 * D_MODEL:].astype(F32) * _dot(om_ref[...], wom_ref[...])
    x1 = _layer_norm(ALPHA * x + _dot(m.astype(BF16), wout_ref[...]), ln1g_ref[...], ln1b_ref[...])
    h = jnp.maximum(_dot(x1.astype(BF16), wup_ref[...]) + bup_ref[...], 0.0)
    h = (h * h).astype(BF16)
    y = ALPHA * x1 + _dot(h, wdown_ref[...]) + bdown_ref[...]
    y_ref[...] = _layer_norm(y, ln2g_ref[...], ln2b_ref[...])


def _merge_ffn(x2d, o_mla, o_ssm, o_mem, g, weights, tm):
    m = x2d.shape[0]

    def row(n):
        return pl.BlockSpec((tm, n), lambda i: (i, 0))

    return pl.pallas_call(
        _merge_ffn_kernel,
        grid=(m // tm,),
        in_specs=[row(D_MODEL), row(MLA_HEADS * V_DIM), row(D_INNER), row(MEM_WIDTH), row(N_BRANCH * D_MODEL)]
                 + [_resident(w.shape) for w in weights],
        out_specs=row(D_MODEL),
        out_shape=jax.ShapeDtypeStruct((m, D_MODEL), F32),
        compiler_params=_cp(("arbitrary",), 56),
        name="merge_ffn",
    )(x2d, o_mla, o_ssm, o_mem, g, *weights)


def _rope_table(pos):
    half = ROPE_DIM // 2
    inv = ROPE_THETA ** (-jnp.arange(half, dtype=F32) / half)
    ang = pos.astype(F32)[:, None] * inv[None, :]
    cos, sin = jnp.cos(ang), jnp.sin(ang)
    return jnp.concatenate([cos, cos, cos, cos, -sin, sin, -sin, sin], axis=1)


def _expand_matrix(width):
    src = np.arange(SSM_HEADS * width) // width
    return jnp.asarray((np.arange(128)[:, None] == src[None, :]).astype(np.float32), dtype=BF16)


def _row(v):
    return v.reshape(1, -1).astype(F32)


def kernel(x_prompt, x_sample, mem_prompt, cache_ckv, cache_krope, page_table, cache_mem_k, cache_mem_v, state_ssm, state_conv, w_in, q_norm_g, w_uq, kv_norm_g, w_uk, w_uv, conv_w, conv_b, dt_bias, a_log, d_skip, ssm_norm_g, w_mem_k, w_mem_v, b_gate, w_o_mla, w_o_ssm, w_o_mem, w_out, ln1_g, ln1_b, w_up, b_up, w_down, b_down, ln2_g, ln2_b):
    bp, seq, _ = x_prompt.shape
    bs, t_new, _ = x_sample.shape
    n_pages = page_table.shape[1]
    past = n_pages * PAGE_SIZE

    o_cq, o_ckv, o_kr, o_z, o_xbc, o_dt, o_mq, o_g = np.cumsum(
        [0, Q_LORA, KV_LORA, ROPE_DIM, D_INNER, CONV_DIM, SSM_HEADS, MEM_WIDTH]).tolist()
    w_kr = w_in[:, o_kr:o_z]
    w_all = jnp.concatenate([
        w_in[:, o_cq:o_ckv], w_in[:, o_ckv:o_kr], w_kr, w_kr,
        w_in[:, o_dt:o_mq], jnp.zeros((D_MODEL, 128 - SSM_HEADS), F32),
        w_in[:, o_z:o_xbc], w_in[:, o_xbc:o_dt], w_in[:, o_mq:o_g], w_in[:, o_g:]], axis=1).astype(BF16)
    half = ROPE_DIM // 2
    w_q_nope = w_uq[:, :, :NOPE_DIM].reshape(Q_LORA, MLA_HEADS * NOPE_DIM)
    w_q_rope = w_uq[:, :, NOPE_DIM:]
    w_q_swap = jnp.concatenate([w_q_rope[:, :, half:], w_q_rope[:, :, :half]], axis=-1)
    wq = jnp.concatenate([w_q_nope, w_q_rope.reshape(Q_LORA, -1), w_q_swap.reshape(Q_LORA, -1)], axis=1).astype(BF16)
    wuk_t = jnp.transpose(w_uk, (1, 2, 0)).astype(BF16)
    wuv = jnp.transpose(w_uv, (1, 0, 2)).astype(BF16)
    w_mem_kv = jnp.concatenate([w_mem_k.reshape(D_MODEL, MEM_WIDTH), w_mem_v.reshape(D_MODEL, MEM_WIDTH)], axis=1).astype(BF16)
    merge_w = (w_o_mla.astype(BF16), w_o_ssm.astype(BF16), w_o_mem.astype(BF16), w_out.astype(BF16),
               _row(ln1_g), _row(ln1_b), w_up.astype(BF16), _row(b_up), w_down.astype(BF16), _row(b_down),
               _row(ln2_g), _row(ln2_b))
    gq, gkv, bg = _row(q_norm_g), _row(kv_norm_g), _row(b_gate)

    pad_heads = jnp.zeros((128 - SSM_HEADS,), F32)
    ssd_consts = (conv_w.astype(F32), _row(conv_b),
                  _row(jnp.concatenate([dt_bias, pad_heads])),
                  jnp.broadcast_to(dt_bias.astype(F32)[:, None], (SSM_HEADS, CHUNK)),
                  _row(jnp.concatenate([a_log, pad_heads])),
                  jnp.broadcast_to(a_log.astype(F32)[:, None], (SSM_HEADS, CHUNK)),
                  _row(jnp.repeat(d_skip, SSM_HEAD_DIM)), _row(ssm_norm_g),
                  _expand_matrix(SSM_HEAD_DIM), _expand_matrix(128))

    mp = bp * seq
    n_chunks = seq // CHUNK
    cs_p = _rope_table(jnp.arange(seq))
    xp2d = x_prompt.reshape(mp, D_MODEL)
    cqn, ckv_p, kr_p, kcat, dt_p, z_p, xbc_p, mq_p, g_p = _in_proj(xp2d, w_all, cs_p, gq, gkv, bg, tm=256)
    q_p = _q_prep(cqn, wq, wuk_t, cs_p, tm=256)
    o_mla_p = _prompt_attn(q_p, kcat, wuv, bp, seq, tq=256)

    dt_p3 = dt_p.reshape(bp * n_chunks, CHUNK, 128)
    dtt_p = jnp.swapaxes(dt_p3[:, :, :SSM_HEADS], 1, 2)
    o_ssm_p, ssm_p, conv_p = _ssd(
        xbc_p.reshape(bp * n_chunks, CHUNK, CONV_DIM), z_p.reshape(bp * n_chunks, CHUNK, D_INNER), dt_p3, dtt_p,
        jnp.zeros((bp, 8, CONV_DIM), F32), jnp.zeros((bp, D_INNER, D_STATE), F32), ssd_consts,
        n_chunks=n_chunks, t_valid=CHUNK, y_dtype=BF16)

    mem_k_p, mem_v_p = _mem_kv(mem_prompt.reshape(bp * N_MEM, D_MODEL), w_mem_kv, tm=min(512, bp * N_MEM))
    o_mem_p = _mem_attn(mq_p.reshape(bp, seq, MEM_WIDTH), mem_k_p.reshape(bp, N_MEM, MEM_WIDTH),
                        mem_v_p.reshape(bp, N_MEM, MEM_WIDTH), nb=1, tq=512)
    y_p = _merge_ffn(xp2d, o_mla_p, o_ssm_p.reshape(mp, D_INNER), o_mem_p.reshape(mp, MEM_WIDTH), g_p, merge_w, tm=256)

    ms = bs * t_new
    cs_s = jnp.tile(_rope_table(past + jnp.arange(t_new)), (bs, 1))
    xs2d = x_sample.reshape(ms, D_MODEL)
    tm_s = min(256, ms)
    cqn, ckv_s, kr_s, kcat, dt_s, z_s, xbc_s, mq_s, g_s = _in_proj(xs2d, w_all, cs_s, gq, gkv, bg, tm=tm_s)
    q_s = _q_prep(cqn, wq, wuk_t, cs_s, tm=tm_s)
    q_s = jnp.transpose(q_s.reshape(MLA_HEADS, bs, t_new, QK_DIM), (1, 0, 2, 3)).reshape(bs, MLA_HEADS * t_new, QK_DIM)
    o_lat = _decode_attn(page_table, q_s, kcat.reshape(bs, t_new, QK_DIM).astype(F32), cache_ckv, cache_krope)
    o_lat = jnp.transpose(o_lat.reshape(bs, MLA_HEADS, t_new, KV_LORA), (1, 0, 2, 3)).reshape(MLA_HEADS, ms, KV_LORA)
    o_mla_s = _uv_proj(o_lat, wuv)

    xbc_s3 = xbc_s.reshape(bs, t_new, CONV_DIM).astype(F32)
    dt_s3 = jnp.pad(dt_s.reshape(bs, t_new, 128), ((0, 0), (0, CHUNK - t_new), (0, 0)))
    dtt_s = jnp.swapaxes(dt_s3[:, :, :SSM_HEADS], 1, 2)
    tail_s = jnp.pad(state_conv.astype(F32), ((0, 0), (8 - (CONV_W - 1), 0), (0, 0)))
    o_ssm_s, ssm_s, _ = _ssd(
        xbc_s3, z_s.reshape(bs, t_new, D_INNER).astype(F32), dt_s3, dtt_s, tail_s,
        state_ssm.astype(F32).reshape(bs, D_INNER, D_STATE), ssd_consts, n_chunks=1, t_valid=t_new, y_dtype=F32)
    conv_s = jnp.concatenate([state_conv.astype(F32), xbc_s3], axis=1)[:, -(CONV_W - 1):]

    rows_s = 16
    mq_pad = jnp.pad(mq_s.reshape(bs, t_new, MEM_WIDTH), ((0, 0), (0, rows_s - t_new), (0, 0)))
    o_mem_s = _mem_attn(mq_pad, cache_mem_k.reshape(bs, N_MEM, MEM_WIDTH), cache_mem_v.reshape(bs, N_MEM, MEM_WIDTH),
                        nb=4, tq=rows_s)[:, :t_new]
    y_s = _merge_ffn(xs2d, o_mla_s, o_ssm_s.reshape(ms, D_INNER).astype(BF16), o_mem_s.reshape(ms, MEM_WIDTH), g_s,
                     merge_w, tm=tm_s)

    return (y_p.reshape(bp, seq, D_MODEL), y_s.reshape(bs, t_new, D_MODEL),
            ckv_p.reshape(bp, seq, KV_LORA), kr_p.reshape(bp, seq, ROPE_DIM),
            mem_k_p.reshape(bp, N_MEM, MEM_HEADS, MEM_HEAD_DIM), mem_v_p.reshape(bp, N_MEM, MEM_HEADS, MEM_HEAD_DIM),
            ssm_p.reshape(bp, SSM_HEADS, SSM_HEAD_DIM, D_STATE), conv_p,
            ckv_s.reshape(bs, t_new, KV_LORA), kr_s.reshape(bs, t_new, ROPE_DIM),
            ssm_s.reshape(bs, SSM_HEADS, SSM_HEAD_DIM, D_STATE), conv_s)
```

```python
import functools
import math

import jax
import jax.numpy as jnp
import numpy as np
from jax import lax
from jax.experimental import pallas as pl
from jax.experimental.pallas import tpu as pltpu

F32 = jnp.float32
BF16 = jnp.bfloat16

D_MODEL = 1024
MLA_HEADS = 8
Q_LORA = 384
KV_LORA = 256
NOPE_DIM = 128
ROPE_DIM = 64
V_DIM = 128
ROPE_THETA = 10000.0
QK_DIM = KV_LORA + ROPE_DIM
SSM_HEADS = 32
SSM_HEAD_DIM = 64
D_INNER = SSM_HEADS * SSM_HEAD_DIM
SSM_GROUPS = 4
GROUP_W = D_INNER // SSM_GROUPS
D_STATE = 128
CONV_W = 4
CONV_DIM = D_INNER + 2 * SSM_GROUPS * D_STATE
CHUNK = 128
N_MEM = 256
MEM_HEADS = 4
MEM_HEAD_DIM = 256
MEM_WIDTH = MEM_HEADS * MEM_HEAD_DIM
D_FF = 4 * D_MODEL
N_BRANCH = 3
DEPTH = 1
ALPHA = (2 * DEPTH) ** 0.25
LN_EPS = 1e-5
RMS_EPS = 1e-6
PAGE_SIZE = 128
SCORE_SCALE = (NOPE_DIM + ROPE_DIM) ** -0.5
NEG_BIG = -1e30
KV_TILE = 256

C_CQ = 0
C_SMALL = C_CQ + Q_LORA
C_Z = C_SMALL + 512
C_XBC = C_Z + D_INNER
C_MQ = C_XBC + CONV_DIM
C_G = C_MQ + MEM_WIDTH
C_END = C_G + N_BRANCH * D_MODEL


def _cp(sem, vmem_mb):
    return pltpu.CompilerParams(dimension_semantics=sem, vmem_limit_bytes=vmem_mb << 20)


def _resident(shape):
    nd = len(shape)
    return pl.BlockSpec(shape, lambda *_: (0,) * nd, pipeline_mode=pl.Buffered(1))


def _dot(a, b):
    return jnp.dot(a, b, preferred_element_type=F32)


def _dot_nt(a, b):
    return lax.dot_general(a, b, (((1,), (1,)), ((), ())), preferred_element_type=F32)


def _rms(v, g, eps):
    return v * lax.rsqrt(jnp.mean(v * v, axis=-1, keepdims=True) + eps) * g


def _layer_norm(v, g, b):
    mu = jnp.mean(v, axis=-1, keepdims=True)
    d = v - mu
    var = jnp.mean(d * d, axis=-1, keepdims=True)
    return d * lax.rsqrt(var + LN_EPS) * g + b


def _in_proj_kernel(x_ref, w_ref, cs_ref, gq_ref, gkv_ref, bg_ref,
                    cqn_ref, ckv_ref, kr_ref, kcat_ref, ckvt_ref, dt_ref, z_ref, xbc_ref, mq_ref, g_ref):
    xb = x_ref[...].astype(BF16)

    def mm(a, b):
        return _dot(xb, w_ref[:, a:b])

    cqn_ref[...] = _rms(mm(C_CQ, C_SMALL), gq_ref[...], RMS_EPS).astype(BF16)

    small = mm(C_SMALL, C_Z)
    ckv = _rms(small[:, :KV_LORA], gkv_ref[...], RMS_EPS)
    ckv_ref[...] = ckv
    kcat_ref[:, 0:KV_LORA] = ckv.astype(BF16)
    for t in range(ckvt_ref.shape[0]):
        ckvt_ref[t] = ckv[t * KV_TILE:(t + 1) * KV_TILE, :].T.astype(BF16)
    a = small[:, 256:384]
    b = pltpu.roll(a, ROPE_DIM // 2, 1)
    cs = cs_ref[...]
    ro = a * cs[:, :128] + b * cs[:, 128:]
    kr_ref[...] = ro[:, :ROPE_DIM]
    kcat_ref[:, KV_LORA:QK_DIM] = ro[:, :ROPE_DIM].astype(BF16)
    dt_ref[...] = small[:, 384:512]

    for c in range(0, D_INNER, 1024):
        z_ref[:, c:c + 1024] = mm(C_Z + c, C_Z + c + 1024).astype(BF16)
    for c in range(0, CONV_DIM, 1024):
        xbc_ref[:, c:c + 1024] = mm(C_XBC + c, C_XBC + c + 1024).astype(BF16)
    mq_ref[...] = mm(C_MQ, C_G).astype(BF16)
    for c in range(0, N_BRANCH * D_MODEL, 1024):
        gr = mm(C_G + c, C_G + c + 1024) + bg_ref[:, c:c + 1024]
        g_ref[:, c:c + 1024] = jax.nn.sigmoid(gr).astype(BF16)


def _in_proj(x2d, w_all, cs_tab, gq, gkv, bg, tm):
    m = x2d.shape[0]
    ncs = cs_tab.shape[0] // tm
    widths = [(Q_LORA, BF16), (KV_LORA, F32), (ROPE_DIM, F32), (QK_DIM, BF16), (128, F32),
              (D_INNER, BF16), (CONV_DIM, BF16), (MEM_WIDTH, BF16), (N_BRANCH * D_MODEL, BF16)]

    def row(n):
        return pl.BlockSpec((tm, n), lambda i: (i, 0))

    out_specs = [row(n) for n, _ in widths]
    out_shape = [jax.ShapeDtypeStruct((m, n), d) for n, d in widths]
    out_specs.insert(4, pl.BlockSpec((tm // KV_TILE, KV_LORA, KV_TILE), lambda i: (i, 0, 0)))
    out_shape.insert(4, jax.ShapeDtypeStruct((m // KV_TILE, KV_LORA, KV_TILE), BF16))
    return pl.pallas_call(
        _in_proj_kernel,
        grid=(m // tm,),
        in_specs=[row(D_MODEL), _resident(w_all.shape),
                  pl.BlockSpec((tm, 256), lambda i: (i % ncs, 0)),
                  _resident(gq.shape), _resident(gkv.shape), _resident(bg.shape)],
        out_specs=out_specs,
        out_shape=out_shape,
        compiler_params=_cp(("arbitrary",), 56),
        name="in_proj",
    )(x2d, w_all, cs_tab, gq, gkv, bg)


def _q_prep_kernel(c_ref, wq_ref, wuk_ref, cs_ref, q_ref, *, transposed):
    c = c_ref[...]
    cs = cs_ref[...]
    for h in range(MLA_HEADS):
        qn = _dot(c, wq_ref[:, h * NOPE_DIM:(h + 1) * NOPE_DIM]).astype(BF16)
        ql = _dot(qn, wuk_ref[h]) * SCORE_SCALE
        if transposed:
            q_ref[h, 0:KV_LORA, :] = ql.T.astype(BF16)
        else:
            q_ref[h, :, 0:KV_LORA] = ql.astype(BF16)
    r0 = MLA_HEADS * NOPE_DIM
    r1 = r0 + MLA_HEADS * ROPE_DIM
    for p in range(MLA_HEADS // 2):
        a = _dot(c, wq_ref[:, r0 + p * 128:r0 + (p + 1) * 128])
        b = _dot(c, wq_ref[:, r1 + p * 128:r1 + (p + 1) * 128])
        ro = (a * cs[:, :128] + b * cs[:, 128:]) * SCORE_SCALE
        if transposed:
            rot = ro.T.astype(BF16)
            q_ref[2 * p, KV_LORA:QK_DIM, :] = rot[:ROPE_DIM]
            q_ref[2 * p + 1, KV_LORA:QK_DIM, :] = rot[ROPE_DIM:]
        else:
            q_ref[2 * p, :, KV_LORA:QK_DIM] = ro[:, :ROPE_DIM].astype(BF16)
            q_ref[2 * p + 1, :, KV_LORA:QK_DIM] = pltpu.roll(ro, ROPE_DIM, 1)[:, :ROPE_DIM].astype(BF16)


def _q_prep(cqn, wq, wuk_t, cs_tab, tm, transposed):
    m = cqn.shape[0]
    ncs = cs_tab.shape[0] // tm
    if transposed:
        out_spec = pl.BlockSpec((MLA_HEADS, QK_DIM, tm), lambda i: (0, 0, i))
        out_shape = jax.ShapeDtypeStruct((MLA_HEADS, QK_DIM, m), BF16)
    else:
        out_spec = pl.BlockSpec((MLA_HEADS, tm, QK_DIM), lambda i: (0, i, 0))
        out_shape = jax.ShapeDtypeStruct((MLA_HEADS, m, QK_DIM), BF16)
    return pl.pallas_call(
        functools.partial(_q_prep_kernel, transposed=transposed),
        grid=(m // tm,),
        in_specs=[pl.BlockSpec((tm, Q_LORA), lambda i: (i, 0)), _resident(wq.shape), _resident(wuk_t.shape),
                  pl.BlockSpec((tm, 256), lambda i: (i % ncs, 0))],
        out_specs=out_spec,
        out_shape=out_shape,
        compiler_params=_cp(("arbitrary",), 32),
        name="q_prep",
    )(cqn, wq, wuk_t, cs_tab)


def _prompt_attn_kernel(qt_ref, k_ref, vt_ref, wuvt_ref, o_ref, m_sc, l_sc, acc_sc):
    i = pl.program_id(1)
    t = KV_TILE
    m_sc[...] = jnp.full(m_sc.shape, NEG_BIG, F32)
    l_sc[...] = jnp.zeros(l_sc.shape, F32)
    acc_sc[...] = jnp.zeros(acc_sc.shape, F32)

    def kv_block(j, masked):
        k = k_ref[pl.ds(pl.multiple_of(j * t, t), t), :]
        kc = k[:, :KV_LORA]
        kr = k[:, KV_LORA:]
        vt = vt_ref[j]
        if masked:
            keep = lax.broadcasted_iota(jnp.int32, (t, t), 0) <= lax.broadcasted_iota(jnp.int32, (t, t), 1)
        def scores(h):
            return _dot(kc, qt_ref[h, 0:KV_LORA, :]) + _dot(kr, qt_ref[h, KV_LORA:QK_DIM, :])

        ahead = 4
        pending = [scores(h) for h in range(ahead)]
        for h in range(MLA_HEADS):
            st = pending.pop(0)
            if h + ahead < MLA_HEADS:
                pending.append(scores(h + ahead))
            if masked:
                st = jnp.where(keep, st, NEG_BIG)
            m_prev = m_sc[h:h + 1, :]
            m_new = jnp.maximum(m_prev, jnp.max(st, axis=0, keepdims=True))
            alpha = jnp.exp(m_prev - m_new)
            p = jnp.exp(st - m_new)
            l_sc[h:h + 1, :] = alpha * l_sc[h:h + 1, :] + jnp.sum(p, axis=0, keepdims=True)
            acc_sc[h] = alpha * acc_sc[h] + _dot(vt, p.astype(BF16))
            m_sc[h:h + 1, :] = m_new

    def body(j, carry):
        kv_block(j, False)
        return carry

    lax.fori_loop(0, i, body, 0)
    kv_block(i, True)

    for h in range(MLA_HEADS):
        ot = (acc_sc[h] / l_sc[h:h + 1, :]).astype(BF16)
        o_ref[:, h * V_DIM:(h + 1) * V_DIM] = _dot(wuvt_ref[h], ot).T.astype(BF16)


def _prompt_attn(qt, kcat, ckvt, wuvt, batch, seq):
    t = KV_TILE
    nq = seq // t
    return pl.pallas_call(
        _prompt_attn_kernel,
        grid=(batch, nq),
        in_specs=[pl.BlockSpec((MLA_HEADS, QK_DIM, t), lambda b, i: (0, 0, b * nq + i)),
                  pl.BlockSpec((seq, QK_DIM), lambda b, i: (b, 0)),
                  pl.BlockSpec((nq, KV_LORA, t), lambda b, i: (b, 0, 0)),
                  _resident(wuvt.shape)],
        out_specs=pl.BlockSpec((t, MLA_HEADS * V_DIM), lambda b, i: (b * nq + i, 0)),
        out_shape=jax.ShapeDtypeStruct((batch * seq, MLA_HEADS * V_DIM), BF16),
        scratch_shapes=[pltpu.VMEM((MLA_HEADS, t), F32), pltpu.VMEM((MLA_HEADS, t), F32),
                        pltpu.VMEM((MLA_HEADS, KV_LORA, t), F32)],
        compiler_params=_cp(("arbitrary", "arbitrary"), 32),
        name="prompt_attn",
    )(qt, kcat, ckvt, wuvt)


def _decode_attn_kernel(pt_ref, q_ref, kn_ref, ckv_hbm, krt_hbm, o_ref, kbuf, rbuf, kb_sc, s_sc, sem, *, n_pages, n_new):
    b = pl.program_id(0)
    nb = pl.num_programs(0)
    slot = b % 2

    def page_copies(page, p, sl):
        rows = pl.ds(pl.multiple_of(p * PAGE_SIZE, PAGE_SIZE), PAGE_SIZE)
        return (pltpu.make_async_copy(ckv_hbm.at[page], kbuf.at[sl, rows, :], sem.at[0, sl]),
                pltpu.make_async_copy(krt_hbm.at[page], rbuf.at[sl, p], sem.at[1, sl]))

    def issue(bi, sl):
        def body(p, carry):
            for cp in page_copies(pt_ref[bi * n_pages + p], p, sl):
                cp.start()
            return carry
        lax.fori_loop(0, n_pages, body, 0, unroll=8)

    @pl.when(b == 0)
    def _():
        issue(0, 0)

    @pl.when(b + 1 < nb)
    def _():
        issue(b + 1, 1 - slot)

    def wait_body(p, carry):
        for cp in page_copies(0, p, slot):
            cp.wait()
        return carry
    lax.fori_loop(0, n_pages, wait_body, 0, unroll=8)

    q = q_ref[0]
    rows = q.shape[0]
    qc = q[:, :KV_LORA]
    qr = q[:, KV_LORA:]
    qf = q.astype(F32)
    kn = kn_ref[0]
    tok = lax.broadcasted_iota(jnp.int32, (rows, 1), 0) % n_new

    s_new = []
    for j in range(n_new):
        sj = jnp.sum(qf * kn[j:j + 1, :], axis=1, keepdims=True)
        s_new.append(jnp.where(tok >= j, sj, NEG_BIG))
    m0 = s_new[0]
    for j in range(1, n_new):
        m0 = jnp.maximum(m0, s_new[j])

    for c in range(n_pages // 2):
        r = slice(2 * c * PAGE_SIZE, 2 * (c + 1) * PAGE_SIZE)
        kc = kbuf[slot, r, :].astype(BF16)
        kb_sc[r, :] = kc
        rt = jnp.concatenate([rbuf[slot, 2 * c], rbuf[slot, 2 * c + 1]], axis=1).astype(BF16)
        s_sc[:, r] = _dot_nt(qc, kc) + _dot(qr, rt)
    s = s_sc[...]
    m = jnp.maximum(m0, jnp.max(s, axis=1, keepdims=True))
    p = jnp.exp(s - m)
    l = jnp.sum(p, axis=1, keepdims=True)
    acc = _dot(p.astype(BF16), kb_sc[...])
    for j in range(n_new):
        pj = jnp.exp(s_new[j] - m)
        l = l + pj
        acc = acc + pj * kn[j:j + 1, :KV_LORA]
    o_ref[0] = (acc / l).astype(BF16)


def _decode_attn(page_table, q, k_new, cache_ckv, cache_krope_t):
    nb, n_pages = page_table.shape
    rows = q.shape[1]
    n_new = k_new.shape[1]
    past = n_pages * PAGE_SIZE
    kern = functools.partial(_decode_attn_kernel, n_pages=n_pages, n_new=n_new)
    grid_spec = pltpu.PrefetchScalarGridSpec(
        num_scalar_prefetch=1,
        grid=(nb,),
        in_specs=[pl.BlockSpec((1, rows, QK_DIM), lambda b, pt: (b, 0, 0)),
                  pl.BlockSpec((1, n_new, QK_DIM), lambda b, pt: (b, 0, 0)),
                  pl.BlockSpec(memory_space=pl.ANY),
                  pl.BlockSpec(memory_space=pl.ANY)],
        out_specs=pl.BlockSpec((1, rows, KV_LORA), lambda b, pt: (b, 0, 0)),
        scratch_shapes=[pltpu.VMEM((2, past, KV_LORA), F32), pltpu.VMEM((2, n_pages, ROPE_DIM, PAGE_SIZE), F32),
                        pltpu.VMEM((past, KV_LORA), BF16), pltpu.VMEM((rows, past), F32),
                        pltpu.SemaphoreType.DMA((2, 2))],
    )
    return pl.pallas_call(
        kern,
        grid_spec=grid_spec,
        out_shape=jax.ShapeDtypeStruct((nb, rows, KV_LORA), BF16),
        compiler_params=_cp(("arbitrary",), 40),
        name="decode_attn",
    )(page_table.reshape(-1), q, k_new, cache_ckv, cache_krope_t)


def _uv_proj_kernel(o_ref, w_ref, out_ref):
    out_ref[...] = _dot(o_ref[0], w_ref[0]).astype(BF16)


def _uv_proj(o_lat, wuv):
    m = o_lat.shape[1]
    return pl.pallas_call(
        _uv_proj_kernel,
        grid=(MLA_HEADS,),
        in_specs=[pl.BlockSpec((1, m, KV_LORA), lambda h: (h, 0, 0)),
                  pl.BlockSpec((1, KV_LORA, V_DIM), lambda h: (h, 0, 0))],
        out_specs=pl.BlockSpec((m, V_DIM), lambda h: (0, h)),
        out_shape=jax.ShapeDtypeStruct((m, MLA_HEADS * V_DIM), BF16),
        compiler_params=_cp(("arbitrary",), 16),
        name="uv_proj",
    )(o_lat, wuv)


def _split_dot(v, e, terms, left=True):
    out = None
    r = v
    for _ in range(terms):
        hi = r.astype(BF16)
        d = _dot(hi, e) if left else _dot(e, hi)
        out = d if out is None else out + d
        r = r - hi.astype(F32)
    return out


def _ssd_kernel(xbc_ref, z_ref, dt_ref, dtt_ref, tail_ref, s0_ref, cw_ref, cb_ref, dtb_ref, dtbt_ref,
                alog_ref, alogt_ref, dskip_ref, ng_ref, e64_ref, e128_ref,
                y_ref, sout_ref, cout_ref, ext_sc, st_sc, y_sc, *, t_valid):
    c = pl.program_id(1)
    n_pairs = SSM_HEADS // 2
    pairs_per_group = n_pairs // SSM_GROUPS

    @pl.when(c == 0)
    def _():
        ext_sc[0:8, :] = tail_ref[0]
        for q in range(n_pairs):
            g, pr = divmod(q, pairs_per_group)
            st_sc[g, :, pr * 128:(pr + 1) * 128] = s0_ref[0, q * 128:(q + 1) * 128, :].T

    if t_valid < CHUNK:
        ext_sc[8:8 + CHUNK, :] = jnp.zeros((CHUNK, CONV_DIM), F32)
    ext_sc[8:8 + t_valid, :] = xbc_ref[0].astype(F32)
    conv = cb_ref[...] + cw_ref[0:1, :] * ext_sc[5:5 + CHUNK, :]
    for k in range(1, CONV_W):
        conv = conv + cw_ref[k:k + 1, :] * ext_sc[5 + k:5 + k + CHUNK, :]
    tail = ext_sc[8 + t_valid - (CONV_W - 1):8 + t_valid, :]
    cout_ref[0] = tail
    ext_sc[8 - (CONV_W - 1):8, :] = tail
    xc = conv * jax.nn.sigmoid(conv)
    xs = xc[:, :D_INNER]
    bm = xc[:, D_INNER:D_INNER + SSM_GROUPS * D_STATE]
    cm = xc[:, D_INNER + SSM_GROUPS * D_STATE:]

    rowi = lax.broadcasted_iota(jnp.int32, (CHUNK, CHUNK), 0)
    coli = lax.broadcasted_iota(jnp.int32, (CHUNK, CHUNK), 1)
    lower = coli <= rowi
    tri = jnp.where(lower, 1.0, 0.0).astype(BF16)
    tri_t = jnp.where(rowi <= coli, 1.0, 0.0).astype(BF16)

    dt = jax.nn.softplus(dt_ref[0] + dtb_ref[...])
    dtt = jax.nn.softplus(dtt_ref[0] + dtbt_ref[...])
    if t_valid < CHUNK:
        dt = jnp.where(rowi < t_valid, dt, 0.0)
        dtt = jnp.where(lax.broadcasted_iota(jnp.int32, (SSM_HEADS, CHUNK), 1) < t_valid, dtt, 0.0)
    da = dt * (-jnp.exp(alog_ref[...]))
    dat = dtt * (-jnp.exp(alogt_ref[...]))
    cum = _split_dot(da, tri, 3, left=False)
    cum_t = _split_dot(dat, tri_t, 3)
    last = cum[CHUNK - 1:CHUNK, :]
    e64 = e64_ref[...]
    ec_e = _split_dot(jnp.exp(cum), e64, 2)
    wb_e = _split_dot(dt * jnp.exp(last - cum), e64, 2)
    dec_e = _split_dot(jnp.broadcast_to(jnp.exp(last), (16, CHUNK)), e64, 2)[0:1, :]
    cum_e = _split_dot(cum, e128_ref[...], 3)

    lane = coli
    for g in range(SSM_GROUPS):
        cg = cm[:, g * D_STATE:(g + 1) * D_STATE].astype(BF16)
        bg32 = bm[:, g * D_STATE:(g + 1) * D_STATE]
        cb = _dot_nt(cg, bg32.astype(BF16))
        st = st_sc[g]
        gs = slice(g * GROUP_W, (g + 1) * GROUP_W)
        y_int = _dot(cg, st.astype(BF16)) * ec_e[:, gs]
        for pr in range(pairs_per_group):
            q = g * pairs_per_group + pr
            ws = []
            for h in (2 * q, 2 * q + 1):
                seg = cum_e[:, h * 128:(h + 1) * 128] - cum_t[h:h + 1, :]
                decay = jnp.exp(jnp.where(lower, seg, NEG_BIG)) * dtt[h:h + 1, :]
                ws.append((cb * decay).astype(BF16))
            w_pair = jnp.concatenate(ws, axis=1)
            xp = xs[:, q * 128:(q + 1) * 128]
            x_bd = jnp.concatenate([jnp.where(lane < SSM_HEAD_DIM, xp, 0.0).astype(BF16),
                                    jnp.where(lane >= SSM_HEAD_DIM, xp, 0.0).astype(BF16)], axis=0)
            y_sc[:, q * 128:(q + 1) * 128] = (_dot(w_pair, x_bd) + y_int[:, pr * 128:(pr + 1) * 128]
                                              + dskip_ref[:, q * 128:(q + 1) * 128] * xp)
        xw = (xs[:, gs] * wb_e[:, gs]).astype(BF16)
        st_sc[g] = st * dec_e[:, gs] + _dot(bg32.T.astype(BF16), xw)

    for g in range(SSM_GROUPS):
        gs = slice(g * GROUP_W, (g + 1) * GROUP_W)
        zf = z_ref[0, :, gs].astype(F32)
        yv = y_sc[0:t_valid, gs] * (zf * jax.nn.sigmoid(zf))
        y_ref[0, :, gs] = _rms(yv, ng_ref[:, gs], RMS_EPS).astype(y_ref.dtype)

    for q in range(n_pairs):
        g, pr = divmod(q, pairs_per_group)
        sout_ref[0, q * 128:(q + 1) * 128, :] = st_sc[g, :, pr * 128:(pr + 1) * 128].T


def _ssd(xbc, z, dt, dtt, tail0, s0, consts, n_chunks, t_valid, y_dtype):
    batch = s0.shape[0]
    kern = functools.partial(_ssd_kernel, t_valid=t_valid)

    def step(n):
        return pl.BlockSpec((1, t_valid, n), lambda b, c: (b * n_chunks + c, 0, 0))

    def per_batch(r, n):
        return pl.BlockSpec((1, r, n), lambda b, c: (b, 0, 0))

    return pl.pallas_call(
        kern,
        grid=(batch, n_chunks),
        in_specs=[step(CONV_DIM), step(D_INNER),
                  pl.BlockSpec((1, CHUNK, 128), lambda b, c: (b * n_chunks + c, 0, 0)),
                  pl.BlockSpec((1, SSM_HEADS, CHUNK), lambda b, c: (b * n_chunks + c, 0, 0)),
                  per_batch(8, CONV_DIM), per_batch(D_INNER, D_STATE)]
                 + [_resident(a.shape) for a in consts],
        out_specs=[step(D_INNER), per_batch(D_INNER, D_STATE), per_batch(CONV_W - 1, CONV_DIM)],
        out_shape=[jax.ShapeDtypeStruct((batch * n_chunks, t_valid, D_INNER), y_dtype),
                   jax.ShapeDtypeStruct((batch, D_INNER, D_STATE), F32),
                   jax.ShapeDtypeStruct((batch, CONV_W - 1, CONV_DIM), F32)],
        scratch_shapes=[pltpu.VMEM((8 + CHUNK, CONV_DIM), F32),
                        pltpu.VMEM((SSM_GROUPS, D_STATE, GROUP_W), F32),
                        pltpu.VMEM((CHUNK, D_INNER), F32)],
        compiler_params=_cp(("arbitrary", "arbitrary"), 48),
        name="ssd",
    )(xbc, z, dt, dtt, tail0, s0, *consts)


def _mem_kv_kernel(m_ref, w_ref, k_ref, v_ref):
    mb = m_ref[...].astype(BF16)
    k_ref[...] = _dot(mb, w_ref[:, :MEM_WIDTH])
    v_ref[...] = _dot(mb, w_ref[:, MEM_WIDTH:])


def _mem_kv(mem2d, w_kv, tm):
    m = mem2d.shape[0]
    return pl.pallas_call(
        _mem_kv_kernel,
        grid=(m // tm,),
        in_specs=[pl.BlockSpec((tm, D_MODEL), lambda i: (i, 0)), _resident(w_kv.shape)],
        out_specs=[pl.BlockSpec((tm, MEM_WIDTH), lambda i: (i, 0))] * 2,
        out_shape=[jax.ShapeDtypeStruct((m, MEM_WIDTH), F32)] * 2,
        compiler_params=_cp(("arbitrary",), 32),
        name="mem_kv",
    )(mem2d, w_kv)


def _mem_attn_kernel(q_ref, k_ref, v_ref, o_ref, *, nb):
    scale = MEM_HEAD_DIM ** -0.5
    for bb in range(nb):
        for h in range(MEM_HEADS):
            hs = slice(h * MEM_HEAD_DIM, (h + 1) * MEM_HEAD_DIM)
            kh = k_ref[bb, :, hs].astype(BF16)
            vh = v_ref[bb, :, hs].astype(BF16)
            s = _dot_nt(q_ref[bb, :, hs], kh) * scale
            p = jnp.exp(s - jnp.max(s, axis=1, keepdims=True))
            l = jnp.sum(p, axis=1, keepdims=True)
            o_ref[bb, :, hs] = (_dot(p.astype(BF16), vh) / l).astype(BF16)


def _mem_attn(mq, mem_k, mem_v, nb, tq):
    batch, rows, _ = mq.shape
    kern = functools.partial(_mem_attn_kernel, nb=nb)
    kv_spec = pl.BlockSpec((nb, N_MEM, MEM_WIDTH), lambda b, i: (b, 0, 0))
    return pl.pallas_call(
        kern,
        grid=(batch // nb, rows // tq),
        in_specs=[pl.BlockSpec((nb, tq, MEM_WIDTH), lambda b, i: (b, i, 0)), kv_spec, kv_spec],
        out_specs=pl.BlockSpec((nb, tq, MEM_WIDTH), lambda b, i: (b, i, 0)),
        out_shape=jax.ShapeDtypeStruct((batch, rows, MEM_WIDTH), BF16),
        compiler_params=_cp(("arbitrary", "arbitrary"), 40),
        name="mem_attn",
    )(mq, mem_k, mem_v)


MEM_HALVES = MEM_HEAD_DIM // 128
MEM_ROW_GROUP = MEM_HALVES * MEM_HEADS


def _mem_attn_cache_kernel(q_ref, k_ref, v_ref, o_ref, *, nb, n_tok):
    rows = MEM_HEADS * n_tok
    n_col = N_MEM * MEM_ROW_GROUP
    col = lax.broadcasted_iota(jnp.int32, (rows, n_col), 1)
    row = lax.broadcasted_iota(jnp.int32, (rows, n_col), 0)
    own = (col % MEM_ROW_GROUP) == (row // n_tok)
    scale = MEM_HEAD_DIM ** -0.5
    for bb in range(nb):
        g = _dot_nt(q_ref[bb], k_ref[bb].astype(BF16))
        s = (g[:rows] + pltpu.roll(g[rows:], n_col - MEM_HEADS, 1)) * scale
        s = jnp.where(own, s, NEG_BIG)
        p = jnp.exp(s - jnp.max(s, axis=1, keepdims=True))
        l = jnp.sum(p, axis=1, keepdims=True)
        p2 = jnp.concatenate([p, pltpu.roll(p, MEM_HEADS, 1)], axis=0).astype(BF16)
        o = _dot(p2, v_ref[bb].astype(BF16))
        o_ref[bb] = (o / jnp.concatenate([l, l], axis=0)).astype(BF16)


def _mem_attn_cache(q, mem_k, mem_v, nb, n_tok):
    batch, rows, _ = q.shape
    kern = functools.partial(_mem_attn_cache_kernel, nb=nb, n_tok=n_tok)
    kv_spec = pl.BlockSpec((nb,) + mem_k.shape[1:], lambda b: (b, 0, 0))
    q_spec = pl.BlockSpec((nb, rows, 128), lambda b: (b, 0, 0))
    return pl.pallas_call(
        kern,
        grid=(batch // nb,),
        in_specs=[q_spec, kv_spec, kv_spec],
        out_specs=q_spec,
        out_shape=jax.ShapeDtypeStruct((batch, rows, 128), BF16),
        compiler_params=_cp(("arbitrary",), 40),
        name="mem_attn_cache",
    )(q, mem_k, mem_v)


def _merge_ffn_kernel(x_ref, oa_ref, ob_ref, om_ref, g_ref, woa_ref, wob_ref, wom_ref, wout_ref,
                      ln1g_ref, ln1b_ref, wup_ref, bup_ref, wdown_ref, bdown_ref, ln2g_ref, ln2b_ref, y_ref):
    x = x_ref[...]
    m = g_ref[:, 0:D_MODEL].astype(F32) * _dot(oa_ref[...], woa_ref[...])
    m = m + g_ref[:, D_MODEL:2 * D_MODEL].astype(F32) * _dot(ob_ref[...], wob_ref[...])
    m = m + g_ref[:, 2 * D_MODEL:].astype(F32) * _dot(om_ref[...], wom_ref[...])
    x1 = _layer_norm(ALPHA * x + _dot(m.astype(BF16), wout_ref[...]), ln1g_ref[...], ln1b_ref[...])
    h = jnp.maximum(_dot(x1.astype(BF16), wup_ref[...]) + bup_ref[...], 0.0)
    h = (h * h).astype(BF16)
    y = ALPHA * x1 + _dot(h, wdown_ref[...]) + bdown_ref[...]
    y_ref[...] = _layer_norm(y, ln2g_ref[...], ln2b_ref[...])


def _merge_ffn(x2d, o_mla, o_ssm, o_mem, g, weights, tm):
    m = x2d.shape[0]

    def row(n):
        return pl.BlockSpec((tm, n), lambda i: (i, 0))

    return pl.pallas_call(
        _merge_ffn_kernel,
        grid=(m // tm,),
        in_specs=[row(D_MODEL), row(MLA_HEADS * V_DIM), row(D_INNER), row(MEM_WIDTH), row(N_BRANCH * D_MODEL)]
                 + [_resident(w.shape) for w in weights],
        out_specs=row(D_MODEL),
        out_shape=jax.ShapeDtypeStruct((m, D_MODEL), F32),
        compiler_params=_cp(("arbitrary",), 56),
        name="merge_ffn",
    )(x2d, o_mla, o_ssm, o_mem, g, *weights)


def _rope_table(pos):
    half = ROPE_DIM // 2
    inv = ROPE_THETA ** (-jnp.arange(half, dtype=F32) / half)
    ang = pos.astype(F32)[:, None] * inv[None, :]
    cos, sin = jnp.cos(ang), jnp.sin(ang)
    return jnp.concatenate([cos, cos, cos, cos, -sin, sin, -sin, sin], axis=1)


def _expand_matrix(width):
    src = np.arange(SSM_HEADS * width) // width
    return jnp.asarray((np.arange(128)[:, None] == src[None, :]).astype(np.float32), dtype=BF16)


def _row(v):
    return v.reshape(1, -1).astype(F32)


def kernel(x_prompt, x_sample, mem_prompt, cache_ckv, cache_krope, page_table, cache_mem_k, cache_mem_v, state_ssm, state_conv, w_in, q_norm_g, w_uq, kv_norm_g, w_uk, w_uv, conv_w, conv_b, dt_bias, a_log, d_skip, ssm_norm_g, w_mem_k, w_mem_v, b_gate, w_o_mla, w_o_ssm, w_o_mem, w_out, ln1_g, ln1_b, w_up, b_up, w_down, b_down, ln2_g, ln2_b):
    bp, seq, _ = x_prompt.shape
    bs, t_new, _ = x_sample.shape
    n_pages = page_table.shape[1]
    past = n_pages * PAGE_SIZE

    o_cq, o_ckv, o_kr, o_z, o_xbc, o_dt, o_mq, o_g = np.cumsum(
        [0, Q_LORA, KV_LORA, ROPE_DIM, D_INNER, CONV_DIM, SSM_HEADS, MEM_WIDTH]).tolist()
    w_kr = w_in[:, o_kr:o_z]
    w_all = jnp.concatenate([
        w_in[:, o_cq:o_ckv], w_in[:, o_ckv:o_kr], w_kr, w_kr,
        w_in[:, o_dt:o_mq], jnp.zeros((D_MODEL, 128 - SSM_HEADS), F32),
        w_in[:, o_z:o_xbc], w_in[:, o_xbc:o_dt], w_in[:, o_mq:o_g], w_in[:, o_g:]], axis=1).astype(BF16)
    half = ROPE_DIM // 2
    w_q_nope = w_uq[:, :, :NOPE_DIM].reshape(Q_LORA, MLA_HEADS * NOPE_DIM)
    w_q_rope = w_uq[:, :, NOPE_DIM:]
    w_q_swap = jnp.concatenate([w_q_rope[:, :, half:], w_q_rope[:, :, :half]], axis=-1)
    wq = jnp.concatenate([w_q_nope, w_q_rope.reshape(Q_LORA, -1), w_q_swap.reshape(Q_LORA, -1)], axis=1).astype(BF16)
    wuk_t = jnp.transpose(w_uk, (1, 2, 0)).astype(BF16)
    wuv = jnp.transpose(w_uv, (1, 0, 2)).astype(BF16)
    wuv_t = jnp.transpose(w_uv, (1, 2, 0)).astype(BF16)
    w_mem_kv = jnp.concatenate([w_mem_k.reshape(D_MODEL, MEM_WIDTH), w_mem_v.reshape(D_MODEL, MEM_WIDTH)], axis=1).astype(BF16)
    merge_w = (w_o_mla.astype(BF16), w_o_ssm.astype(BF16), w_o_mem.astype(BF16), w_out.astype(BF16),
               _row(ln1_g), _row(ln1_b), w_up.astype(BF16), _row(b_up), w_down.astype(BF16), _row(b_down),
               _row(ln2_g), _row(ln2_b))
    gq, gkv, bg = _row(q_norm_g), _row(kv_norm_g), _row(b_gate)

    pad_heads = jnp.zeros((128 - SSM_HEADS,), F32)
    ssd_consts = (conv_w.astype(F32), _row(conv_b),
                  _row(jnp.concatenate([dt_bias, pad_heads])),
                  jnp.broadcast_to(dt_bias.astype(F32)[:, None], (SSM_HEADS, CHUNK)),
                  _row(jnp.concatenate([a_log, pad_heads])),
                  jnp.broadcast_to(a_log.astype(F32)[:, None], (SSM_HEADS, CHUNK)),
                  _row(jnp.repeat(d_skip, SSM_HEAD_DIM)), _row(ssm_norm_g),
                  _expand_matrix(SSM_HEAD_DIM), _expand_matrix(128))

    mp = bp * seq
    n_chunks = seq // CHUNK
    cs_p = _rope_table(jnp.arange(seq))
    xp2d = x_prompt.reshape(mp, D_MODEL)
    cqn, ckv_p, kr_p, kcat, ckvt, dt_p, z_p, xbc_p, mq_p, g_p = _in_proj(xp2d, w_all, cs_p, gq, gkv, bg, tm=256)
    qt_p = _q_prep(cqn, wq, wuk_t, cs_p, tm=256, transposed=True)
    o_mla_p = _prompt_attn(qt_p, kcat, ckvt, wuv_t, bp, seq)

    dt_p3 = dt_p.reshape(bp * n_chunks, CHUNK, 128)
    dtt_p = jnp.swapaxes(dt_p3[:, :, :SSM_HEADS], 1, 2)
    o_ssm_p, ssm_p, conv_p = _ssd(
        xbc_p.reshape(bp * n_chunks, CHUNK, CONV_DIM), z_p.reshape(bp * n_chunks, CHUNK, D_INNER), dt_p3, dtt_p,
        jnp.zeros((bp, 8, CONV_DIM), F32), jnp.zeros((bp, D_INNER, D_STATE), F32), ssd_consts,
        n_chunks=n_chunks, t_valid=CHUNK, y_dtype=BF16)

    mem_k_p, mem_v_p = _mem_kv(mem_prompt.reshape(bp * N_MEM, D_MODEL), w_mem_kv, tm=min(512, bp * N_MEM))
    o_mem_p = _mem_attn(mq_p.reshape(bp, seq, MEM_WIDTH), mem_k_p.reshape(bp, N_MEM, MEM_WIDTH),
                        mem_v_p.reshape(bp, N_MEM, MEM_WIDTH), nb=1, tq=512)
    y_p = _merge_ffn(xp2d, o_mla_p, o_ssm_p.reshape(mp, D_INNER), o_mem_p.reshape(mp, MEM_WIDTH), g_p, merge_w, tm=256)

    ms = bs * t_new
    cs_s = jnp.tile(_rope_table(past + jnp.arange(t_new)), (bs, 1))
    xs2d = x_sample.reshape(ms, D_MODEL)
    tm_s = min(256, ms)
    cqn, ckv_s, kr_s, kcat, _, dt_s, z_s, xbc_s, mq_s, g_s = _in_proj(xs2d, w_all, cs_s, gq, gkv, bg, tm=tm_s)
    q_s = _q_prep(cqn, wq, wuk_t, cs_s, tm=tm_s, transposed=False)
    q_s = jnp.transpose(q_s.reshape(MLA_HEADS, bs, t_new, QK_DIM), (1, 0, 2, 3)).reshape(bs, MLA_HEADS * t_new, QK_DIM)
    o_lat = _decode_attn(page_table, q_s, kcat.reshape(bs, t_new, QK_DIM).astype(F32), cache_ckv,
                         jnp.swapaxes(cache_krope, 1, 2))
    o_lat = jnp.transpose(o_lat.reshape(bs, MLA_HEADS, t_new, KV_LORA), (1, 0, 2, 3)).reshape(MLA_HEADS, ms, KV_LORA)
    o_mla_s = _uv_proj(o_lat, wuv)

    xbc_s3 = xbc_s.reshape(bs, t_new, CONV_DIM).astype(F32)
    dt_s3 = jnp.pad(dt_s.reshape(bs, t_new, 128), ((0, 0), (0, CHUNK - t_new), (0, 0)))
    dtt_s = jnp.swapaxes(dt_s3[:, :, :SSM_HEADS], 1, 2)
    tail_s = jnp.pad(state_conv.astype(F32), ((0, 0), (8 - (CONV_W - 1), 0), (0, 0)))
    o_ssm_s, ssm_s, _ = _ssd(
        xbc_s3, z_s.reshape(bs, t_new, D_INNER).astype(F32), dt_s3, dtt_s, tail_s,
        state_ssm.astype(F32).reshape(bs, D_INNER, D_STATE), ssd_consts, n_chunks=1, t_valid=t_new, y_dtype=F32)
    conv_s = jnp.concatenate([state_conv.astype(F32), xbc_s3], axis=1)[:, -(CONV_W - 1):]

    def cache_rows(c):
        c = c.reshape(bs, N_MEM, MEM_HEADS, MEM_HALVES, 128)
        return jnp.transpose(c, (0, 1, 3, 2, 4)).reshape(bs, N_MEM * MEM_ROW_GROUP, 128)

    mq_rows = jnp.transpose(mq_s.reshape(bs, t_new, MEM_HEADS, MEM_HALVES, 128), (0, 3, 2, 1, 4))
    o_mem_s = _mem_attn_cache(mq_rows.reshape(bs, MEM_ROW_GROUP * t_new, 128), cache_rows(cache_mem_k),
                              cache_rows(cache_mem_v), nb=4, n_tok=t_new)
    o_mem_s = jnp.transpose(o_mem_s.reshape(bs, MEM_HALVES, MEM_HEADS, t_new, 128), (0, 3, 2, 1, 4))
    y_s = _merge_ffn(xs2d, o_mla_s, o_ssm_s.reshape(ms, D_INNER).astype(BF16), o_mem_s.reshape(ms, MEM_WIDTH), g_s,
                     merge_w, tm=tm_s)

    return (y_p.reshape(bp, seq, D_MODEL), y_s.reshape(bs, t_new, D_MODEL),
            ckv_p.reshape(bp, seq, KV_LORA), kr_p.reshape(bp, seq, ROPE_DIM),
            mem_k_p.reshape(bp, N_MEM, MEM_HEADS, MEM_HEAD_DIM), mem_v_p.reshape(bp, N_MEM, MEM_HEADS, MEM_HEAD_DIM),
            ssm_p.reshape(bp, SSM_HEADS, SSM_HEAD_DIM, D_STATE), conv_p,
            ckv_s.reshape(bs, t_new, KV_LORA), kr_s.reshape(bs, t_new, ROPE_DIM),
            ssm_s.reshape(bs, SSM_HEADS, SSM_HEAD_DIM, D_STATE), conv_s)
```

```python
import functools
import math

import jax
import jax.numpy as jnp
import numpy as np
from jax import lax
from jax.experimental import pallas as pl
from jax.experimental.pallas import tpu as pltpu

F32 = jnp.float32
BF16 = jnp.bfloat16

D_MODEL = 1024
MLA_HEADS = 8
Q_LORA = 384
KV_LORA = 256
NOPE_DIM = 128
ROPE_DIM = 64
V_DIM = 128
ROPE_THETA = 10000.0
QK_DIM = KV_LORA + ROPE_DIM
SSM_HEADS = 32
SSM_HEAD_DIM = 64
D_INNER = SSM_HEADS * SSM_HEAD_DIM
SSM_GROUPS = 4
GROUP_W = D_INNER // SSM_GROUPS
D_STATE = 128
CONV_W = 4
CONV_DIM = D_INNER + 2 * SSM_GROUPS * D_STATE
CHUNK = 128
N_MEM = 256
MEM_HEADS = 4
MEM_HEAD_DIM = 256
MEM_WIDTH = MEM_HEADS * MEM_HEAD_DIM
D_FF = 4 * D_MODEL
N_BRANCH = 3
DEPTH = 1
ALPHA = (2 * DEPTH) ** 0.25
LN_EPS = 1e-5
RMS_EPS = 1e-6
PAGE_SIZE = 128
SCORE_SCALE = (NOPE_DIM + ROPE_DIM) ** -0.5
NEG_BIG = -1e30
KV_TILE = 256

C_CQ = 0
C_SMALL = C_CQ + Q_LORA
C_Z = C_SMALL + 512
C_XBC = C_Z + D_INNER
C_MQ = C_XBC + CONV_DIM
C_G = C_MQ + MEM_WIDTH
C_END = C_G + N_BRANCH * D_MODEL


def _cp(sem, vmem_mb):
    return pltpu.CompilerParams(dimension_semantics=sem, vmem_limit_bytes=vmem_mb << 20)


def _resident(shape):
    nd = len(shape)
    return pl.BlockSpec(shape, lambda *_: (0,) * nd, pipeline_mode=pl.Buffered(1))


def _dot(a, b):
    return jnp.dot(a, b, preferred_element_type=F32)


def _dot_nt(a, b):
    return lax.dot_general(a, b, (((1,), (1,)), ((), ())), preferred_element_type=F32)


def _rms(v, g, eps):
    return v * lax.rsqrt(jnp.mean(v * v, axis=-1, keepdims=True) + eps) * g


def _layer_norm(v, g, b):
    mu = jnp.mean(v, axis=-1, keepdims=True)
    d = v - mu
    var = jnp.mean(d * d, axis=-1, keepdims=True)
    return d * lax.rsqrt(var + LN_EPS) * g + b


def _in_proj_kernel(x_ref, w_ref, cs_ref, gq_ref, gkv_ref, bg_ref,
                    cqn_ref, ckv_ref, kr_ref, kcat_ref, ckvt_ref, dt_ref, z_ref, xbc_ref, mq_ref, g_ref):
    xb = x_ref[...].astype(BF16)

    def mm(a, b):
        return _dot(xb, w_ref[:, a:b])

    cqn_ref[...] = _rms(mm(C_CQ, C_SMALL), gq_ref[...], RMS_EPS).astype(BF16)

    small = mm(C_SMALL, C_Z)
    ckv = _rms(small[:, :KV_LORA], gkv_ref[...], RMS_EPS)
    ckv_ref[...] = ckv
    kcat_ref[:, 0:KV_LORA] = ckv.astype(BF16)
    for t in range(ckvt_ref.shape[0]):
        ckvt_ref[t] = ckv[t * KV_TILE:(t + 1) * KV_TILE, :].T.astype(BF16)
    a = small[:, 256:384]
    b = pltpu.roll(a, ROPE_DIM // 2, 1)
    cs = cs_ref[...]
    ro = a * cs[:, :128] + b * cs[:, 128:]
    kr_ref[...] = ro[:, :ROPE_DIM]
    kcat_ref[:, KV_LORA:QK_DIM] = ro[:, :ROPE_DIM].astype(BF16)
    dt_ref[...] = small[:, 384:512]

    for c in range(0, D_INNER, 1024):
        z_ref[:, c:c + 1024] = mm(C_Z + c, C_Z + c + 1024).astype(BF16)
    for c in range(0, CONV_DIM, 1024):
        xbc_ref[:, c:c + 1024] = mm(C_XBC + c, C_XBC + c + 1024).astype(BF16)
    mq_ref[...] = mm(C_MQ, C_G).astype(BF16)
    for c in range(0, N_BRANCH * D_MODEL, 1024):
        gr = mm(C_G + c, C_G + c + 1024) + bg_ref[:, c:c + 1024]
        g_ref[:, c:c + 1024] = jax.nn.sigmoid(gr).astype(BF16)


def _in_proj(x2d, w_all, cs_tab, gq, gkv, bg, tm):
    m = x2d.shape[0]
    ncs = cs_tab.shape[0] // tm
    widths = [(Q_LORA, BF16), (KV_LORA, F32), (ROPE_DIM, F32), (QK_DIM, BF16), (128, F32),
              (D_INNER, BF16), (CONV_DIM, BF16), (MEM_WIDTH, BF16), (N_BRANCH * D_MODEL, BF16)]

    def row(n):
        return pl.BlockSpec((tm, n), lambda i: (i, 0))

    out_specs = [row(n) for n, _ in widths]
    out_shape = [jax.ShapeDtypeStruct((m, n), d) for n, d in widths]
    out_specs.insert(4, pl.BlockSpec((tm // KV_TILE, KV_LORA, KV_TILE), lambda i: (i, 0, 0)))
    out_shape.insert(4, jax.ShapeDtypeStruct((m // KV_TILE, KV_LORA, KV_TILE), BF16))
    return pl.pallas_call(
        _in_proj_kernel,
        grid=(m // tm,),
        in_specs=[row(D_MODEL), _resident(w_all.shape),
                  pl.BlockSpec((tm, 256), lambda i: (i % ncs, 0)),
                  _resident(gq.shape), _resident(gkv.shape), _resident(bg.shape)],
        out_specs=out_specs,
        out_shape=out_shape,
        compiler_params=_cp(("arbitrary",), 56),
        name="in_proj",
    )(x2d, w_all, cs_tab, gq, gkv, bg)


def _q_prep_kernel(c_ref, wq_ref, wuk_ref, cs_ref, q_ref, *, transposed):
    c = c_ref[...]
    cs = cs_ref[...]
    for h in range(MLA_HEADS):
        qn = _dot(c, wq_ref[:, h * NOPE_DIM:(h + 1) * NOPE_DIM]).astype(BF16)
        ql = _dot(qn, wuk_ref[h]) * SCORE_SCALE
        if transposed:
            q_ref[h, 0:KV_LORA, :] = ql.T.astype(BF16)
        else:
            q_ref[h, :, 0:KV_LORA] = ql.astype(BF16)
    r0 = MLA_HEADS * NOPE_DIM
    for p in range(MLA_HEADS // 2):
        ab = _dot(c, wq_ref[:, r0 + p * 256:r0 + (p + 1) * 256])
        ro = (ab[:, :128] * cs[:, :128] + ab[:, 128:] * cs[:, 128:]) * SCORE_SCALE
        if transposed:
            rot = ro.T.astype(BF16)
            q_ref[2 * p, KV_LORA:QK_DIM, :] = rot[:ROPE_DIM]
            q_ref[2 * p + 1, KV_LORA:QK_DIM, :] = rot[ROPE_DIM:]
        else:
            q_ref[2 * p, :, KV_LORA:QK_DIM] = ro[:, :ROPE_DIM].astype(BF16)
            q_ref[2 * p + 1, :, KV_LORA:QK_DIM] = pltpu.roll(ro, ROPE_DIM, 1)[:, :ROPE_DIM].astype(BF16)


def _q_prep(cqn, wq, wuk_t, cs_tab, tm, transposed):
    m = cqn.shape[0]
    ncs = cs_tab.shape[0] // tm
    if transposed:
        out_spec = pl.BlockSpec((MLA_HEADS, QK_DIM, tm), lambda i: (0, 0, i))
        out_shape = jax.ShapeDtypeStruct((MLA_HEADS, QK_DIM, m), BF16)
    else:
        out_spec = pl.BlockSpec((MLA_HEADS, tm, QK_DIM), lambda i: (0, i, 0))
        out_shape = jax.ShapeDtypeStruct((MLA_HEADS, m, QK_DIM), BF16)
    return pl.pallas_call(
        functools.partial(_q_prep_kernel, transposed=transposed),
        grid=(m // tm,),
        in_specs=[pl.BlockSpec((tm, Q_LORA), lambda i: (i, 0)), _resident(wq.shape), _resident(wuk_t.shape),
                  pl.BlockSpec((tm, 256), lambda i: (i % ncs, 0))],
        out_specs=out_spec,
        out_shape=out_shape,
        compiler_params=_cp(("arbitrary",), 32),
        name="q_prep",
    )(cqn, wq, wuk_t, cs_tab)


def _prompt_attn_kernel(qt_ref, k_ref, vt_ref, wuvt_ref, o_ref, m_sc, l_sc, acc_sc):
    i = pl.program_id(1)
    t = KV_TILE
    m_sc[...] = jnp.full(m_sc.shape, NEG_BIG, F32)
    l_sc[...] = jnp.zeros(l_sc.shape, F32)
    acc_sc[...] = jnp.zeros(acc_sc.shape, F32)

    def kv_block(j, masked):
        k = k_ref[pl.ds(pl.multiple_of(j * t, t), t), :]
        kc = k[:, :KV_LORA]
        kr = k[:, KV_LORA:]
        vt = vt_ref[j]
        if masked:
            keep = lax.broadcasted_iota(jnp.int32, (t, t), 0) <= lax.broadcasted_iota(jnp.int32, (t, t), 1)
        def scores(h):
            return _dot(kc, qt_ref[h, 0:KV_LORA, :]) + _dot(kr, qt_ref[h, KV_LORA:QK_DIM, :])

        ahead = 4
        pending = [scores(h) for h in range(ahead)]
        for h in range(MLA_HEADS):
            st = pending.pop(0)
            if h + ahead < MLA_HEADS:
                pending.append(scores(h + ahead))
            if masked:
                st = jnp.where(keep, st, NEG_BIG)
            m_prev = m_sc[h:h + 1, :]
            m_new = jnp.maximum(m_prev, jnp.max(st, axis=0, keepdims=True))
            alpha = jnp.exp(m_prev - m_new)
            p = jnp.exp(st - m_new)
            l_sc[h:h + 1, :] = alpha * l_sc[h:h + 1, :] + jnp.sum(p, axis=0, keepdims=True)
            acc_sc[h] = alpha * acc_sc[h] + _dot(vt, p.astype(BF16))
            m_sc[h:h + 1, :] = m_new

    def body(j, carry):
        kv_block(j, False)
        return carry

    lax.fori_loop(0, i, body, 0)
    kv_block(i, True)

    for h in range(MLA_HEADS):
        ot = (acc_sc[h] / l_sc[h:h + 1, :]).astype(BF16)
        o_ref[:, h * V_DIM:(h + 1) * V_DIM] = _dot(wuvt_ref[h], ot).T.astype(BF16)


def _prompt_attn(qt, kcat, ckvt, wuvt, batch, seq):
    t = KV_TILE
    nq = seq // t
    return pl.pallas_call(
        _prompt_attn_kernel,
        grid=(batch, nq),
        in_specs=[pl.BlockSpec((MLA_HEADS, QK_DIM, t), lambda b, i: (0, 0, b * nq + i)),
                  pl.BlockSpec((seq, QK_DIM), lambda b, i: (b, 0)),
                  pl.BlockSpec((nq, KV_LORA, t), lambda b, i: (b, 0, 0)),
                  _resident(wuvt.shape)],
        out_specs=pl.BlockSpec((t, MLA_HEADS * V_DIM), lambda b, i: (b * nq + i, 0)),
        out_shape=jax.ShapeDtypeStruct((batch * seq, MLA_HEADS * V_DIM), BF16),
        scratch_shapes=[pltpu.VMEM((MLA_HEADS, t), F32), pltpu.VMEM((MLA_HEADS, t), F32),
                        pltpu.VMEM((MLA_HEADS, KV_LORA, t), F32)],
        compiler_params=_cp(("arbitrary", "arbitrary"), 32),
        name="prompt_attn",
    )(qt, kcat, ckvt, wuvt)


def _decode_attn_kernel(pt_ref, q_ref, kn_ref, ckv_hbm, krt_hbm, o_ref, kbuf, rbuf, kb_sc, s_sc, sem, *, n_pages, n_new):
    b = pl.program_id(0)
    nb = pl.num_programs(0)
    slot = b % 2

    def page_copies(page, p, sl):
        rows = pl.ds(pl.multiple_of(p * PAGE_SIZE, PAGE_SIZE), PAGE_SIZE)
        return (pltpu.make_async_copy(ckv_hbm.at[page], kbuf.at[sl, rows, :], sem.at[0, sl]),
                pltpu.make_async_copy(krt_hbm.at[page], rbuf.at[sl, p], sem.at[1, sl]))

    def issue(bi, sl):
        def body(p, carry):
            for cp in page_copies(pt_ref[bi * n_pages + p], p, sl):
                cp.start()
            return carry
        lax.fori_loop(0, n_pages, body, 0, unroll=8)

    @pl.when(b == 0)
    def _():
        issue(0, 0)

    @pl.when(b + 1 < nb)
    def _():
        issue(b + 1, 1 - slot)

    def wait_body(p, carry):
        for cp in page_copies(0, p, slot):
            cp.wait()
        return carry
    lax.fori_loop(0, n_pages, wait_body, 0, unroll=8)

    q = q_ref[0]
    rows = q.shape[0]
    qc = q[:, :KV_LORA]
    qr = q[:, KV_LORA:]
    qf = q.astype(F32)
    kn = kn_ref[0]
    tok = lax.broadcasted_iota(jnp.int32, (rows, 1), 0) % n_new

    s_new = []
    for j in range(n_new):
        sj = jnp.sum(qf * kn[j:j + 1, :], axis=1, keepdims=True)
        s_new.append(jnp.where(tok >= j, sj, NEG_BIG))
    m0 = s_new[0]
    for j in range(1, n_new):
        m0 = jnp.maximum(m0, s_new[j])

    for c in range(n_pages // 2):
        r = slice(2 * c * PAGE_SIZE, 2 * (c + 1) * PAGE_SIZE)
        kc = kbuf[slot, r, :].astype(BF16)
        kb_sc[r, :] = kc
        rt = jnp.concatenate([rbuf[slot, 2 * c], rbuf[slot, 2 * c + 1]], axis=1).astype(BF16)
        s_sc[:, r] = _dot_nt(qc, kc) + _dot(qr, rt)
    s = s_sc[...]
    m = jnp.maximum(m0, jnp.max(s, axis=1, keepdims=True))
    p = jnp.exp(s - m)
    l = jnp.sum(p, axis=1, keepdims=True)
    acc = _dot(p.astype(BF16), kb_sc[...])
    for j in range(n_new):
        pj = jnp.exp(s_new[j] - m)
        l = l + pj
        acc = acc + pj * kn[j:j + 1, :KV_LORA]
    o_ref[0] = (acc / l).astype(BF16)


def _decode_attn(page_table, q, k_new, cache_ckv, cache_krope_t):
    nb, n_pages = page_table.shape
    rows = q.shape[1]
    n_new = k_new.shape[1]
    past = n_pages * PAGE_SIZE
    kern = functools.partial(_decode_attn_kernel, n_pages=n_pages, n_new=n_new)
    grid_spec = pltpu.PrefetchScalarGridSpec(
        num_scalar_prefetch=1,
        grid=(nb,),
        in_specs=[pl.BlockSpec((1, rows, QK_DIM), lambda b, pt: (b, 0, 0)),
                  pl.BlockSpec((1, n_new, QK_DIM), lambda b, pt: (b, 0, 0)),
                  pl.BlockSpec(memory_space=pl.ANY),
                  pl.BlockSpec(memory_space=pl.ANY)],
        out_specs=pl.BlockSpec((1, rows, KV_LORA), lambda b, pt: (b, 0, 0)),
        scratch_shapes=[pltpu.VMEM((2, past, KV_LORA), F32), pltpu.VMEM((2, n_pages, ROPE_DIM, PAGE_SIZE), F32),
                        pltpu.VMEM((past, KV_LORA), BF16), pltpu.VMEM((rows, past), F32),
                        pltpu.SemaphoreType.DMA((2, 2))],
    )
    return pl.pallas_call(
        kern,
        grid_spec=grid_spec,
        out_shape=jax.ShapeDtypeStruct((nb, rows, KV_LORA), BF16),
        compiler_params=_cp(("arbitrary",), 40),
        name="decode_attn",
    )(page_table.reshape(-1), q, k_new, cache_ckv, cache_krope_t)


def _uv_proj_kernel(o_ref, w_ref, out_ref):
    out_ref[...] = _dot(o_ref[0], w_ref[0]).astype(BF16)


def _uv_proj(o_lat, wuv):
    m = o_lat.shape[1]
    return pl.pallas_call(
        _uv_proj_kernel,
        grid=(MLA_HEADS,),
        in_specs=[pl.BlockSpec((1, m, KV_LORA), lambda h: (h, 0, 0)),
                  pl.BlockSpec((1, KV_LORA, V_DIM), lambda h: (h, 0, 0))],
        out_specs=pl.BlockSpec((m, V_DIM), lambda h: (0, h)),
        out_shape=jax.ShapeDtypeStruct((m, MLA_HEADS * V_DIM), BF16),
        compiler_params=_cp(("arbitrary",), 16),
        name="uv_proj",
    )(o_lat, wuv)


def _split_dot(v, e, terms, left=True):
    parts = []
    r = v
    for _ in range(terms):
        hi = r.astype(BF16)
        parts.append(hi)
        r = r - hi.astype(F32)
    if left:
        return _dot(jnp.concatenate(parts, axis=1), e)
    return _dot(e, jnp.concatenate(parts, axis=0))


def _conv_silu(taps, cw_ref, cb_ref):
    conv = cb_ref[...] + cw_ref[0:1, :] * taps[0]
    for k in range(1, CONV_W):
        conv = conv + cw_ref[k:k + 1, :] * taps[k]
    xc = conv * jax.nn.sigmoid(conv)
    n_bc = SSM_GROUPS * D_STATE
    return xc[:, :D_INNER], xc[:, D_INNER:D_INNER + n_bc], xc[:, D_INNER + n_bc:]


def _ssd_sums(dt, dtt, keep, keep_t, alog_ref, alogt_ref):
    tri = jnp.where(keep, 1.0, 0.0).astype(BF16)
    tri_t = jnp.where(keep_t, 1.0, 0.0).astype(BF16)
    da = dt * (-jnp.exp(alog_ref[...]))
    dat = dtt * (-jnp.exp(alogt_ref[...]))
    cum = _split_dot(da, jnp.concatenate([tri] * 3, axis=1), 3, left=False)
    cum_t = _split_dot(dat, jnp.concatenate([tri_t] * 3, axis=0), 3)
    return da, cum, cum_t


def _ssd_intra(xs, bm, cm, cum_e, cum_t, dtt, keep, dskip_ref, y_sc, carried=None):
    pairs_per_group = SSM_HEADS // 2 // SSM_GROUPS
    lane = lax.broadcasted_iota(jnp.int32, (CHUNK, CHUNK), 1)
    for g in range(SSM_GROUPS):
        cg = cm[:, g * D_STATE:(g + 1) * D_STATE].astype(BF16)
        cb = _dot_nt(cg, bm[:, g * D_STATE:(g + 1) * D_STATE].astype(BF16))
        extra = None if carried is None else carried(g, cg)
        for pr in range(pairs_per_group):
            q = g * pairs_per_group + pr
            ws = []
            for h in (2 * q, 2 * q + 1):
                seg = cum_e[:, h * 128:(h + 1) * 128] - cum_t[h:h + 1, :]
                decay = jnp.exp(jnp.where(keep, seg, NEG_BIG)) * dtt[h:h + 1, :]
                ws.append((cb * decay).astype(BF16))
            w_pair = jnp.concatenate(ws, axis=1)
            xp = xs[:, q * 128:(q + 1) * 128]
            x_bd = jnp.concatenate([jnp.where(lane < SSM_HEAD_DIM, xp, 0.0).astype(BF16),
                                    jnp.where(lane >= SSM_HEAD_DIM, xp, 0.0).astype(BF16)], axis=0)
            y = _dot(w_pair, x_bd) + dskip_ref[:, q * 128:(q + 1) * 128] * xp
            if extra is not None:
                y = y + extra[:, pr * 128:(pr + 1) * 128]
            y_sc[:, q * 128:(q + 1) * 128] = y


def _ssd_kernel(xbc_ref, z_ref, dt_ref, dtt_ref, cw_ref, cb_ref, dtb_ref, dtbt_ref,
                alog_ref, alogt_ref, dskip_ref, ng_ref, e64_ref, e128_ref,
                y_ref, sout_ref, cout_ref, ext_sc, st_sc, y_sc):
    c = pl.program_id(1)
    n_pairs = SSM_HEADS // 2
    pairs_per_group = n_pairs // SSM_GROUPS

    @pl.when(c == 0)
    def _():
        ext_sc[0:8, :] = jnp.zeros((8, CONV_DIM), F32)
        st_sc[...] = jnp.zeros(st_sc.shape, F32)

    ext_sc[8:8 + CHUNK, :] = xbc_ref[0].astype(F32)
    first = 8 - (CONV_W - 1)
    xs, bm, cm = _conv_silu([ext_sc[first + k:first + k + CHUNK, :] for k in range(CONV_W)], cw_ref, cb_ref)
    tail = ext_sc[8 + CHUNK - (CONV_W - 1):8 + CHUNK, :]
    cout_ref[0] = tail
    ext_sc[first:8, :] = tail

    rowi = lax.broadcasted_iota(jnp.int32, (CHUNK, CHUNK), 0)
    coli = lax.broadcasted_iota(jnp.int32, (CHUNK, CHUNK), 1)
    keep = coli <= rowi
    dt = jax.nn.softplus(dt_ref[0] + dtb_ref[...])
    dtt = jax.nn.softplus(dtt_ref[0] + dtbt_ref[...])
    _, cum, cum_t = _ssd_sums(dt, dtt, keep, rowi <= coli, alog_ref, alogt_ref)
    last = cum[CHUNK - 1:CHUNK, :]
    e64 = e64_ref[...]
    ec_e = _split_dot(jnp.exp(cum), e64, 2)
    wb_e = _split_dot(dt * jnp.exp(last - cum), e64, 2)
    dec_e = _split_dot(jnp.broadcast_to(jnp.exp(last), (16, CHUNK)), e64, 2)[0:1, :]
    cum_e = _split_dot(cum, e128_ref[...], 3)

    def carried(g, cg):
        return _dot(cg, st_sc[g].astype(BF16)) * ec_e[:, g * GROUP_W:(g + 1) * GROUP_W]

    _ssd_intra(xs, bm, cm, cum_e, cum_t, dtt, keep, dskip_ref, y_sc, carried)
    for g in range(SSM_GROUPS):
        gs = slice(g * GROUP_W, (g + 1) * GROUP_W)
        xw = (xs[:, gs] * wb_e[:, gs]).astype(BF16)
        bt = bm[:, g * D_STATE:(g + 1) * D_STATE].T.astype(BF16)
        st_sc[g] = st_sc[g] * dec_e[:, gs] + _dot(bt, xw)

    for g in range(SSM_GROUPS):
        gs = slice(g * GROUP_W, (g + 1) * GROUP_W)
        zf = z_ref[0, :, gs].astype(F32)
        yv = y_sc[:, gs] * (zf * jax.nn.sigmoid(zf))
        y_ref[0, :, gs] = _rms(yv, ng_ref[:, gs], RMS_EPS).astype(y_ref.dtype)

    for q in range(n_pairs):
        g, pr = divmod(q, pairs_per_group)
        sout_ref[0, q * 128:(q + 1) * 128, :] = st_sc[g, :, pr * 128:(pr + 1) * 128].T


def _ssd(xbc, z, dt, dtt, consts, batch, n_chunks):
    def step(r, n):
        return pl.BlockSpec((1, r, n), lambda b, c: (b * n_chunks + c, 0, 0))

    def per_batch(r, n):
        return pl.BlockSpec((1, r, n), lambda b, c: (b, 0, 0))

    return pl.pallas_call(
        _ssd_kernel,
        grid=(batch, n_chunks),
        in_specs=[step(CHUNK, CONV_DIM), step(CHUNK, D_INNER), step(CHUNK, 128), step(SSM_HEADS, CHUNK)]
                 + [_resident(a.shape) for a in consts],
        out_specs=[step(CHUNK, D_INNER), per_batch(D_INNER, D_STATE), per_batch(CONV_W - 1, CONV_DIM)],
        out_shape=[jax.ShapeDtypeStruct((batch * n_chunks, CHUNK, D_INNER), BF16),
                   jax.ShapeDtypeStruct((batch, D_INNER, D_STATE), F32),
                   jax.ShapeDtypeStruct((batch, CONV_W - 1, CONV_DIM), F32)],
        scratch_shapes=[pltpu.VMEM((8 + CHUNK, CONV_DIM), F32),
                        pltpu.VMEM((SSM_GROUPS, D_STATE, GROUP_W), F32),
                        pltpu.VMEM((CHUNK, D_INNER), F32)],
        compiler_params=_cp(("arbitrary", "arbitrary"), 48),
        name="ssd",
    )(xbc, z, dt, dtt, *consts)


SLOTS = 16


def _ssd_short_kernel(taps_ref, dt_ref, dtt_ref, cw_ref, cb_ref, dtb_ref, dtbt_ref, alog_ref, alogt_ref, dskip_ref,
                      e64_ref, e128_ref, place_ref, ypre_ref, ec_ref, xw_ref, c_ref, bw_ref, y_sc, *, seq_len):
    xs, bm, cm = _conv_silu([taps_ref[k] for k in range(CONV_W)], cw_ref, cb_ref)
    rowi = lax.broadcasted_iota(jnp.int32, (CHUNK, CHUNK), 0)
    coli = lax.broadcasted_iota(jnp.int32, (CHUNK, CHUNK), 1)
    same = (rowi // seq_len) == (coli // seq_len)
    keep = same & (coli <= rowi)
    dt = jax.nn.softplus(dt_ref[...] + dtb_ref[...])
    dtt = jax.nn.softplus(dtt_ref[...] + dtbt_ref[...])
    da, cum, cum_t = _ssd_sums(dt, dtt, keep, same & (rowi <= coli), alog_ref, alogt_ref)
    whole = jnp.where(same, 1.0, 0.0).astype(BF16)
    last = _split_dot(da, jnp.concatenate([whole] * 3, axis=1), 3, left=False)
    e64 = e64_ref[...]
    ec_ref[...] = _split_dot(jnp.exp(cum), e64, 2)
    wb_e = _split_dot(dt * jnp.exp(last - cum), e64, 2)
    dec_e = _split_dot(jnp.exp(last), e64, 2)
    cum_e = _split_dot(cum, e128_ref[...], 3)
    _ssd_intra(xs, bm, cm, cum_e, cum_t, dtt, keep, dskip_ref, y_sc)
    ypre_ref[...] = y_sc[...]

    place = place_ref[...]
    dec_hi = dec_e.astype(BF16)
    dec_lo = (dec_e - dec_hi.astype(F32)).astype(BF16)
    xw = (xs * wb_e).astype(BF16)
    xw_ref[...] = _dot(place, jnp.concatenate([xw, dec_hi, dec_lo], axis=0)).astype(BF16)
    tokens = place[:, :CHUNK]
    c_ref[...] = _dot(tokens, cm.astype(BF16)).astype(BF16)
    zeros = jnp.zeros((CHUNK, D_STATE), BF16)
    b_wide = jnp.concatenate(
        [piece for g in range(SSM_GROUPS) for piece in (bm[:, g * D_STATE:(g + 1) * D_STATE].astype(BF16), zeros)], axis=1)
    n_rows = place.shape[0]
    slot = lax.broadcasted_iota(jnp.int32, (n_rows, 2 * D_STATE * SSM_GROUPS), 0) % SLOTS
    col = lax.broadcasted_iota(jnp.int32, (n_rows, 2 * D_STATE * SSM_GROUPS), 1) % (2 * D_STATE)
    ones = ((slot == seq_len) | (slot == seq_len + 1)) & (col >= D_STATE)
    bw_ref[...] = jnp.where(ones, 1.0, _dot(tokens, b_wide)).astype(BF16)


def _ssd_short(taps, dt, dtt, consts, place, seq_len):
    rows = dt.shape[0]
    n_chunks = rows // CHUNK
    n_slot_rows = place.shape[0]
    kern = functools.partial(_ssd_short_kernel, seq_len=seq_len)

    def out(r, n, dtype):
        return pl.BlockSpec((r, n), lambda c: (c, 0)), jax.ShapeDtypeStruct((n_chunks * r, n), dtype)

    outs = [out(CHUNK, D_INNER, F32), out(CHUNK, D_INNER, F32), out(n_slot_rows, D_INNER, BF16),
            out(n_slot_rows, SSM_GROUPS * D_STATE, BF16), out(n_slot_rows, 2 * SSM_GROUPS * D_STATE, BF16)]
    return pl.pallas_call(
        kern,
        grid=(n_chunks,),
        in_specs=[pl.BlockSpec((CONV_W, CHUNK, CONV_DIM), lambda c: (0, c, 0)),
                  pl.BlockSpec((CHUNK, 128), lambda c: (c, 0)),
                  pl.BlockSpec((SSM_HEADS, CHUNK), lambda c: (0, c))]
                 + [_resident(a.shape) for a in consts] + [_resident(place.shape)],
        out_specs=[o[0] for o in outs],
        out_shape=[o[1] for o in outs],
        scratch_shapes=[pltpu.VMEM((CHUNK, D_INNER), F32)],
        compiler_params=_cp(("arbitrary",), 48),
        name="ssd_short",
    )(taps, dt, dtt, *consts, place)


def _ssd_state_kernel(s0_ref, xw_ref, c_ref, bw_ref, ypre_ref, ec_ref, z_ref, ng_ref, y_ref, sout_ref, ci_sc,
                      *, nb, seq_len):
    for bb in range(nb):
        slots = slice(bb * SLOTS, (bb + 1) * SLOTS)
        toks = slice(bb * seq_len, (bb + 1) * seq_len)
        for g in range(SSM_GROUPS):
            gs = slice(g * GROUP_W, (g + 1) * GROUP_W)
            s0g = s0_ref[bb, gs, :]
            ci_sc[...] = _dot_nt(c_ref[slots, g * D_STATE:(g + 1) * D_STATE], s0g.astype(BF16))
            u = lax.dot_general(xw_ref[slots, gs], bw_ref[slots, 2 * g * D_STATE:2 * (g + 1) * D_STATE],
                                (((0,), (0,)), ((), ())), preferred_element_type=F32)
            sout_ref[bb, gs, :] = u[:, :D_STATE] + s0g * u[:, D_STATE:]
            zf = z_ref[toks, gs]
            yv = (ypre_ref[toks, gs] + ec_ref[toks, gs] * ci_sc[0:seq_len, :]) * (zf * jax.nn.sigmoid(zf))
            y_ref[toks, gs] = _rms(yv, ng_ref[:, gs], RMS_EPS)


def _ssd_state(s0, xw, cmat, bw, ypre, ec, z, ng, nb, seq_len):
    batch = s0.shape[0]
    kern = functools.partial(_ssd_state_kernel, nb=nb, seq_len=seq_len)

    def rows(r, n):
        return pl.BlockSpec((nb * r, n), lambda b: (b, 0))

    state_spec = pl.BlockSpec((nb, D_INNER, D_STATE), lambda b: (b, 0, 0))
    return pl.pallas_call(
        kern,
        grid=(batch // nb,),
        in_specs=[state_spec, rows(SLOTS, D_INNER), rows(SLOTS, SSM_GROUPS * D_STATE),
                  rows(SLOTS, 2 * SSM_GROUPS * D_STATE), rows(seq_len, D_INNER), rows(seq_len, D_INNER),
                  rows(seq_len, D_INNER), _resident(ng.shape)],
        out_specs=[rows(seq_len, D_INNER), state_spec],
        out_shape=[jax.ShapeDtypeStruct((batch * seq_len, D_INNER), F32),
                   jax.ShapeDtypeStruct((batch, D_INNER, D_STATE), F32)],
        scratch_shapes=[pltpu.VMEM((SLOTS, GROUP_W), F32)],
        compiler_params=_cp(("arbitrary",), 40),
        name="ssd_state",
    )(s0, xw, cmat, bw, ypre, ec, z, ng)


def _mem_kv_kernel(m_ref, w_ref, k_ref, v_ref):
    mb = m_ref[...].astype(BF16)
    k_ref[...] = _dot(mb, w_ref[:, :MEM_WIDTH])
    v_ref[...] = _dot(mb, w_ref[:, MEM_WIDTH:])


def _mem_kv(mem2d, w_kv, tm):
    m = mem2d.shape[0]
    return pl.pallas_call(
        _mem_kv_kernel,
        grid=(m // tm,),
        in_specs=[pl.BlockSpec((tm, D_MODEL), lambda i: (i, 0)), _resident(w_kv.shape)],
        out_specs=[pl.BlockSpec((tm, MEM_WIDTH), lambda i: (i, 0))] * 2,
        out_shape=[jax.ShapeDtypeStruct((m, MEM_WIDTH), F32)] * 2,
        compiler_params=_cp(("arbitrary",), 32),
        name="mem_kv",
    )(mem2d, w_kv)


def _mem_attn_kernel(q_ref, k_ref, v_ref, o_ref, *, nb):
    scale = MEM_HEAD_DIM ** -0.5
    for bb in range(nb):
        for h in range(MEM_HEADS):
            hs = slice(h * MEM_HEAD_DIM, (h + 1) * MEM_HEAD_DIM)
            kh = k_ref[bb, :, hs].astype(BF16)
            vh = v_ref[bb, :, hs].astype(BF16)
            s = _dot_nt(q_ref[bb, :, hs], kh) * scale
            p = jnp.exp(s - jnp.max(s, axis=1, keepdims=True))
            l = jnp.sum(p, axis=1, keepdims=True)
            o_ref[bb, :, hs] = (_dot(p.astype(BF16), vh) / l).astype(BF16)


def _mem_attn(mq, mem_k, mem_v, nb, tq):
    batch, rows, _ = mq.shape
    kern = functools.partial(_mem_attn_kernel, nb=nb)
    kv_spec = pl.BlockSpec((nb, N_MEM, MEM_WIDTH), lambda b, i: (b, 0, 0))
    return pl.pallas_call(
        kern,
        grid=(batch // nb, rows // tq),
        in_specs=[pl.BlockSpec((nb, tq, MEM_WIDTH), lambda b, i: (b, i, 0)), kv_spec, kv_spec],
        out_specs=pl.BlockSpec((nb, tq, MEM_WIDTH), lambda b, i: (b, i, 0)),
        out_shape=jax.ShapeDtypeStruct((batch, rows, MEM_WIDTH), BF16),
        compiler_params=_cp(("arbitrary", "arbitrary"), 40),
        name="mem_attn",
    )(mq, mem_k, mem_v)


MEM_HALVES = MEM_HEAD_DIM // 128
MEM_ROW_GROUP = MEM_HALVES * MEM_HEADS


def _mem_attn_cache_kernel(q_ref, k_ref, v_ref, o_ref, *, nb, n_tok):
    rows = MEM_HEADS * n_tok
    n_col = N_MEM * MEM_ROW_GROUP
    col = lax.broadcasted_iota(jnp.int32, (rows, n_col), 1)
    row = lax.broadcasted_iota(jnp.int32, (rows, n_col), 0)
    own = (col % MEM_ROW_GROUP) == (row // n_tok)
    scale = MEM_HEAD_DIM ** -0.5
    for bb in range(nb):
        g = _dot_nt(q_ref[bb], k_ref[bb].astype(BF16))
        s = (g[:rows] + pltpu.roll(g[rows:], n_col - MEM_HEADS, 1)) * scale
        s = jnp.where(own, s, NEG_BIG)
        p = jnp.exp(s - jnp.max(s, axis=1, keepdims=True))
        l = jnp.sum(p, axis=1, keepdims=True)
        p2 = jnp.concatenate([p, pltpu.roll(p, MEM_HEADS, 1)], axis=0).astype(BF16)
        o = _dot(p2, v_ref[bb].astype(BF16))
        o_ref[bb] = (o / jnp.concatenate([l, l], axis=0)).astype(BF16)


def _mem_attn_cache(q, mem_k, mem_v, nb, n_tok):
    batch, rows, _ = q.shape
    kern = functools.partial(_mem_attn_cache_kernel, nb=nb, n_tok=n_tok)
    kv_spec = pl.BlockSpec((nb,) + mem_k.shape[1:], lambda b: (b, 0, 0))
    q_spec = pl.BlockSpec((nb, rows, 128), lambda b: (b, 0, 0))
    return pl.pallas_call(
        kern,
        grid=(batch // nb,),
        in_specs=[q_spec, kv_spec, kv_spec],
        out_specs=q_spec,
        out_shape=jax.ShapeDtypeStruct((batch, rows, 128), BF16),
        compiler_params=_cp(("arbitrary",), 40),
        name="mem_attn_cache",
    )(q, mem_k, mem_v)


def _merge_ffn_kernel(x_ref, oa_ref, ob_ref, om_ref, g_ref, woa_ref, wob_ref, wom_ref, wout_ref,
                      ln1g_ref, ln1b_ref, wup_ref, bup_ref, wdown_ref, bdown_ref, ln2g_ref, ln2b_ref, y_ref):
    x = x_ref[...]
    m = g_ref[:, 0:D_MODEL].astype(F32) * _dot(oa_ref[...], woa_ref[...])
    m = m + g_ref[:, D_MODEL:2 * D_MODEL].astype(F32) * _dot(ob_ref[...], wob_ref[...])
    m = m + g_ref[:, 2 * D_MODEL:].astype(F32) * _dot(om_ref[...], wom_ref[...])
    x1 = _layer_norm(ALPHA * x + _dot(m.astype(BF16), wout_ref[...]), ln1g_ref[...], ln1b_ref[...])
    h = jnp.maximum(_dot(x1.astype(BF16), wup_ref[...]) + bup_ref[...], 0.0)
    h = (h * h).astype(BF16)
    y = ALPHA * x1 + _dot(h, wdown_ref[...]) + bdown_ref[...]
    y_ref[...] = _layer_norm(y, ln2g_ref[...], ln2b_ref[...])


def _merge_ffn(x2d, o_mla, o_ssm, o_mem, g, weights, tm):
    m = x2d.shape[0]

    def row(n):
        return pl.BlockSpec((tm, n), lambda i: (i, 0))

    return pl.pallas_call(
        _merge_ffn_kernel,
        grid=(m // tm,),
        in_specs=[row(D_MODEL), row(MLA_HEADS * V_DIM), row(D_INNER), row(MEM_WIDTH), row(N_BRANCH * D_MODEL)]
                 + [_resident(w.shape) for w in weights],
        out_specs=row(D_MODEL),
        out_shape=jax.ShapeDtypeStruct((m, D_MODEL), F32),
        compiler_params=_cp(("arbitrary",), 56),
        name="merge_ffn",
    )(x2d, o_mla, o_ssm, o_mem, g, *weights)


def _rope_table(pos):
    half = ROPE_DIM // 2
    inv = ROPE_THETA ** (-jnp.arange(half, dtype=F32) / half)
    ang = pos.astype(F32)[:, None] * inv[None, :]
    cos, sin = jnp.cos(ang), jnp.sin(ang)
    return jnp.concatenate([cos, cos, cos, cos, -sin, sin, -sin, sin], axis=1)


def _expand_matrix(width, terms):
    src = np.arange(SSM_HEADS * width) // width
    one = (np.arange(128)[:, None] == src[None, :]).astype(np.float32)
    return jnp.asarray(np.concatenate([one] * terms, axis=0), dtype=BF16)


def _placement_matrix(seq_len):
    n_seq = CHUNK // seq_len
    place = np.zeros((n_seq * SLOTS, 3 * CHUNK), np.float32)
    for s in range(n_seq):
        for j in range(seq_len):
            place[s * SLOTS + j, s * seq_len + j] = 1.0
        place[s * SLOTS + seq_len, CHUNK + s * seq_len] = 1.0
        place[s * SLOTS + seq_len + 1, 2 * CHUNK + s * seq_len] = 1.0
    return jnp.asarray(place, dtype=BF16)


def _row(v):
    return v.reshape(1, -1).astype(F32)


def kernel(x_prompt, x_sample, mem_prompt, cache_ckv, cache_krope, page_table, cache_mem_k, cache_mem_v, state_ssm, state_conv, w_in, q_norm_g, w_uq, kv_norm_g, w_uk, w_uv, conv_w, conv_b, dt_bias, a_log, d_skip, ssm_norm_g, w_mem_k, w_mem_v, b_gate, w_o_mla, w_o_ssm, w_o_mem, w_out, ln1_g, ln1_b, w_up, b_up, w_down, b_down, ln2_g, ln2_b):
    bp, seq, _ = x_prompt.shape
    bs, t_new, _ = x_sample.shape
    n_pages = page_table.shape[1]
    past = n_pages * PAGE_SIZE

    o_cq, o_ckv, o_kr, o_z, o_xbc, o_dt, o_mq, o_g = np.cumsum(
        [0, Q_LORA, KV_LORA, ROPE_DIM, D_INNER, CONV_DIM, SSM_HEADS, MEM_WIDTH]).tolist()
    w_kr = w_in[:, o_kr:o_z]
    w_all = jnp.concatenate([
        w_in[:, o_cq:o_ckv], w_in[:, o_ckv:o_kr], w_kr, w_kr,
        w_in[:, o_dt:o_mq], jnp.zeros((D_MODEL, 128 - SSM_HEADS), F32),
        w_in[:, o_z:o_xbc], w_in[:, o_xbc:o_dt], w_in[:, o_mq:o_g], w_in[:, o_g:]], axis=1).astype(BF16)
    half = ROPE_DIM // 2
    w_q_nope = w_uq[:, :, :NOPE_DIM].reshape(Q_LORA, MLA_HEADS * NOPE_DIM)
    w_q_rope = w_uq[:, :, NOPE_DIM:]
    w_q_swap = jnp.concatenate([w_q_rope[:, :, half:], w_q_rope[:, :, :half]], axis=-1)
    pair_w = 2 * ROPE_DIM
    w_q_pairs = jnp.concatenate([w_q_rope.reshape(Q_LORA, MLA_HEADS // 2, pair_w),
                                 w_q_swap.reshape(Q_LORA, MLA_HEADS // 2, pair_w)], axis=-1)
    wq = jnp.concatenate([w_q_nope, w_q_pairs.reshape(Q_LORA, -1)], axis=1).astype(BF16)
    wuk_t = jnp.transpose(w_uk, (1, 2, 0)).astype(BF16)
    wuv = jnp.transpose(w_uv, (1, 0, 2)).astype(BF16)
    wuv_t = jnp.transpose(w_uv, (1, 2, 0)).astype(BF16)
    w_mem_kv = jnp.concatenate([w_mem_k.reshape(D_MODEL, MEM_WIDTH), w_mem_v.reshape(D_MODEL, MEM_WIDTH)], axis=1).astype(BF16)
    merge_w = (w_o_mla.astype(BF16), w_o_ssm.astype(BF16), w_o_mem.astype(BF16), w_out.astype(BF16),
               _row(ln1_g), _row(ln1_b), w_up.astype(BF16), _row(b_up), w_down.astype(BF16), _row(b_down),
               _row(ln2_g), _row(ln2_b))
    gq, gkv, bg = _row(q_norm_g), _row(kv_norm_g), _row(b_gate)

    pad_heads = jnp.zeros((128 - SSM_HEADS,), F32)
    ssd_head = (conv_w.astype(F32), _row(conv_b),
                _row(jnp.concatenate([dt_bias, pad_heads])),
                jnp.broadcast_to(dt_bias.astype(F32)[:, None], (SSM_HEADS, CHUNK)),
                _row(jnp.concatenate([a_log, pad_heads])),
                jnp.broadcast_to(a_log.astype(F32)[:, None], (SSM_HEADS, CHUNK)),
                _row(jnp.repeat(d_skip, SSM_HEAD_DIM)))
    norm_g = _row(ssm_norm_g)
    expanders = (_expand_matrix(SSM_HEAD_DIM, 2), _expand_matrix(128, 3))

    mp = bp * seq
    n_chunks = seq // CHUNK
    cs_p = _rope_table(jnp.arange(seq))
    xp2d = x_prompt.reshape(mp, D_MODEL)
    cqn, ckv_p, kr_p, kcat, ckvt, dt_p, z_p, xbc_p, mq_p, g_p = _in_proj(xp2d, w_all, cs_p, gq, gkv, bg, tm=256)
    qt_p = _q_prep(cqn, wq, wuk_t, cs_p, tm=256, transposed=True)
    o_mla_p = _prompt_attn(qt_p, kcat, ckvt, wuv_t, bp, seq)

    dt_p3 = dt_p.reshape(bp * n_chunks, CHUNK, 128)
    dtt_p = jnp.swapaxes(dt_p3[:, :, :SSM_HEADS], 1, 2)
    o_ssm_p, ssm_p, conv_p = _ssd(
        xbc_p.reshape(bp * n_chunks, CHUNK, CONV_DIM), z_p.reshape(bp * n_chunks, CHUNK, D_INNER), dt_p3, dtt_p,
        ssd_head + (norm_g,) + expanders, batch=bp, n_chunks=n_chunks)

    mem_k_p, mem_v_p = _mem_kv(mem_prompt.reshape(bp * N_MEM, D_MODEL), w_mem_kv, tm=min(512, bp * N_MEM))
    o_mem_p = _mem_attn(mq_p.reshape(bp, seq, MEM_WIDTH), mem_k_p.reshape(bp, N_MEM, MEM_WIDTH),
                        mem_v_p.reshape(bp, N_MEM, MEM_WIDTH), nb=1, tq=512)
    y_p = _merge_ffn(xp2d, o_mla_p, o_ssm_p.reshape(mp, D_INNER), o_mem_p.reshape(mp, MEM_WIDTH), g_p, merge_w, tm=256)

    ms = bs * t_new
    cs_s = jnp.tile(_rope_table(past + jnp.arange(t_new)), (bs, 1))
    xs2d = x_sample.reshape(ms, D_MODEL)
    tm_s = min(256, ms)
    cqn, ckv_s, kr_s, kcat, _, dt_s, z_s, xbc_s, mq_s, g_s = _in_proj(xs2d, w_all, cs_s, gq, gkv, bg, tm=tm_s)
    q_s = _q_prep(cqn, wq, wuk_t, cs_s, tm=tm_s, transposed=False)
    q_s = jnp.transpose(q_s.reshape(MLA_HEADS, bs, t_new, QK_DIM), (1, 0, 2, 3)).reshape(bs, MLA_HEADS * t_new, QK_DIM)
    o_lat = _decode_attn(page_table, q_s, kcat.reshape(bs, t_new, QK_DIM).astype(F32), cache_ckv,
                         jnp.swapaxes(cache_krope, 1, 2))
    o_lat = jnp.transpose(o_lat.reshape(bs, MLA_HEADS, t_new, KV_LORA), (1, 0, 2, 3)).reshape(MLA_HEADS, ms, KV_LORA)
    o_mla_s = _uv_proj(o_lat, wuv)

    ext_s = jnp.concatenate([state_conv.astype(F32), xbc_s.reshape(bs, t_new, CONV_DIM).astype(F32)], axis=1)
    taps_s = jnp.stack([ext_s[:, k:k + t_new] for k in range(CONV_W)]).reshape(CONV_W, ms, CONV_DIM)
    conv_s = ext_s[:, -(CONV_W - 1):]
    ypre_s, ec_s, xw_s, c_s, bw_s = _ssd_short(taps_s, dt_s, dt_s[:, :SSM_HEADS].T, ssd_head + expanders,
                                              _placement_matrix(t_new), seq_len=t_new)
    o_ssm_s, ssm_s = _ssd_state(state_ssm.astype(F32).reshape(bs, D_INNER, D_STATE), xw_s, c_s, bw_s, ypre_s, ec_s,
                                z_s.astype(F32), norm_g, nb=4, seq_len=t_new)

    def cache_rows(c):
        c = c.reshape(bs, N_MEM, MEM_HEADS, MEM_HALVES, 128)
        return jnp.transpose(c, (0, 1, 3, 2, 4)).reshape(bs, N_MEM * MEM_ROW_GROUP, 128)

    mq_rows = jnp.transpose(mq_s.reshape(bs, t_new, MEM_HEADS, MEM_HALVES, 128), (0, 3, 2, 1, 4))
    o_mem_s = _mem_attn_cache(mq_rows.reshape(bs, MEM_ROW_GROUP * t_new, 128), cache_rows(cache_mem_k),
                              cache_rows(cache_mem_v), nb=4, n_tok=t_new)
    o_mem_s = jnp.transpose(o_mem_s.reshape(bs, MEM_HALVES, MEM_HEADS, t_new, 128), (0, 3, 2, 1, 4))
    y_s = _merge_ffn(xs2d, o_mla_s, o_ssm_s.reshape(ms, D_INNER).astype(BF16), o_mem_s.reshape(ms, MEM_WIDTH), g_s,
                     merge_w, tm=tm_s)

    return (y_p.reshape(bp, seq, D_MODEL), y_s.reshape(bs, t_new, D_MODEL),
            ckv_p.reshape(bp, seq, KV_LORA), kr_p.reshape(bp, seq, ROPE_DIM),
            mem_k_p.reshape(bp, N_MEM, MEM_HEADS, MEM_HEAD_DIM), mem_v_p.reshape(bp, N_MEM, MEM_HEADS, MEM_HEAD_DIM),
            ssm_p.reshape(bp, SSM_HEADS, SSM_HEAD_DIM, D_STATE), conv_p,
            ckv_s.reshape(bs, t_new, KV_LORA), kr_s.reshape(bs, t_new, ROPE_DIM),
            ssm_s.reshape(bs, SSM_HEADS, SSM_HEAD_DIM, D_STATE), conv_s)
```

```python
import functools
import math

import jax
import jax.numpy as jnp
import numpy as np
from jax import lax
from jax.experimental import pallas as pl
from jax.experimental.pallas import tpu as pltpu

F32 = jnp.float32
BF16 = jnp.bfloat16

D_MODEL = 1024
MLA_HEADS = 8
Q_LORA = 384
KV_LORA = 256
NOPE_DIM = 128
ROPE_DIM = 64
V_DIM = 128
ROPE_THETA = 10000.0
QK_DIM = KV_LORA + ROPE_DIM
SSM_HEADS = 32
SSM_HEAD_DIM = 64
D_INNER = SSM_HEADS * SSM_HEAD_DIM
SSM_GROUPS = 4
GROUP_W = D_INNER // SSM_GROUPS
D_STATE = 128
CONV_W = 4
CONV_DIM = D_INNER + 2 * SSM_GROUPS * D_STATE
CHUNK = 128
N_MEM = 256
MEM_HEADS = 4
MEM_HEAD_DIM = 256
MEM_WIDTH = MEM_HEADS * MEM_HEAD_DIM
D_FF = 4 * D_MODEL
N_BRANCH = 3
DEPTH = 1
ALPHA = (2 * DEPTH) ** 0.25
LN_EPS = 1e-5
RMS_EPS = 1e-6
PAGE_SIZE = 128
SCORE_SCALE = (NOPE_DIM + ROPE_DIM) ** -0.5
NEG_BIG = -1e30
KV_TILE = 256


def _cp(sem, vmem_mb):
    return pltpu.CompilerParams(dimension_semantics=sem, vmem_limit_bytes=vmem_mb << 20)


def _resident(shape):
    nd = len(shape)
    return pl.BlockSpec(shape, lambda *_: (0,) * nd, pipeline_mode=pl.Buffered(1))


def _dot(a, b):
    return jnp.dot(a, b, preferred_element_type=F32)


def _dot_nt(a, b):
    return lax.dot_general(a, b, (((1,), (1,)), ((), ())), preferred_element_type=F32)


def _rms(v, g, eps):
    return v * lax.rsqrt(jnp.mean(v * v, axis=-1, keepdims=True) + eps) * g


def _layer_norm(v, g, b):
    mu = jnp.mean(v, axis=-1, keepdims=True)
    d = v - mu
    var = jnp.mean(d * d, axis=-1, keepdims=True)
    return d * lax.rsqrt(var + LN_EPS) * g + b


def _in_proj_kernel(x_ref, wcq_ref, wsmall_ref, wz_ref, wxbc_ref, wmq_ref, wg_ref, cs_ref, gq_ref, gkv_ref, bg_ref,
                    cqn_ref, ckv_ref, kr_ref, kcat_ref, ckvt_ref, dt_ref, dtt_ref, z_ref, xbc_ref, mq_ref, g_ref):
    xb = x_ref[...].astype(BF16)
    cqn_ref[...] = _rms(_dot(xb, wcq_ref[...]), gq_ref[...], RMS_EPS).astype(BF16)

    small = _dot(xb, wsmall_ref[...])
    ckv = _rms(small[:, :KV_LORA], gkv_ref[...], RMS_EPS)
    ckv_ref[...] = ckv
    kcat_ref[:, 0:KV_LORA] = ckv.astype(BF16)
    for t in range(ckvt_ref.shape[0]):
        ckvt_ref[t] = ckv[t * KV_TILE:(t + 1) * KV_TILE, :].T.astype(BF16)
    a = small[:, 256:384]
    b = pltpu.roll(a, ROPE_DIM // 2, 1)
    cs = cs_ref[...]
    ro = a * cs[:, :128] + b * cs[:, 128:]
    kr_ref[...] = ro[:, :ROPE_DIM]
    kcat_ref[:, KV_LORA:QK_DIM] = ro[:, :ROPE_DIM].astype(BF16)
    dt = small[:, 384:512]
    dt_ref[...] = dt
    dtt_ref[...] = dt.T[0:SSM_HEADS, :]

    for c in range(0, D_INNER, 1024):
        z_ref[:, c:c + 1024] = _dot(xb, wz_ref[:, c:c + 1024]).astype(BF16)
    for c in range(0, CONV_DIM, 1024):
        xbc_ref[:, c:c + 1024] = _dot(xb, wxbc_ref[:, c:c + 1024]).astype(BF16)
    mq_ref[...] = _dot(xb, wmq_ref[...]).astype(BF16)
    for c in range(0, N_BRANCH * D_MODEL, 1024):
        gr = _dot(xb, wg_ref[:, c:c + 1024]) + bg_ref[:, c:c + 1024]
        g_ref[:, c:c + 1024] = jax.nn.sigmoid(gr).astype(BF16)


def _in_proj(x2d, weights, cs_tab, gq, gkv, bg, tm):
    m = x2d.shape[0]
    ncs = cs_tab.shape[0] // tm

    def row(n, dtype):
        return pl.BlockSpec((tm, n), lambda i: (i, 0)), jax.ShapeDtypeStruct((m, n), dtype)

    outs = [row(Q_LORA, BF16), row(KV_LORA, F32), row(ROPE_DIM, F32), row(QK_DIM, BF16),
            (pl.BlockSpec((tm // KV_TILE, KV_LORA, KV_TILE), lambda i: (i, 0, 0)),
             jax.ShapeDtypeStruct((m // KV_TILE, KV_LORA, KV_TILE), BF16)),
            row(128, F32),
            (pl.BlockSpec((SSM_HEADS, tm), lambda i: (0, i)), jax.ShapeDtypeStruct((SSM_HEADS, m), F32)),
            row(D_INNER, BF16), row(CONV_DIM, BF16), row(MEM_WIDTH, BF16), row(N_BRANCH * D_MODEL, BF16)]
    return pl.pallas_call(
        _in_proj_kernel,
        grid=(m // tm,),
        in_specs=[pl.BlockSpec((tm, D_MODEL), lambda i: (i, 0))] + [_resident(w.shape) for w in weights]
                 + [pl.BlockSpec((tm, 256), lambda i: (i % ncs, 0)),
                    _resident(gq.shape), _resident(gkv.shape), _resident(bg.shape)],
        out_specs=[o[0] for o in outs],
        out_shape=[o[1] for o in outs],
        compiler_params=_cp(("arbitrary",), 56),
        name="in_proj",
    )(x2d, *weights, cs_tab, gq, gkv, bg)


def _q_prep_kernel(c_ref, wq_ref, wuk_ref, cs_ref, q_ref, *, transposed):
    c = c_ref[...]
    cs = cs_ref[...]
    for h in range(MLA_HEADS):
        qn = _dot(c, wq_ref[:, h * NOPE_DIM:(h + 1) * NOPE_DIM]).astype(BF16)
        ql = _dot(qn, wuk_ref[h]) * SCORE_SCALE
        if transposed:
            q_ref[h, 0:KV_LORA, :] = ql.T.astype(BF16)
        else:
            q_ref[h, :, 0:KV_LORA] = ql.astype(BF16)
    r0 = MLA_HEADS * NOPE_DIM
    for p in range(MLA_HEADS // 2):
        ab = _dot(c, wq_ref[:, r0 + p * 256:r0 + (p + 1) * 256])
        ro = (ab[:, :128] * cs[:, :128] + ab[:, 128:] * cs[:, 128:]) * SCORE_SCALE
        if transposed:
            rot = ro.T.astype(BF16)
            q_ref[2 * p, KV_LORA:QK_DIM, :] = rot[:ROPE_DIM]
            q_ref[2 * p + 1, KV_LORA:QK_DIM, :] = rot[ROPE_DIM:]
        else:
            q_ref[2 * p, :, KV_LORA:QK_DIM] = ro[:, :ROPE_DIM].astype(BF16)
            q_ref[2 * p + 1, :, KV_LORA:QK_DIM] = pltpu.roll(ro, ROPE_DIM, 1)[:, :ROPE_DIM].astype(BF16)


def _q_prep(cqn, wq, wuk_t, cs_tab, tm, transposed):
    m = cqn.shape[0]
    ncs = cs_tab.shape[0] // tm
    if transposed:
        out_spec = pl.BlockSpec((MLA_HEADS, QK_DIM, tm), lambda i: (0, 0, i))
        out_shape = jax.ShapeDtypeStruct((MLA_HEADS, QK_DIM, m), BF16)
    else:
        out_spec = pl.BlockSpec((MLA_HEADS, tm, QK_DIM), lambda i: (0, i, 0))
        out_shape = jax.ShapeDtypeStruct((MLA_HEADS, m, QK_DIM), BF16)
    return pl.pallas_call(
        functools.partial(_q_prep_kernel, transposed=transposed),
        grid=(m // tm,),
        in_specs=[pl.BlockSpec((tm, Q_LORA), lambda i: (i, 0)), _resident(wq.shape), _resident(wuk_t.shape),
                  pl.BlockSpec((tm, 256), lambda i: (i % ncs, 0))],
        out_specs=out_spec,
        out_shape=out_shape,
        compiler_params=_cp(("arbitrary",), 32),
        name="q_prep",
    )(cqn, wq, wuk_t, cs_tab)


def _prompt_attn_kernel(qt_ref, k_ref, vt_ref, wuvt_ref, o_ref, m_sc, l_sc, acc_sc):
    i = pl.program_id(1)
    t = KV_TILE
    m_sc[...] = jnp.full(m_sc.shape, NEG_BIG, F32)
    l_sc[...] = jnp.zeros(l_sc.shape, F32)
    acc_sc[...] = jnp.zeros(acc_sc.shape, F32)

    def kv_block(j, masked):
        k = k_ref[pl.ds(pl.multiple_of(j * t, t), t), :]
        kc = k[:, :KV_LORA]
        kr = k[:, KV_LORA:]
        vt = vt_ref[j]
        if masked:
            keep = lax.broadcasted_iota(jnp.int32, (t, t), 0) <= lax.broadcasted_iota(jnp.int32, (t, t), 1)
        def scores(h):
            return _dot(kc, qt_ref[h, 0:KV_LORA, :]) + _dot(kr, qt_ref[h, KV_LORA:QK_DIM, :])

        ahead = 4
        pending = [scores(h) for h in range(ahead)]
        for h in range(MLA_HEADS):
            st = pending.pop(0)
            if h + ahead < MLA_HEADS:
                pending.append(scores(h + ahead))
            if masked:
                st = jnp.where(keep, st, NEG_BIG)
            m_prev = m_sc[h:h + 1, :]
            m_new = jnp.maximum(m_prev, jnp.max(st, axis=0, keepdims=True))
            alpha = jnp.exp(m_prev - m_new)
            p = jnp.exp(st - m_new)
            l_sc[h:h + 1, :] = alpha * l_sc[h:h + 1, :] + jnp.sum(p, axis=0, keepdims=True)
            acc_sc[h] = alpha * acc_sc[h] + _dot(vt, p.astype(BF16))
            m_sc[h:h + 1, :] = m_new

    def body(j, carry):
        kv_block(j, False)
        return carry

    lax.fori_loop(0, i, body, 0)
    kv_block(i, True)

    for h in range(MLA_HEADS):
        ot = (acc_sc[h] / l_sc[h:h + 1, :]).astype(BF16)
        o_ref[:, h * V_DIM:(h + 1) * V_DIM] = _dot(wuvt_ref[h], ot).T.astype(BF16)


def _prompt_attn(qt, kcat, ckvt, wuvt, batch, seq):
    t = KV_TILE
    nq = seq // t
    return pl.pallas_call(
        _prompt_attn_kernel,
        grid=(batch, nq),
        in_specs=[pl.BlockSpec((MLA_HEADS, QK_DIM, t), lambda b, i: (0, 0, b * nq + i)),
                  pl.BlockSpec((seq, QK_DIM), lambda b, i: (b, 0)),
                  pl.BlockSpec((nq, KV_LORA, t), lambda b, i: (b, 0, 0)),
                  _resident(wuvt.shape)],
        out_specs=pl.BlockSpec((t, MLA_HEADS * V_DIM), lambda b, i: (b * nq + i, 0)),
        out_shape=jax.ShapeDtypeStruct((batch * seq, MLA_HEADS * V_DIM), BF16),
        scratch_shapes=[pltpu.VMEM((MLA_HEADS, t), F32), pltpu.VMEM((MLA_HEADS, t), F32),
                        pltpu.VMEM((MLA_HEADS, KV_LORA, t), F32)],
        compiler_params=_cp(("arbitrary", "arbitrary"), 32),
        name="prompt_attn",
    )(qt, kcat, ckvt, wuvt)


def _decode_attn_kernel(pt_ref, q_ref, kn_ref, ckv_hbm, krt_hbm, o_ref, kbuf, rbuf, kb_sc, s_sc, sem, *, n_pages, n_new):
    b = pl.program_id(0)
    nb = pl.num_programs(0)
    slot = b % 2

    def page_copies(page, p, sl):
        rows = pl.ds(pl.multiple_of(p * PAGE_SIZE, PAGE_SIZE), PAGE_SIZE)
        return (pltpu.make_async_copy(ckv_hbm.at[page], kbuf.at[sl, rows, :], sem.at[0, sl]),
                pltpu.make_async_copy(krt_hbm.at[page], rbuf.at[sl, p], sem.at[1, sl]))

    def issue(bi, sl):
        def body(p, carry):
            for cp in page_copies(pt_ref[bi * n_pages + p], p, sl):
                cp.start()
            return carry
        lax.fori_loop(0, n_pages, body, 0, unroll=8)

    @pl.when(b == 0)
    def _():
        issue(0, 0)

    @pl.when(b + 1 < nb)
    def _():
        issue(b + 1, 1 - slot)

    def wait_body(p, carry):
        for cp in page_copies(0, p, slot):
            cp.wait()
        return carry
    lax.fori_loop(0, n_pages, wait_body, 0, unroll=8)

    q = q_ref[0]
    rows = q.shape[0]
    qc = q[:, :KV_LORA]
    qr = q[:, KV_LORA:]
    qf = q.astype(F32)
    kn = kn_ref[0]
    tok = lax.broadcasted_iota(jnp.int32, (rows, 1), 0) % n_new

    s_new = []
    for j in range(n_new):
        sj = jnp.sum(qf * kn[j:j + 1, :], axis=1, keepdims=True)
        s_new.append(jnp.where(tok >= j, sj, NEG_BIG))
    m0 = s_new[0]
    for j in range(1, n_new):
        m0 = jnp.maximum(m0, s_new[j])

    for c in range(n_pages // 2):
        r = slice(2 * c * PAGE_SIZE, 2 * (c + 1) * PAGE_SIZE)
        kc = kbuf[slot, r, :].astype(BF16)
        kb_sc[r, :] = kc
        rt = jnp.concatenate([rbuf[slot, 2 * c], rbuf[slot, 2 * c + 1]], axis=1).astype(BF16)
        s_sc[:, r] = _dot_nt(qc, kc) + _dot(qr, rt)
    s = s_sc[...]
    m = jnp.maximum(m0, jnp.max(s, axis=1, keepdims=True))
    p = jnp.exp(s - m)
    l = jnp.sum(p, axis=1, keepdims=True)
    acc = _dot(p.astype(BF16), kb_sc[...])
    for j in range(n_new):
        pj = jnp.exp(s_new[j] - m)
        l = l + pj
        acc = acc + pj * kn[j:j + 1, :KV_LORA]
    o_ref[0] = (acc / l).astype(BF16)


def _decode_attn(page_table, q, k_new, cache_ckv, cache_krope_t):
    nb, n_pages = page_table.shape
    rows = q.shape[1]
    n_new = k_new.shape[1]
    past = n_pages * PAGE_SIZE
    kern = functools.partial(_decode_attn_kernel, n_pages=n_pages, n_new=n_new)
    grid_spec = pltpu.PrefetchScalarGridSpec(
        num_scalar_prefetch=1,
        grid=(nb,),
        in_specs=[pl.BlockSpec((1, rows, QK_DIM), lambda b, pt: (b, 0, 0)),
                  pl.BlockSpec((1, n_new, QK_DIM), lambda b, pt: (b, 0, 0)),
                  pl.BlockSpec(memory_space=pl.ANY),
                  pl.BlockSpec(memory_space=pl.ANY)],
        out_specs=pl.BlockSpec((1, rows, KV_LORA), lambda b, pt: (b, 0, 0)),
        scratch_shapes=[pltpu.VMEM((2, past, KV_LORA), F32), pltpu.VMEM((2, n_pages, ROPE_DIM, PAGE_SIZE), F32),
                        pltpu.VMEM((past, KV_LORA), BF16), pltpu.VMEM((rows, past), F32),
                        pltpu.SemaphoreType.DMA((2, 2))],
    )
    return pl.pallas_call(
        kern,
        grid_spec=grid_spec,
        out_shape=jax.ShapeDtypeStruct((nb, rows, KV_LORA), BF16),
        compiler_params=_cp(("arbitrary",), 40),
        name="decode_attn",
    )(page_table.reshape(-1), q, k_new, cache_ckv, cache_krope_t)


def _uv_proj_kernel(o_ref, w_ref, out_ref):
    out_ref[...] = _dot(o_ref[0], w_ref[0]).astype(BF16)


def _uv_proj(o_lat, wuv):
    m = o_lat.shape[1]
    return pl.pallas_call(
        _uv_proj_kernel,
        grid=(MLA_HEADS,),
        in_specs=[pl.BlockSpec((1, m, KV_LORA), lambda h: (h, 0, 0)),
                  pl.BlockSpec((1, KV_LORA, V_DIM), lambda h: (h, 0, 0))],
        out_specs=pl.BlockSpec((m, V_DIM), lambda h: (0, h)),
        out_shape=jax.ShapeDtypeStruct((m, MLA_HEADS * V_DIM), BF16),
        compiler_params=_cp(("arbitrary",), 16),
        name="uv_proj",
    )(o_lat, wuv)


def _split_dot(v, e, terms, left=True):
    parts = []
    r = v
    for _ in range(terms):
        hi = r.astype(BF16)
        parts.append(hi)
        r = r - hi.astype(F32)
    if left:
        return _dot(jnp.concatenate(parts, axis=1), e)
    return _dot(e, jnp.concatenate(parts, axis=0))


def _conv_silu(taps, cw_ref, cb_ref):
    conv = cb_ref[...] + cw_ref[0:1, :] * taps[0]
    for k in range(1, CONV_W):
        conv = conv + cw_ref[k:k + 1, :] * taps[k]
    xc = conv * jax.nn.sigmoid(conv)
    n_bc = SSM_GROUPS * D_STATE
    return xc[:, :D_INNER], xc[:, D_INNER:D_INNER + n_bc], xc[:, D_INNER + n_bc:]


def _select_rows(select, rows_ref):
    n = rows_ref.shape[1]
    return jnp.concatenate([_dot(select, rows_ref[:, c:c + 1024]) for c in range(0, n, 1024)], axis=1)


def _ssd_sums(dt, dtt, keep, keep_t, alog_ref, alogt_ref):
    tri = jnp.where(keep, 1.0, 0.0).astype(BF16)
    tri_t = jnp.where(keep_t, 1.0, 0.0).astype(BF16)
    da = dt * (-jnp.exp(alog_ref[...]))
    dat = dtt * (-jnp.exp(alogt_ref[...]))
    cum = _split_dot(da, jnp.concatenate([tri] * 3, axis=1), 3, left=False)
    cum_t = _split_dot(dat, jnp.concatenate([tri_t] * 3, axis=0), 3)
    return da, cum, cum_t


def _ssd_intra(xs, bm, cm, cum_e, cum_t, dtt, keep, dskip_ref, y_sc, carried=None):
    pairs_per_group = SSM_HEADS // 2 // SSM_GROUPS
    lane = lax.broadcasted_iota(jnp.int32, (CHUNK, CHUNK), 1)
    for g in range(SSM_GROUPS):
        cg = cm[:, g * D_STATE:(g + 1) * D_STATE].astype(BF16)
        cb = _dot_nt(cg, bm[:, g * D_STATE:(g + 1) * D_STATE].astype(BF16))
        extra = None if carried is None else carried(g, cg)
        for pr in range(pairs_per_group):
            q = g * pairs_per_group + pr
            ws = []
            for h in (2 * q, 2 * q + 1):
                seg = cum_e[:, h * 128:(h + 1) * 128] - cum_t[h:h + 1, :]
                decay = jnp.exp(jnp.where(keep, seg, NEG_BIG)) * dtt[h:h + 1, :]
                ws.append((cb * decay).astype(BF16))
            w_pair = jnp.concatenate(ws, axis=1)
            xp = xs[:, q * 128:(q + 1) * 128]
            x_bd = jnp.concatenate([jnp.where(lane < SSM_HEAD_DIM, xp, 0.0).astype(BF16),
                                    jnp.where(lane >= SSM_HEAD_DIM, xp, 0.0).astype(BF16)], axis=0)
            y = _dot(w_pair, x_bd) + dskip_ref[:, q * 128:(q + 1) * 128] * xp
            if extra is not None:
                y = y + extra[:, pr * 128:(pr + 1) * 128]
            y_sc[:, q * 128:(q + 1) * 128] = y


CARRY = 16


def _ssd_kernel(xbc_ref, z_ref, dt_ref, dtt_ref, cw_ref, cb_ref, dtb_ref, dtbt_ref,
                alog_ref, alogt_ref, dskip_ref, ng_ref, e64_ref, e128_ref, shift_ref,
                y_ref, sout_ref, cout_ref, xe_sc, tail_sc, st_sc, y_sc):
    c = pl.program_id(1)
    n_pairs = SSM_HEADS // 2
    pairs_per_group = n_pairs // SSM_GROUPS

    @pl.when(c == 0)
    def _():
        xe_sc[0:CARRY, :] = jnp.zeros((CARRY, CONV_DIM), BF16)
        st_sc[...] = jnp.zeros(st_sc.shape, F32)

    xe_sc[CARRY:CARRY + CHUNK, :] = xbc_ref[...]
    shifted = _select_rows(shift_ref[...], xe_sc)
    taps = [shifted[k * CHUNK:(k + 1) * CHUNK] for k in range(CONV_W - 1)] + [xbc_ref[...].astype(F32)]
    xs, bm, cm = _conv_silu(taps, cw_ref, cb_ref)
    xe_sc[0:CARRY, :] = xe_sc[CHUNK:CHUNK + CARRY, :]

    rowi = lax.broadcasted_iota(jnp.int32, (CHUNK, CHUNK), 0)
    coli = lax.broadcasted_iota(jnp.int32, (CHUNK, CHUNK), 1)
    keep = coli <= rowi
    dt = jax.nn.softplus(dt_ref[...] + dtb_ref[...])
    dtt = jax.nn.softplus(dtt_ref[...] + dtbt_ref[...])
    _, cum, cum_t = _ssd_sums(dt, dtt, keep, rowi <= coli, alog_ref, alogt_ref)
    last = cum[CHUNK - 1:CHUNK, :]
    e64 = e64_ref[...]
    ec_e = _split_dot(jnp.exp(cum), e64, 2)
    wb_e = _split_dot(dt * jnp.exp(last - cum), e64, 2)
    dec_e = _split_dot(jnp.broadcast_to(jnp.exp(last), (16, CHUNK)), e64, 2)[0:1, :]
    cum_e = _split_dot(cum, e128_ref[...], 3)

    def carried(g, cg):
        return _dot(cg, st_sc[g].astype(BF16)) * ec_e[:, g * GROUP_W:(g + 1) * GROUP_W]

    _ssd_intra(xs, bm, cm, cum_e, cum_t, dtt, keep, dskip_ref, y_sc, carried)
    for g in range(SSM_GROUPS):
        gs = slice(g * GROUP_W, (g + 1) * GROUP_W)
        xw = (xs[:, gs] * wb_e[:, gs]).astype(BF16)
        bt = bm[:, g * D_STATE:(g + 1) * D_STATE].T.astype(BF16)
        st_sc[g] = st_sc[g] * dec_e[:, gs] + _dot(bt, xw)

    for g in range(SSM_GROUPS):
        gs = slice(g * GROUP_W, (g + 1) * GROUP_W)
        zf = z_ref[:, gs].astype(F32)
        yv = y_sc[:, gs] * (zf * jax.nn.sigmoid(zf))
        y_ref[:, gs] = _rms(yv, ng_ref[:, gs], RMS_EPS).astype(y_ref.dtype)

    @pl.when(c == pl.num_programs(1) - 1)
    def _():
        tail_sc[...] = xbc_ref[CHUNK - CARRY:CHUNK, :].astype(F32)
        cout_ref[0] = tail_sc[CARRY - (CONV_W - 1):CARRY, :]
        for q in range(n_pairs):
            g, pr = divmod(q, pairs_per_group)
            sout_ref[0, q * 128:(q + 1) * 128, :] = st_sc[g, :, pr * 128:(pr + 1) * 128].T


def _ssd(xbc, z, dt, dtt, consts, batch, n_chunks):
    def step(n):
        return pl.BlockSpec((CHUNK, n), lambda b, c: (b * n_chunks + c, 0))

    def per_batch(r, n):
        return pl.BlockSpec((1, r, n), lambda b, c: (b, 0, 0))

    return pl.pallas_call(
        _ssd_kernel,
        grid=(batch, n_chunks),
        in_specs=[step(CONV_DIM), step(D_INNER), step(128),
                  pl.BlockSpec((SSM_HEADS, CHUNK), lambda b, c: (0, b * n_chunks + c))]
                 + [_resident(a.shape) for a in consts],
        out_specs=[step(D_INNER), per_batch(D_INNER, D_STATE), per_batch(CONV_W - 1, CONV_DIM)],
        out_shape=[jax.ShapeDtypeStruct((batch * n_chunks * CHUNK, D_INNER), BF16),
                   jax.ShapeDtypeStruct((batch, D_INNER, D_STATE), F32),
                   jax.ShapeDtypeStruct((batch, CONV_W - 1, CONV_DIM), F32)],
        scratch_shapes=[pltpu.VMEM((CARRY + CHUNK, CONV_DIM), BF16),
                        pltpu.VMEM((CARRY, CONV_DIM), F32),
                        pltpu.VMEM((SSM_GROUPS, D_STATE, GROUP_W), F32),
                        pltpu.VMEM((CHUNK, D_INNER), F32)],
        compiler_params=_cp(("arbitrary", "arbitrary"), 48),
        name="ssd",
    )(xbc, z, dt, dtt, *consts)


SLOTS = 16


def _ssd_short_kernel(x_ref, sc_ref, dt_ref, dtt_ref, cw_ref, cb_ref, dtb_ref, dtbt_ref, alog_ref, alogt_ref,
                      dskip_ref, e64_ref, e128_ref, taps_sel_ref, place_ref,
                      ypre_ref, ec_ref, xw_ref, c_ref, bw_ref, tails_ref, rows_sc, y_sc, *, seq_len):
    n_cached = sc_ref.shape[0] * sc_ref.shape[1]
    rows_sc[0:CHUNK, :] = x_ref[...]
    r = sc_ref[...].reshape(n_cached, CONV_DIM)
    for t in range(3):
        hi = r.astype(BF16)
        rows_sc[CHUNK + t * n_cached:CHUNK + (t + 1) * n_cached, :] = hi
        r = r - hi.astype(F32)
    picked = _select_rows(taps_sel_ref[...], rows_sc)
    taps = [picked[k * CHUNK:(k + 1) * CHUNK] for k in range(CONV_W - 1)] + [x_ref[...].astype(F32)]
    tails_ref[...] = picked[(CONV_W - 1) * CHUNK:].reshape(tails_ref.shape)
    xs, bm, cm = _conv_silu(taps, cw_ref, cb_ref)
    rowi = lax.broadcasted_iota(jnp.int32, (CHUNK, CHUNK), 0)
    coli = lax.broadcasted_iota(jnp.int32, (CHUNK, CHUNK), 1)
    same = (rowi // seq_len) == (coli // seq_len)
    keep = same & (coli <= rowi)
    dt = jax.nn.softplus(dt_ref[...] + dtb_ref[...])
    dtt = jax.nn.softplus(dtt_ref[...] + dtbt_ref[...])
    da, cum, cum_t = _ssd_sums(dt, dtt, keep, same & (rowi <= coli), alog_ref, alogt_ref)
    whole = jnp.where(same, 1.0, 0.0).astype(BF16)
    last = _split_dot(da, jnp.concatenate([whole] * 3, axis=1), 3, left=False)
    e64 = e64_ref[...]
    ec_ref[...] = _split_dot(jnp.exp(cum), e64, 2)
    wb_e = _split_dot(dt * jnp.exp(last - cum), e64, 2)
    dec_e = _split_dot(jnp.exp(last), e64, 2)
    cum_e = _split_dot(cum, e128_ref[...], 3)
    _ssd_intra(xs, bm, cm, cum_e, cum_t, dtt, keep, dskip_ref, y_sc)
    ypre_ref[...] = y_sc[...]

    place = place_ref[...]
    dec_hi = dec_e.astype(BF16)
    dec_lo = (dec_e - dec_hi.astype(F32)).astype(BF16)
    xw = (xs * wb_e).astype(BF16)
    xw_ref[...] = _dot(place, jnp.concatenate([xw, dec_hi, dec_lo], axis=0)).astype(BF16)
    tokens = place[:, :CHUNK]
    c_ref[...] = _dot(tokens, cm.astype(BF16)).astype(BF16)
    zeros = jnp.zeros((CHUNK, D_STATE), BF16)
    b_wide = jnp.concatenate(
        [piece for g in range(SSM_GROUPS) for piece in (bm[:, g * D_STATE:(g + 1) * D_STATE].astype(BF16), zeros)], axis=1)
    n_rows = place.shape[0]
    slot = lax.broadcasted_iota(jnp.int32, (n_rows, 2 * D_STATE * SSM_GROUPS), 0) % SLOTS
    col = lax.broadcasted_iota(jnp.int32, (n_rows, 2 * D_STATE * SSM_GROUPS), 1) % (2 * D_STATE)
    ones = ((slot == seq_len) | (slot == seq_len + 1)) & (col >= D_STATE)
    bw_ref[...] = jnp.where(ones, 1.0, _dot(tokens, b_wide)).astype(BF16)


def _ssd_short(xbc, conv_rows, dt, dtt, consts, taps_sel, place, seq_len):
    rows = dt.shape[0]
    n_chunks = rows // CHUNK
    n_seq = CHUNK // seq_len
    n_slot_rows = place.shape[0]
    kern = functools.partial(_ssd_short_kernel, seq_len=seq_len)

    def out(r, n, dtype):
        return pl.BlockSpec((r, n), lambda c: (c, 0)), jax.ShapeDtypeStruct((n_chunks * r, n), dtype)

    conv_spec = pl.BlockSpec((CONV_W - 1, n_seq, CONV_DIM), lambda c: (0, c, 0))
    outs = [out(CHUNK, D_INNER, F32), out(CHUNK, D_INNER, F32), out(n_slot_rows, D_INNER, BF16),
            out(n_slot_rows, SSM_GROUPS * D_STATE, BF16), out(n_slot_rows, 2 * SSM_GROUPS * D_STATE, BF16),
            (conv_spec, jax.ShapeDtypeStruct(conv_rows.shape, F32))]
    return pl.pallas_call(
        kern,
        grid=(n_chunks,),
        in_specs=[pl.BlockSpec((CHUNK, CONV_DIM), lambda c: (c, 0)), conv_spec,
                  pl.BlockSpec((CHUNK, 128), lambda c: (c, 0)),
                  pl.BlockSpec((SSM_HEADS, CHUNK), lambda c: (0, c))]
                 + [_resident(a.shape) for a in consts] + [_resident(taps_sel.shape), _resident(place.shape)],
        out_specs=[o[0] for o in outs],
        out_shape=[o[1] for o in outs],
        scratch_shapes=[pltpu.VMEM((taps_sel.shape[1], CONV_DIM), BF16), pltpu.VMEM((CHUNK, D_INNER), F32)],
        compiler_params=_cp(("arbitrary",), 48),
        name="ssd_short",
    )(xbc, conv_rows, dt, dtt, *consts, taps_sel, place)


def _ssd_state_kernel(s0_ref, xw_ref, c_ref, bw_ref, ypre_ref, ec_ref, z_ref, ng_ref, y_ref, sout_ref, ci_sc,
                      *, nb, seq_len):
    for bb in range(nb):
        slots = slice(bb * SLOTS, (bb + 1) * SLOTS)
        toks = slice(bb * seq_len, (bb + 1) * seq_len)
        for g in range(SSM_GROUPS):
            gs = slice(g * GROUP_W, (g + 1) * GROUP_W)
            s0g = s0_ref[bb, gs, :]
            ci_sc[...] = _dot_nt(c_ref[slots, g * D_STATE:(g + 1) * D_STATE], s0g.astype(BF16))
            u = lax.dot_general(xw_ref[slots, gs], bw_ref[slots, 2 * g * D_STATE:2 * (g + 1) * D_STATE],
                                (((0,), (0,)), ((), ())), preferred_element_type=F32)
            sout_ref[bb, gs, :] = u[:, :D_STATE] + s0g * u[:, D_STATE:]
            zf = z_ref[toks, gs]
            yv = (ypre_ref[toks, gs] + ec_ref[toks, gs] * ci_sc[0:seq_len, :]) * (zf * jax.nn.sigmoid(zf))
            y_ref[toks, gs] = _rms(yv, ng_ref[:, gs], RMS_EPS)


def _ssd_state(s0, xw, cmat, bw, ypre, ec, z, ng, nb, seq_len):
    batch = s0.shape[0]
    kern = functools.partial(_ssd_state_kernel, nb=nb, seq_len=seq_len)

    def rows(r, n):
        return pl.BlockSpec((nb * r, n), lambda b: (b, 0))

    state_spec = pl.BlockSpec((nb, D_INNER, D_STATE), lambda b: (b, 0, 0))
    return pl.pallas_call(
        kern,
        grid=(batch // nb,),
        in_specs=[state_spec, rows(SLOTS, D_INNER), rows(SLOTS, SSM_GROUPS * D_STATE),
                  rows(SLOTS, 2 * SSM_GROUPS * D_STATE), rows(seq_len, D_INNER), rows(seq_len, D_INNER),
                  rows(seq_len, D_INNER), _resident(ng.shape)],
        out_specs=[rows(seq_len, D_INNER), state_spec],
        out_shape=[jax.ShapeDtypeStruct((batch * seq_len, D_INNER), F32),
                   jax.ShapeDtypeStruct((batch, D_INNER, D_STATE), F32)],
        scratch_shapes=[pltpu.VMEM((SLOTS, GROUP_W), F32)],
        compiler_params=_cp(("arbitrary",), 40),
        name="ssd_state",
    )(s0, xw, cmat, bw, ypre, ec, z, ng)


def _mem_kv_kernel(m_ref, w_ref, k_ref, v_ref):
    mb = m_ref[...].astype(BF16)
    k_ref[...] = _dot(mb, w_ref[:, :MEM_WIDTH])
    v_ref[...] = _dot(mb, w_ref[:, MEM_WIDTH:])


def _mem_kv(mem2d, w_kv, tm):
    m = mem2d.shape[0]
    return pl.pallas_call(
        _mem_kv_kernel,
        grid=(m // tm,),
        in_specs=[pl.BlockSpec((tm, D_MODEL), lambda i: (i, 0)), _resident(w_kv.shape)],
        out_specs=[pl.BlockSpec((tm, MEM_WIDTH), lambda i: (i, 0))] * 2,
        out_shape=[jax.ShapeDtypeStruct((m, MEM_WIDTH), F32)] * 2,
        compiler_params=_cp(("arbitrary",), 32),
        name="mem_kv",
    )(mem2d, w_kv)


def _mem_attn_kernel(q_ref, k_ref, v_ref, o_ref, *, nb):
    scale = MEM_HEAD_DIM ** -0.5
    for bb in range(nb):
        for h in range(MEM_HEADS):
            hs = slice(h * MEM_HEAD_DIM, (h + 1) * MEM_HEAD_DIM)
            kh = k_ref[bb, :, hs].astype(BF16)
            vh = v_ref[bb, :, hs].astype(BF16)
            s = _dot_nt(q_ref[bb, :, hs], kh) * scale
            p = jnp.exp(s - jnp.max(s, axis=1, keepdims=True))
            l = jnp.sum(p, axis=1, keepdims=True)
            o_ref[bb, :, hs] = (_dot(p.astype(BF16), vh) / l).astype(BF16)


def _mem_attn(mq, mem_k, mem_v, nb, tq):
    batch, rows, _ = mq.shape
    kern = functools.partial(_mem_attn_kernel, nb=nb)
    kv_spec = pl.BlockSpec((nb, N_MEM, MEM_WIDTH), lambda b, i: (b, 0, 0))
    return pl.pallas_call(
        kern,
        grid=(batch // nb, rows // tq),
        in_specs=[pl.BlockSpec((nb, tq, MEM_WIDTH), lambda b, i: (b, i, 0)), kv_spec, kv_spec],
        out_specs=pl.BlockSpec((nb, tq, MEM_WIDTH), lambda b, i: (b, i, 0)),
        out_shape=jax.ShapeDtypeStruct((batch, rows, MEM_WIDTH), BF16),
        compiler_params=_cp(("arbitrary", "arbitrary"), 40),
        name="mem_attn",
    )(mq, mem_k, mem_v)


MEM_HALVES = MEM_HEAD_DIM // 128
MEM_ROW_GROUP = MEM_HALVES * MEM_HEADS


def _mem_attn_cache_kernel(q_ref, k_ref, v_ref, o_ref, *, nb, n_tok):
    rows = MEM_HEADS * n_tok
    n_col = N_MEM * MEM_ROW_GROUP
    col = lax.broadcasted_iota(jnp.int32, (rows, n_col), 1)
    row = lax.broadcasted_iota(jnp.int32, (rows, n_col), 0)
    own = (col % MEM_ROW_GROUP) == (row // n_tok)
    scale = MEM_HEAD_DIM ** -0.5
    for bb in range(nb):
        g = _dot_nt(q_ref[bb], k_ref[bb].astype(BF16))
        s = (g[:rows] + pltpu.roll(g[rows:], n_col - MEM_HEADS, 1)) * scale
        s = jnp.where(own, s, NEG_BIG)
        p = jnp.exp(s - jnp.max(s, axis=1, keepdims=True))
        l = jnp.sum(p, axis=1, keepdims=True)
        p2 = jnp.concatenate([p, pltpu.roll(p, MEM_HEADS, 1)], axis=0).astype(BF16)
        o = _dot(p2, v_ref[bb].astype(BF16))
        o_ref[bb] = (o / jnp.concatenate([l, l], axis=0)).astype(BF16)


def _mem_attn_cache(q, mem_k, mem_v, nb, n_tok):
    batch, rows, _ = q.shape
    kern = functools.partial(_mem_attn_cache_kernel, nb=nb, n_tok=n_tok)
    kv_spec = pl.BlockSpec((nb,) + mem_k.shape[1:], lambda b: (b, 0, 0))
    q_spec = pl.BlockSpec((nb, rows, 128), lambda b: (b, 0, 0))
    return pl.pallas_call(
        kern,
        grid=(batch // nb,),
        in_specs=[q_spec, kv_spec, kv_spec],
        out_specs=q_spec,
        out_shape=jax.ShapeDtypeStruct((batch, rows, 128), BF16),
        compiler_params=_cp(("arbitrary",), 40),
        name="mem_attn_cache",
    )(q, mem_k, mem_v)


def _merge_ffn_kernel(x_ref, oa_ref, ob_ref, om_ref, g_ref, woa_ref, wob_ref, wom_ref, wout_ref,
                      ln1g_ref, ln1b_ref, wup_ref, bup_ref, wdown_ref, bdown_ref, ln2g_ref, ln2b_ref, y_ref):
    x = x_ref[...]
    m = g_ref[:, 0:D_MODEL].astype(F32) * _dot(oa_ref[...], woa_ref[...])
    m = m + g_ref[:, D_MODEL:2 * D_MODEL].astype(F32) * _dot(ob_ref[...], wob_ref[...])
    m = m + g_ref[:, 2 * D_MODEL:].astype(F32) * _dot(om_ref[...], wom_ref[...])
    x1 = _layer_norm(ALPHA * x + _dot(m.astype(BF16), wout_ref[...]), ln1g_ref[...], ln1b_ref[...])
    h = jnp.maximum(_dot(x1.astype(BF16), wup_ref[...]) + bup_ref[...], 0.0)
    h = (h * h).astype(BF16)
    y = ALPHA * x1 + _dot(h, wdown_ref[...]) + bdown_ref[...]
    y_ref[...] = _layer_norm(y, ln2g_ref[...], ln2b_ref[...])


def _merge_ffn(x2d, o_mla, o_ssm, o_mem, g, weights, tm):
    m = x2d.shape[0]

    def row(n):
        return pl.BlockSpec((tm, n), lambda i: (i, 0))

    return pl.pallas_call(
        _merge_ffn_kernel,
        grid=(m // tm,),
        in_specs=[row(D_MODEL), row(MLA_HEADS * V_DIM), row(D_INNER), row(MEM_WIDTH), row(N_BRANCH * D_MODEL)]
                 + [_resident(w.shape) for w in weights],
        out_specs=row(D_MODEL),
        out_shape=jax.ShapeDtypeStruct((m, D_MODEL), F32),
        compiler_params=_cp(("arbitrary",), 56),
        name="merge_ffn",
    )(x2d, o_mla, o_ssm, o_mem, g, *weights)


def _rope_table(pos):
    half = ROPE_DIM // 2
    inv = ROPE_THETA ** (-jnp.arange(half, dtype=F32) / half)
    ang = pos.astype(F32)[:, None] * inv[None, :]
    cos, sin = jnp.cos(ang), jnp.sin(ang)
    return jnp.concatenate([cos, cos, cos, cos, -sin, sin, -sin, sin], axis=1)


def _expand_matrix(width, terms):
    src = np.arange(SSM_HEADS * width) // width
    one = (np.arange(128)[:, None] == src[None, :]).astype(np.float32)
    return jnp.asarray(np.concatenate([one] * terms, axis=0), dtype=BF16)


def _shift_matrix():
    sel = np.zeros(((CONV_W - 1) * CHUNK, CARRY + CHUNK), np.float32)
    for k in range(CONV_W - 1):
        for t in range(CHUNK):
            sel[k * CHUNK + t, CARRY + t + k - (CONV_W - 1)] = 1.0
    return jnp.asarray(sel, dtype=BF16)


def _short_taps_matrix(seq_len):
    n_seq = CHUNK // seq_len
    n_old = CONV_W - 1
    n_cached = n_old * n_seq
    sel = np.zeros((n_old * CHUNK + n_cached, CHUNK + 3 * n_cached), np.float32)

    def take(out_row, s, m):
        if m >= n_old:
            sel[out_row, s * seq_len + m - n_old] = 1.0
        else:
            for term in range(3):
                sel[out_row, CHUNK + term * n_cached + m * n_seq + s] = 1.0

    for s in range(n_seq):
        for k in range(n_old):
            for i in range(seq_len):
                take(k * CHUNK + s * seq_len + i, s, i + k)
        for j in range(n_old):
            take(n_old * CHUNK + j * n_seq + s, s, seq_len + j)
    return jnp.asarray(sel, dtype=BF16)


def _placement_matrix(seq_len):
    n_seq = CHUNK // seq_len
    place = np.zeros((n_seq * SLOTS, 3 * CHUNK), np.float32)
    for s in range(n_seq):
        for j in range(seq_len):
            place[s * SLOTS + j, s * seq_len + j] = 1.0
        place[s * SLOTS + seq_len, CHUNK + s * seq_len] = 1.0
        place[s * SLOTS + seq_len + 1, 2 * CHUNK + s * seq_len] = 1.0
    return jnp.asarray(place, dtype=BF16)


def _row(v):
    return v.reshape(1, -1).astype(F32)


def kernel(x_prompt, x_sample, mem_prompt, cache_ckv, cache_krope, page_table, cache_mem_k, cache_mem_v, state_ssm, state_conv, w_in, q_norm_g, w_uq, kv_norm_g, w_uk, w_uv, conv_w, conv_b, dt_bias, a_log, d_skip, ssm_norm_g, w_mem_k, w_mem_v, b_gate, w_o_mla, w_o_ssm, w_o_mem, w_out, ln1_g, ln1_b, w_up, b_up, w_down, b_down, ln2_g, ln2_b):
    bp, seq, _ = x_prompt.shape
    bs, t_new, _ = x_sample.shape
    n_pages = page_table.shape[1]
    past = n_pages * PAGE_SIZE

    o_cq, o_ckv, o_kr, o_z, o_xbc, o_dt, o_mq, o_g = np.cumsum(
        [0, Q_LORA, KV_LORA, ROPE_DIM, D_INNER, CONV_DIM, SSM_HEADS, MEM_WIDTH]).tolist()
    w_kr = w_in[:, o_kr:o_z]
    w_small = jnp.concatenate([w_in[:, o_ckv:o_kr], w_kr, w_kr, w_in[:, o_dt:o_mq],
                               jnp.zeros((D_MODEL, 128 - SSM_HEADS), F32)], axis=1)
    w_proj = tuple(w.astype(BF16) for w in (w_in[:, o_cq:o_ckv], w_small, w_in[:, o_z:o_xbc], w_in[:, o_xbc:o_dt],
                                            w_in[:, o_mq:o_g], w_in[:, o_g:]))
    half = ROPE_DIM // 2
    w_q_nope = w_uq[:, :, :NOPE_DIM].reshape(Q_LORA, MLA_HEADS * NOPE_DIM)
    w_q_rope = w_uq[:, :, NOPE_DIM:]
    w_q_swap = jnp.concatenate([w_q_rope[:, :, half:], w_q_rope[:, :, :half]], axis=-1)
    pair_w = 2 * ROPE_DIM
    w_q_pairs = jnp.concatenate([w_q_rope.reshape(Q_LORA, MLA_HEADS // 2, pair_w),
                                 w_q_swap.reshape(Q_LORA, MLA_HEADS // 2, pair_w)], axis=-1)
    wq = jnp.concatenate([w_q_nope, w_q_pairs.reshape(Q_LORA, -1)], axis=1).astype(BF16)
    wuk_t = jnp.transpose(w_uk, (1, 2, 0)).astype(BF16)
    wuv = jnp.transpose(w_uv, (1, 0, 2)).astype(BF16)
    wuv_t = jnp.transpose(w_uv, (1, 2, 0)).astype(BF16)
    w_mem_kv = jnp.concatenate([w_mem_k.reshape(D_MODEL, MEM_WIDTH), w_mem_v.reshape(D_MODEL, MEM_WIDTH)], axis=1).astype(BF16)
    merge_w = (w_o_mla.astype(BF16), w_o_ssm.astype(BF16), w_o_mem.astype(BF16), w_out.astype(BF16),
               _row(ln1_g), _row(ln1_b), w_up.astype(BF16), _row(b_up), w_down.astype(BF16), _row(b_down),
               _row(ln2_g), _row(ln2_b))
    gq, gkv, bg = _row(q_norm_g), _row(kv_norm_g), _row(b_gate)

    pad_heads = jnp.zeros((128 - SSM_HEADS,), F32)
    ssd_head = (conv_w.astype(F32), _row(conv_b),
                _row(jnp.concatenate([dt_bias, pad_heads])),
                jnp.broadcast_to(dt_bias.astype(F32)[:, None], (SSM_HEADS, CHUNK)),
                _row(jnp.concatenate([a_log, pad_heads])),
                jnp.broadcast_to(a_log.astype(F32)[:, None], (SSM_HEADS, CHUNK)),
                _row(jnp.repeat(d_skip, SSM_HEAD_DIM)))
    norm_g = _row(ssm_norm_g)
    expanders = (_expand_matrix(SSM_HEAD_DIM, 2), _expand_matrix(128, 3))

    mp = bp * seq
    n_chunks = seq // CHUNK
    cs_p = _rope_table(jnp.arange(seq))
    xp2d = x_prompt.reshape(mp, D_MODEL)
    cqn, ckv_p, kr_p, kcat, ckvt, dt_p, dtt_p, z_p, xbc_p, mq_p, g_p = _in_proj(xp2d, w_proj, cs_p, gq, gkv, bg, tm=256)
    qt_p = _q_prep(cqn, wq, wuk_t, cs_p, tm=256, transposed=True)
    o_mla_p = _prompt_attn(qt_p, kcat, ckvt, wuv_t, bp, seq)

    o_ssm_p, ssm_p, conv_p = _ssd(xbc_p, z_p, dt_p, dtt_p, ssd_head + (norm_g,) + expanders + (_shift_matrix(),),
                                  batch=bp, n_chunks=n_chunks)

    mem_k_p, mem_v_p = _mem_kv(mem_prompt.reshape(bp * N_MEM, D_MODEL), w_mem_kv, tm=min(512, bp * N_MEM))
    o_mem_p = _mem_attn(mq_p.reshape(bp, seq, MEM_WIDTH), mem_k_p.reshape(bp, N_MEM, MEM_WIDTH),
                        mem_v_p.reshape(bp, N_MEM, MEM_WIDTH), nb=1, tq=512)
    y_p = _merge_ffn(xp2d, o_mla_p, o_ssm_p.reshape(mp, D_INNER), o_mem_p.reshape(mp, MEM_WIDTH), g_p, merge_w, tm=256)

    ms = bs * t_new
    cs_s = jnp.tile(_rope_table(past + jnp.arange(t_new)), (bs, 1))
    xs2d = x_sample.reshape(ms, D_MODEL)
    tm_s = min(256, ms)
    cqn, ckv_s, kr_s, kcat, _, dt_s, dtt_s, z_s, xbc_s, mq_s, g_s = _in_proj(xs2d, w_proj, cs_s, gq, gkv, bg, tm=tm_s)
    q_s = _q_prep(cqn, wq, wuk_t, cs_s, tm=tm_s, transposed=False)
    q_s = jnp.transpose(q_s.reshape(MLA_HEADS, bs, t_new, QK_DIM), (1, 0, 2, 3)).reshape(bs, MLA_HEADS * t_new, QK_DIM)
    o_lat = _decode_attn(page_table, q_s, kcat.reshape(bs, t_new, QK_DIM).astype(F32), cache_ckv,
                         jnp.swapaxes(cache_krope, 1, 2))
    o_lat = jnp.transpose(o_lat.reshape(bs, MLA_HEADS, t_new, KV_LORA), (1, 0, 2, 3)).reshape(MLA_HEADS, ms, KV_LORA)
    o_mla_s = _uv_proj(o_lat, wuv)

    ypre_s, ec_s, xw_s, c_s, bw_s, conv_rows = _ssd_short(
        xbc_s, jnp.transpose(state_conv.astype(F32), (1, 0, 2)), dt_s, dtt_s, ssd_head + expanders,
        _short_taps_matrix(t_new), _placement_matrix(t_new), seq_len=t_new)
    conv_s = jnp.transpose(conv_rows, (1, 0, 2))
    o_ssm_s, ssm_s = _ssd_state(state_ssm.astype(F32).reshape(bs, D_INNER, D_STATE), xw_s, c_s, bw_s, ypre_s, ec_s,
                                z_s.astype(F32), norm_g, nb=4, seq_len=t_new)

    def cache_rows(c):
        c = c.reshape(bs, N_MEM, MEM_HEADS, MEM_HALVES, 128)
        return jnp.transpose(c, (0, 1, 3, 2, 4)).reshape(bs, N_MEM * MEM_ROW_GROUP, 128)

    mq_rows = jnp.transpose(mq_s.reshape(bs, t_new, MEM_HEADS, MEM_HALVES, 128), (0, 3, 2, 1, 4))
    o_mem_s = _mem_attn_cache(mq_rows.reshape(bs, MEM_ROW_GROUP * t_new, 128), cache_rows(cache_mem_k),
                              cache_rows(cache_mem_v), nb=4, n_tok=t_new)
    o_mem_s = jnp.transpose(o_mem_s.reshape(bs, MEM_HALVES, MEM_HEADS, t_new, 128), (0, 3, 2, 1, 4))
    y_s = _merge_ffn(xs2d, o_mla_s, o_ssm_s.reshape(ms, D_INNER).astype(BF16), o_mem_s.reshape(ms, MEM_WIDTH), g_s,
                     merge_w, tm=tm_s)

    return (y_p.reshape(bp, seq, D_MODEL), y_s.reshape(bs, t_new, D_MODEL),
            ckv_p.reshape(bp, seq, KV_LORA), kr_p.reshape(bp, seq, ROPE_DIM),
            mem_k_p.reshape(bp, N_MEM, MEM_HEADS, MEM_HEAD_DIM), mem_v_p.reshape(bp, N_MEM, MEM_HEADS, MEM_HEAD_DIM),
            ssm_p.reshape(bp, SSM_HEADS, SSM_HEAD_DIM, D_STATE), conv_p,
            ckv_s.reshape(bs, t_new, KV_LORA), kr_s.reshape(bs, t_new, ROPE_DIM),
            ssm_s.reshape(bs, SSM_HEADS, SSM_HEAD_DIM, D_STATE), conv_s)
```

```python
import functools
import math

import jax
import jax.numpy as jnp
import numpy as np
from jax import lax
from jax.experimental import pallas as pl
from jax.experimental.pallas import tpu as pltpu

F32 = jnp.float32
BF16 = jnp.bfloat16

D_MODEL = 1024
MLA_HEADS = 8
Q_LORA = 384
KV_LORA = 256
NOPE_DIM = 128
ROPE_DIM = 64
V_DIM = 128
ROPE_THETA = 10000.0
QK_DIM = KV_LORA + ROPE_DIM
SSM_HEADS = 32
SSM_HEAD_DIM = 64
D_INNER = SSM_HEADS * SSM_HEAD_DIM
SSM_GROUPS = 4
GROUP_W = D_INNER // SSM_GROUPS
D_STATE = 128
CONV_W = 4
CONV_DIM = D_INNER + 2 * SSM_GROUPS * D_STATE
CHUNK = 128
N_MEM = 256
MEM_HEADS = 4
MEM_HEAD_DIM = 256
MEM_WIDTH = MEM_HEADS * MEM_HEAD_DIM
D_FF = 4 * D_MODEL
N_BRANCH = 3
DEPTH = 1
ALPHA = (2 * DEPTH) ** 0.25
LN_EPS = 1e-5
RMS_EPS = 1e-6
PAGE_SIZE = 128
SCORE_SCALE = (NOPE_DIM + ROPE_DIM) ** -0.5
NEG_BIG = -1e30
KV_TILE = 256


def _cp(sem, vmem_mb):
    return pltpu.CompilerParams(dimension_semantics=sem, vmem_limit_bytes=vmem_mb << 20)


def _resident(shape):
    nd = len(shape)
    return pl.BlockSpec(shape, lambda *_: (0,) * nd, pipeline_mode=pl.Buffered(1))


def _dot(a, b):
    return jnp.dot(a, b, preferred_element_type=F32)


def _dot_nt(a, b):
    return lax.dot_general(a, b, (((1,), (1,)), ((), ())), preferred_element_type=F32)


def _rms(v, g, eps):
    return v * lax.rsqrt(jnp.mean(v * v, axis=-1, keepdims=True) + eps) * g


def _layer_norm(v, g, b):
    mu = jnp.mean(v, axis=-1, keepdims=True)
    d = v - mu
    var = jnp.mean(d * d, axis=-1, keepdims=True)
    return d * lax.rsqrt(var + LN_EPS) * g + b


def _in_proj_kernel(x_ref, wcq_ref, wsmall_ref, wz_ref, wxbc_ref, wmq_ref, wg_ref, cs_ref, gq_ref, gkv_ref, bg_ref,
                    cqn_ref, ckv_ref, kr_ref, kcat_ref, ckvt_ref, dt_ref, dtt_ref, z_ref, xbc_ref, mq_ref, g_ref):
    xb = x_ref[...].astype(BF16)
    cqn_ref[...] = _rms(_dot(xb, wcq_ref[...]), gq_ref[...], RMS_EPS).astype(BF16)

    small = _dot(xb, wsmall_ref[...])
    ckv = _rms(small[:, :KV_LORA], gkv_ref[...], RMS_EPS)
    ckv_ref[...] = ckv
    kcat_ref[:, 0:KV_LORA] = ckv.astype(BF16)
    for t in range(ckvt_ref.shape[0]):
        ckvt_ref[t] = ckv[t * KV_TILE:(t + 1) * KV_TILE, :].T.astype(BF16)
    a = small[:, 256:384]
    b = pltpu.roll(a, ROPE_DIM // 2, 1)
    cs = cs_ref[...]
    ro = a * cs[:, :128] + b * cs[:, 128:]
    kr_ref[...] = ro[:, :ROPE_DIM]
    kcat_ref[:, KV_LORA:QK_DIM] = ro[:, :ROPE_DIM].astype(BF16)
    dt = small[:, 384:512]
    dt_ref[...] = dt
    dtt_ref[...] = dt.T[0:SSM_HEADS, :]

    for c in range(0, D_INNER, 1024):
        z_ref[:, c:c + 1024] = _dot(xb, wz_ref[:, c:c + 1024]).astype(BF16)
    for c in range(0, CONV_DIM, 1024):
        xbc_ref[:, c:c + 1024] = _dot(xb, wxbc_ref[:, c:c + 1024]).astype(BF16)
    mq_ref[...] = _dot(xb, wmq_ref[...]).astype(BF16)
    for c in range(0, N_BRANCH * D_MODEL, 1024):
        gr = _dot(xb, wg_ref[:, c:c + 1024]) + bg_ref[:, c:c + 1024]
        g_ref[:, c:c + 1024] = jax.nn.sigmoid(gr).astype(BF16)


def _in_proj(x2d, weights, cs_tab, gq, gkv, bg, tm):
    m = x2d.shape[0]
    ncs = cs_tab.shape[0] // tm

    def row(n, dtype):
        return pl.BlockSpec((tm, n), lambda i: (i, 0)), jax.ShapeDtypeStruct((m, n), dtype)

    outs = [row(Q_LORA, BF16), row(KV_LORA, F32), row(ROPE_DIM, F32), row(QK_DIM, BF16),
            (pl.BlockSpec((tm // KV_TILE, KV_LORA, KV_TILE), lambda i: (i, 0, 0)),
             jax.ShapeDtypeStruct((m // KV_TILE, KV_LORA, KV_TILE), BF16)),
            row(128, F32),
            (pl.BlockSpec((SSM_HEADS, tm), lambda i: (0, i)), jax.ShapeDtypeStruct((SSM_HEADS, m), F32)),
            row(D_INNER, BF16), row(CONV_DIM, BF16), row(MEM_WIDTH, BF16), row(N_BRANCH * D_MODEL, BF16)]
    return pl.pallas_call(
        _in_proj_kernel,
        grid=(m // tm,),
        in_specs=[pl.BlockSpec((tm, D_MODEL), lambda i: (i, 0))] + [_resident(w.shape) for w in weights]
                 + [pl.BlockSpec((tm, 256), lambda i: (i % ncs, 0)),
                    _resident(gq.shape), _resident(gkv.shape), _resident(bg.shape)],
        out_specs=[o[0] for o in outs],
        out_shape=[o[1] for o in outs],
        compiler_params=_cp(("arbitrary",), 56),
        name="in_proj",
    )(x2d, *weights, cs_tab, gq, gkv, bg)


def _q_prep_kernel(c_ref, wq_ref, wuk_ref, cs_ref, q_ref, *, transposed):
    c = c_ref[...]
    cs = cs_ref[...]
    qn = [_dot(c, wq_ref[:, h * NOPE_DIM:(h + 1) * NOPE_DIM]).astype(BF16) for h in range(MLA_HEADS)]
    for h in range(MLA_HEADS):
        ql = _dot(qn[h], wuk_ref[h]) * SCORE_SCALE
        if transposed:
            q_ref[h, 0:KV_LORA, :] = ql.T.astype(BF16)
        else:
            q_ref[h, :, 0:KV_LORA] = ql.astype(BF16)
    r0 = MLA_HEADS * NOPE_DIM
    for p in range(MLA_HEADS // 2):
        ab = _dot(c, wq_ref[:, r0 + p * 256:r0 + (p + 1) * 256])
        ro = (ab[:, :128] * cs[:, :128] + ab[:, 128:] * cs[:, 128:]) * SCORE_SCALE
        if transposed:
            rot = ro.T.astype(BF16)
            q_ref[2 * p, KV_LORA:QK_DIM, :] = rot[:ROPE_DIM]
            q_ref[2 * p + 1, KV_LORA:QK_DIM, :] = rot[ROPE_DIM:]
        else:
            q_ref[2 * p, :, KV_LORA:QK_DIM] = ro[:, :ROPE_DIM].astype(BF16)
            q_ref[2 * p + 1, :, KV_LORA:QK_DIM] = pltpu.roll(ro, ROPE_DIM, 1)[:, :ROPE_DIM].astype(BF16)


def _q_prep(cqn, wq, wuk_t, cs_tab, tm, transposed):
    m = cqn.shape[0]
    ncs = cs_tab.shape[0] // tm
    if transposed:
        out_spec = pl.BlockSpec((MLA_HEADS, QK_DIM, tm), lambda i: (0, 0, i))
        out_shape = jax.ShapeDtypeStruct((MLA_HEADS, QK_DIM, m), BF16)
    else:
        out_spec = pl.BlockSpec((MLA_HEADS, tm, QK_DIM), lambda i: (0, i, 0))
        out_shape = jax.ShapeDtypeStruct((MLA_HEADS, m, QK_DIM), BF16)
    return pl.pallas_call(
        functools.partial(_q_prep_kernel, transposed=transposed),
        grid=(m // tm,),
        in_specs=[pl.BlockSpec((tm, Q_LORA), lambda i: (i, 0)), _resident(wq.shape), _resident(wuk_t.shape),
                  pl.BlockSpec((tm, 256), lambda i: (i % ncs, 0))],
        out_specs=out_spec,
        out_shape=out_shape,
        compiler_params=_cp(("arbitrary",), 32),
        name="q_prep",
    )(cqn, wq, wuk_t, cs_tab)


def _prompt_attn_kernel(qt_ref, k_ref, vt_ref, wuvt_ref, o_ref, m_sc, l_sc, acc_sc):
    i = pl.program_id(1)
    t = KV_TILE
    m_sc[...] = jnp.full(m_sc.shape, NEG_BIG, F32)
    l_sc[...] = jnp.zeros(l_sc.shape, F32)
    acc_sc[...] = jnp.zeros(acc_sc.shape, F32)

    def kv_block(j, masked):
        k = k_ref[pl.ds(pl.multiple_of(j * t, t), t), :]
        kc = k[:, :KV_LORA]
        kr = k[:, KV_LORA:]
        vt = vt_ref[j]
        if masked:
            keep = lax.broadcasted_iota(jnp.int32, (t, t), 0) <= lax.broadcasted_iota(jnp.int32, (t, t), 1)
        def scores(h):
            return _dot(kc, qt_ref[h, 0:KV_LORA, :]) + _dot(kr, qt_ref[h, KV_LORA:QK_DIM, :])

        pending = [scores(h) for h in range(MLA_HEADS)]
        for h in range(MLA_HEADS):
            st = pending[h]
            if masked:
                st = jnp.where(keep, st, NEG_BIG)
            m_prev = m_sc[h:h + 1, :]
            m_new = jnp.maximum(m_prev, jnp.max(st, axis=0, keepdims=True))
            alpha = jnp.exp(m_prev - m_new)
            p = jnp.exp(st - m_new)
            l_sc[h:h + 1, :] = alpha * l_sc[h:h + 1, :] + jnp.sum(p, axis=0, keepdims=True)
            acc_sc[h] = alpha * acc_sc[h] + _dot(vt, p.astype(BF16))
            m_sc[h:h + 1, :] = m_new

    def body(j, carry):
        kv_block(j, False)
        return carry

    lax.fori_loop(0, i, body, 0)
    kv_block(i, True)

    for h in range(MLA_HEADS):
        ot = (acc_sc[h] / l_sc[h:h + 1, :]).astype(BF16)
        o_ref[:, h * V_DIM:(h + 1) * V_DIM] = _dot(wuvt_ref[h], ot).T.astype(BF16)


def _prompt_attn(qt, kcat, ckvt, wuvt, batch, seq):
    t = KV_TILE
    nq = seq // t
    return pl.pallas_call(
        _prompt_attn_kernel,
        grid=(batch, nq),
        in_specs=[pl.BlockSpec((MLA_HEADS, QK_DIM, t), lambda b, i: (0, 0, b * nq + i)),
                  pl.BlockSpec((seq, QK_DIM), lambda b, i: (b, 0)),
                  pl.BlockSpec((nq, KV_LORA, t), lambda b, i: (b, 0, 0)),
                  _resident(wuvt.shape)],
        out_specs=pl.BlockSpec((t, MLA_HEADS * V_DIM), lambda b, i: (b * nq + i, 0)),
        out_shape=jax.ShapeDtypeStruct((batch * seq, MLA_HEADS * V_DIM), BF16),
        scratch_shapes=[pltpu.VMEM((MLA_HEADS, t), F32), pltpu.VMEM((MLA_HEADS, t), F32),
                        pltpu.VMEM((MLA_HEADS, KV_LORA, t), F32)],
        compiler_params=_cp(("arbitrary", "arbitrary"), 32),
        name="prompt_attn",
    )(qt, kcat, ckvt, wuvt)


def _decode_attn_kernel(pt_ref, q_ref, kn_ref, ckv_hbm, krt_hbm, o_ref, kbuf, rbuf, kb_sc, s_sc, sem, *, n_pages, n_new):
    b = pl.program_id(0)
    nb = pl.num_programs(0)
    slot = b % 2

    def page_copies(page, p, sl):
        rows = pl.ds(pl.multiple_of(p * PAGE_SIZE, PAGE_SIZE), PAGE_SIZE)
        return (pltpu.make_async_copy(ckv_hbm.at[page], kbuf.at[sl, rows, :], sem.at[0, sl]),
                pltpu.make_async_copy(krt_hbm.at[page], rbuf.at[sl, p], sem.at[1, sl]))

    def issue(bi, sl):
        def body(p, carry):
            for cp in page_copies(pt_ref[bi * n_pages + p], p, sl):
                cp.start()
            return carry
        lax.fori_loop(0, n_pages, body, 0, unroll=8)

    @pl.when(b == 0)
    def _():
        issue(0, 0)

    @pl.when(b + 1 < nb)
    def _():
        issue(b + 1, 1 - slot)

    def wait_body(p, carry):
        for cp in page_copies(0, p, slot):
            cp.wait()
        return carry
    lax.fori_loop(0, n_pages, wait_body, 0, unroll=8)

    q = q_ref[0]
    rows = q.shape[0]
    qc = q[:, :KV_LORA]
    qr = q[:, KV_LORA:]
    qf = q.astype(F32)
    kn = kn_ref[0]
    tok = lax.broadcasted_iota(jnp.int32, (rows, 1), 0) % n_new

    s_new = []
    for j in range(n_new):
        sj = jnp.sum(qf * kn[j:j + 1, :], axis=1, keepdims=True)
        s_new.append(jnp.where(tok >= j, sj, NEG_BIG))
    m0 = s_new[0]
    for j in range(1, n_new):
        m0 = jnp.maximum(m0, s_new[j])

    for c in range(n_pages // 2):
        r = slice(2 * c * PAGE_SIZE, 2 * (c + 1) * PAGE_SIZE)
        kc = kbuf[slot, r, :].astype(BF16)
        kb_sc[r, :] = kc
        rt = jnp.concatenate([rbuf[slot, 2 * c], rbuf[slot, 2 * c + 1]], axis=1).astype(BF16)
        s_sc[:, r] = _dot_nt(qc, kc) + _dot(qr, rt)
    s = s_sc[...]
    m = jnp.maximum(m0, jnp.max(s, axis=1, keepdims=True))
    p = jnp.exp(s - m)
    l = jnp.sum(p, axis=1, keepdims=True)
    acc = _dot(p.astype(BF16), kb_sc[...])
    for j in range(n_new):
        pj = jnp.exp(s_new[j] - m)
        l = l + pj
        acc = acc + pj * kn[j:j + 1, :KV_LORA]
    o_ref[0] = (acc / l).astype(BF16)


def _decode_attn(page_table, q, k_new, cache_ckv, cache_krope_t):
    nb, n_pages = page_table.shape
    rows = q.shape[1]
    n_new = k_new.shape[1]
    past = n_pages * PAGE_SIZE
    kern = functools.partial(_decode_attn_kernel, n_pages=n_pages, n_new=n_new)
    grid_spec = pltpu.PrefetchScalarGridSpec(
        num_scalar_prefetch=1,
        grid=(nb,),
        in_specs=[pl.BlockSpec((1, rows, QK_DIM), lambda b, pt: (b, 0, 0)),
                  pl.BlockSpec((1, n_new, QK_DIM), lambda b, pt: (b, 0, 0)),
                  pl.BlockSpec(memory_space=pl.ANY),
                  pl.BlockSpec(memory_space=pl.ANY)],
        out_specs=pl.BlockSpec((1, rows, KV_LORA), lambda b, pt: (b, 0, 0)),
        scratch_shapes=[pltpu.VMEM((2, past, KV_LORA), F32), pltpu.VMEM((2, n_pages, ROPE_DIM, PAGE_SIZE), F32),
                        pltpu.VMEM((past, KV_LORA), BF16), pltpu.VMEM((rows, past), F32),
                        pltpu.SemaphoreType.DMA((2, 2))],
    )
    return pl.pallas_call(
        kern,
        grid_spec=grid_spec,
        out_shape=jax.ShapeDtypeStruct((nb, rows, KV_LORA), BF16),
        compiler_params=_cp(("arbitrary",), 40),
        name="decode_attn",
    )(page_table.reshape(-1), q, k_new, cache_ckv, cache_krope_t)


def _uv_proj_kernel(o_ref, w_ref, out_ref):
    out_ref[...] = _dot(o_ref[0], w_ref[0]).astype(BF16)


def _uv_proj(o_lat, wuv):
    m = o_lat.shape[1]
    return pl.pallas_call(
        _uv_proj_kernel,
        grid=(MLA_HEADS,),
        in_specs=[pl.BlockSpec((1, m, KV_LORA), lambda h: (h, 0, 0)),
                  pl.BlockSpec((1, KV_LORA, V_DIM), lambda h: (h, 0, 0))],
        out_specs=pl.BlockSpec((m, V_DIM), lambda h: (0, h)),
        out_shape=jax.ShapeDtypeStruct((m, MLA_HEADS * V_DIM), BF16),
        compiler_params=_cp(("arbitrary",), 16),
        name="uv_proj",
    )(o_lat, wuv)


def _split_dot(v, e, terms, left=True):
    parts = []
    r = v
    for _ in range(terms):
        hi = r.astype(BF16)
        parts.append(hi)
        r = r - hi.astype(F32)
    if left:
        return _dot(jnp.concatenate(parts, axis=1), e)
    return _dot(e, jnp.concatenate(parts, axis=0))


def _conv_silu(taps, cw_ref, cb_ref):
    conv = cb_ref[...] + cw_ref[0:1, :] * taps[0]
    for k in range(1, CONV_W):
        conv = conv + cw_ref[k:k + 1, :] * taps[k]
    xc = conv * jax.nn.sigmoid(conv)
    n_bc = SSM_GROUPS * D_STATE
    return xc[:, :D_INNER], xc[:, D_INNER:D_INNER + n_bc], xc[:, D_INNER + n_bc:]


def _select_rows(select, rows_ref):
    n = rows_ref.shape[1]
    return jnp.concatenate([_dot(select, rows_ref[:, c:c + 1024]) for c in range(0, n, 1024)], axis=1)


def _ssd_sums(dt, dtt, keep, keep_t, alog_ref, alogt_ref):
    tri = jnp.where(keep, 1.0, 0.0).astype(BF16)
    tri_t = jnp.where(keep_t, 1.0, 0.0).astype(BF16)
    da = dt * (-jnp.exp(alog_ref[...]))
    dat = dtt * (-jnp.exp(alogt_ref[...]))
    cum = _split_dot(da, jnp.concatenate([tri] * 3, axis=1), 3, left=False)
    cum_t = _split_dot(dat, jnp.concatenate([tri_t] * 3, axis=0), 3)
    return da, cum, cum_t


def _ssd_intra(xs, bm, cm, cum_e, cum_t, dtt, keep, dskip_ref, y_sc, carried=None):
    pairs_per_group = SSM_HEADS // 2 // SSM_GROUPS
    lane = lax.broadcasted_iota(jnp.int32, (CHUNK, CHUNK), 1)
    for g in range(SSM_GROUPS):
        cg = cm[:, g * D_STATE:(g + 1) * D_STATE].astype(BF16)
        cb = _dot_nt(cg, bm[:, g * D_STATE:(g + 1) * D_STATE].astype(BF16))
        extra = None if carried is None else carried(g, cg)
        for pr in range(pairs_per_group):
            q = g * pairs_per_group + pr
            ws = []
            for h in (2 * q, 2 * q + 1):
                seg = cum_e[:, h * 128:(h + 1) * 128] - cum_t[h:h + 1, :]
                decay = jnp.exp(jnp.where(keep, seg, NEG_BIG)) * dtt[h:h + 1, :]
                ws.append((cb * decay).astype(BF16))
            w_pair = jnp.concatenate(ws, axis=1)
            xp = xs[:, q * 128:(q + 1) * 128]
            x_bd = jnp.concatenate([jnp.where(lane < SSM_HEAD_DIM, xp, 0.0).astype(BF16),
                                    jnp.where(lane >= SSM_HEAD_DIM, xp, 0.0).astype(BF16)], axis=0)
            y = _dot(w_pair, x_bd) + dskip_ref[:, q * 128:(q + 1) * 128] * xp
            if extra is not None:
                y = y + extra[:, pr * 128:(pr + 1) * 128]
            y_sc[:, q * 128:(q + 1) * 128] = y


CARRY = 16


def _ssd_kernel(xbc_ref, z_ref, dt_ref, dtt_ref, cw_ref, cb_ref, dtb_ref, dtbt_ref,
                alog_ref, alogt_ref, dskip_ref, ng_ref, e64_ref, e128_ref, shift_ref,
                y_ref, sout_ref, cout_ref, xe_sc, tail_sc, st_sc, y_sc):
    c = pl.program_id(1)
    n_pairs = SSM_HEADS // 2
    pairs_per_group = n_pairs // SSM_GROUPS

    @pl.when(c == 0)
    def _():
        xe_sc[0:CARRY, :] = jnp.zeros((CARRY, CONV_DIM), BF16)
        st_sc[...] = jnp.zeros(st_sc.shape, F32)

    xe_sc[CARRY:CARRY + CHUNK, :] = xbc_ref[...]
    shifted = _select_rows(shift_ref[...], xe_sc)
    taps = [shifted[k * CHUNK:(k + 1) * CHUNK] for k in range(CONV_W - 1)] + [xbc_ref[...].astype(F32)]
    xs, bm, cm = _conv_silu(taps, cw_ref, cb_ref)
    xe_sc[0:CARRY, :] = xe_sc[CHUNK:CHUNK + CARRY, :]

    rowi = lax.broadcasted_iota(jnp.int32, (CHUNK, CHUNK), 0)
    coli = lax.broadcasted_iota(jnp.int32, (CHUNK, CHUNK), 1)
    keep = coli <= rowi
    dt = jax.nn.softplus(dt_ref[...] + dtb_ref[...])
    dtt = jax.nn.softplus(dtt_ref[...] + dtbt_ref[...])
    _, cum, cum_t = _ssd_sums(dt, dtt, keep, rowi <= coli, alog_ref, alogt_ref)
    last = cum[CHUNK - 1:CHUNK, :]
    e64 = e64_ref[...]
    ec_e = _split_dot(jnp.exp(cum), e64, 2)
    wb_e = _split_dot(dt * jnp.exp(last - cum), e64, 2)
    dec_e = _split_dot(jnp.broadcast_to(jnp.exp(last), (16, CHUNK)), e64, 2)[0:1, :]
    cum_e = _split_dot(cum, e128_ref[...], 3)

    def carried(g, cg):
        return _dot(cg, st_sc[g].astype(BF16)) * ec_e[:, g * GROUP_W:(g + 1) * GROUP_W]

    _ssd_intra(xs, bm, cm, cum_e, cum_t, dtt, keep, dskip_ref, y_sc, carried)
    for g in range(SSM_GROUPS):
        gs = slice(g * GROUP_W, (g + 1) * GROUP_W)
        xw = (xs[:, gs] * wb_e[:, gs]).astype(BF16)
        bt = bm[:, g * D_STATE:(g + 1) * D_STATE].T.astype(BF16)
        st_sc[g] = st_sc[g] * dec_e[:, gs] + _dot(bt, xw)

    for g in range(SSM_GROUPS):
        gs = slice(g * GROUP_W, (g + 1) * GROUP_W)
        zf = z_ref[:, gs].astype(F32)
        yv = y_sc[:, gs] * (zf * jax.nn.sigmoid(zf))
        y_ref[:, gs] = _rms(yv, ng_ref[:, gs], RMS_EPS).astype(y_ref.dtype)

    @pl.when(c == pl.num_programs(1) - 1)
    def _():
        tail_sc[...] = xbc_ref[CHUNK - CARRY:CHUNK, :].astype(F32)
        cout_ref[0] = tail_sc[CARRY - (CONV_W - 1):CARRY, :]
        for q in range(n_pairs):
            g, pr = divmod(q, pairs_per_group)
            sout_ref[0, q * 128:(q + 1) * 128, :] = st_sc[g, :, pr * 128:(pr + 1) * 128].T


def _ssd(xbc, z, dt, dtt, consts, batch, n_chunks):
    def step(n):
        return pl.BlockSpec((CHUNK, n), lambda b, c: (b * n_chunks + c, 0))

    def per_batch(r, n):
        return pl.BlockSpec((1, r, n), lambda b, c: (b, 0, 0))

    return pl.pallas_call(
        _ssd_kernel,
        grid=(batch, n_chunks),
        in_specs=[step(CONV_DIM), step(D_INNER), step(128),
                  pl.BlockSpec((SSM_HEADS, CHUNK), lambda b, c: (0, b * n_chunks + c))]
                 + [_resident(a.shape) for a in consts],
        out_specs=[step(D_INNER), per_batch(D_INNER, D_STATE), per_batch(CONV_W - 1, CONV_DIM)],
        out_shape=[jax.ShapeDtypeStruct((batch * n_chunks * CHUNK, D_INNER), BF16),
                   jax.ShapeDtypeStruct((batch, D_INNER, D_STATE), F32),
                   jax.ShapeDtypeStruct((batch, CONV_W - 1, CONV_DIM), F32)],
        scratch_shapes=[pltpu.VMEM((CARRY + CHUNK, CONV_DIM), BF16),
                        pltpu.VMEM((CARRY, CONV_DIM), F32),
                        pltpu.VMEM((SSM_GROUPS, D_STATE, GROUP_W), F32),
                        pltpu.VMEM((CHUNK, D_INNER), F32)],
        compiler_params=_cp(("arbitrary", "arbitrary"), 48),
        name="ssd",
    )(xbc, z, dt, dtt, *consts)


SLOTS = 16


def _ssd_short_kernel(x_ref, sc_ref, dt_ref, dtt_ref, cw_ref, cb_ref, dtb_ref, dtbt_ref, alog_ref, alogt_ref,
                      dskip_ref, e64_ref, e128_ref, taps_sel_ref, place_ref,
                      ypre_ref, ec_ref, xw_ref, c_ref, bw_ref, tails_ref, rows_sc, y_sc, *, seq_len):
    n_cached = sc_ref.shape[0] * sc_ref.shape[1]
    rows_sc[0:CHUNK, :] = x_ref[...]
    r = sc_ref[...].reshape(n_cached, CONV_DIM)
    for t in range(3):
        hi = r.astype(BF16)
        rows_sc[CHUNK + t * n_cached:CHUNK + (t + 1) * n_cached, :] = hi
        r = r - hi.astype(F32)
    picked = _select_rows(taps_sel_ref[...], rows_sc)
    taps = [picked[k * CHUNK:(k + 1) * CHUNK] for k in range(CONV_W - 1)] + [x_ref[...].astype(F32)]
    tails_ref[...] = picked[(CONV_W - 1) * CHUNK:].reshape(tails_ref.shape)
    xs, bm, cm = _conv_silu(taps, cw_ref, cb_ref)
    rowi = lax.broadcasted_iota(jnp.int32, (CHUNK, CHUNK), 0)
    coli = lax.broadcasted_iota(jnp.int32, (CHUNK, CHUNK), 1)
    same = (rowi // seq_len) == (coli // seq_len)
    keep = same & (coli <= rowi)
    dt = jax.nn.softplus(dt_ref[...] + dtb_ref[...])
    dtt = jax.nn.softplus(dtt_ref[...] + dtbt_ref[...])
    da, cum, cum_t = _ssd_sums(dt, dtt, keep, same & (rowi <= coli), alog_ref, alogt_ref)
    whole = jnp.where(same, 1.0, 0.0).astype(BF16)
    last = _split_dot(da, jnp.concatenate([whole] * 3, axis=1), 3, left=False)
    e64 = e64_ref[...]
    ec_ref[...] = _split_dot(jnp.exp(cum), e64, 2)
    wb_e = _split_dot(dt * jnp.exp(last - cum), e64, 2)
    dec_e = _split_dot(jnp.exp(last), e64, 2)
    cum_e = _split_dot(cum, e128_ref[...], 3)
    _ssd_intra(xs, bm, cm, cum_e, cum_t, dtt, keep, dskip_ref, y_sc)
    ypre_ref[...] = y_sc[...]

    place = place_ref[...]
    dec_hi = dec_e.astype(BF16)
    dec_lo = (dec_e - dec_hi.astype(F32)).astype(BF16)
    xw = (xs * wb_e).astype(BF16)
    xw_ref[...] = _dot(place, jnp.concatenate([xw, dec_hi, dec_lo], axis=0)).astype(BF16)
    tokens = place[:, :CHUNK]
    c_ref[...] = _dot(tokens, cm.astype(BF16)).astype(BF16)
    zeros = jnp.zeros((CHUNK, D_STATE), BF16)
    b_wide = jnp.concatenate(
        [piece for g in range(SSM_GROUPS) for piece in (bm[:, g * D_STATE:(g + 1) * D_STATE].astype(BF16), zeros)], axis=1)
    n_rows = place.shape[0]
    slot = lax.broadcasted_iota(jnp.int32, (n_rows, 2 * D_STATE * SSM_GROUPS), 0) % SLOTS
    col = lax.broadcasted_iota(jnp.int32, (n_rows, 2 * D_STATE * SSM_GROUPS), 1) % (2 * D_STATE)
    ones = ((slot == seq_len) | (slot == seq_len + 1)) & (col >= D_STATE)
    bw_ref[...] = jnp.where(ones, 1.0, _dot(tokens, b_wide)).astype(BF16)


def _ssd_short(xbc, conv_rows, dt, dtt, consts, taps_sel, place, seq_len):
    rows = dt.shape[0]
    n_chunks = rows // CHUNK
    n_seq = CHUNK // seq_len
    n_slot_rows = place.shape[0]
    kern = functools.partial(_ssd_short_kernel, seq_len=seq_len)

    def out(r, n, dtype):
        return pl.BlockSpec((r, n), lambda c: (c, 0)), jax.ShapeDtypeStruct((n_chunks * r, n), dtype)

    conv_spec = pl.BlockSpec((CONV_W - 1, n_seq, CONV_DIM), lambda c: (0, c, 0))
    outs = [out(CHUNK, D_INNER, F32), out(CHUNK, D_INNER, F32), out(n_slot_rows, D_INNER, BF16),
            out(n_slot_rows, SSM_GROUPS * D_STATE, BF16), out(n_slot_rows, 2 * SSM_GROUPS * D_STATE, BF16),
            (conv_spec, jax.ShapeDtypeStruct(conv_rows.shape, F32))]
    return pl.pallas_call(
        kern,
        grid=(n_chunks,),
        in_specs=[pl.BlockSpec((CHUNK, CONV_DIM), lambda c: (c, 0)), conv_spec,
                  pl.BlockSpec((CHUNK, 128), lambda c: (c, 0)),
                  pl.BlockSpec((SSM_HEADS, CHUNK), lambda c: (0, c))]
                 + [_resident(a.shape) for a in consts] + [_resident(taps_sel.shape), _resident(place.shape)],
        out_specs=[o[0] for o in outs],
        out_shape=[o[1] for o in outs],
        scratch_shapes=[pltpu.VMEM((taps_sel.shape[1], CONV_DIM), BF16), pltpu.VMEM((CHUNK, D_INNER), F32)],
        compiler_params=_cp(("arbitrary",), 48),
        name="ssd_short",
    )(xbc, conv_rows, dt, dtt, *consts, taps_sel, place)


def _ssd_state_kernel(s0_ref, xw_ref, c_ref, bw_ref, ypre_ref, ec_ref, z_ref, ng_ref, y_ref, sout_ref, ci_sc,
                      *, nb, seq_len):
    for bb in range(nb):
        slots = slice(bb * SLOTS, (bb + 1) * SLOTS)
        toks = slice(bb * seq_len, (bb + 1) * seq_len)
        for g in range(SSM_GROUPS):
            gs = slice(g * GROUP_W, (g + 1) * GROUP_W)
            s0g = s0_ref[bb, gs, :]
            ci_sc[...] = _dot_nt(c_ref[slots, g * D_STATE:(g + 1) * D_STATE], s0g.astype(BF16))
            u = lax.dot_general(xw_ref[slots, gs], bw_ref[slots, 2 * g * D_STATE:2 * (g + 1) * D_STATE],
                                (((0,), (0,)), ((), ())), preferred_element_type=F32)
            sout_ref[bb, gs, :] = u[:, :D_STATE] + s0g * u[:, D_STATE:]
            zf = z_ref[toks, gs]
            yv = (ypre_ref[toks, gs] + ec_ref[toks, gs] * ci_sc[0:seq_len, :]) * (zf * jax.nn.sigmoid(zf))
            y_ref[toks, gs] = _rms(yv, ng_ref[:, gs], RMS_EPS)


def _ssd_state(s0, xw, cmat, bw, ypre, ec, z, ng, nb, seq_len):
    batch = s0.shape[0]
    kern = functools.partial(_ssd_state_kernel, nb=nb, seq_len=seq_len)

    def rows(r, n):
        return pl.BlockSpec((nb * r, n), lambda b: (b, 0))

    state_spec = pl.BlockSpec((nb, D_INNER, D_STATE), lambda b: (b, 0, 0))
    return pl.pallas_call(
        kern,
        grid=(batch // nb,),
        in_specs=[state_spec, rows(SLOTS, D_INNER), rows(SLOTS, SSM_GROUPS * D_STATE),
                  rows(SLOTS, 2 * SSM_GROUPS * D_STATE), rows(seq_len, D_INNER), rows(seq_len, D_INNER),
                  rows(seq_len, D_INNER), _resident(ng.shape)],
        out_specs=[rows(seq_len, D_INNER), state_spec],
        out_shape=[jax.ShapeDtypeStruct((batch * seq_len, D_INNER), F32),
                   jax.ShapeDtypeStruct((batch, D_INNER, D_STATE), F32)],
        scratch_shapes=[pltpu.VMEM((SLOTS, GROUP_W), F32)],
        compiler_params=_cp(("arbitrary",), 40),
        name="ssd_state",
    )(s0, xw, cmat, bw, ypre, ec, z, ng)


def _mem_kv_kernel(m_ref, w_ref, k_ref, v_ref):
    mb = m_ref[...].astype(BF16)
    k_ref[...] = _dot(mb, w_ref[:, :MEM_WIDTH])
    v_ref[...] = _dot(mb, w_ref[:, MEM_WIDTH:])


def _mem_kv(mem2d, w_kv, tm):
    m = mem2d.shape[0]
    return pl.pallas_call(
        _mem_kv_kernel,
        grid=(m // tm,),
        in_specs=[pl.BlockSpec((tm, D_MODEL), lambda i: (i, 0)), _resident(w_kv.shape)],
        out_specs=[pl.BlockSpec((tm, MEM_WIDTH), lambda i: (i, 0))] * 2,
        out_shape=[jax.ShapeDtypeStruct((m, MEM_WIDTH), F32)] * 2,
        compiler_params=_cp(("arbitrary",), 32),
        name="mem_kv",
    )(mem2d, w_kv)


def _mem_attn_kernel(q_ref, k_ref, v_ref, o_ref, *, nb):
    scale = MEM_HEAD_DIM ** -0.5
    for bb in range(nb):
        heads = [slice(h * MEM_HEAD_DIM, (h + 1) * MEM_HEAD_DIM) for h in range(MEM_HEADS)]
        scores = [_dot_nt(q_ref[bb, :, hs], k_ref[bb, :, hs].astype(BF16)) for hs in heads]
        for h, hs in enumerate(heads):
            vh = v_ref[bb, :, hs].astype(BF16)
            s = scores[h] * scale
            p = jnp.exp(s - jnp.max(s, axis=1, keepdims=True))
            l = jnp.sum(p, axis=1, keepdims=True)
            o_ref[bb, :, hs] = (_dot(p.astype(BF16), vh) / l).astype(BF16)


def _mem_attn(mq, mem_k, mem_v, nb, tq):
    batch, rows, _ = mq.shape
    kern = functools.partial(_mem_attn_kernel, nb=nb)
    kv_spec = pl.BlockSpec((nb, N_MEM, MEM_WIDTH), lambda b, i: (b, 0, 0))
    return pl.pallas_call(
        kern,
        grid=(batch // nb, rows // tq),
        in_specs=[pl.BlockSpec((nb, tq, MEM_WIDTH), lambda b, i: (b, i, 0)), kv_spec, kv_spec],
        out_specs=pl.BlockSpec((nb, tq, MEM_WIDTH), lambda b, i: (b, i, 0)),
        out_shape=jax.ShapeDtypeStruct((batch, rows, MEM_WIDTH), BF16),
        compiler_params=_cp(("arbitrary", "arbitrary"), 40),
        name="mem_attn",
    )(mq, mem_k, mem_v)


MEM_HALVES = MEM_HEAD_DIM // 128
MEM_ROW_GROUP = MEM_HALVES * MEM_HEADS


def _mem_attn_cache_kernel(q_ref, k_ref, v_ref, o_ref, *, nb, n_tok):
    rows = MEM_HEADS * n_tok
    n_col = N_MEM * MEM_ROW_GROUP
    col = lax.broadcasted_iota(jnp.int32, (rows, n_col), 1)
    row = lax.broadcasted_iota(jnp.int32, (rows, n_col), 0)
    own = (col % MEM_ROW_GROUP) == (row // n_tok)
    scale = MEM_HEAD_DIM ** -0.5
    for bb in range(nb):
        g = _dot_nt(q_ref[bb], k_ref[bb].astype(BF16))
        s = (g[:rows] + pltpu.roll(g[rows:], n_col - MEM_HEADS, 1)) * scale
        s = jnp.where(own, s, NEG_BIG)
        p = jnp.exp(s - jnp.max(s, axis=1, keepdims=True))
        l = jnp.sum(p, axis=1, keepdims=True)
        p2 = jnp.concatenate([p, pltpu.roll(p, MEM_HEADS, 1)], axis=0).astype(BF16)
        o = _dot(p2, v_ref[bb].astype(BF16))
        o_ref[bb] = (o / jnp.concatenate([l, l], axis=0)).astype(BF16)


def _mem_attn_cache(q, mem_k, mem_v, nb, n_tok):
    batch, rows, _ = q.shape
    kern = functools.partial(_mem_attn_cache_kernel, nb=nb, n_tok=n_tok)
    kv_spec = pl.BlockSpec((nb,) + mem_k.shape[1:], lambda b: (b, 0, 0))
    q_spec = pl.BlockSpec((nb, rows, 128), lambda b: (b, 0, 0))
    return pl.pallas_call(
        kern,
        grid=(batch // nb,),
        in_specs=[q_spec, kv_spec, kv_spec],
        out_specs=q_spec,
        out_shape=jax.ShapeDtypeStruct((batch, rows, 128), BF16),
        compiler_params=_cp(("arbitrary",), 40),
        name="mem_attn_cache",
    )(q, mem_k, mem_v)


def _merge_ffn_kernel(x_ref, oa_ref, ob_ref, om_ref, g_ref, woa_ref, wob_ref, wom_ref, wout_ref,
                      ln1g_ref, ln1b_ref, wup_ref, bup_ref, wdown_ref, bdown_ref, ln2g_ref, ln2b_ref, y_ref):
    x = x_ref[...]
    m = g_ref[:, 0:D_MODEL].astype(F32) * _dot(oa_ref[...], woa_ref[...])
    m = m + g_ref[:, D_MODEL:2 * D_MODEL].astype(F32) * _dot(ob_ref[...], wob_ref[...])
    m = m + g_ref[:, 2 * D_MODEL:].astype(F32) * _dot(om_ref[...], wom_ref[...])
    x1 = _layer_norm(ALPHA * x + _dot(m.astype(BF16), wout_ref[...]), ln1g_ref[...], ln1b_ref[...])
    h = jnp.maximum(_dot(x1.astype(BF16), wup_ref[...]) + bup_ref[...], 0.0)
    h = (h * h).astype(BF16)
    y = ALPHA * x1 + _dot(h, wdown_ref[...]) + bdown_ref[...]
    y_ref[...] = _layer_norm(y, ln2g_ref[...], ln2b_ref[...])


def _merge_ffn(x2d, o_mla, o_ssm, o_mem, g, weights, tm):
    m = x2d.shape[0]

    def row(n):
        return pl.BlockSpec((tm, n), lambda i: (i, 0))

    return pl.pallas_call(
        _merge_ffn_kernel,
        grid=(m // tm,),
        in_specs=[row(D_MODEL), row(MLA_HEADS * V_DIM), row(D_INNER), row(MEM_WIDTH), row(N_BRANCH * D_MODEL)]
                 + [_resident(w.shape) for w in weights],
        out_specs=row(D_MODEL),
        out_shape=jax.ShapeDtypeStruct((m, D_MODEL), F32),
        compiler_params=_cp(("arbitrary",), 56),
        name="merge_ffn",
    )(x2d, o_mla, o_ssm, o_mem, g, *weights)


def _rope_table(pos):
    half = ROPE_DIM // 2
    inv = ROPE_THETA ** (-jnp.arange(half, dtype=F32) / half)
    ang = pos.astype(F32)[:, None] * inv[None, :]
    cos, sin = jnp.cos(ang), jnp.sin(ang)
    return jnp.concatenate([cos, cos, cos, cos, -sin, sin, -sin, sin], axis=1)


def _expand_matrix(width, terms):
    src = np.arange(SSM_HEADS * width) // width
    one = (np.arange(128)[:, None] == src[None, :]).astype(np.float32)
    return jnp.asarray(np.concatenate([one] * terms, axis=0), dtype=BF16)


def _shift_matrix():
    sel = np.zeros(((CONV_W - 1) * CHUNK, CARRY + CHUNK), np.float32)
    for k in range(CONV_W - 1):
        for t in range(CHUNK):
            sel[k * CHUNK + t, CARRY + t + k - (CONV_W - 1)] = 1.0
    return jnp.asarray(sel, dtype=BF16)


def _short_taps_matrix(seq_len):
    n_seq = CHUNK // seq_len
    n_old = CONV_W - 1
    n_cached = n_old * n_seq
    sel = np.zeros((n_old * CHUNK + n_cached, CHUNK + 3 * n_cached), np.float32)

    def take(out_row, s, m):
        if m >= n_old:
            sel[out_row, s * seq_len + m - n_old] = 1.0
        else:
            for term in range(3):
                sel[out_row, CHUNK + term * n_cached + m * n_seq + s] = 1.0

    for s in range(n_seq):
        for k in range(n_old):
            for i in range(seq_len):
                take(k * CHUNK + s * seq_len + i, s, i + k)
        for j in range(n_old):
            take(n_old * CHUNK + j * n_seq + s, s, seq_len + j)
    return jnp.asarray(sel, dtype=BF16)


def _placement_matrix(seq_len):
    n_seq = CHUNK // seq_len
    place = np.zeros((n_seq * SLOTS, 3 * CHUNK), np.float32)
    for s in range(n_seq):
        for j in range(seq_len):
            place[s * SLOTS + j, s * seq_len + j] = 1.0
        place[s * SLOTS + seq_len, CHUNK + s * seq_len] = 1.0
        place[s * SLOTS + seq_len + 1, 2 * CHUNK + s * seq_len] = 1.0
    return jnp.asarray(place, dtype=BF16)


def _row(v):
    return v.reshape(1, -1).astype(F32)


def kernel(x_prompt, x_sample, mem_prompt, cache_ckv, cache_krope, page_table, cache_mem_k, cache_mem_v, state_ssm, state_conv, w_in, q_norm_g, w_uq, kv_norm_g, w_uk, w_uv, conv_w, conv_b, dt_bias, a_log, d_skip, ssm_norm_g, w_mem_k, w_mem_v, b_gate, w_o_mla, w_o_ssm, w_o_mem, w_out, ln1_g, ln1_b, w_up, b_up, w_down, b_down, ln2_g, ln2_b):
    bp, seq, _ = x_prompt.shape
    bs, t_new, _ = x_sample.shape
    n_pages = page_table.shape[1]
    past = n_pages * PAGE_SIZE

    o_cq, o_ckv, o_kr, o_z, o_xbc, o_dt, o_mq, o_g = np.cumsum(
        [0, Q_LORA, KV_LORA, ROPE_DIM, D_INNER, CONV_DIM, SSM_HEADS, MEM_WIDTH]).tolist()
    w_kr = w_in[:, o_kr:o_z]
    w_small = jnp.concatenate([w_in[:, o_ckv:o_kr], w_kr, w_kr, w_in[:, o_dt:o_mq],
                               jnp.zeros((D_MODEL, 128 - SSM_HEADS), F32)], axis=1)
    w_proj = tuple(w.astype(BF16) for w in (w_in[:, o_cq:o_ckv], w_small, w_in[:, o_z:o_xbc], w_in[:, o_xbc:o_dt],
                                            w_in[:, o_mq:o_g], w_in[:, o_g:]))
    half = ROPE_DIM // 2
    w_q_nope = w_uq[:, :, :NOPE_DIM].reshape(Q_LORA, MLA_HEADS * NOPE_DIM)
    w_q_rope = w_uq[:, :, NOPE_DIM:]
    w_q_swap = jnp.concatenate([w_q_rope[:, :, half:], w_q_rope[:, :, :half]], axis=-1)
    pair_w = 2 * ROPE_DIM
    w_q_pairs = jnp.concatenate([w_q_rope.reshape(Q_LORA, MLA_HEADS // 2, pair_w),
                                 w_q_swap.reshape(Q_LORA, MLA_HEADS // 2, pair_w)], axis=-1)
    wq = jnp.concatenate([w_q_nope, w_q_pairs.reshape(Q_LORA, -1)], axis=1).astype(BF16)
    wuk_t = jnp.transpose(w_uk, (1, 2, 0)).astype(BF16)
    wuv = jnp.transpose(w_uv, (1, 0, 2)).astype(BF16)
    wuv_t = jnp.transpose(w_uv, (1, 2, 0)).astype(BF16)
    w_mem_kv = jnp.concatenate([w_mem_k.reshape(D_MODEL, MEM_WIDTH), w_mem_v.reshape(D_MODEL, MEM_WIDTH)], axis=1).astype(BF16)
    merge_w = (w_o_mla.astype(BF16), w_o_ssm.astype(BF16), w_o_mem.astype(BF16), w_out.astype(BF16),
               _row(ln1_g), _row(ln1_b), w_up.astype(BF16), _row(b_up), w_down.astype(BF16), _row(b_down),
               _row(ln2_g), _row(ln2_b))
    gq, gkv, bg = _row(q_norm_g), _row(kv_norm_g), _row(b_gate)

    pad_heads = jnp.zeros((128 - SSM_HEADS,), F32)
    ssd_head = (conv_w.astype(F32), _row(conv_b),
                _row(jnp.concatenate([dt_bias, pad_heads])),
                jnp.broadcast_to(dt_bias.astype(F32)[:, None], (SSM_HEADS, CHUNK)),
                _row(jnp.concatenate([a_log, pad_heads])),
                jnp.broadcast_to(a_log.astype(F32)[:, None], (SSM_HEADS, CHUNK)),
                _row(jnp.repeat(d_skip, SSM_HEAD_DIM)))
    norm_g = _row(ssm_norm_g)
    expanders = (_expand_matrix(SSM_HEAD_DIM, 2), _expand_matrix(128, 3))

    mp = bp * seq
    n_chunks = seq // CHUNK
    cs_p = _rope_table(jnp.arange(seq))
    xp2d = x_prompt.reshape(mp, D_MODEL)
    cqn, ckv_p, kr_p, kcat, ckvt, dt_p, dtt_p, z_p, xbc_p, mq_p, g_p = _in_proj(xp2d, w_proj, cs_p, gq, gkv, bg, tm=256)
    qt_p = _q_prep(cqn, wq, wuk_t, cs_p, tm=256, transposed=True)
    o_mla_p = _prompt_attn(qt_p, kcat, ckvt, wuv_t, bp, seq)

    o_ssm_p, ssm_p, conv_p = _ssd(xbc_p, z_p, dt_p, dtt_p, ssd_head + (norm_g,) + expanders + (_shift_matrix(),),
                                  batch=bp, n_chunks=n_chunks)

    mem_k_p, mem_v_p = _mem_kv(mem_prompt.reshape(bp * N_MEM, D_MODEL), w_mem_kv, tm=min(512, bp * N_MEM))
    o_mem_p = _mem_attn(mq_p.reshape(bp, seq, MEM_WIDTH), mem_k_p.reshape(bp, N_MEM, MEM_WIDTH),
                        mem_v_p.reshape(bp, N_MEM, MEM_WIDTH), nb=1, tq=512)
    y_p = _merge_ffn(xp2d, o_mla_p, o_ssm_p.reshape(mp, D_INNER), o_mem_p.reshape(mp, MEM_WIDTH), g_p, merge_w, tm=256)

    ms = bs * t_new
    cs_s = jnp.tile(_rope_table(past + jnp.arange(t_new)), (bs, 1))
    xs2d = x_sample.reshape(ms, D_MODEL)
    tm_s = min(256, ms)
    cqn, ckv_s, kr_s, kcat, _, dt_s, dtt_s, z_s, xbc_s, mq_s, g_s = _in_proj(xs2d, w_proj, cs_s, gq, gkv, bg, tm=tm_s)
    q_s = _q_prep(cqn, wq, wuk_t, cs_s, tm=tm_s, transposed=False)
    q_s = jnp.transpose(q_s.reshape(MLA_HEADS, bs, t_new, QK_DIM), (1, 0, 2, 3)).reshape(bs, MLA_HEADS * t_new, QK_DIM)
    o_lat = _decode_attn(page_table, q_s, kcat.reshape(bs, t_new, QK_DIM).astype(F32), cache_ckv,
                         jnp.swapaxes(cache_krope, 1, 2))
    o_lat = jnp.transpose(o_lat.reshape(bs, MLA_HEADS, t_new, KV_LORA), (1, 0, 2, 3)).reshape(MLA_HEADS, ms, KV_LORA)
    o_mla_s = _uv_proj(o_lat, wuv)

    ypre_s, ec_s, xw_s, c_s, bw_s, conv_rows = _ssd_short(
        xbc_s, jnp.transpose(state_conv.astype(F32), (1, 0, 2)), dt_s, dtt_s, ssd_head + expanders,
        _short_taps_matrix(t_new), _placement_matrix(t_new), seq_len=t_new)
    conv_s = jnp.transpose(conv_rows, (1, 0, 2))
    o_ssm_s, ssm_s = _ssd_state(state_ssm.astype(F32).reshape(bs, D_INNER, D_STATE), xw_s, c_s, bw_s, ypre_s, ec_s,
                                z_s.astype(F32), norm_g, nb=4, seq_len=t_new)

    def cache_rows(c):
        c = c.reshape(bs, N_MEM, MEM_HEADS, MEM_HALVES, 128)
        return jnp.transpose(c, (0, 1, 3, 2, 4)).reshape(bs, N_MEM * MEM_ROW_GROUP, 128)

    mq_rows = jnp.transpose(mq_s.reshape(bs, t_new, MEM_HEADS, MEM_HALVES, 128), (0, 3, 2, 1, 4))
    o_mem_s = _mem_attn_cache(mq_rows.reshape(bs, MEM_ROW_GROUP * t_new, 128), cache_rows(cache_mem_k),
                              cache_rows(cache_mem_v), nb=4, n_tok=t_new)
    o_mem_s = jnp.transpose(o_mem_s.reshape(bs, MEM_HALVES, MEM_HEADS, t_new, 128), (0, 3, 2, 1, 4))
    y_s = _merge_ffn(xs2d, o_mla_s, o_ssm_s.reshape(ms, D_INNER).astype(BF16), o_mem_s.reshape(ms, MEM_WIDTH), g_s,
                     merge_w, tm=tm_s)

    return (y_p.reshape(bp, seq, D_MODEL), y_s.reshape(bs, t_new, D_MODEL),
            ckv_p.reshape(bp, seq, KV_LORA), kr_p.reshape(bp, seq, ROPE_DIM),
            mem_k_p.reshape(bp, N_MEM, MEM_HEADS, MEM_HEAD_DIM), mem_v_p.reshape(bp, N_MEM, MEM_HEADS, MEM_HEAD_DIM),
            ssm_p.reshape(bp, SSM_HEADS, SSM_HEAD_DIM, D_STATE), conv_p,
            ckv_s.reshape(bs, t_new, KV_LORA), kr_s.reshape(bs, t_new, ROPE_DIM),
            ssm_s.reshape(bs, SSM_HEADS, SSM_HEAD_DIM, D_STATE), conv_s)
```

```python
import functools

import jax
import jax.numpy as jnp
import numpy as np
from jax import lax
from jax.experimental import pallas as pl
from jax.experimental.pallas import tpu as pltpu

F32 = jnp.float32
BF16 = jnp.bfloat16

D_MODEL = 1024
MLA_HEADS = 8
Q_LORA = 384
KV_LORA = 256
NOPE_DIM = 128
ROPE_DIM = 64
V_DIM = 128
ROPE_THETA = 10000.0
QK_DIM = KV_LORA + ROPE_DIM
SSM_HEADS = 32
SSM_HEAD_DIM = 64
D_INNER = SSM_HEADS * SSM_HEAD_DIM
SSM_GROUPS = 4
GROUP_W = D_INNER // SSM_GROUPS
D_STATE = 128
CONV_W = 4
CONV_DIM = D_INNER + 2 * SSM_GROUPS * D_STATE
CHUNK = 128
N_MEM = 256
MEM_HEADS = 4
MEM_HEAD_DIM = 256
MEM_WIDTH = MEM_HEADS * MEM_HEAD_DIM
D_FF = 4 * D_MODEL
N_BRANCH = 3
DEPTH = 1
ALPHA = (2 * DEPTH) ** 0.25
LN_EPS = 1e-5
RMS_EPS = 1e-6
PAGE_SIZE = 128
SCORE_SCALE = (NOPE_DIM + ROPE_DIM) ** -0.5
NEG_BIG = -1e30
KV_TILE = 256

TOKEN_TILE = 256
MEM_ROWS_TILE = 512
SAMPLE_ROWS = 4


def _cp(sem, vmem_mb):
    return pltpu.CompilerParams(dimension_semantics=sem, vmem_limit_bytes=vmem_mb << 20)


def _resident(shape):
    nd = len(shape)
    return pl.BlockSpec(shape, lambda *_: (0,) * nd, pipeline_mode=pl.Buffered(1))


def _dot(a, b):
    return jnp.dot(a, b, preferred_element_type=F32)


def _dot_nt(a, b):
    return lax.dot_general(a, b, (((1,), (1,)), ((), ())), preferred_element_type=F32)


def _rms(v, g, eps):
    return v * lax.rsqrt(jnp.mean(v * v, axis=-1, keepdims=True) + eps) * g


def _layer_norm(v, g, b):
    mu = jnp.mean(v, axis=-1, keepdims=True)
    d = v - mu
    var = jnp.mean(d * d, axis=-1, keepdims=True)
    return d * lax.rsqrt(var + LN_EPS) * g + b


def _in_proj_kernel(x_ref, wcq_ref, wsmall_ref, wz_ref, wxbc_ref, wmq_ref, wg_ref, cs_ref, gq_ref, gkv_ref, bg_ref,
                    cqn_ref, ckv_ref, kr_ref, kcat_ref, ckvt_ref, dt_ref, dtt_ref, z_ref, xbc_ref, mq_ref, g_ref):
    xb = x_ref[...].astype(BF16)
    cqn_ref[...] = _rms(_dot(xb, wcq_ref[...]), gq_ref[...], RMS_EPS).astype(BF16)

    small = _dot(xb, wsmall_ref[...])
    ckv = _rms(small[:, :KV_LORA], gkv_ref[...], RMS_EPS)
    ckv_ref[...] = ckv
    kcat_ref[:, 0:KV_LORA] = ckv.astype(BF16)
    for t in range(ckvt_ref.shape[0]):
        ckvt_ref[t] = ckv[t * KV_TILE:(t + 1) * KV_TILE, :].T.astype(BF16)
    a = small[:, 256:384]
    b = pltpu.roll(a, ROPE_DIM // 2, 1)
    cs = cs_ref[...]
    ro = a * cs[:, :128] + b * cs[:, 128:]
    kr_ref[...] = ro[:, :ROPE_DIM]
    kcat_ref[:, KV_LORA:QK_DIM] = ro[:, :ROPE_DIM].astype(BF16)
    dt = small[:, 384:512]
    dt_ref[...] = dt
    dtt_ref[...] = dt.T[0:SSM_HEADS, :]

    for c in range(0, D_INNER, 1024):
        z_ref[:, c:c + 1024] = _dot(xb, wz_ref[:, c:c + 1024]).astype(BF16)
    for c in range(0, CONV_DIM, 1024):
        xbc_ref[:, c:c + 1024] = _dot(xb, wxbc_ref[:, c:c + 1024]).astype(BF16)
    mq_ref[...] = _dot(xb, wmq_ref[...]).astype(BF16)
    for c in range(0, N_BRANCH * D_MODEL, 1024):
        gr = _dot(xb, wg_ref[:, c:c + 1024]) + bg_ref[:, c:c + 1024]
        g_ref[:, c:c + 1024] = jax.nn.sigmoid(gr).astype(BF16)


def _in_proj(x2d, weights, cs_tab, gq, gkv, bg, tm):
    m = x2d.shape[0]
    ncs = cs_tab.shape[0] // tm

    def row(n, dtype):
        return pl.BlockSpec((tm, n), lambda i: (i, 0)), jax.ShapeDtypeStruct((m, n), dtype)

    outs = [row(Q_LORA, BF16), row(KV_LORA, F32), row(ROPE_DIM, F32), row(QK_DIM, BF16),
            (pl.BlockSpec((tm // KV_TILE, KV_LORA, KV_TILE), lambda i: (i, 0, 0)),
             jax.ShapeDtypeStruct((m // KV_TILE, KV_LORA, KV_TILE), BF16)),
            row(128, F32),
            (pl.BlockSpec((SSM_HEADS, tm), lambda i: (0, i)), jax.ShapeDtypeStruct((SSM_HEADS, m), F32)),
            row(D_INNER, BF16), row(CONV_DIM, BF16), row(MEM_WIDTH, BF16), row(N_BRANCH * D_MODEL, BF16)]
    return pl.pallas_call(
        _in_proj_kernel,
        grid=(m // tm,),
        in_specs=[pl.BlockSpec((tm, D_MODEL), lambda i: (i, 0))] + [_resident(w.shape) for w in weights]
                 + [pl.BlockSpec((tm, 256), lambda i: (i % ncs, 0)),
                    _resident(gq.shape), _resident(gkv.shape), _resident(bg.shape)],
        out_specs=[o[0] for o in outs],
        out_shape=[o[1] for o in outs],
        compiler_params=_cp(("arbitrary",), 56),
        name="in_proj",
    )(x2d, *weights, cs_tab, gq, gkv, bg)


def _q_prep_kernel(c_ref, wq_ref, wuk_ref, cs_ref, q_ref, *, transposed):
    c = c_ref[...]
    cs = cs_ref[...]
    qn = [_dot(c, wq_ref[:, h * NOPE_DIM:(h + 1) * NOPE_DIM]).astype(BF16) for h in range(MLA_HEADS)]
    for h in range(MLA_HEADS):
        ql = _dot(qn[h], wuk_ref[h]) * SCORE_SCALE
        if transposed:
            q_ref[h, 0:KV_LORA, :] = ql.T.astype(BF16)
        else:
            q_ref[h, :, 0:KV_LORA] = ql.astype(BF16)
    r0 = MLA_HEADS * NOPE_DIM
    for p in range(MLA_HEADS // 2):
        ab = _dot(c, wq_ref[:, r0 + p * 256:r0 + (p + 1) * 256])
        ro = (ab[:, :128] * cs[:, :128] + ab[:, 128:] * cs[:, 128:]) * SCORE_SCALE
        if transposed:
            rot = ro.T.astype(BF16)
            q_ref[2 * p, KV_LORA:QK_DIM, :] = rot[:ROPE_DIM]
            q_ref[2 * p + 1, KV_LORA:QK_DIM, :] = rot[ROPE_DIM:]
        else:
            q_ref[2 * p, :, KV_LORA:QK_DIM] = ro[:, :ROPE_DIM].astype(BF16)
            q_ref[2 * p + 1, :, KV_LORA:QK_DIM] = pltpu.roll(ro, ROPE_DIM, 1)[:, :ROPE_DIM].astype(BF16)


def _q_prep(cqn, wq, wuk_t, cs_tab, tm, transposed):
    m = cqn.shape[0]
    ncs = cs_tab.shape[0] // tm
    if transposed:
        out_spec = pl.BlockSpec((MLA_HEADS, QK_DIM, tm), lambda i: (0, 0, i))
        out_shape = jax.ShapeDtypeStruct((MLA_HEADS, QK_DIM, m), BF16)
    else:
        out_spec = pl.BlockSpec((MLA_HEADS, tm, QK_DIM), lambda i: (0, i, 0))
        out_shape = jax.ShapeDtypeStruct((MLA_HEADS, m, QK_DIM), BF16)
    return pl.pallas_call(
        functools.partial(_q_prep_kernel, transposed=transposed),
        grid=(m // tm,),
        in_specs=[pl.BlockSpec((tm, Q_LORA), lambda i: (i, 0)), _resident(wq.shape), _resident(wuk_t.shape),
                  pl.BlockSpec((tm, 256), lambda i: (i % ncs, 0))],
        out_specs=out_spec,
        out_shape=out_shape,
        compiler_params=_cp(("arbitrary",), 32),
        name="q_prep",
    )(cqn, wq, wuk_t, cs_tab)


def _prompt_attn_kernel(qt_ref, k_ref, vt_ref, wuvt_ref, o_ref, m_sc, l_sc, acc_sc):
    i = pl.program_id(1)
    t = KV_TILE
    m_sc[...] = jnp.full(m_sc.shape, NEG_BIG, F32)
    l_sc[...] = jnp.zeros(l_sc.shape, F32)
    acc_sc[...] = jnp.zeros(acc_sc.shape, F32)

    def kv_block(j, masked):
        k = k_ref[pl.ds(pl.multiple_of(j * t, t), t), :]
        kc = k[:, :KV_LORA]
        kr = k[:, KV_LORA:]
        vt = vt_ref[j]
        if masked:
            keep = lax.broadcasted_iota(jnp.int32, (t, t), 0) <= lax.broadcasted_iota(jnp.int32, (t, t), 1)
        def scores(h):
            return _dot(kc, qt_ref[h, 0:KV_LORA, :]) + _dot(kr, qt_ref[h, KV_LORA:QK_DIM, :])

        pending = [scores(h) for h in range(MLA_HEADS)]
        for h in range(MLA_HEADS):
            st = pending[h]
            if masked:
                st = jnp.where(keep, st, NEG_BIG)
            m_prev = m_sc[h:h + 1, :]
            m_new = jnp.maximum(m_prev, jnp.max(st, axis=0, keepdims=True))
            alpha = jnp.exp(m_prev - m_new)
            p = jnp.exp(st - m_new)
            l_sc[h:h + 1, :] = alpha * l_sc[h:h + 1, :] + jnp.sum(p, axis=0, keepdims=True)
            acc_sc[h] = alpha * acc_sc[h] + _dot(vt, p.astype(BF16))
            m_sc[h:h + 1, :] = m_new

    def body(j, carry):
        kv_block(j, False)
        return carry

    lax.fori_loop(0, i, body, 0)
    kv_block(i, True)

    for h in range(MLA_HEADS):
        ot = (acc_sc[h] / l_sc[h:h + 1, :]).astype(BF16)
        o_ref[:, h * V_DIM:(h + 1) * V_DIM] = _dot(wuvt_ref[h], ot).T.astype(BF16)


def _prompt_attn(qt, kcat, ckvt, wuvt, batch, seq):
    t = KV_TILE
    nq = seq // t
    return pl.pallas_call(
        _prompt_attn_kernel,
        grid=(batch, nq),
        in_specs=[pl.BlockSpec((MLA_HEADS, QK_DIM, t), lambda b, i: (0, 0, b * nq + i)),
                  pl.BlockSpec((seq, QK_DIM), lambda b, i: (b, 0)),
                  pl.BlockSpec((nq, KV_LORA, t), lambda b, i: (b, 0, 0)),
                  _resident(wuvt.shape)],
        out_specs=pl.BlockSpec((t, MLA_HEADS * V_DIM), lambda b, i: (b * nq + i, 0)),
        out_shape=jax.ShapeDtypeStruct((batch * seq, MLA_HEADS * V_DIM), BF16),
        scratch_shapes=[pltpu.VMEM((MLA_HEADS, t), F32), pltpu.VMEM((MLA_HEADS, t), F32),
                        pltpu.VMEM((MLA_HEADS, KV_LORA, t), F32)],
        compiler_params=_cp(("arbitrary", "arbitrary"), 32),
        name="prompt_attn",
    )(qt, kcat, ckvt, wuvt)


def _decode_attn_kernel(pt_ref, q_ref, kn_ref, ckv_hbm, krt_hbm, o_ref, kbuf, rbuf, kb_sc, s_sc, sem, *, n_pages, n_new):
    b = pl.program_id(0)
    nb = pl.num_programs(0)
    slot = b % 2

    def page_copies(page, p, sl):
        rows = pl.ds(pl.multiple_of(p * PAGE_SIZE, PAGE_SIZE), PAGE_SIZE)
        return (pltpu.make_async_copy(ckv_hbm.at[page], kbuf.at[sl, rows, :], sem.at[0, sl]),
                pltpu.make_async_copy(krt_hbm.at[page], rbuf.at[sl, p], sem.at[1, sl]))

    def issue(bi, sl):
        def body(p, carry):
            for cp in page_copies(pt_ref[bi * n_pages + p], p, sl):
                cp.start()
            return carry
        lax.fori_loop(0, n_pages, body, 0, unroll=8)

    @pl.when(b == 0)
    def _():
        issue(0, 0)

    @pl.when(b + 1 < nb)
    def _():
        issue(b + 1, 1 - slot)

    def wait_body(p, carry):
        for cp in page_copies(0, p, slot):
            cp.wait()
        return carry
    lax.fori_loop(0, n_pages, wait_body, 0, unroll=8)

    q = q_ref[0]
    rows = q.shape[0]
    qc = q[:, :KV_LORA]
    qr = q[:, KV_LORA:]
    qf = q.astype(F32)
    kn = kn_ref[0]
    tok = lax.broadcasted_iota(jnp.int32, (rows, 1), 0) % n_new

    s_new = []
    for j in range(n_new):
        sj = jnp.sum(qf * kn[j:j + 1, :], axis=1, keepdims=True)
        s_new.append(jnp.where(tok >= j, sj, NEG_BIG))
    m0 = s_new[0]
    for j in range(1, n_new):
        m0 = jnp.maximum(m0, s_new[j])

    for c in range(n_pages // 2):
        r = slice(2 * c * PAGE_SIZE, 2 * (c + 1) * PAGE_SIZE)
        kc = kbuf[slot, r, :].astype(BF16)
        kb_sc[r, :] = kc
        rt = jnp.concatenate([rbuf[slot, 2 * c], rbuf[slot, 2 * c + 1]], axis=1).astype(BF16)
        s_sc[:, r] = _dot_nt(qc, kc) + _dot(qr, rt)
    s = s_sc[...]
    m = jnp.maximum(m0, jnp.max(s, axis=1, keepdims=True))
    p = jnp.exp(s - m)
    l = jnp.sum(p, axis=1, keepdims=True)
    acc = _dot(p.astype(BF16), kb_sc[...])
    for j in range(n_new):
        pj = jnp.exp(s_new[j] - m)
        l = l + pj
        acc = acc + pj * kn[j:j + 1, :KV_LORA]
    o_ref[0] = (acc / l).astype(BF16)


def _decode_attn(page_table, q, k_new, cache_ckv, cache_krope_t):
    nb, n_pages = page_table.shape
    rows = q.shape[1]
    n_new = k_new.shape[1]
    past = n_pages * PAGE_SIZE
    kern = functools.partial(_decode_attn_kernel, n_pages=n_pages, n_new=n_new)
    grid_spec = pltpu.PrefetchScalarGridSpec(
        num_scalar_prefetch=1,
        grid=(nb,),
        in_specs=[pl.BlockSpec((1, rows, QK_DIM), lambda b, pt: (b, 0, 0)),
                  pl.BlockSpec((1, n_new, QK_DIM), lambda b, pt: (b, 0, 0)),
                  pl.BlockSpec(memory_space=pl.ANY),
                  pl.BlockSpec(memory_space=pl.ANY)],
        out_specs=pl.BlockSpec((1, rows, KV_LORA), lambda b, pt: (b, 0, 0)),
        scratch_shapes=[pltpu.VMEM((2, past, KV_LORA), F32), pltpu.VMEM((2, n_pages, ROPE_DIM, PAGE_SIZE), F32),
                        pltpu.VMEM((past, KV_LORA), BF16), pltpu.VMEM((rows, past), F32),
                        pltpu.SemaphoreType.DMA((2, 2))],
    )
    return pl.pallas_call(
        kern,
        grid_spec=grid_spec,
        out_shape=jax.ShapeDtypeStruct((nb, rows, KV_LORA), BF16),
        compiler_params=_cp(("arbitrary",), 40),
        name="decode_attn",
    )(page_table.reshape(-1), q, k_new, cache_ckv, cache_krope_t)


def _uv_proj_kernel(o_ref, w_ref, out_ref):
    out_ref[...] = _dot(o_ref[0], w_ref[0]).astype(BF16)


def _uv_proj(o_lat, wuv):
    m = o_lat.shape[1]
    return pl.pallas_call(
        _uv_proj_kernel,
        grid=(MLA_HEADS,),
        in_specs=[pl.BlockSpec((1, m, KV_LORA), lambda h: (h, 0, 0)),
                  pl.BlockSpec((1, KV_LORA, V_DIM), lambda h: (h, 0, 0))],
        out_specs=pl.BlockSpec((m, V_DIM), lambda h: (0, h)),
        out_shape=jax.ShapeDtypeStruct((m, MLA_HEADS * V_DIM), BF16),
        compiler_params=_cp(("arbitrary",), 16),
        name="uv_proj",
    )(o_lat, wuv)


def _split_dot(v, e, terms, left=True):
    parts = []
    r = v
    for _ in range(terms):
        hi = r.astype(BF16)
        parts.append(hi)
        r = r - hi.astype(F32)
    if left:
        return _dot(jnp.concatenate(parts, axis=1), e)
    return _dot(e, jnp.concatenate(parts, axis=0))


def _conv_silu(taps, cw_ref, cb_ref):
    conv = cb_ref[...] + cw_ref[0:1, :] * taps[0]
    for k in range(1, CONV_W):
        conv = conv + cw_ref[k:k + 1, :] * taps[k]
    xc = conv * jax.nn.sigmoid(conv)
    n_bc = SSM_GROUPS * D_STATE
    return xc[:, :D_INNER], xc[:, D_INNER:D_INNER + n_bc], xc[:, D_INNER + n_bc:]


def _select_rows(select, rows_ref):
    n = rows_ref.shape[1]
    return jnp.concatenate([_dot(select, rows_ref[:, c:c + 1024]) for c in range(0, n, 1024)], axis=1)


def _ssd_sums(dt, dtt, keep, keep_t, alog_ref, alogt_ref):
    tri = jnp.where(keep, 1.0, 0.0).astype(BF16)
    tri_t = jnp.where(keep_t, 1.0, 0.0).astype(BF16)
    da = dt * (-jnp.exp(alog_ref[...]))
    dat = dtt * (-jnp.exp(alogt_ref[...]))
    cum = _split_dot(da, jnp.concatenate([tri] * 3, axis=1), 3, left=False)
    cum_t = _split_dot(dat, jnp.concatenate([tri_t] * 3, axis=0), 3)
    return da, cum, cum_t


def _ssd_intra(xs, bm, cm, cum_e, cum_t, dtt, keep, dskip_ref, y_sc, carried=None):
    pairs_per_group = SSM_HEADS // 2 // SSM_GROUPS
    lane = lax.broadcasted_iota(jnp.int32, (CHUNK, CHUNK), 1)
    for g in range(SSM_GROUPS):
        cg = cm[:, g * D_STATE:(g + 1) * D_STATE].astype(BF16)
        cb = _dot_nt(cg, bm[:, g * D_STATE:(g + 1) * D_STATE].astype(BF16))
        extra = None if carried is None else carried(g, cg)
        for pr in range(pairs_per_group):
            q = g * pairs_per_group + pr
            ws = []
            for h in (2 * q, 2 * q + 1):
                seg = cum_e[:, h * 128:(h + 1) * 128] - cum_t[h:h + 1, :]
                decay = jnp.exp(jnp.where(keep, seg, NEG_BIG)) * dtt[h:h + 1, :]
                ws.append((cb * decay).astype(BF16))
            w_pair = jnp.concatenate(ws, axis=1)
            xp = xs[:, q * 128:(q + 1) * 128]
            x_bd = jnp.concatenate([jnp.where(lane < SSM_HEAD_DIM, xp, 0.0).astype(BF16),
                                    jnp.where(lane >= SSM_HEAD_DIM, xp, 0.0).astype(BF16)], axis=0)
            y = _dot(w_pair, x_bd) + dskip_ref[:, q * 128:(q + 1) * 128] * xp
            if extra is not None:
                y = y + extra[:, pr * 128:(pr + 1) * 128]
            y_sc[:, q * 128:(q + 1) * 128] = y


CARRY = 16


def _ssd_kernel(xbc_ref, z_ref, dt_ref, dtt_ref, cw_ref, cb_ref, dtb_ref, dtbt_ref,
                alog_ref, alogt_ref, dskip_ref, ng_ref, e64_ref, e128_ref, shift_ref,
                y_ref, sout_ref, cout_ref, xe_sc, tail_sc, st_sc, y_sc):
    c = pl.program_id(1)
    n_pairs = SSM_HEADS // 2
    pairs_per_group = n_pairs // SSM_GROUPS

    @pl.when(c == 0)
    def _():
        xe_sc[0:CARRY, :] = jnp.zeros((CARRY, CONV_DIM), BF16)
        st_sc[...] = jnp.zeros(st_sc.shape, F32)

    xe_sc[CARRY:CARRY + CHUNK, :] = xbc_ref[...]
    shifted = _select_rows(shift_ref[...], xe_sc)
    taps = [shifted[k * CHUNK:(k + 1) * CHUNK] for k in range(CONV_W - 1)] + [xbc_ref[...].astype(F32)]
    xs, bm, cm = _conv_silu(taps, cw_ref, cb_ref)
    xe_sc[0:CARRY, :] = xe_sc[CHUNK:CHUNK + CARRY, :]

    rowi = lax.broadcasted_iota(jnp.int32, (CHUNK, CHUNK), 0)
    coli = lax.broadcasted_iota(jnp.int32, (CHUNK, CHUNK), 1)
    keep = coli <= rowi
    dt = jax.nn.softplus(dt_ref[...] + dtb_ref[...])
    dtt = jax.nn.softplus(dtt_ref[...] + dtbt_ref[...])
    _, cum, cum_t = _ssd_sums(dt, dtt, keep, rowi <= coli, alog_ref, alogt_ref)
    last = cum[CHUNK - 1:CHUNK, :]
    e64 = e64_ref[...]
    ec_e = _split_dot(jnp.exp(cum), e64, 2)
    wb_e = _split_dot(dt * jnp.exp(last - cum), e64, 2)
    dec_e = _split_dot(jnp.broadcast_to(jnp.exp(last), (16, CHUNK)), e64, 2)[0:1, :]
    cum_e = _split_dot(cum, e128_ref[...], 3)

    def carried(g, cg):
        return _dot(cg, st_sc[g].astype(BF16)) * ec_e[:, g * GROUP_W:(g + 1) * GROUP_W]

    _ssd_intra(xs, bm, cm, cum_e, cum_t, dtt, keep, dskip_ref, y_sc, carried)
    for g in range(SSM_GROUPS):
        gs = slice(g * GROUP_W, (g + 1) * GROUP_W)
        xw = (xs[:, gs] * wb_e[:, gs]).astype(BF16)
        bt = bm[:, g * D_STATE:(g + 1) * D_STATE].T.astype(BF16)
        st_sc[g] = st_sc[g] * dec_e[:, gs] + _dot(bt, xw)

    for g in range(SSM_GROUPS):
        gs = slice(g * GROUP_W, (g + 1) * GROUP_W)
        zf = z_ref[:, gs].astype(F32)
        yv = y_sc[:, gs] * (zf * jax.nn.sigmoid(zf))
        y_ref[:, gs] = _rms(yv, ng_ref[:, gs], RMS_EPS).astype(y_ref.dtype)

    @pl.when(c == pl.num_programs(1) - 1)
    def _():
        tail_sc[...] = xbc_ref[CHUNK - CARRY:CHUNK, :].astype(F32)
        cout_ref[0] = tail_sc[CARRY - (CONV_W - 1):CARRY, :]
        for q in range(n_pairs):
            g, pr = divmod(q, pairs_per_group)
            sout_ref[0, q * 128:(q + 1) * 128, :] = st_sc[g, :, pr * 128:(pr + 1) * 128].T


def _ssd(xbc, z, dt, dtt, consts, batch, n_chunks):
    def step(n):
        return pl.BlockSpec((CHUNK, n), lambda b, c: (b * n_chunks + c, 0))

    def per_batch(r, n):
        return pl.BlockSpec((1, r, n), lambda b, c: (b, 0, 0))

    return pl.pallas_call(
        _ssd_kernel,
        grid=(batch, n_chunks),
        in_specs=[step(CONV_DIM), step(D_INNER), step(128),
                  pl.BlockSpec((SSM_HEADS, CHUNK), lambda b, c: (0, b * n_chunks + c))]
                 + [_resident(a.shape) for a in consts],
        out_specs=[step(D_INNER), per_batch(D_INNER, D_STATE), per_batch(CONV_W - 1, CONV_DIM)],
        out_shape=[jax.ShapeDtypeStruct((batch * n_chunks * CHUNK, D_INNER), BF16),
                   jax.ShapeDtypeStruct((batch, D_INNER, D_STATE), F32),
                   jax.ShapeDtypeStruct((batch, CONV_W - 1, CONV_DIM), F32)],
        scratch_shapes=[pltpu.VMEM((CARRY + CHUNK, CONV_DIM), BF16),
                        pltpu.VMEM((CARRY, CONV_DIM), F32),
                        pltpu.VMEM((SSM_GROUPS, D_STATE, GROUP_W), F32),
                        pltpu.VMEM((CHUNK, D_INNER), F32)],
        compiler_params=_cp(("arbitrary", "arbitrary"), 48),
        name="ssd",
    )(xbc, z, dt, dtt, *consts)


SLOTS = 16


def _ssd_short_kernel(x_ref, sc_ref, dt_ref, dtt_ref, cw_ref, cb_ref, dtb_ref, dtbt_ref, alog_ref, alogt_ref,
                      dskip_ref, e64_ref, e128_ref, taps_sel_ref, place_ref,
                      ypre_ref, ec_ref, xw_ref, c_ref, bw_ref, tails_ref, rows_sc, y_sc, *, seq_len):
    n_cached = sc_ref.shape[0] * sc_ref.shape[1]
    rows_sc[0:CHUNK, :] = x_ref[...]
    r = sc_ref[...].reshape(n_cached, CONV_DIM)
    for t in range(3):
        hi = r.astype(BF16)
        rows_sc[CHUNK + t * n_cached:CHUNK + (t + 1) * n_cached, :] = hi
        r = r - hi.astype(F32)
    picked = _select_rows(taps_sel_ref[...], rows_sc)
    taps = [picked[k * CHUNK:(k + 1) * CHUNK] for k in range(CONV_W - 1)] + [x_ref[...].astype(F32)]
    tails_ref[...] = picked[(CONV_W - 1) * CHUNK:].reshape(tails_ref.shape)
    xs, bm, cm = _conv_silu(taps, cw_ref, cb_ref)
    rowi = lax.broadcasted_iota(jnp.int32, (CHUNK, CHUNK), 0)
    coli = lax.broadcasted_iota(jnp.int32, (CHUNK, CHUNK), 1)
    same = (rowi // seq_len) == (coli // seq_len)
    keep = same & (coli <= rowi)
    dt = jax.nn.softplus(dt_ref[...] + dtb_ref[...])
    dtt = jax.nn.softplus(dtt_ref[...] + dtbt_ref[...])
    da, cum, cum_t = _ssd_sums(dt, dtt, keep, same & (rowi <= coli), alog_ref, alogt_ref)
    whole = jnp.where(same, 1.0, 0.0).astype(BF16)
    last = _split_dot(da, jnp.concatenate([whole] * 3, axis=1), 3, left=False)
    e64 = e64_ref[...]
    ec_ref[...] = _split_dot(jnp.exp(cum), e64, 2)
    wb_e = _split_dot(dt * jnp.exp(last - cum), e64, 2)
    dec_e = _split_dot(jnp.exp(last), e64, 2)
    cum_e = _split_dot(cum, e128_ref[...], 3)
    _ssd_intra(xs, bm, cm, cum_e, cum_t, dtt, keep, dskip_ref, y_sc)
    ypre_ref[...] = y_sc[...]

    place = place_ref[...]
    dec_hi = dec_e.astype(BF16)
    dec_lo = (dec_e - dec_hi.astype(F32)).astype(BF16)
    xw = (xs * wb_e).astype(BF16)
    xw_ref[...] = _dot(place, jnp.concatenate([xw, dec_hi, dec_lo], axis=0)).astype(BF16)
    tokens = place[:, :CHUNK]
    c_ref[...] = _dot(tokens, cm.astype(BF16)).astype(BF16)
    zeros = jnp.zeros((CHUNK, D_STATE), BF16)
    b_wide = jnp.concatenate(
        [piece for g in range(SSM_GROUPS) for piece in (bm[:, g * D_STATE:(g + 1) * D_STATE].astype(BF16), zeros)], axis=1)
    n_rows = place.shape[0]
    slot = lax.broadcasted_iota(jnp.int32, (n_rows, 2 * D_STATE * SSM_GROUPS), 0) % SLOTS
    col = lax.broadcasted_iota(jnp.int32, (n_rows, 2 * D_STATE * SSM_GROUPS), 1) % (2 * D_STATE)
    ones = ((slot == seq_len) | (slot == seq_len + 1)) & (col >= D_STATE)
    bw_ref[...] = jnp.where(ones, 1.0, _dot(tokens, b_wide)).astype(BF16)


def _ssd_short(xbc, conv_rows, dt, dtt, consts, taps_sel, place, seq_len):
    rows = dt.shape[0]
    n_chunks = rows // CHUNK
    n_seq = CHUNK // seq_len
    n_slot_rows = place.shape[0]
    kern = functools.partial(_ssd_short_kernel, seq_len=seq_len)

    def out(r, n, dtype):
        return pl.BlockSpec((r, n), lambda c: (c, 0)), jax.ShapeDtypeStruct((n_chunks * r, n), dtype)

    conv_spec = pl.BlockSpec((CONV_W - 1, n_seq, CONV_DIM), lambda c: (0, c, 0))
    outs = [out(CHUNK, D_INNER, F32), out(CHUNK, D_INNER, F32), out(n_slot_rows, D_INNER, BF16),
            out(n_slot_rows, SSM_GROUPS * D_STATE, BF16), out(n_slot_rows, 2 * SSM_GROUPS * D_STATE, BF16),
            (conv_spec, jax.ShapeDtypeStruct(conv_rows.shape, F32))]
    return pl.pallas_call(
        kern,
        grid=(n_chunks,),
        in_specs=[pl.BlockSpec((CHUNK, CONV_DIM), lambda c: (c, 0)), conv_spec,
                  pl.BlockSpec((CHUNK, 128), lambda c: (c, 0)),
                  pl.BlockSpec((SSM_HEADS, CHUNK), lambda c: (0, c))]
                 + [_resident(a.shape) for a in consts] + [_resident(taps_sel.shape), _resident(place.shape)],
        out_specs=[o[0] for o in outs],
        out_shape=[o[1] for o in outs],
        scratch_shapes=[pltpu.VMEM((taps_sel.shape[1], CONV_DIM), BF16), pltpu.VMEM((CHUNK, D_INNER), F32)],
        compiler_params=_cp(("arbitrary",), 48),
        name="ssd_short",
    )(xbc, conv_rows, dt, dtt, *consts, taps_sel, place)


def _ssd_state_kernel(s0_ref, xw_ref, c_ref, bw_ref, ypre_ref, ec_ref, z_ref, ng_ref, y_ref, sout_ref, ci_sc,
                      *, nb, seq_len):
    for bb in range(nb):
        slots = slice(bb * SLOTS, (bb + 1) * SLOTS)
        toks = slice(bb * seq_len, (bb + 1) * seq_len)
        for g in range(SSM_GROUPS):
            gs = slice(g * GROUP_W, (g + 1) * GROUP_W)
            s0g = s0_ref[bb, gs, :]
            ci_sc[...] = _dot_nt(c_ref[slots, g * D_STATE:(g + 1) * D_STATE], s0g.astype(BF16))
            u = lax.dot_general(xw_ref[slots, gs], bw_ref[slots, 2 * g * D_STATE:2 * (g + 1) * D_STATE],
                                (((0,), (0,)), ((), ())), preferred_element_type=F32)
            sout_ref[bb, gs, :] = u[:, :D_STATE] + s0g * u[:, D_STATE:]
            zf = z_ref[toks, gs]
            yv = (ypre_ref[toks, gs] + ec_ref[toks, gs] * ci_sc[0:seq_len, :]) * (zf * jax.nn.sigmoid(zf))
            y_ref[toks, gs] = _rms(yv, ng_ref[:, gs], RMS_EPS)


def _ssd_state(s0, xw, cmat, bw, ypre, ec, z, ng, nb, seq_len):
    batch = s0.shape[0]
    kern = functools.partial(_ssd_state_kernel, nb=nb, seq_len=seq_len)

    def rows(r, n):
        return pl.BlockSpec((nb * r, n), lambda b: (b, 0))

    state_spec = pl.BlockSpec((nb, D_INNER, D_STATE), lambda b: (b, 0, 0))
    return pl.pallas_call(
        kern,
        grid=(batch // nb,),
        in_specs=[state_spec, rows(SLOTS, D_INNER), rows(SLOTS, SSM_GROUPS * D_STATE),
                  rows(SLOTS, 2 * SSM_GROUPS * D_STATE), rows(seq_len, D_INNER), rows(seq_len, D_INNER),
                  rows(seq_len, D_INNER), _resident(ng.shape)],
        out_specs=[rows(seq_len, D_INNER), state_spec],
        out_shape=[jax.ShapeDtypeStruct((batch * seq_len, D_INNER), F32),
                   jax.ShapeDtypeStruct((batch, D_INNER, D_STATE), F32)],
        scratch_shapes=[pltpu.VMEM((SLOTS, GROUP_W), F32)],
        compiler_params=_cp(("arbitrary",), 40),
        name="ssd_state",
    )(s0, xw, cmat, bw, ypre, ec, z, ng)


def _mem_kv_kernel(m_ref, w_ref, k_ref, v_ref):
    mb = m_ref[...].astype(BF16)
    k_ref[...] = _dot(mb, w_ref[:, :MEM_WIDTH])
    v_ref[...] = _dot(mb, w_ref[:, MEM_WIDTH:])


def _mem_kv(mem2d, w_kv, tm):
    m = mem2d.shape[0]
    return pl.pallas_call(
        _mem_kv_kernel,
        grid=(m // tm,),
        in_specs=[pl.BlockSpec((tm, D_MODEL), lambda i: (i, 0)), _resident(w_kv.shape)],
        out_specs=[pl.BlockSpec((tm, MEM_WIDTH), lambda i: (i, 0))] * 2,
        out_shape=[jax.ShapeDtypeStruct((m, MEM_WIDTH), F32)] * 2,
        compiler_params=_cp(("arbitrary",), 32),
        name="mem_kv",
    )(mem2d, w_kv)


def _mem_attn_kernel(q_ref, k_ref, v_ref, o_ref, *, nb):
    scale = MEM_HEAD_DIM ** -0.5
    for bb in range(nb):
        heads = [slice(h * MEM_HEAD_DIM, (h + 1) * MEM_HEAD_DIM) for h in range(MEM_HEADS)]
        scores = [_dot_nt(q_ref[bb, :, hs], k_ref[bb, :, hs].astype(BF16)) for hs in heads]
        for h, hs in enumerate(heads):
            vh = v_ref[bb, :, hs].astype(BF16)
            s = scores[h] * scale
            p = jnp.exp(s - jnp.max(s, axis=1, keepdims=True))
            l = jnp.sum(p, axis=1, keepdims=True)
            o_ref[bb, :, hs] = (_dot(p.astype(BF16), vh) / l).astype(BF16)


def _mem_attn(mq, mem_k, mem_v, nb, tq):
    batch, rows, _ = mq.shape
    kern = functools.partial(_mem_attn_kernel, nb=nb)
    kv_spec = pl.BlockSpec((nb, N_MEM, MEM_WIDTH), lambda b, i: (b, 0, 0))
    return pl.pallas_call(
        kern,
        grid=(batch // nb, rows // tq),
        in_specs=[pl.BlockSpec((nb, tq, MEM_WIDTH), lambda b, i: (b, i, 0)), kv_spec, kv_spec],
        out_specs=pl.BlockSpec((nb, tq, MEM_WIDTH), lambda b, i: (b, i, 0)),
        out_shape=jax.ShapeDtypeStruct((batch, rows, MEM_WIDTH), BF16),
        compiler_params=_cp(("arbitrary", "arbitrary"), 40),
        name="mem_attn",
    )(mq, mem_k, mem_v)


MEM_HALVES = MEM_HEAD_DIM // 128
MEM_ROW_GROUP = MEM_HALVES * MEM_HEADS


def _mem_attn_cache_kernel(q_ref, k_ref, v_ref, o_ref, *, nb, n_tok):
    rows = MEM_HEADS * n_tok
    n_col = N_MEM * MEM_ROW_GROUP
    col = lax.broadcasted_iota(jnp.int32, (rows, n_col), 1)
    row = lax.broadcasted_iota(jnp.int32, (rows, n_col), 0)
    own = (col % MEM_ROW_GROUP) == (row // n_tok)
    scale = MEM_HEAD_DIM ** -0.5
    for bb in range(nb):
        g = _dot_nt(q_ref[bb], k_ref[bb].astype(BF16))
        s = (g[:rows] + pltpu.roll(g[rows:], n_col - MEM_HEADS, 1)) * scale
        s = jnp.where(own, s, NEG_BIG)
        p = jnp.exp(s - jnp.max(s, axis=1, keepdims=True))
        l = jnp.sum(p, axis=1, keepdims=True)
        p2 = jnp.concatenate([p, pltpu.roll(p, MEM_HEADS, 1)], axis=0).astype(BF16)
        o = _dot(p2, v_ref[bb].astype(BF16))
        o_ref[bb] = (o / jnp.concatenate([l, l], axis=0)).astype(BF16)


def _mem_attn_cache(q, mem_k, mem_v, nb, n_tok):
    batch, rows, _ = q.shape
    kern = functools.partial(_mem_attn_cache_kernel, nb=nb, n_tok=n_tok)
    kv_spec = pl.BlockSpec((nb,) + mem_k.shape[1:], lambda b: (b, 0, 0))
    q_spec = pl.BlockSpec((nb, rows, 128), lambda b: (b, 0, 0))
    return pl.pallas_call(
        kern,
        grid=(batch // nb,),
        in_specs=[q_spec, kv_spec, kv_spec],
        out_specs=q_spec,
        out_shape=jax.ShapeDtypeStruct((batch, rows, 128), BF16),
        compiler_params=_cp(("arbitrary",), 40),
        name="mem_attn_cache",
    )(q, mem_k, mem_v)


def _merge_ffn_kernel(x_ref, oa_ref, ob_ref, om_ref, g_ref, woa_ref, wob_ref, wom_ref, wout_ref,
                      ln1g_ref, ln1b_ref, wup_ref, bup_ref, wdown_ref, bdown_ref, ln2g_ref, ln2b_ref, y_ref):
    n = x_ref.shape[0] // 2
    halves = (slice(0, n), slice(n, 2 * n))
    a = [_dot(oa_ref[r, :], woa_ref[...]) for r in halves]
    b = [_dot(ob_ref[r, :], wob_ref[...]) for r in halves]
    c = [_dot(om_ref[r, :], wom_ref[...]) for r in halves]
    m = [(g_ref[r, 0:D_MODEL].astype(F32) * a[i] + g_ref[r, D_MODEL:2 * D_MODEL].astype(F32) * b[i]
          + g_ref[r, 2 * D_MODEL:].astype(F32) * c[i]).astype(BF16) for i, r in enumerate(halves)]
    t = [_dot(mi, wout_ref[...]) for mi in m]
    x1 = [_layer_norm(ALPHA * x_ref[r, :] + t[i], ln1g_ref[...], ln1b_ref[...]) for i, r in enumerate(halves)]
    u = [_dot(xi.astype(BF16), wup_ref[...]) for xi in x1]
    h = [jnp.square(jnp.maximum(ui + bup_ref[...], 0.0)).astype(BF16) for ui in u]
    d = [_dot(hi, wdown_ref[...]) for hi in h]
    for i, r in enumerate(halves):
        y_ref[r, :] = _layer_norm(ALPHA * x1[i] + d[i] + bdown_ref[...], ln2g_ref[...], ln2b_ref[...])


def _merge_ffn(x2d, o_mla, o_ssm, o_mem, g, weights, tm):
    m = x2d.shape[0]

    def row(n):
        return pl.BlockSpec((tm, n), lambda i: (i, 0))

    return pl.pallas_call(
        _merge_ffn_kernel,
        grid=(m // tm,),
        in_specs=[row(D_MODEL), row(MLA_HEADS * V_DIM), row(D_INNER), row(MEM_WIDTH), row(N_BRANCH * D_MODEL)]
                 + [_resident(w.shape) for w in weights],
        out_specs=row(D_MODEL),
        out_shape=jax.ShapeDtypeStruct((m, D_MODEL), F32),
        compiler_params=_cp(("arbitrary",), 56),
        name="merge_ffn",
    )(x2d, o_mla, o_ssm, o_mem, g, *weights)


def _rope_table(pos):
    half = ROPE_DIM // 2
    inv = ROPE_THETA ** (-jnp.arange(half, dtype=F32) / half)
    ang = pos.astype(F32)[:, None] * inv[None, :]
    cos, sin = jnp.cos(ang), jnp.sin(ang)
    return jnp.concatenate([cos, cos, cos, cos, -sin, sin, -sin, sin], axis=1)


def _expand_matrix(width, terms):
    src = np.arange(SSM_HEADS * width) // width
    one = (np.arange(128)[:, None] == src[None, :]).astype(np.float32)
    return jnp.asarray(np.concatenate([one] * terms, axis=0), dtype=BF16)


def _shift_matrix():
    sel = np.zeros(((CONV_W - 1) * CHUNK, CARRY + CHUNK), np.float32)
    for k in range(CONV_W - 1):
        for t in range(CHUNK):
            sel[k * CHUNK + t, CARRY + t + k - (CONV_W - 1)] = 1.0
    return jnp.asarray(sel, dtype=BF16)


def _short_taps_matrix(seq_len):
    n_seq = CHUNK // seq_len
    n_old = CONV_W - 1
    n_cached = n_old * n_seq
    sel = np.zeros((n_old * CHUNK + n_cached, CHUNK + 3 * n_cached), np.float32)

    def take(out_row, s, m):
        if m >= n_old:
            sel[out_row, s * seq_len + m - n_old] = 1.0
        else:
            for term in range(3):
                sel[out_row, CHUNK + term * n_cached + m * n_seq + s] = 1.0

    for s in range(n_seq):
        for k in range(n_old):
            for i in range(seq_len):
                take(k * CHUNK + s * seq_len + i, s, i + k)
        for j in range(n_old):
            take(n_old * CHUNK + j * n_seq + s, s, seq_len + j)
    return jnp.asarray(sel, dtype=BF16)


def _placement_matrix(seq_len):
    n_seq = CHUNK // seq_len
    place = np.zeros((n_seq * SLOTS, 3 * CHUNK), np.float32)
    for s in range(n_seq):
        for j in range(seq_len):
            place[s * SLOTS + j, s * seq_len + j] = 1.0
        place[s * SLOTS + seq_len, CHUNK + s * seq_len] = 1.0
        place[s * SLOTS + seq_len + 1, 2 * CHUNK + s * seq_len] = 1.0
    return jnp.asarray(place, dtype=BF16)


def _row(v):
    return v.reshape(1, -1).astype(F32)


def kernel(x_prompt, x_sample, mem_prompt, cache_ckv, cache_krope, page_table, cache_mem_k, cache_mem_v, state_ssm, state_conv, w_in, q_norm_g, w_uq, kv_norm_g, w_uk, w_uv, conv_w, conv_b, dt_bias, a_log, d_skip, ssm_norm_g, w_mem_k, w_mem_v, b_gate, w_o_mla, w_o_ssm, w_o_mem, w_out, ln1_g, ln1_b, w_up, b_up, w_down, b_down, ln2_g, ln2_b):
    bp, seq, _ = x_prompt.shape
    bs, t_new, _ = x_sample.shape
    n_pages = page_table.shape[1]
    past = n_pages * PAGE_SIZE

    o_cq, o_ckv, o_kr, o_z, o_xbc, o_dt, o_mq, o_g = np.cumsum(
        [0, Q_LORA, KV_LORA, ROPE_DIM, D_INNER, CONV_DIM, SSM_HEADS, MEM_WIDTH]).tolist()
    w_kr = w_in[:, o_kr:o_z]
    w_small = jnp.concatenate([w_in[:, o_ckv:o_kr], w_kr, w_kr, w_in[:, o_dt:o_mq],
                               jnp.zeros((D_MODEL, 128 - SSM_HEADS), F32)], axis=1)
    w_proj = tuple(w.astype(BF16) for w in (w_in[:, o_cq:o_ckv], w_small, w_in[:, o_z:o_xbc], w_in[:, o_xbc:o_dt],
                                            w_in[:, o_mq:o_g], w_in[:, o_g:]))
    half = ROPE_DIM // 2
    w_q_nope = w_uq[:, :, :NOPE_DIM].reshape(Q_LORA, MLA_HEADS * NOPE_DIM)
    w_q_rope = w_uq[:, :, NOPE_DIM:]
    w_q_swap = jnp.concatenate([w_q_rope[:, :, half:], w_q_rope[:, :, :half]], axis=-1)
    pair_w = 2 * ROPE_DIM
    w_q_pairs = jnp.concatenate([w_q_rope.reshape(Q_LORA, MLA_HEADS // 2, pair_w),
                                 w_q_swap.reshape(Q_LORA, MLA_HEADS // 2, pair_w)], axis=-1)
    wq = jnp.concatenate([w_q_nope, w_q_pairs.reshape(Q_LORA, -1)], axis=1).astype(BF16)
    wuk_t = jnp.transpose(w_uk, (1, 2, 0)).astype(BF16)
    wuv = jnp.transpose(w_uv, (1, 0, 2)).astype(BF16)
    wuv_t = jnp.transpose(w_uv, (1, 2, 0)).astype(BF16)
    w_mem_kv = jnp.concatenate([w_mem_k.reshape(D_MODEL, MEM_WIDTH), w_mem_v.reshape(D_MODEL, MEM_WIDTH)], axis=1).astype(BF16)
    merge_w = (w_o_mla.astype(BF16), w_o_ssm.astype(BF16), w_o_mem.astype(BF16), w_out.astype(BF16),
               _row(ln1_g), _row(ln1_b), w_up.astype(BF16), _row(b_up), w_down.astype(BF16), _row(b_down),
               _row(ln2_g), _row(ln2_b))
    gq, gkv, bg = _row(q_norm_g), _row(kv_norm_g), _row(b_gate)

    pad_heads = jnp.zeros((128 - SSM_HEADS,), F32)
    ssd_head = (conv_w.astype(F32), _row(conv_b),
                _row(jnp.concatenate([dt_bias, pad_heads])),
                jnp.broadcast_to(dt_bias.astype(F32)[:, None], (SSM_HEADS, CHUNK)),
                _row(jnp.concatenate([a_log, pad_heads])),
                jnp.broadcast_to(a_log.astype(F32)[:, None], (SSM_HEADS, CHUNK)),
                _row(jnp.repeat(d_skip, SSM_HEAD_DIM)))
    norm_g = _row(ssm_norm_g)
    expanders = (_expand_matrix(SSM_HEAD_DIM, 2), _expand_matrix(128, 3))

    mp = bp * seq
    n_chunks = seq // CHUNK
    cs_p = _rope_table(jnp.arange(seq))
    xp2d = x_prompt.reshape(mp, D_MODEL)
    cqn, ckv_p, kr_p, kcat, ckvt, dt_p, dtt_p, z_p, xbc_p, mq_p, g_p = _in_proj(xp2d, w_proj, cs_p, gq, gkv, bg, tm=TOKEN_TILE)
    qt_p = _q_prep(cqn, wq, wuk_t, cs_p, tm=TOKEN_TILE, transposed=True)
    o_mla_p = _prompt_attn(qt_p, kcat, ckvt, wuv_t, bp, seq)

    o_ssm_p, ssm_p, conv_p = _ssd(xbc_p, z_p, dt_p, dtt_p, ssd_head + (norm_g,) + expanders + (_shift_matrix(),),
                                  batch=bp, n_chunks=n_chunks)

    mem_k_p, mem_v_p = _mem_kv(mem_prompt.reshape(bp * N_MEM, D_MODEL), w_mem_kv, tm=min(MEM_ROWS_TILE, bp * N_MEM))
    o_mem_p = _mem_attn(mq_p.reshape(bp, seq, MEM_WIDTH), mem_k_p.reshape(bp, N_MEM, MEM_WIDTH),
                        mem_v_p.reshape(bp, N_MEM, MEM_WIDTH), nb=1, tq=min(MEM_ROWS_TILE, seq))
    y_p = _merge_ffn(xp2d, o_mla_p, o_ssm_p.reshape(mp, D_INNER), o_mem_p.reshape(mp, MEM_WIDTH), g_p, merge_w, tm=TOKEN_TILE)

    ms = bs * t_new
    cs_s = jnp.tile(_rope_table(past + jnp.arange(t_new)), (bs, 1))
    xs2d = x_sample.reshape(ms, D_MODEL)
    tm_s = min(TOKEN_TILE, ms)
    cqn, ckv_s, kr_s, kcat, _, dt_s, dtt_s, z_s, xbc_s, mq_s, g_s = _in_proj(xs2d, w_proj, cs_s, gq, gkv, bg, tm=tm_s)
    q_s = _q_prep(cqn, wq, wuk_t, cs_s, tm=tm_s, transposed=False)
    q_s = jnp.transpose(q_s.reshape(MLA_HEADS, bs, t_new, QK_DIM), (1, 0, 2, 3)).reshape(bs, MLA_HEADS * t_new, QK_DIM)
    o_lat = _decode_attn(page_table, q_s, kcat.reshape(bs, t_new, QK_DIM).astype(F32), cache_ckv,
                         jnp.swapaxes(cache_krope, 1, 2))
    o_lat = jnp.transpose(o_lat.reshape(bs, MLA_HEADS, t_new, KV_LORA), (1, 0, 2, 3)).reshape(MLA_HEADS, ms, KV_LORA)
    o_mla_s = _uv_proj(o_lat, wuv)

    ypre_s, ec_s, xw_s, c_s, bw_s, conv_rows = _ssd_short(
        xbc_s, jnp.transpose(state_conv.astype(F32), (1, 0, 2)), dt_s, dtt_s, ssd_head + expanders,
        _short_taps_matrix(t_new), _placement_matrix(t_new), seq_len=t_new)
    conv_s = jnp.transpose(conv_rows, (1, 0, 2))
    o_ssm_s, ssm_s = _ssd_state(state_ssm.astype(F32).reshape(bs, D_INNER, D_STATE), xw_s, c_s, bw_s, ypre_s, ec_s,
                                z_s.astype(F32), norm_g, nb=SAMPLE_ROWS, seq_len=t_new)

    def cache_rows(c):
        c = c.reshape(bs, N_MEM, MEM_HEADS, MEM_HALVES, 128)
        return jnp.transpose(c, (0, 1, 3, 2, 4)).reshape(bs, N_MEM * MEM_ROW_GROUP, 128)

    mq_rows = jnp.transpose(mq_s.reshape(bs, t_new, MEM_HEADS, MEM_HALVES, 128), (0, 3, 2, 1, 4))
    o_mem_s = _mem_attn_cache(mq_rows.reshape(bs, MEM_ROW_GROUP * t_new, 128), cache_rows(cache_mem_k),
                              cache_rows(cache_mem_v), nb=SAMPLE_ROWS, n_tok=t_new)
    o_mem_s = jnp.transpose(o_mem_s.reshape(bs, MEM_HALVES, MEM_HEADS, t_new, 128), (0, 3, 2, 1, 4))
    y_s = _merge_ffn(xs2d, o_mla_s, o_ssm_s.reshape(ms, D_INNER).astype(BF16), o_mem_s.reshape(ms, MEM_WIDTH), g_s,
                     merge_w, tm=tm_s)

    return (y_p.reshape(bp, seq, D_MODEL), y_s.reshape(bs, t_new, D_MODEL),
            ckv_p.reshape(bp, seq, KV_LORA), kr_p.reshape(bp, seq, ROPE_DIM),
            mem_k_p.reshape(bp, N_MEM, MEM_HEADS, MEM_HEAD_DIM), mem_v_p.reshape(bp, N_MEM, MEM_HEADS, MEM_HEAD_DIM),
            ssm_p.reshape(bp, SSM_HEADS, SSM_HEAD_DIM, D_STATE), conv_p,
            ckv_s.reshape(bs, t_new, KV_LORA), kr_s.reshape(bs, t_new, ROPE_DIM),
            ssm_s.reshape(bs, SSM_HEADS, SSM_HEAD_DIM, D_STATE), conv_s)
```

```python
import functools

import jax
import jax.numpy as jnp
import numpy as np
from jax import lax
from jax.experimental import pallas as pl
from jax.experimental.pallas import tpu as pltpu

F32 = jnp.float32
BF16 = jnp.bfloat16

D_MODEL = 1024
MLA_HEADS = 8
Q_LORA = 384
KV_LORA = 256
NOPE_DIM = 128
ROPE_DIM = 64
V_DIM = 128
ROPE_THETA = 10000.0
QK_DIM = KV_LORA + ROPE_DIM
HEAD_QK = 256
SSM_HEADS = 32
SSM_HEAD_DIM = 64
D_INNER = SSM_HEADS * SSM_HEAD_DIM
SSM_GROUPS = 4
GROUP_W = D_INNER // SSM_GROUPS
D_STATE = 128
CONV_W = 4
CONV_DIM = D_INNER + 2 * SSM_GROUPS * D_STATE
CHUNK = 128
N_MEM = 256
MEM_HEADS = 4
MEM_HEAD_DIM = 256
MEM_WIDTH = MEM_HEADS * MEM_HEAD_DIM
D_FF = 4 * D_MODEL
N_BRANCH = 3
DEPTH = 1
ALPHA = (2 * DEPTH) ** 0.25
LN_EPS = 1e-5
RMS_EPS = 1e-6
PAGE_SIZE = 128
SCORE_SCALE = (NOPE_DIM + ROPE_DIM) ** -0.5
NEG_BIG = -1e30
KV_TILE = 256

TOKEN_TILE = 256
MEM_ROWS_TILE = 512
SAMPLE_ROWS = 4


def _cp(sem, vmem_mb):
    return pltpu.CompilerParams(dimension_semantics=sem, vmem_limit_bytes=vmem_mb << 20)


def _resident(shape):
    nd = len(shape)
    return pl.BlockSpec(shape, lambda *_: (0,) * nd, pipeline_mode=pl.Buffered(1))


def _dot(a, b):
    return jnp.dot(a, b, preferred_element_type=F32)


def _dot_nt(a, b):
    return lax.dot_general(a, b, (((1,), (1,)), ((), ())), preferred_element_type=F32)


def _rms(v, g, eps):
    return v * lax.rsqrt(jnp.mean(v * v, axis=-1, keepdims=True) + eps) * g


def _layer_norm(v, g, b):
    mu = jnp.mean(v, axis=-1, keepdims=True)
    d = v - mu
    var = jnp.mean(d * d, axis=-1, keepdims=True)
    return d * lax.rsqrt(var + LN_EPS) * g + b


def _in_proj_kernel(x_ref, wcq_ref, wsmall_ref, wz_ref, wxbc_ref, wmq_ref, wg_ref, cs_ref, gq_ref, gkv_ref, bg_ref,
                    *rest, per_head_kv):
    if per_head_kv:
        wuk_ref, wuvt_ref = rest[:2]
        rest = rest[2:]
    cqn_ref, ckv_ref, kr_ref, kcat_ref, dt_ref, dtt_ref, z_ref, xbc_ref, mq_ref, g_ref = rest[:10]
    xb = x_ref[...].astype(BF16)
    cqn_ref[...] = _rms(_dot(xb, wcq_ref[...]), gq_ref[...], RMS_EPS).astype(BF16)

    small = _dot(xb, wsmall_ref[...])
    ckv = _rms(small[:, :KV_LORA], gkv_ref[...], RMS_EPS)
    ckv_ref[...] = ckv
    ckv_b = ckv.astype(BF16)
    kcat_ref[:, 0:KV_LORA] = ckv_b
    a = small[:, 256:384]
    b = pltpu.roll(a, ROPE_DIM // 2, 1)
    cs = cs_ref[...]
    ro = a * cs[:, :128] + b * cs[:, 128:]
    kr_ref[...] = ro[:, :ROPE_DIM]
    kcat_ref[:, KV_LORA:QK_DIM] = ro[:, :ROPE_DIM].astype(BF16)
    if per_head_kv:
        kh_ref, vt_ref = rest[10:]
        lane = lax.broadcasted_iota(jnp.int32, ro.shape, 1)
        rope_pad = jnp.where(lane < ROPE_DIM, ro, 0.0).astype(BF16)
        ckv_t = [ckv[t * KV_TILE:(t + 1) * KV_TILE, :].T.astype(BF16) for t in range(vt_ref.shape[1])]
        for p in range(MLA_HEADS // 2):
            kk = _dot(ckv_b, wuk_ref[p]).astype(BF16)
            for i in range(2):
                kh_ref[2 * p + i, :, 0:NOPE_DIM] = kk[:, i * NOPE_DIM:(i + 1) * NOPE_DIM]
                kh_ref[2 * p + i, :, NOPE_DIM:HEAD_QK] = rope_pad
            for t, ct in enumerate(ckv_t):
                vv = _dot(wuvt_ref[p], ct).astype(BF16)
                vt_ref[2 * p, t] = vv[:V_DIM]
                vt_ref[2 * p + 1, t] = vv[V_DIM:]
    dt = small[:, 384:512]
    dt_ref[...] = dt
    dtt_ref[...] = dt.T[0:SSM_HEADS, :]

    for c in range(0, D_INNER, 1024):
        z_ref[:, c:c + 1024] = _dot(xb, wz_ref[:, c:c + 1024]).astype(BF16)
    for c in range(0, CONV_DIM, 1024):
        xbc_ref[:, c:c + 1024] = _dot(xb, wxbc_ref[:, c:c + 1024]).astype(BF16)
    mq_ref[...] = _dot(xb, wmq_ref[...]).astype(BF16)
    for c in range(0, N_BRANCH * D_MODEL, 1024):
        gr = _dot(xb, wg_ref[:, c:c + 1024]) + bg_ref[:, c:c + 1024]
        g_ref[:, c:c + 1024] = jax.nn.sigmoid(gr).astype(BF16)


def _in_proj(x2d, weights, cs_tab, gq, gkv, bg, tm, head_kv_weights=()):
    m = x2d.shape[0]
    ncs = cs_tab.shape[0] // tm

    def row(n, dtype):
        return pl.BlockSpec((tm, n), lambda i: (i, 0)), jax.ShapeDtypeStruct((m, n), dtype)

    outs = [row(Q_LORA, BF16), row(KV_LORA, F32), row(ROPE_DIM, F32), row(QK_DIM, BF16), row(128, F32),
            (pl.BlockSpec((SSM_HEADS, tm), lambda i: (0, i)), jax.ShapeDtypeStruct((SSM_HEADS, m), F32)),
            row(D_INNER, BF16), row(CONV_DIM, BF16), row(MEM_WIDTH, BF16), row(N_BRANCH * D_MODEL, BF16)]
    if head_kv_weights:
        outs += [(pl.BlockSpec((MLA_HEADS, tm, HEAD_QK), lambda i: (0, i, 0)),
                  jax.ShapeDtypeStruct((MLA_HEADS, m, HEAD_QK), BF16)),
                 (pl.BlockSpec((MLA_HEADS, tm // KV_TILE, V_DIM, KV_TILE), lambda i: (0, i, 0, 0)),
                  jax.ShapeDtypeStruct((MLA_HEADS, m // KV_TILE, V_DIM, KV_TILE), BF16))]
    return pl.pallas_call(
        functools.partial(_in_proj_kernel, per_head_kv=bool(head_kv_weights)),
        grid=(m // tm,),
        in_specs=[pl.BlockSpec((tm, D_MODEL), lambda i: (i, 0))] + [_resident(w.shape) for w in weights]
                 + [pl.BlockSpec((tm, 256), lambda i: (i % ncs, 0)),
                    _resident(gq.shape), _resident(gkv.shape), _resident(bg.shape)]
                 + [_resident(w.shape) for w in head_kv_weights],
        out_specs=[o[0] for o in outs],
        out_shape=[o[1] for o in outs],
        compiler_params=_cp(("arbitrary",), 56),
        name="in_proj",
    )(x2d, *weights, cs_tab, gq, gkv, bg, *head_kv_weights)


def _q_prep_kernel(c_ref, wq_ref, wuk_ref, cs_ref, q_ref, *, absorb):
    c = c_ref[...]
    cs = cs_ref[...]
    qn = [_dot(c, wq_ref[:, h * NOPE_DIM:(h + 1) * NOPE_DIM]) for h in range(MLA_HEADS)]
    for h in range(MLA_HEADS):
        if absorb:
            q_ref[h, :, 0:KV_LORA] = (_dot(qn[h].astype(BF16), wuk_ref[h]) * SCORE_SCALE).astype(BF16)
        else:
            q_ref[h, 0:NOPE_DIM, :] = (qn[h] * SCORE_SCALE).T.astype(BF16)
            q_ref[h, NOPE_DIM + ROPE_DIM:HEAD_QK, :] = jnp.zeros((HEAD_QK - NOPE_DIM - ROPE_DIM, c.shape[0]), BF16)
    r0 = MLA_HEADS * NOPE_DIM
    for p in range(MLA_HEADS // 2):
        ab = _dot(c, wq_ref[:, r0 + p * 256:r0 + (p + 1) * 256])
        ro = (ab[:, :128] * cs[:, :128] + ab[:, 128:] * cs[:, 128:]) * SCORE_SCALE
        if absorb:
            q_ref[2 * p, :, KV_LORA:QK_DIM] = ro[:, :ROPE_DIM].astype(BF16)
            q_ref[2 * p + 1, :, KV_LORA:QK_DIM] = pltpu.roll(ro, ROPE_DIM, 1)[:, :ROPE_DIM].astype(BF16)
        else:
            rot = ro.T.astype(BF16)
            q_ref[2 * p, NOPE_DIM:NOPE_DIM + ROPE_DIM, :] = rot[:ROPE_DIM]
            q_ref[2 * p + 1, NOPE_DIM:NOPE_DIM + ROPE_DIM, :] = rot[ROPE_DIM:]


def _q_prep(cqn, wq, wuk_t, cs_tab, tm, absorb):
    m = cqn.shape[0]
    ncs = cs_tab.shape[0] // tm
    if absorb:
        out_spec = pl.BlockSpec((MLA_HEADS, tm, QK_DIM), lambda i: (0, i, 0))
        out_shape = jax.ShapeDtypeStruct((MLA_HEADS, m, QK_DIM), BF16)
    else:
        out_spec = pl.BlockSpec((MLA_HEADS, HEAD_QK, tm), lambda i: (0, 0, i))
        out_shape = jax.ShapeDtypeStruct((MLA_HEADS, HEAD_QK, m), BF16)
    return pl.pallas_call(
        functools.partial(_q_prep_kernel, absorb=absorb),
        grid=(m // tm,),
        in_specs=[pl.BlockSpec((tm, Q_LORA), lambda i: (i, 0)), _resident(wq.shape), _resident(wuk_t.shape),
                  pl.BlockSpec((tm, 256), lambda i: (i % ncs, 0))],
        out_specs=out_spec,
        out_shape=out_shape,
        compiler_params=_cp(("arbitrary",), 32),
        name="q_prep",
    )(cqn, wq, wuk_t, cs_tab)


def _prompt_attn_kernel(qt_ref, k_ref, vt_ref, o_ref, m_sc, l_sc, acc_sc):
    i = pl.program_id(1)
    t = KV_TILE
    m_sc[...] = jnp.full(m_sc.shape, NEG_BIG, F32)
    l_sc[...] = jnp.zeros(l_sc.shape, F32)
    acc_sc[...] = jnp.zeros(acc_sc.shape, F32)

    def kv_block(j, masked):
        rows = pl.ds(pl.multiple_of(j * t, t), t)
        if masked:
            keep = lax.broadcasted_iota(jnp.int32, (t, t), 0) <= lax.broadcasted_iota(jnp.int32, (t, t), 1)
        pending = [_dot(k_ref[h, rows, :], qt_ref[h]) for h in range(MLA_HEADS)]
        for h in range(MLA_HEADS):
            st = pending[h]
            if masked:
                st = jnp.where(keep, st, NEG_BIG)
            m_prev = m_sc[h:h + 1, :]
            m_new = jnp.maximum(m_prev, jnp.max(st, axis=0, keepdims=True))
            alpha = jnp.exp(m_prev - m_new)
            p = jnp.exp(st - m_new)
            l_sc[h:h + 1, :] = alpha * l_sc[h:h + 1, :] + jnp.sum(p, axis=0, keepdims=True)
            m_sc[h:h + 1, :] = m_new
            acc_sc[h] = alpha * acc_sc[h] + _dot(vt_ref[h, j], p.astype(BF16))

    def body(j, carry):
        kv_block(j, False)
        return carry

    lax.fori_loop(0, i, body, 0)
    kv_block(i, True)

    for h in range(MLA_HEADS):
        o_ref[:, h * V_DIM:(h + 1) * V_DIM] = (acc_sc[h] / l_sc[h:h + 1, :]).T.astype(BF16)


def _prompt_attn(qt, k_heads, vt_heads, batch, seq):
    t = KV_TILE
    nq = seq // t
    return pl.pallas_call(
        _prompt_attn_kernel,
        grid=(batch, nq),
        in_specs=[pl.BlockSpec((MLA_HEADS, HEAD_QK, t), lambda b, i: (0, 0, b * nq + i)),
                  pl.BlockSpec((MLA_HEADS, seq, HEAD_QK), lambda b, i: (0, b, 0)),
                  pl.BlockSpec((MLA_HEADS, nq, V_DIM, t), lambda b, i: (0, b, 0, 0))],
        out_specs=pl.BlockSpec((t, MLA_HEADS * V_DIM), lambda b, i: (b * nq + i, 0)),
        out_shape=jax.ShapeDtypeStruct((batch * seq, MLA_HEADS * V_DIM), BF16),
        scratch_shapes=[pltpu.VMEM((MLA_HEADS, t), F32), pltpu.VMEM((MLA_HEADS, t), F32),
                        pltpu.VMEM((MLA_HEADS, V_DIM, t), F32)],
        compiler_params=_cp(("arbitrary", "arbitrary"), 48),
        name="prompt_attn",
    )(qt, k_heads, vt_heads)


def _decode_attn_kernel(pt_ref, q_ref, kn_ref, ckv_hbm, krt_hbm, o_ref, kbuf, rbuf, kb_sc, s_sc, sem, *, n_pages, n_new):
    b = pl.program_id(0)
    nb = pl.num_programs(0)
    slot = b % 2

    def page_copies(page, p, sl):
        rows = pl.ds(pl.multiple_of(p * PAGE_SIZE, PAGE_SIZE), PAGE_SIZE)
        return (pltpu.make_async_copy(ckv_hbm.at[page], kbuf.at[sl, rows, :], sem.at[0, sl]),
                pltpu.make_async_copy(krt_hbm.at[page], rbuf.at[sl, p], sem.at[1, sl]))

    def issue(bi, sl):
        def body(p, carry):
            for cp in page_copies(pt_ref[bi * n_pages + p], p, sl):
                cp.start()
            return carry
        lax.fori_loop(0, n_pages, body, 0, unroll=8)

    @pl.when(b == 0)
    def _():
        issue(0, 0)

    @pl.when(b + 1 < nb)
    def _():
        issue(b + 1, 1 - slot)

    def wait_body(p, carry):
        for cp in page_copies(0, p, slot):
            cp.wait()
        return carry
    lax.fori_loop(0, n_pages, wait_body, 0, unroll=8)

    q = q_ref[0]
    rows = q.shape[0]
    qc = q[:, :KV_LORA]
    qr = q[:, KV_LORA:]
    qf = q.astype(F32)
    kn = kn_ref[0]
    tok = lax.broadcasted_iota(jnp.int32, (rows, 1), 0) % n_new

    s_new = []
    for j in range(n_new):
        sj = jnp.sum(qf * kn[j:j + 1, :], axis=1, keepdims=True)
        s_new.append(jnp.where(tok >= j, sj, NEG_BIG))
    m0 = s_new[0]
    for j in range(1, n_new):
        m0 = jnp.maximum(m0, s_new[j])

    for c in range(n_pages // 2):
        r = slice(2 * c * PAGE_SIZE, 2 * (c + 1) * PAGE_SIZE)
        kc = kbuf[slot, r, :].astype(BF16)
        kb_sc[r, :] = kc
        rt = jnp.concatenate([rbuf[slot, 2 * c], rbuf[slot, 2 * c + 1]], axis=1).astype(BF16)
        s_sc[:, r] = _dot_nt(qc, kc) + _dot(qr, rt)
    s = s_sc[...]
    m = jnp.maximum(m0, jnp.max(s, axis=1, keepdims=True))
    p = jnp.exp(s - m)
    l = jnp.sum(p, axis=1, keepdims=True)
    acc = _dot(p.astype(BF16), kb_sc[...])
    for j in range(n_new):
        pj = jnp.exp(s_new[j] - m)
        l = l + pj
        acc = acc + pj * kn[j:j + 1, :KV_LORA]
    o_ref[0] = (acc / l).astype(BF16)


def _decode_attn(page_table, q, k_new, cache_ckv, cache_krope_t):
    nb, n_pages = page_table.shape
    rows = q.shape[1]
    n_new = k_new.shape[1]
    past = n_pages * PAGE_SIZE
    kern = functools.partial(_decode_attn_kernel, n_pages=n_pages, n_new=n_new)
    grid_spec = pltpu.PrefetchScalarGridSpec(
        num_scalar_prefetch=1,
        grid=(nb,),
        in_specs=[pl.BlockSpec((1, rows, QK_DIM), lambda b, pt: (b, 0, 0)),
                  pl.BlockSpec((1, n_new, QK_DIM), lambda b, pt: (b, 0, 0)),
                  pl.BlockSpec(memory_space=pl.ANY),
                  pl.BlockSpec(memory_space=pl.ANY)],
        out_specs=pl.BlockSpec((1, rows, KV_LORA), lambda b, pt: (b, 0, 0)),
        scratch_shapes=[pltpu.VMEM((2, past, KV_LORA), F32), pltpu.VMEM((2, n_pages, ROPE_DIM, PAGE_SIZE), F32),
                        pltpu.VMEM((past, KV_LORA), BF16), pltpu.VMEM((rows, past), F32),
                        pltpu.SemaphoreType.DMA((2, 2))],
    )
    return pl.pallas_call(
        kern,
        grid_spec=grid_spec,
        out_shape=jax.ShapeDtypeStruct((nb, rows, KV_LORA), BF16),
        compiler_params=_cp(("arbitrary",), 40),
        name="decode_attn",
    )(page_table.reshape(-1), q, k_new, cache_ckv, cache_krope_t)


def _uv_proj_kernel(o_ref, w_ref, out_ref):
    out_ref[...] = _dot(o_ref[0], w_ref[0]).astype(BF16)


def _uv_proj(o_lat, wuv):
    m = o_lat.shape[1]
    return pl.pallas_call(
        _uv_proj_kernel,
        grid=(MLA_HEADS,),
        in_specs=[pl.BlockSpec((1, m, KV_LORA), lambda h: (h, 0, 0)),
                  pl.BlockSpec((1, KV_LORA, V_DIM), lambda h: (h, 0, 0))],
        out_specs=pl.BlockSpec((m, V_DIM), lambda h: (0, h)),
        out_shape=jax.ShapeDtypeStruct((m, MLA_HEADS * V_DIM), BF16),
        compiler_params=_cp(("arbitrary",), 16),
        name="uv_proj",
    )(o_lat, wuv)


def _split_dot(v, e, terms, left=True):
    parts = []
    r = v
    for _ in range(terms):
        hi = r.astype(BF16)
        parts.append(hi)
        r = r - hi.astype(F32)
    if left:
        return _dot(jnp.concatenate(parts, axis=1), e)
    return _dot(e, jnp.concatenate(parts, axis=0))


def _conv_silu(taps, cw_ref, cb_ref):
    conv = cb_ref[...] + cw_ref[0:1, :] * taps[0]
    for k in range(1, CONV_W):
        conv = conv + cw_ref[k:k + 1, :] * taps[k]
    xc = conv * jax.nn.sigmoid(conv)
    n_bc = SSM_GROUPS * D_STATE
    return xc[:, :D_INNER], xc[:, D_INNER:D_INNER + n_bc], xc[:, D_INNER + n_bc:]


def _select_rows(select, rows_ref):
    n = rows_ref.shape[1]
    return jnp.concatenate([_dot(select, rows_ref[:, c:c + 1024]) for c in range(0, n, 1024)], axis=1)


def _ssd_sums(dt, dtt, keep, keep_t, alog_ref, alogt_ref):
    tri = jnp.where(keep, 1.0, 0.0).astype(BF16)
    tri_t = jnp.where(keep_t, 1.0, 0.0).astype(BF16)
    da = dt * (-jnp.exp(alog_ref[...]))
    dat = dtt * (-jnp.exp(alogt_ref[...]))
    cum = _split_dot(da, jnp.concatenate([tri] * 3, axis=1), 3, left=False)
    cum_t = _split_dot(dat, jnp.concatenate([tri_t] * 3, axis=0), 3)
    return da, cum, cum_t


def _ssd_intra(xs, bm, cm, cum_e, cum_t, dtt, keep, dskip_ref, y_sc, carried=None):
    pairs_per_group = SSM_HEADS // 2 // SSM_GROUPS
    lane = lax.broadcasted_iota(jnp.int32, (CHUNK, CHUNK), 1)
    for g in range(SSM_GROUPS):
        cg = cm[:, g * D_STATE:(g + 1) * D_STATE].astype(BF16)
        cb = _dot_nt(cg, bm[:, g * D_STATE:(g + 1) * D_STATE].astype(BF16))
        extra = None if carried is None else carried(g, cg)
        for pr in range(pairs_per_group):
            q = g * pairs_per_group + pr
            ws = []
            for h in (2 * q, 2 * q + 1):
                seg = cum_e[:, h * 128:(h + 1) * 128] - cum_t[h:h + 1, :]
                decay = jnp.exp(jnp.where(keep, seg, NEG_BIG)) * dtt[h:h + 1, :]
                ws.append((cb * decay).astype(BF16))
            w_pair = jnp.concatenate(ws, axis=1)
            xp = xs[:, q * 128:(q + 1) * 128]
            x_bd = jnp.concatenate([jnp.where(lane < SSM_HEAD_DIM, xp, 0.0).astype(BF16),
                                    jnp.where(lane >= SSM_HEAD_DIM, xp, 0.0).astype(BF16)], axis=0)
            y = _dot(w_pair, x_bd) + dskip_ref[:, q * 128:(q + 1) * 128] * xp
            if extra is not None:
                y = y + extra[:, pr * 128:(pr + 1) * 128]
            y_sc[:, q * 128:(q + 1) * 128] = y


CARRY = 16


def _ssd_kernel(xbc_ref, z_ref, dt_ref, dtt_ref, cw_ref, cb_ref, dtb_ref, dtbt_ref,
                alog_ref, alogt_ref, dskip_ref, ng_ref, e64_ref, e128_ref, shift_ref,
                y_ref, sout_ref, cout_ref, xe_sc, tail_sc, st_sc, y_sc):
    c = pl.program_id(1)
    n_pairs = SSM_HEADS // 2
    pairs_per_group = n_pairs // SSM_GROUPS

    @pl.when(c == 0)
    def _():
        xe_sc[0:CARRY, :] = jnp.zeros((CARRY, CONV_DIM), BF16)
        st_sc[...] = jnp.zeros(st_sc.shape, F32)

    xe_sc[CARRY:CARRY + CHUNK, :] = xbc_ref[...]
    shifted = _select_rows(shift_ref[...], xe_sc)
    taps = [shifted[k * CHUNK:(k + 1) * CHUNK] for k in range(CONV_W - 1)] + [xbc_ref[...].astype(F32)]
    xs, bm, cm = _conv_silu(taps, cw_ref, cb_ref)
    xe_sc[0:CARRY, :] = xe_sc[CHUNK:CHUNK + CARRY, :]

    rowi = lax.broadcasted_iota(jnp.int32, (CHUNK, CHUNK), 0)
    coli = lax.broadcasted_iota(jnp.int32, (CHUNK, CHUNK), 1)
    keep = coli <= rowi
    dt = jax.nn.softplus(dt_ref[...] + dtb_ref[...])
    dtt = jax.nn.softplus(dtt_ref[...] + dtbt_ref[...])
    _, cum, cum_t = _ssd_sums(dt, dtt, keep, rowi <= coli, alog_ref, alogt_ref)
    last = cum[CHUNK - 1:CHUNK, :]
    e64 = e64_ref[...]
    ec_e = _split_dot(jnp.exp(cum), e64, 2)
    wb_e = _split_dot(dt * jnp.exp(last - cum), e64, 2)
    dec_e = _split_dot(jnp.broadcast_to(jnp.exp(last), (16, CHUNK)), e64, 2)[0:1, :]
    cum_e = _split_dot(cum, e128_ref[...], 3)

    def carried(g, cg):
        return _dot(cg, st_sc[g].astype(BF16)) * ec_e[:, g * GROUP_W:(g + 1) * GROUP_W]

    _ssd_intra(xs, bm, cm, cum_e, cum_t, dtt, keep, dskip_ref, y_sc, carried)
    for g in range(SSM_GROUPS):
        gs = slice(g * GROUP_W, (g + 1) * GROUP_W)
        xw = (xs[:, gs] * wb_e[:, gs]).astype(BF16)
        bt = bm[:, g * D_STATE:(g + 1) * D_STATE].T.astype(BF16)
        st_sc[g] = st_sc[g] * dec_e[:, gs] + _dot(bt, xw)

    for g in range(SSM_GROUPS):
        gs = slice(g * GROUP_W, (g + 1) * GROUP_W)
        zf = z_ref[:, gs].astype(F32)
        yv = y_sc[:, gs] * (zf * jax.nn.sigmoid(zf))
        y_ref[:, gs] = _rms(yv, ng_ref[:, gs], RMS_EPS).astype(y_ref.dtype)

    @pl.when(c == pl.num_programs(1) - 1)
    def _():
        tail_sc[...] = xbc_ref[CHUNK - CARRY:CHUNK, :].astype(F32)
        cout_ref[0] = tail_sc[CARRY - (CONV_W - 1):CARRY, :]
        for q in range(n_pairs):
            g, pr = divmod(q, pairs_per_group)
            sout_ref[0, q * 128:(q + 1) * 128, :] = st_sc[g, :, pr * 128:(pr + 1) * 128].T


def _ssd(xbc, z, dt, dtt, consts, batch, n_chunks):
    def step(n):
        return pl.BlockSpec((CHUNK, n), lambda b, c: (b * n_chunks + c, 0))

    def per_batch(r, n):
        return pl.BlockSpec((1, r, n), lambda b, c: (b, 0, 0))

    return pl.pallas_call(
        _ssd_kernel,
        grid=(batch, n_chunks),
        in_specs=[step(CONV_DIM), step(D_INNER), step(128),
                  pl.BlockSpec((SSM_HEADS, CHUNK), lambda b, c: (0, b * n_chunks + c))]
                 + [_resident(a.shape) for a in consts],
        out_specs=[step(D_INNER), per_batch(D_INNER, D_STATE), per_batch(CONV_W - 1, CONV_DIM)],
        out_shape=[jax.ShapeDtypeStruct((batch * n_chunks * CHUNK, D_INNER), BF16),
                   jax.ShapeDtypeStruct((batch, D_INNER, D_STATE), F32),
                   jax.ShapeDtypeStruct((batch, CONV_W - 1, CONV_DIM), F32)],
        scratch_shapes=[pltpu.VMEM((CARRY + CHUNK, CONV_DIM), BF16),
                        pltpu.VMEM((CARRY, CONV_DIM), F32),
                        pltpu.VMEM((SSM_GROUPS, D_STATE, GROUP_W), F32),
                        pltpu.VMEM((CHUNK, D_INNER), F32)],
        compiler_params=_cp(("arbitrary", "arbitrary"), 48),
        name="ssd",
    )(xbc, z, dt, dtt, *consts)


SLOTS = 16


def _ssd_short_kernel(x_ref, sc_ref, dt_ref, dtt_ref, cw_ref, cb_ref, dtb_ref, dtbt_ref, alog_ref, alogt_ref,
                      dskip_ref, e64_ref, e128_ref, taps_sel_ref, place_ref,
                      ypre_ref, ec_ref, xw_ref, c_ref, bw_ref, tails_ref, rows_sc, y_sc, *, seq_len):
    n_cached = sc_ref.shape[0] * sc_ref.shape[1]
    rows_sc[0:CHUNK, :] = x_ref[...]
    r = sc_ref[...].reshape(n_cached, CONV_DIM)
    for t in range(3):
        hi = r.astype(BF16)
        rows_sc[CHUNK + t * n_cached:CHUNK + (t + 1) * n_cached, :] = hi
        r = r - hi.astype(F32)
    picked = _select_rows(taps_sel_ref[...], rows_sc)
    taps = [picked[k * CHUNK:(k + 1) * CHUNK] for k in range(CONV_W - 1)] + [x_ref[...].astype(F32)]
    tails_ref[...] = picked[(CONV_W - 1) * CHUNK:].reshape(tails_ref.shape)
    xs, bm, cm = _conv_silu(taps, cw_ref, cb_ref)
    rowi = lax.broadcasted_iota(jnp.int32, (CHUNK, CHUNK), 0)
    coli = lax.broadcasted_iota(jnp.int32, (CHUNK, CHUNK), 1)
    same = (rowi // seq_len) == (coli // seq_len)
    keep = same & (coli <= rowi)
    dt = jax.nn.softplus(dt_ref[...] + dtb_ref[...])
    dtt = jax.nn.softplus(dtt_ref[...] + dtbt_ref[...])
    da, cum, cum_t = _ssd_sums(dt, dtt, keep, same & (rowi <= coli), alog_ref, alogt_ref)
    whole = jnp.where(same, 1.0, 0.0).astype(BF16)
    last = _split_dot(da, jnp.concatenate([whole] * 3, axis=1), 3, left=False)
    e64 = e64_ref[...]
    ec_ref[...] = _split_dot(jnp.exp(cum), e64, 2)
    wb_e = _split_dot(dt * jnp.exp(last - cum), e64, 2)
    dec_e = _split_dot(jnp.exp(last), e64, 2)
    cum_e = _split_dot(cum, e128_ref[...], 3)
    _ssd_intra(xs, bm, cm, cum_e, cum_t, dtt, keep, dskip_ref, y_sc)
    ypre_ref[...] = y_sc[...]

    place = place_ref[...]
    dec_hi = dec_e.astype(BF16)
    dec_lo = (dec_e - dec_hi.astype(F32)).astype(BF16)
    xw = (xs * wb_e).astype(BF16)
    xw_ref[...] = _dot(place, jnp.concatenate([xw, dec_hi, dec_lo], axis=0)).astype(BF16)
    tokens = place[:, :CHUNK]
    c_ref[...] = _dot(tokens, cm.astype(BF16)).astype(BF16)
    zeros = jnp.zeros((CHUNK, D_STATE), BF16)
    b_wide = jnp.concatenate(
        [piece for g in range(SSM_GROUPS) for piece in (bm[:, g * D_STATE:(g + 1) * D_STATE].astype(BF16), zeros)], axis=1)
    n_rows = place.shape[0]
    slot = lax.broadcasted_iota(jnp.int32, (n_rows, 2 * D_STATE * SSM_GROUPS), 0) % SLOTS
    col = lax.broadcasted_iota(jnp.int32, (n_rows, 2 * D_STATE * SSM_GROUPS), 1) % (2 * D_STATE)
    ones = ((slot == seq_len) | (slot == seq_len + 1)) & (col >= D_STATE)
    bw_ref[...] = jnp.where(ones, 1.0, _dot(tokens, b_wide)).astype(BF16)


def _ssd_short(xbc, conv_rows, dt, dtt, consts, taps_sel, place, seq_len):
    rows = dt.shape[0]
    n_chunks = rows // CHUNK
    n_seq = CHUNK // seq_len
    n_slot_rows = place.shape[0]
    kern = functools.partial(_ssd_short_kernel, seq_len=seq_len)

    def out(r, n, dtype):
        return pl.BlockSpec((r, n), lambda c: (c, 0)), jax.ShapeDtypeStruct((n_chunks * r, n), dtype)

    conv_spec = pl.BlockSpec((CONV_W - 1, n_seq, CONV_DIM), lambda c: (0, c, 0))
    outs = [out(CHUNK, D_INNER, F32), out(CHUNK, D_INNER, F32), out(n_slot_rows, D_INNER, BF16),
            out(n_slot_rows, SSM_GROUPS * D_STATE, BF16), out(n_slot_rows, 2 * SSM_GROUPS * D_STATE, BF16),
            (conv_spec, jax.ShapeDtypeStruct(conv_rows.shape, F32))]
    return pl.pallas_call(
        kern,
        grid=(n_chunks,),
        in_specs=[pl.BlockSpec((CHUNK, CONV_DIM), lambda c: (c, 0)), conv_spec,
                  pl.BlockSpec((CHUNK, 128), lambda c: (c, 0)),
                  pl.BlockSpec((SSM_HEADS, CHUNK), lambda c: (0, c))]
                 + [_resident(a.shape) for a in consts] + [_resident(taps_sel.shape), _resident(place.shape)],
        out_specs=[o[0] for o in outs],
        out_shape=[o[1] for o in outs],
        scratch_shapes=[pltpu.VMEM((taps_sel.shape[1], CONV_DIM), BF16), pltpu.VMEM((CHUNK, D_INNER), F32)],
        compiler_params=_cp(("arbitrary",), 48),
        name="ssd_short",
    )(xbc, conv_rows, dt, dtt, *consts, taps_sel, place)


def _ssd_state_kernel(s0_ref, xw_ref, c_ref, bw_ref, ypre_ref, ec_ref, z_ref, ng_ref, y_ref, sout_ref, ci_sc,
                      *, nb, seq_len):
    for bb in range(nb):
        slots = slice(bb * SLOTS, (bb + 1) * SLOTS)
        toks = slice(bb * seq_len, (bb + 1) * seq_len)
        for g in range(SSM_GROUPS):
            gs = slice(g * GROUP_W, (g + 1) * GROUP_W)
            s0g = s0_ref[bb, gs, :]
            ci_sc[...] = _dot_nt(c_ref[slots, g * D_STATE:(g + 1) * D_STATE], s0g.astype(BF16))
            u = lax.dot_general(xw_ref[slots, gs], bw_ref[slots, 2 * g * D_STATE:2 * (g + 1) * D_STATE],
                                (((0,), (0,)), ((), ())), preferred_element_type=F32)
            sout_ref[bb, gs, :] = u[:, :D_STATE] + s0g * u[:, D_STATE:]
            zf = z_ref[toks, gs]
            yv = (ypre_ref[toks, gs] + ec_ref[toks, gs] * ci_sc[0:seq_len, :]) * (zf * jax.nn.sigmoid(zf))
            y_ref[toks, gs] = _rms(yv, ng_ref[:, gs], RMS_EPS)


def _ssd_state(s0, xw, cmat, bw, ypre, ec, z, ng, nb, seq_len):
    batch = s0.shape[0]
    kern = functools.partial(_ssd_state_kernel, nb=nb, seq_len=seq_len)

    def rows(r, n):
        return pl.BlockSpec((nb * r, n), lambda b: (b, 0))

    state_spec = pl.BlockSpec((nb, D_INNER, D_STATE), lambda b: (b, 0, 0))
    return pl.pallas_call(
        kern,
        grid=(batch // nb,),
        in_specs=[state_spec, rows(SLOTS, D_INNER), rows(SLOTS, SSM_GROUPS * D_STATE),
                  rows(SLOTS, 2 * SSM_GROUPS * D_STATE), rows(seq_len, D_INNER), rows(seq_len, D_INNER),
                  rows(seq_len, D_INNER), _resident(ng.shape)],
        out_specs=[rows(seq_len, D_INNER), state_spec],
        out_shape=[jax.ShapeDtypeStruct((batch * seq_len, D_INNER), F32),
                   jax.ShapeDtypeStruct((batch, D_INNER, D_STATE), F32)],
        scratch_shapes=[pltpu.VMEM((SLOTS, GROUP_W), F32)],
        compiler_params=_cp(("arbitrary",), 40),
        name="ssd_state",
    )(s0, xw, cmat, bw, ypre, ec, z, ng)


def _mem_kv_kernel(m_ref, w_ref, k_ref, v_ref):
    mb = m_ref[...].astype(BF16)
    k_ref[...] = _dot(mb, w_ref[:, :MEM_WIDTH])
    v_ref[...] = _dot(mb, w_ref[:, MEM_WIDTH:])


def _mem_kv(mem2d, w_kv, tm):
    m = mem2d.shape[0]
    return pl.pallas_call(
        _mem_kv_kernel,
        grid=(m // tm,),
        in_specs=[pl.BlockSpec((tm, D_MODEL), lambda i: (i, 0)), _resident(w_kv.shape)],
        out_specs=[pl.BlockSpec((tm, MEM_WIDTH), lambda i: (i, 0))] * 2,
        out_shape=[jax.ShapeDtypeStruct((m, MEM_WIDTH), F32)] * 2,
        compiler_params=_cp(("arbitrary",), 32),
        name="mem_kv",
    )(mem2d, w_kv)


def _mem_attn_kernel(q_ref, k_ref, v_ref, o_ref, *, nb):
    scale = MEM_HEAD_DIM ** -0.5
    for bb in range(nb):
        heads = [slice(h * MEM_HEAD_DIM, (h + 1) * MEM_HEAD_DIM) for h in range(MEM_HEADS)]
        scores = [_dot_nt(q_ref[bb, :, hs], k_ref[bb, :, hs].astype(BF16)) for hs in heads]
        for h, hs in enumerate(heads):
            vh = v_ref[bb, :, hs].astype(BF16)
            s = scores[h] * scale
            p = jnp.exp(s - jnp.max(s, axis=1, keepdims=True))
            l = jnp.sum(p, axis=1, keepdims=True)
            o_ref[bb, :, hs] = (_dot(p.astype(BF16), vh) / l).astype(BF16)


def _mem_attn(mq, mem_k, mem_v, nb, tq):
    batch, rows, _ = mq.shape
    kern = functools.partial(_mem_attn_kernel, nb=nb)
    kv_spec = pl.BlockSpec((nb, N_MEM, MEM_WIDTH), lambda b, i: (b, 0, 0))
    return pl.pallas_call(
        kern,
        grid=(batch // nb, rows // tq),
        in_specs=[pl.BlockSpec((nb, tq, MEM_WIDTH), lambda b, i: (b, i, 0)), kv_spec, kv_spec],
        out_specs=pl.BlockSpec((nb, tq, MEM_WIDTH), lambda b, i: (b, i, 0)),
        out_shape=jax.ShapeDtypeStruct((batch, rows, MEM_WIDTH), BF16),
        compiler_params=_cp(("arbitrary", "arbitrary"), 40),
        name="mem_attn",
    )(mq, mem_k, mem_v)


MEM_HALVES = MEM_HEAD_DIM // 128
MEM_ROW_GROUP = MEM_HALVES * MEM_HEADS


def _mem_attn_cache_kernel(q_ref, k_ref, v_ref, o_ref, *, nb, n_tok):
    rows = MEM_HEADS * n_tok
    n_col = N_MEM * MEM_ROW_GROUP
    col = lax.broadcasted_iota(jnp.int32, (rows, n_col), 1)
    row = lax.broadcasted_iota(jnp.int32, (rows, n_col), 0)
    own = (col % MEM_ROW_GROUP) == (row // n_tok)
    scale = MEM_HEAD_DIM ** -0.5
    for bb in range(nb):
        g = _dot_nt(q_ref[bb], k_ref[bb].astype(BF16))
        s = (g[:rows] + pltpu.roll(g[rows:], n_col - MEM_HEADS, 1)) * scale
        s = jnp.where(own, s, NEG_BIG)
        p = jnp.exp(s - jnp.max(s, axis=1, keepdims=True))
        l = jnp.sum(p, axis=1, keepdims=True)
        p2 = jnp.concatenate([p, pltpu.roll(p, MEM_HEADS, 1)], axis=0).astype(BF16)
        o = _dot(p2, v_ref[bb].astype(BF16))
        o_ref[bb] = (o / jnp.concatenate([l, l], axis=0)).astype(BF16)


def _mem_attn_cache(q, mem_k, mem_v, nb, n_tok):
    batch, rows, _ = q.shape
    kern = functools.partial(_mem_attn_cache_kernel, nb=nb, n_tok=n_tok)
    kv_spec = pl.BlockSpec((nb,) + mem_k.shape[1:], lambda b: (b, 0, 0))
    q_spec = pl.BlockSpec((nb, rows, 128), lambda b: (b, 0, 0))
    return pl.pallas_call(
        kern,
        grid=(batch // nb,),
        in_specs=[q_spec, kv_spec, kv_spec],
        out_specs=q_spec,
        out_shape=jax.ShapeDtypeStruct((batch, rows, 128), BF16),
        compiler_params=_cp(("arbitrary",), 40),
        name="mem_attn_cache",
    )(q, mem_k, mem_v)


def _merge_ffn_kernel(x_ref, oa_ref, ob_ref, om_ref, g_ref, woa_ref, wob_ref, wom_ref, wout_ref,
                      ln1g_ref, ln1b_ref, wup_ref, bup_ref, wdown_ref, bdown_ref, ln2g_ref, ln2b_ref, y_ref):
    n = x_ref.shape[0] // 2
    halves = (slice(0, n), slice(n, 2 * n))
    a = [_dot(oa_ref[r, :], woa_ref[...]) for r in halves]
    b = [_dot(ob_ref[r, :], wob_ref[...]) for r in halves]
    c = [_dot(om_ref[r, :], wom_ref[...]) for r in halves]
    m = [(g_ref[r, 0:D_MODEL].astype(F32) * a[i] + g_ref[r, D_MODEL:2 * D_MODEL].astype(F32) * b[i]
          + g_ref[r, 2 * D_MODEL:].astype(F32) * c[i]).astype(BF16) for i, r in enumerate(halves)]
    t = [_dot(mi, wout_ref[...]) for mi in m]
    x1 = [_layer_norm(ALPHA * x_ref[r, :] + t[i], ln1g_ref[...], ln1b_ref[...]) for i, r in enumerate(halves)]
    u = [_dot(xi.astype(BF16), wup_ref[...]) for xi in x1]
    h = [jnp.square(jnp.maximum(ui + bup_ref[...], 0.0)).astype(BF16) for ui in u]
    d = [_dot(hi, wdown_ref[...]) for hi in h]
    for i, r in enumerate(halves):
        y_ref[r, :] = _layer_norm(ALPHA * x1[i] + d[i] + bdown_ref[...], ln2g_ref[...], ln2b_ref[...])


def _merge_ffn(x2d, o_mla, o_ssm, o_mem, g, weights, tm):
    m = x2d.shape[0]

    def row(n):
        return pl.BlockSpec((tm, n), lambda i: (i, 0))

    return pl.pallas_call(
        _merge_ffn_kernel,
        grid=(m // tm,),
        in_specs=[row(D_MODEL), row(MLA_HEADS * V_DIM), row(D_INNER), row(MEM_WIDTH), row(N_BRANCH * D_MODEL)]
                 + [_resident(w.shape) for w in weights],
        out_specs=row(D_MODEL),
        out_shape=jax.ShapeDtypeStruct((m, D_MODEL), F32),
        compiler_params=_cp(("arbitrary",), 56),
        name="merge_ffn",
    )(x2d, o_mla, o_ssm, o_mem, g, *weights)


def _rope_table(pos):
    half = ROPE_DIM // 2
    inv = ROPE_THETA ** (-jnp.arange(half, dtype=F32) / half)
    ang = pos.astype(F32)[:, None] * inv[None, :]
    cos, sin = jnp.cos(ang), jnp.sin(ang)
    return jnp.concatenate([cos, cos, cos, cos, -sin, sin, -sin, sin], axis=1)


def _expand_matrix(width, terms):
    src = np.arange(SSM_HEADS * width) // width
    one = (np.arange(128)[:, None] == src[None, :]).astype(np.float32)
    return jnp.asarray(np.concatenate([one] * terms, axis=0), dtype=BF16)


def _shift_matrix():
    sel = np.zeros(((CONV_W - 1) * CHUNK, CARRY + CHUNK), np.float32)
    for k in range(CONV_W - 1):
        for t in range(CHUNK):
            sel[k * CHUNK + t, CARRY + t + k - (CONV_W - 1)] = 1.0
    return jnp.asarray(sel, dtype=BF16)


def _short_taps_matrix(seq_len):
    n_seq = CHUNK // seq_len
    n_old = CONV_W - 1
    n_cached = n_old * n_seq
    sel = np.zeros((n_old * CHUNK + n_cached, CHUNK + 3 * n_cached), np.float32)

    def take(out_row, s, m):
        if m >= n_old:
            sel[out_row, s * seq_len + m - n_old] = 1.0
        else:
            for term in range(3):
                sel[out_row, CHUNK + term * n_cached + m * n_seq + s] = 1.0

    for s in range(n_seq):
        for k in range(n_old):
            for i in range(seq_len):
                take(k * CHUNK + s * seq_len + i, s, i + k)
        for j in range(n_old):
            take(n_old * CHUNK + j * n_seq + s, s, seq_len + j)
    return jnp.asarray(sel, dtype=BF16)


def _placement_matrix(seq_len):
    n_seq = CHUNK // seq_len
    place = np.zeros((n_seq * SLOTS, 3 * CHUNK), np.float32)
    for s in range(n_seq):
        for j in range(seq_len):
            place[s * SLOTS + j, s * seq_len + j] = 1.0
        place[s * SLOTS + seq_len, CHUNK + s * seq_len] = 1.0
        place[s * SLOTS + seq_len + 1, 2 * CHUNK + s * seq_len] = 1.0
    return jnp.asarray(place, dtype=BF16)


def _row(v):
    return v.reshape(1, -1).astype(F32)


def kernel(x_prompt, x_sample, mem_prompt, cache_ckv, cache_krope, page_table, cache_mem_k, cache_mem_v, state_ssm, state_conv, w_in, q_norm_g, w_uq, kv_norm_g, w_uk, w_uv, conv_w, conv_b, dt_bias, a_log, d_skip, ssm_norm_g, w_mem_k, w_mem_v, b_gate, w_o_mla, w_o_ssm, w_o_mem, w_out, ln1_g, ln1_b, w_up, b_up, w_down, b_down, ln2_g, ln2_b):
    bp, seq, _ = x_prompt.shape
    bs, t_new, _ = x_sample.shape
    n_pages = page_table.shape[1]
    past = n_pages * PAGE_SIZE

    o_cq, o_ckv, o_kr, o_z, o_xbc, o_dt, o_mq, o_g = np.cumsum(
        [0, Q_LORA, KV_LORA, ROPE_DIM, D_INNER, CONV_DIM, SSM_HEADS, MEM_WIDTH]).tolist()
    w_kr = w_in[:, o_kr:o_z]
    w_small = jnp.concatenate([w_in[:, o_ckv:o_kr], w_kr, w_kr, w_in[:, o_dt:o_mq],
                               jnp.zeros((D_MODEL, 128 - SSM_HEADS), F32)], axis=1)
    w_proj = tuple(w.astype(BF16) for w in (w_in[:, o_cq:o_ckv], w_small, w_in[:, o_z:o_xbc], w_in[:, o_xbc:o_dt],
                                            w_in[:, o_mq:o_g], w_in[:, o_g:]))
    half = ROPE_DIM // 2
    w_q_nope = w_uq[:, :, :NOPE_DIM].reshape(Q_LORA, MLA_HEADS * NOPE_DIM)
    w_q_rope = w_uq[:, :, NOPE_DIM:]
    w_q_swap = jnp.concatenate([w_q_rope[:, :, half:], w_q_rope[:, :, :half]], axis=-1)
    pair_w = 2 * ROPE_DIM
    w_q_pairs = jnp.concatenate([w_q_rope.reshape(Q_LORA, MLA_HEADS // 2, pair_w),
                                 w_q_swap.reshape(Q_LORA, MLA_HEADS // 2, pair_w)], axis=-1)
    wq = jnp.concatenate([w_q_nope, w_q_pairs.reshape(Q_LORA, -1)], axis=1).astype(BF16)
    wuk_t = jnp.transpose(w_uk, (1, 2, 0)).astype(BF16)
    wuk_pairs = w_uk.reshape(KV_LORA, MLA_HEADS // 2, 2 * NOPE_DIM).transpose(1, 0, 2).astype(BF16)
    wuv = jnp.transpose(w_uv, (1, 0, 2)).astype(BF16)
    wuv_t = jnp.transpose(w_uv, (1, 2, 0)).astype(BF16)
    w_mem_kv = jnp.concatenate([w_mem_k.reshape(D_MODEL, MEM_WIDTH), w_mem_v.reshape(D_MODEL, MEM_WIDTH)], axis=1).astype(BF16)
    merge_w = (w_o_mla.astype(BF16), w_o_ssm.astype(BF16), w_o_mem.astype(BF16), w_out.astype(BF16),
               _row(ln1_g), _row(ln1_b), w_up.astype(BF16), _row(b_up), w_down.astype(BF16), _row(b_down),
               _row(ln2_g), _row(ln2_b))
    gq, gkv, bg = _row(q_norm_g), _row(kv_norm_g), _row(b_gate)

    pad_heads = jnp.zeros((128 - SSM_HEADS,), F32)
    ssd_head = (conv_w.astype(F32), _row(conv_b),
                _row(jnp.concatenate([dt_bias, pad_heads])),
                jnp.broadcast_to(dt_bias.astype(F32)[:, None], (SSM_HEADS, CHUNK)),
                _row(jnp.concatenate([a_log, pad_heads])),
                jnp.broadcast_to(a_log.astype(F32)[:, None], (SSM_HEADS, CHUNK)),
                _row(jnp.repeat(d_skip, SSM_HEAD_DIM)))
    norm_g = _row(ssm_norm_g)
    expanders = (_expand_matrix(SSM_HEAD_DIM, 2), _expand_matrix(128, 3))

    mp = bp * seq
    n_chunks = seq // CHUNK
    cs_p = _rope_table(jnp.arange(seq))
    xp2d = x_prompt.reshape(mp, D_MODEL)
    cqn, ckv_p, kr_p, _, dt_p, dtt_p, z_p, xbc_p, mq_p, g_p, k_heads, vt_heads = _in_proj(
        xp2d, w_proj, cs_p, gq, gkv, bg, tm=TOKEN_TILE, head_kv_weights=(wuk_pairs, wuv_t.reshape(MLA_HEADS // 2, 2 * V_DIM, KV_LORA)))
    qt_p = _q_prep(cqn, wq, wuk_t, cs_p, tm=TOKEN_TILE, absorb=False)
    o_mla_p = _prompt_attn(qt_p, k_heads, vt_heads, bp, seq)

    o_ssm_p, ssm_p, conv_p = _ssd(xbc_p, z_p, dt_p, dtt_p, ssd_head + (norm_g,) + expanders + (_shift_matrix(),),
                                  batch=bp, n_chunks=n_chunks)

    mem_k_p, mem_v_p = _mem_kv(mem_prompt.reshape(bp * N_MEM, D_MODEL), w_mem_kv, tm=min(MEM_ROWS_TILE, bp * N_MEM))
    o_mem_p = _mem_attn(mq_p.reshape(bp, seq, MEM_WIDTH), mem_k_p.reshape(bp, N_MEM, MEM_WIDTH),
                        mem_v_p.reshape(bp, N_MEM, MEM_WIDTH), nb=1, tq=min(MEM_ROWS_TILE, seq))
    y_p = _merge_ffn(xp2d, o_mla_p, o_ssm_p.reshape(mp, D_INNER), o_mem_p.reshape(mp, MEM_WIDTH), g_p, merge_w, tm=TOKEN_TILE)

    ms = bs * t_new
    cs_s = jnp.tile(_rope_table(past + jnp.arange(t_new)), (bs, 1))
    xs2d = x_sample.reshape(ms, D_MODEL)
    tm_s = min(TOKEN_TILE, ms)
    cqn, ckv_s, kr_s, kcat, dt_s, dtt_s, z_s, xbc_s, mq_s, g_s = _in_proj(xs2d, w_proj, cs_s, gq, gkv, bg, tm=tm_s)
    q_s = _q_prep(cqn, wq, wuk_t, cs_s, tm=tm_s, absorb=True)
    q_s = jnp.transpose(q_s.reshape(MLA_HEADS, bs, t_new, QK_DIM), (1, 0, 2, 3)).reshape(bs, MLA_HEADS * t_new, QK_DIM)
    o_lat = _decode_attn(page_table, q_s, kcat.reshape(bs, t_new, QK_DIM).astype(F32), cache_ckv,
                         jnp.swapaxes(cache_krope, 1, 2))
    o_lat = jnp.transpose(o_lat.reshape(bs, MLA_HEADS, t_new, KV_LORA), (1, 0, 2, 3)).reshape(MLA_HEADS, ms, KV_LORA)
    o_mla_s = _uv_proj(o_lat, wuv)

    ypre_s, ec_s, xw_s, c_s, bw_s, conv_rows = _ssd_short(
        xbc_s, jnp.transpose(state_conv.astype(F32), (1, 0, 2)), dt_s, dtt_s, ssd_head + expanders,
        _short_taps_matrix(t_new), _placement_matrix(t_new), seq_len=t_new)
    conv_s = jnp.transpose(conv_rows, (1, 0, 2))
    o_ssm_s, ssm_s = _ssd_state(state_ssm.astype(F32).reshape(bs, D_INNER, D_STATE), xw_s, c_s, bw_s, ypre_s, ec_s,
                                z_s.astype(F32), norm_g, nb=SAMPLE_ROWS, seq_len=t_new)

    def cache_rows(c):
        c = c.reshape(bs, N_MEM, MEM_HEADS, MEM_HALVES, 128)
        return jnp.transpose(c, (0, 1, 3, 2, 4)).reshape(bs, N_MEM * MEM_ROW_GROUP, 128)

    mq_rows = jnp.transpose(mq_s.reshape(bs, t_new, MEM_HEADS, MEM_HALVES, 128), (0, 3, 2, 1, 4))
    o_mem_s = _mem_attn_cache(mq_rows.reshape(bs, MEM_ROW_GROUP * t_new, 128), cache_rows(cache_mem_k),
                              cache_rows(cache_mem_v), nb=SAMPLE_ROWS, n_tok=t_new)
    o_mem_s = jnp.transpose(o_mem_s.reshape(bs, MEM_HALVES, MEM_HEADS, t_new, 128), (0, 3, 2, 1, 4))
    y_s = _merge_ffn(xs2d, o_mla_s, o_ssm_s.reshape(ms, D_INNER).astype(BF16), o_mem_s.reshape(ms, MEM_WIDTH), g_s,
                     merge_w, tm=tm_s)

    return (y_p.reshape(bp, seq, D_MODEL), y_s.reshape(bs, t_new, D_MODEL),
            ckv_p.reshape(bp, seq, KV_LORA), kr_p.reshape(bp, seq, ROPE_DIM),
            mem_k_p.reshape(bp, N_MEM, MEM_HEADS, MEM_HEAD_DIM), mem_v_p.reshape(bp, N_MEM, MEM_HEADS, MEM_HEAD_DIM),
            ssm_p.reshape(bp, SSM_HEADS, SSM_HEAD_DIM, D_STATE), conv_p,
            ckv_s.reshape(bs, t_new, KV_LORA), kr_s.reshape(bs, t_new, ROPE_DIM),
            ssm_s.reshape(bs, SSM_HEADS, SSM_HEAD_DIM, D_STATE), conv_s)
```

```python
import functools

import jax
import jax.numpy as jnp
import numpy as np
from jax import lax
from jax.experimental import pallas as pl
from jax.experimental.pallas import tpu as pltpu

F32 = jnp.float32
BF16 = jnp.bfloat16

D_MODEL = 1024
MLA_HEADS = 8
Q_LORA = 384
KV_LORA = 256
NOPE_DIM = 128
ROPE_DIM = 64
V_DIM = 128
ROPE_THETA = 10000.0
QK_DIM = KV_LORA + ROPE_DIM
HEAD_QK = 256
SSM_HEADS = 32
SSM_HEAD_DIM = 64
D_INNER = SSM_HEADS * SSM_HEAD_DIM
SSM_GROUPS = 4
GROUP_W = D_INNER // SSM_GROUPS
D_STATE = 128
CONV_W = 4
CONV_DIM = D_INNER + 2 * SSM_GROUPS * D_STATE
CHUNK = 128
N_MEM = 256
MEM_HEADS = 4
MEM_HEAD_DIM = 256
MEM_WIDTH = MEM_HEADS * MEM_HEAD_DIM
D_FF = 4 * D_MODEL
N_BRANCH = 3
DEPTH = 1
ALPHA = (2 * DEPTH) ** 0.25
LN_EPS = 1e-5
RMS_EPS = 1e-6
PAGE_SIZE = 128
SCORE_SCALE = (NOPE_DIM + ROPE_DIM) ** -0.5
NEG_BIG = -1e30
KV_TILE = 256

TOKEN_TILE = 256
MEM_ROWS_TILE = 512
SAMPLE_ROWS = 4


def _cp(sem, vmem_mb):
    return pltpu.CompilerParams(dimension_semantics=sem, vmem_limit_bytes=vmem_mb << 20)


def _resident(shape):
    nd = len(shape)
    return pl.BlockSpec(shape, lambda *_: (0,) * nd, pipeline_mode=pl.Buffered(1))


def _dot(a, b):
    return jnp.dot(a, b, preferred_element_type=F32)


def _dot_nt(a, b):
    return lax.dot_general(a, b, (((1,), (1,)), ((), ())), preferred_element_type=F32)


def _rms(v, g, eps):
    return v * lax.rsqrt(jnp.mean(v * v, axis=-1, keepdims=True) + eps) * g


def _layer_norm(v, g, b):
    mu = jnp.mean(v, axis=-1, keepdims=True)
    d = v - mu
    var = jnp.mean(d * d, axis=-1, keepdims=True)
    return d * lax.rsqrt(var + LN_EPS) * g + b


def _in_proj_kernel(x_ref, wcq_ref, wsmall_ref, wz_ref, wxbc_ref, wmq_ref, wg_ref, cs_ref, gq_ref, gkv_ref, bg_ref,
                    *rest, per_head_kv):
    if per_head_kv:
        wuk_ref, wuvt_ref, wq_ref = rest[:3]
        rest = rest[3:]
    cqn_ref, ckv_ref, kr_ref, kcat_ref, dt_ref, dtt_ref, z_ref, xbc_ref, mq_ref, g_ref = rest[:10]
    xb = x_ref[...].astype(BF16)
    cqn = _rms(_dot(xb, wcq_ref[...]), gq_ref[...], RMS_EPS).astype(BF16)
    cqn_ref[...] = cqn

    small = _dot(xb, wsmall_ref[...])
    ckv = _rms(small[:, :KV_LORA], gkv_ref[...], RMS_EPS)
    ckv_ref[...] = ckv
    ckv_b = ckv.astype(BF16)
    kcat_ref[:, 0:KV_LORA] = ckv_b
    a = small[:, 256:384]
    b = pltpu.roll(a, ROPE_DIM // 2, 1)
    cs = cs_ref[...]
    ro = a * cs[:, :128] + b * cs[:, 128:]
    kr_ref[...] = ro[:, :ROPE_DIM]
    kcat_ref[:, KV_LORA:QK_DIM] = ro[:, :ROPE_DIM].astype(BF16)
    if per_head_kv:
        kh_ref, vt_ref, qt_ref = rest[10:]
        _write_queries(cqn, cs, wq_ref, None, qt_ref, absorb=False)
        lane = lax.broadcasted_iota(jnp.int32, ro.shape, 1)
        rope_pad = jnp.where(lane < ROPE_DIM, ro, 0.0).astype(BF16)
        ckv_t = [ckv[t * KV_TILE:(t + 1) * KV_TILE, :].T.astype(BF16) for t in range(vt_ref.shape[1])]
        for p in range(MLA_HEADS // 2):
            kk = _dot(ckv_b, wuk_ref[p]).astype(BF16)
            for i in range(2):
                kh_ref[2 * p + i, :, 0:NOPE_DIM] = kk[:, i * NOPE_DIM:(i + 1) * NOPE_DIM]
                kh_ref[2 * p + i, :, NOPE_DIM:HEAD_QK] = rope_pad
            for t, ct in enumerate(ckv_t):
                vv = _dot(wuvt_ref[p], ct).astype(BF16)
                vt_ref[2 * p, t] = vv[:V_DIM]
                vt_ref[2 * p + 1, t] = vv[V_DIM:]
    dt = small[:, 384:512]
    dt_ref[...] = dt
    dtt_ref[...] = dt.T[0:SSM_HEADS, :]

    for c in range(0, D_INNER, 1024):
        z_ref[:, c:c + 1024] = _dot(xb, wz_ref[:, c:c + 1024]).astype(BF16)
    for c in range(0, CONV_DIM, 1024):
        xbc_ref[:, c:c + 1024] = _dot(xb, wxbc_ref[:, c:c + 1024]).astype(BF16)
    mq_ref[...] = _dot(xb, wmq_ref[...]).astype(BF16)
    for c in range(0, N_BRANCH * D_MODEL, 1024):
        gr = _dot(xb, wg_ref[:, c:c + 1024]) + bg_ref[:, c:c + 1024]
        g_ref[:, c:c + 1024] = jax.nn.sigmoid(gr).astype(BF16)


def _in_proj(x2d, weights, cs_tab, gq, gkv, bg, tm, head_kv_weights=()):
    m = x2d.shape[0]
    ncs = cs_tab.shape[0] // tm

    def row(n, dtype):
        return pl.BlockSpec((tm, n), lambda i: (i, 0)), jax.ShapeDtypeStruct((m, n), dtype)

    outs = [row(Q_LORA, BF16), row(KV_LORA, F32), row(ROPE_DIM, F32), row(QK_DIM, BF16), row(128, F32),
            (pl.BlockSpec((SSM_HEADS, tm), lambda i: (0, i)), jax.ShapeDtypeStruct((SSM_HEADS, m), F32)),
            row(D_INNER, BF16), row(CONV_DIM, BF16), row(MEM_WIDTH, BF16), row(N_BRANCH * D_MODEL, BF16)]
    if head_kv_weights:
        outs += [(pl.BlockSpec((MLA_HEADS, tm, HEAD_QK), lambda i: (0, i, 0)),
                  jax.ShapeDtypeStruct((MLA_HEADS, m, HEAD_QK), BF16)),
                 (pl.BlockSpec((MLA_HEADS, tm // KV_TILE, V_DIM, KV_TILE), lambda i: (0, i, 0, 0)),
                  jax.ShapeDtypeStruct((MLA_HEADS, m // KV_TILE, V_DIM, KV_TILE), BF16)),
                 (pl.BlockSpec((MLA_HEADS, HEAD_QK, tm), lambda i: (0, 0, i)),
                  jax.ShapeDtypeStruct((MLA_HEADS, HEAD_QK, m), BF16))]
    return pl.pallas_call(
        functools.partial(_in_proj_kernel, per_head_kv=bool(head_kv_weights)),
        grid=(m // tm,),
        in_specs=[pl.BlockSpec((tm, D_MODEL), lambda i: (i, 0))] + [_resident(w.shape) for w in weights]
                 + [pl.BlockSpec((tm, 256), lambda i: (i % ncs, 0)),
                    _resident(gq.shape), _resident(gkv.shape), _resident(bg.shape)]
                 + [_resident(w.shape) for w in head_kv_weights],
        out_specs=[o[0] for o in outs],
        out_shape=[o[1] for o in outs],
        compiler_params=_cp(("arbitrary",), 56),
        name="in_proj",
    )(x2d, *weights, cs_tab, gq, gkv, bg, *head_kv_weights)


def _write_queries(c, cs, wq_ref, wuk_ref, q_ref, absorb):
    qn = [_dot(c, wq_ref[:, h * NOPE_DIM:(h + 1) * NOPE_DIM]) for h in range(MLA_HEADS)]
    for h in range(MLA_HEADS):
        if absorb:
            q_ref[h, :, 0:KV_LORA] = (_dot(qn[h].astype(BF16), wuk_ref[h]) * SCORE_SCALE).astype(BF16)
        else:
            q_ref[h, 0:NOPE_DIM, :] = (qn[h] * SCORE_SCALE).T.astype(BF16)
            q_ref[h, NOPE_DIM + ROPE_DIM:HEAD_QK, :] = jnp.zeros((HEAD_QK - NOPE_DIM - ROPE_DIM, c.shape[0]), BF16)
    r0 = MLA_HEADS * NOPE_DIM
    for p in range(MLA_HEADS // 2):
        ab = _dot(c, wq_ref[:, r0 + p * 256:r0 + (p + 1) * 256])
        ro = (ab[:, :128] * cs[:, :128] + ab[:, 128:] * cs[:, 128:]) * SCORE_SCALE
        if absorb:
            q_ref[2 * p, :, KV_LORA:QK_DIM] = ro[:, :ROPE_DIM].astype(BF16)
            q_ref[2 * p + 1, :, KV_LORA:QK_DIM] = pltpu.roll(ro, ROPE_DIM, 1)[:, :ROPE_DIM].astype(BF16)
        else:
            rot = ro.T.astype(BF16)
            q_ref[2 * p, NOPE_DIM:NOPE_DIM + ROPE_DIM, :] = rot[:ROPE_DIM]
            q_ref[2 * p + 1, NOPE_DIM:NOPE_DIM + ROPE_DIM, :] = rot[ROPE_DIM:]


def _q_prep_kernel(c_ref, wq_ref, wuk_ref, cs_ref, q_ref):
    _write_queries(c_ref[...], cs_ref[...], wq_ref, wuk_ref, q_ref, absorb=True)


def _q_prep(cqn, wq, wuk_t, cs_tab, tm):
    m = cqn.shape[0]
    ncs = cs_tab.shape[0] // tm
    return pl.pallas_call(
        _q_prep_kernel,
        grid=(m // tm,),
        in_specs=[pl.BlockSpec((tm, Q_LORA), lambda i: (i, 0)), _resident(wq.shape), _resident(wuk_t.shape),
                  pl.BlockSpec((tm, 256), lambda i: (i % ncs, 0))],
        out_specs=pl.BlockSpec((MLA_HEADS, tm, QK_DIM), lambda i: (0, i, 0)),
        out_shape=jax.ShapeDtypeStruct((MLA_HEADS, m, QK_DIM), BF16),
        compiler_params=_cp(("arbitrary",), 32),
        name="q_prep",
    )(cqn, wq, wuk_t, cs_tab)


def _prompt_attn_kernel(qt_ref, k_ref, vt_ref, o_ref, m_sc, l_sc, acc_sc):
    i = pl.program_id(1)
    t = KV_TILE
    m_sc[...] = jnp.full(m_sc.shape, NEG_BIG, F32)
    l_sc[...] = jnp.zeros(l_sc.shape, F32)
    acc_sc[...] = jnp.zeros(acc_sc.shape, F32)

    def kv_block(j, masked):
        rows = pl.ds(pl.multiple_of(j * t, t), t)
        if masked:
            keep = lax.broadcasted_iota(jnp.int32, (t, t), 0) <= lax.broadcasted_iota(jnp.int32, (t, t), 1)
        pending = [_dot(k_ref[h, rows, :], qt_ref[h]) for h in range(MLA_HEADS)]
        for h in range(MLA_HEADS):
            st = pending[h]
            if masked:
                st = jnp.where(keep, st, NEG_BIG)
            m_prev = m_sc[h:h + 1, :]
            m_new = jnp.maximum(m_prev, jnp.max(st, axis=0, keepdims=True))
            alpha = jnp.exp(m_prev - m_new)
            p = jnp.exp(st - m_new)
            l_sc[h:h + 1, :] = alpha * l_sc[h:h + 1, :] + jnp.sum(p, axis=0, keepdims=True)
            m_sc[h:h + 1, :] = m_new
            acc_sc[h] = alpha * acc_sc[h] + _dot(vt_ref[h, j], p.astype(BF16))

    def body(j, carry):
        kv_block(j, False)
        return carry

    lax.fori_loop(0, i, body, 0)
    kv_block(i, True)

    for h in range(MLA_HEADS):
        o_ref[:, h * V_DIM:(h + 1) * V_DIM] = (acc_sc[h] / l_sc[h:h + 1, :]).T.astype(BF16)


def _prompt_attn(qt, k_heads, vt_heads, batch, seq):
    t = KV_TILE
    nq = seq // t
    return pl.pallas_call(
        _prompt_attn_kernel,
        grid=(batch, nq),
        in_specs=[pl.BlockSpec((MLA_HEADS, HEAD_QK, t), lambda b, i: (0, 0, b * nq + i)),
                  pl.BlockSpec((MLA_HEADS, seq, HEAD_QK), lambda b, i: (0, b, 0)),
                  pl.BlockSpec((MLA_HEADS, nq, V_DIM, t), lambda b, i: (0, b, 0, 0))],
        out_specs=pl.BlockSpec((t, MLA_HEADS * V_DIM), lambda b, i: (b * nq + i, 0)),
        out_shape=jax.ShapeDtypeStruct((batch * seq, MLA_HEADS * V_DIM), BF16),
        scratch_shapes=[pltpu.VMEM((MLA_HEADS, t), F32), pltpu.VMEM((MLA_HEADS, t), F32),
                        pltpu.VMEM((MLA_HEADS, V_DIM, t), F32)],
        compiler_params=_cp(("arbitrary", "arbitrary"), 48),
        name="prompt_attn",
    )(qt, k_heads, vt_heads)


def _decode_attn_kernel(pt_ref, q_ref, kn_ref, ckv_hbm, krt_hbm, o_ref, kbuf, rbuf, kb_sc, s_sc, sem, *, n_pages, n_new):
    b = pl.program_id(0)
    nb = pl.num_programs(0)
    slot = b % 2

    def page_copies(page, p, sl):
        rows = pl.ds(pl.multiple_of(p * PAGE_SIZE, PAGE_SIZE), PAGE_SIZE)
        return (pltpu.make_async_copy(ckv_hbm.at[page], kbuf.at[sl, rows, :], sem.at[0, sl]),
                pltpu.make_async_copy(krt_hbm.at[page], rbuf.at[sl, p], sem.at[1, sl]))

    def issue(bi, sl):
        def body(p, carry):
            for cp in page_copies(pt_ref[bi * n_pages + p], p, sl):
                cp.start()
            return carry
        lax.fori_loop(0, n_pages, body, 0, unroll=8)

    @pl.when(b == 0)
    def _():
        issue(0, 0)

    @pl.when(b + 1 < nb)
    def _():
        issue(b + 1, 1 - slot)

    def wait_body(p, carry):
        for cp in page_copies(0, p, slot):
            cp.wait()
        return carry
    lax.fori_loop(0, n_pages, wait_body, 0, unroll=8)

    q = q_ref[0]
    rows = q.shape[0]
    qc = q[:, :KV_LORA]
    qr = q[:, KV_LORA:]
    qf = q.astype(F32)
    kn = kn_ref[0]
    tok = lax.broadcasted_iota(jnp.int32, (rows, 1), 0) % n_new

    s_new = []
    for j in range(n_new):
        sj = jnp.sum(qf * kn[j:j + 1, :], axis=1, keepdims=True)
        s_new.append(jnp.where(tok >= j, sj, NEG_BIG))
    m0 = s_new[0]
    for j in range(1, n_new):
        m0 = jnp.maximum(m0, s_new[j])

    for c in range(n_pages // 2):
        r = slice(2 * c * PAGE_SIZE, 2 * (c + 1) * PAGE_SIZE)
        kc = kbuf[slot, r, :].astype(BF16)
        kb_sc[r, :] = kc
        rt = jnp.concatenate([rbuf[slot, 2 * c], rbuf[slot, 2 * c + 1]], axis=1).astype(BF16)
        s_sc[:, r] = _dot_nt(qc, kc) + _dot(qr, rt)
    s = s_sc[...]
    m = jnp.maximum(m0, jnp.max(s, axis=1, keepdims=True))
    p = jnp.exp(s - m)
    l = jnp.sum(p, axis=1, keepdims=True)
    acc = _dot(p.astype(BF16), kb_sc[...])
    for j in range(n_new):
        pj = jnp.exp(s_new[j] - m)
        l = l + pj
        acc = acc + pj * kn[j:j + 1, :KV_LORA]
    o_ref[0] = (acc / l).astype(BF16)


def _decode_attn(page_table, q, k_new, cache_ckv, cache_krope_t):
    nb, n_pages = page_table.shape
    rows = q.shape[1]
    n_new = k_new.shape[1]
    past = n_pages * PAGE_SIZE
    kern = functools.partial(_decode_attn_kernel, n_pages=n_pages, n_new=n_new)
    grid_spec = pltpu.PrefetchScalarGridSpec(
        num_scalar_prefetch=1,
        grid=(nb,),
        in_specs=[pl.BlockSpec((1, rows, QK_DIM), lambda b, pt: (b, 0, 0)),
                  pl.BlockSpec((1, n_new, QK_DIM), lambda b, pt: (b, 0, 0)),
                  pl.BlockSpec(memory_space=pl.ANY),
                  pl.BlockSpec(memory_space=pl.ANY)],
        out_specs=pl.BlockSpec((1, rows, KV_LORA), lambda b, pt: (b, 0, 0)),
        scratch_shapes=[pltpu.VMEM((2, past, KV_LORA), F32), pltpu.VMEM((2, n_pages, ROPE_DIM, PAGE_SIZE), F32),
                        pltpu.VMEM((past, KV_LORA), BF16), pltpu.VMEM((rows, past), F32),
                        pltpu.SemaphoreType.DMA((2, 2))],
    )
    return pl.pallas_call(
        kern,
        grid_spec=grid_spec,
        out_shape=jax.ShapeDtypeStruct((nb, rows, KV_LORA), BF16),
        compiler_params=_cp(("arbitrary",), 40),
        name="decode_attn",
    )(page_table.reshape(-1), q, k_new, cache_ckv, cache_krope_t)


def _uv_proj_kernel(o_ref, w_ref, out_ref):
    out_ref[...] = _dot(o_ref[0], w_ref[0]).astype(BF16)


def _uv_proj(o_lat, wuv):
    m = o_lat.shape[1]
    return pl.pallas_call(
        _uv_proj_kernel,
        grid=(MLA_HEADS,),
        in_specs=[pl.BlockSpec((1, m, KV_LORA), lambda h: (h, 0, 0)),
                  pl.BlockSpec((1, KV_LORA, V_DIM), lambda h: (h, 0, 0))],
        out_specs=pl.BlockSpec((m, V_DIM), lambda h: (0, h)),
        out_shape=jax.ShapeDtypeStruct((m, MLA_HEADS * V_DIM), BF16),
        compiler_params=_cp(("arbitrary",), 16),
        name="uv_proj",
    )(o_lat, wuv)


def _split_dot(v, e, terms, left=True):
    parts = []
    r = v
    for _ in range(terms):
        hi = r.astype(BF16)
        parts.append(hi)
        r = r - hi.astype(F32)
    if left:
        return _dot(jnp.concatenate(parts, axis=1), e)
    return _dot(e, jnp.concatenate(parts, axis=0))


def _conv_silu(taps, cw_ref, cb_ref):
    conv = cb_ref[...] + cw_ref[0:1, :] * taps[0]
    for k in range(1, CONV_W):
        conv = conv + cw_ref[k:k + 1, :] * taps[k]
    xc = conv * jax.nn.sigmoid(conv)
    n_bc = SSM_GROUPS * D_STATE
    return xc[:, :D_INNER], xc[:, D_INNER:D_INNER + n_bc], xc[:, D_INNER + n_bc:]


def _select_rows(select, rows_ref):
    n = rows_ref.shape[1]
    return jnp.concatenate([_dot(select, rows_ref[:, c:c + 1024]) for c in range(0, n, 1024)], axis=1)


def _ssd_sums(dt, dtt, keep, keep_t, alog_ref, alogt_ref):
    tri = jnp.where(keep, 1.0, 0.0).astype(BF16)
    tri_t = jnp.where(keep_t, 1.0, 0.0).astype(BF16)
    da = dt * (-jnp.exp(alog_ref[...]))
    dat = dtt * (-jnp.exp(alogt_ref[...]))
    cum = _split_dot(da, jnp.concatenate([tri] * 3, axis=1), 3, left=False)
    cum_t = _split_dot(dat, jnp.concatenate([tri_t] * 3, axis=0), 3)
    return da, cum, cum_t


def _ssd_intra(xs, bm, cm, cum_e, cum_t, dtt, keep, dskip_ref, y_sc, carried=None):
    pairs_per_group = SSM_HEADS // 2 // SSM_GROUPS
    lane = lax.broadcasted_iota(jnp.int32, (CHUNK, CHUNK), 1)
    for g in range(SSM_GROUPS):
        cg = cm[:, g * D_STATE:(g + 1) * D_STATE].astype(BF16)
        cb = _dot_nt(cg, bm[:, g * D_STATE:(g + 1) * D_STATE].astype(BF16))
        extra = None if carried is None else carried(g, cg)
        for pr in range(pairs_per_group):
            q = g * pairs_per_group + pr
            ws = []
            for h in (2 * q, 2 * q + 1):
                seg = cum_e[:, h * 128:(h + 1) * 128] - cum_t[h:h + 1, :]
                decay = jnp.exp(jnp.where(keep, seg, NEG_BIG)) * dtt[h:h + 1, :]
                ws.append((cb * decay).astype(BF16))
            w_pair = jnp.concatenate(ws, axis=1)
            xp = xs[:, q * 128:(q + 1) * 128]
            x_bd = jnp.concatenate([jnp.where(lane < SSM_HEAD_DIM, xp, 0.0).astype(BF16),
                                    jnp.where(lane >= SSM_HEAD_DIM, xp, 0.0).astype(BF16)], axis=0)
            y = _dot(w_pair, x_bd) + dskip_ref[:, q * 128:(q + 1) * 128] * xp
            if extra is not None:
                y = y + extra[:, pr * 128:(pr + 1) * 128]
            y_sc[:, q * 128:(q + 1) * 128] = y


CARRY = 16


def _ssd_kernel(xbc_ref, z_ref, dt_ref, dtt_ref, cw_ref, cb_ref, dtb_ref, dtbt_ref,
                alog_ref, alogt_ref, dskip_ref, ng_ref, e64_ref, e128_ref, shift_ref,
                y_ref, sout_ref, cout_ref, xe_sc, tail_sc, st_sc, y_sc):
    c = pl.program_id(1)
    n_pairs = SSM_HEADS // 2
    pairs_per_group = n_pairs // SSM_GROUPS

    @pl.when(c == 0)
    def _():
        xe_sc[0:CARRY, :] = jnp.zeros((CARRY, CONV_DIM), BF16)
        st_sc[...] = jnp.zeros(st_sc.shape, F32)

    xe_sc[CARRY:CARRY + CHUNK, :] = xbc_ref[...]
    shifted = _select_rows(shift_ref[...], xe_sc)
    taps = [shifted[k * CHUNK:(k + 1) * CHUNK] for k in range(CONV_W - 1)] + [xbc_ref[...].astype(F32)]
    xs, bm, cm = _conv_silu(taps, cw_ref, cb_ref)
    xe_sc[0:CARRY, :] = xe_sc[CHUNK:CHUNK + CARRY, :]

    rowi = lax.broadcasted_iota(jnp.int32, (CHUNK, CHUNK), 0)
    coli = lax.broadcasted_iota(jnp.int32, (CHUNK, CHUNK), 1)
    keep = coli <= rowi
    dt = jax.nn.softplus(dt_ref[...] + dtb_ref[...])
    dtt = jax.nn.softplus(dtt_ref[...] + dtbt_ref[...])
    _, cum, cum_t = _ssd_sums(dt, dtt, keep, rowi <= coli, alog_ref, alogt_ref)
    last = cum[CHUNK - 1:CHUNK, :]
    e64 = e64_ref[...]
    ec_e = _split_dot(jnp.exp(cum), e64, 2)
    wb_e = _split_dot(dt * jnp.exp(last - cum), e64, 2)
    dec_e = _split_dot(jnp.broadcast_to(jnp.exp(last), (16, CHUNK)), e64, 2)[0:1, :]
    cum_e = _split_dot(cum, e128_ref[...], 3)

    def carried(g, cg):
        return _dot(cg, st_sc[g].astype(BF16)) * ec_e[:, g * GROUP_W:(g + 1) * GROUP_W]

    _ssd_intra(xs, bm, cm, cum_e, cum_t, dtt, keep, dskip_ref, y_sc, carried)
    for g in range(SSM_GROUPS):
        gs = slice(g * GROUP_W, (g + 1) * GROUP_W)
        xw = (xs[:, gs] * wb_e[:, gs]).astype(BF16)
        bt = bm[:, g * D_STATE:(g + 1) * D_STATE].T.astype(BF16)
        st_sc[g] = st_sc[g] * dec_e[:, gs] + _dot(bt, xw)

    for g in range(SSM_GROUPS):
        gs = slice(g * GROUP_W, (g + 1) * GROUP_W)
        zf = z_ref[:, gs].astype(F32)
        yv = y_sc[:, gs] * (zf * jax.nn.sigmoid(zf))
        y_ref[:, gs] = _rms(yv, ng_ref[:, gs], RMS_EPS).astype(y_ref.dtype)

    @pl.when(c == pl.num_programs(1) - 1)
    def _():
        tail_sc[...] = xbc_ref[CHUNK - CARRY:CHUNK, :].astype(F32)
        cout_ref[0] = tail_sc[CARRY - (CONV_W - 1):CARRY, :]
        for q in range(n_pairs):
            g, pr = divmod(q, pairs_per_group)
            sout_ref[0, q * 128:(q + 1) * 128, :] = st_sc[g, :, pr * 128:(pr + 1) * 128].T


def _ssd(xbc, z, dt, dtt, consts, batch, n_chunks):
    def step(n):
        return pl.BlockSpec((CHUNK, n), lambda b, c: (b * n_chunks + c, 0))

    def per_batch(r, n):
        return pl.BlockSpec((1, r, n), lambda b, c: (b, 0, 0))

    return pl.pallas_call(
        _ssd_kernel,
        grid=(batch, n_chunks),
        in_specs=[step(CONV_DIM), step(D_INNER), step(128),
                  pl.BlockSpec((SSM_HEADS, CHUNK), lambda b, c: (0, b * n_chunks + c))]
                 + [_resident(a.shape) for a in consts],
        out_specs=[step(D_INNER), per_batch(D_INNER, D_STATE), per_batch(CONV_W - 1, CONV_DIM)],
        out_shape=[jax.ShapeDtypeStruct((batch * n_chunks * CHUNK, D_INNER), BF16),
                   jax.ShapeDtypeStruct((batch, D_INNER, D_STATE), F32),
                   jax.ShapeDtypeStruct((batch, CONV_W - 1, CONV_DIM), F32)],
        scratch_shapes=[pltpu.VMEM((CARRY + CHUNK, CONV_DIM), BF16),
                        pltpu.VMEM((CARRY, CONV_DIM), F32),
                        pltpu.VMEM((SSM_GROUPS, D_STATE, GROUP_W), F32),
                        pltpu.VMEM((CHUNK, D_INNER), F32)],
        compiler_params=_cp(("arbitrary", "arbitrary"), 48),
        name="ssd",
    )(xbc, z, dt, dtt, *consts)


SLOTS = 16


def _ssd_short_kernel(x_ref, sc_ref, dt_ref, dtt_ref, cw_ref, cb_ref, dtb_ref, dtbt_ref, alog_ref, alogt_ref,
                      dskip_ref, e64_ref, e128_ref, taps_sel_ref, place_ref,
                      ypre_ref, ec_ref, xw_ref, c_ref, bw_ref, tails_ref, rows_sc, y_sc, *, seq_len):
    n_cached = sc_ref.shape[0] * sc_ref.shape[1]
    rows_sc[0:CHUNK, :] = x_ref[...]
    r = sc_ref[...].reshape(n_cached, CONV_DIM)
    for t in range(3):
        hi = r.astype(BF16)
        rows_sc[CHUNK + t * n_cached:CHUNK + (t + 1) * n_cached, :] = hi
        r = r - hi.astype(F32)
    picked = _select_rows(taps_sel_ref[...], rows_sc)
    taps = [picked[k * CHUNK:(k + 1) * CHUNK] for k in range(CONV_W - 1)] + [x_ref[...].astype(F32)]
    tails_ref[...] = picked[(CONV_W - 1) * CHUNK:].reshape(tails_ref.shape)
    xs, bm, cm = _conv_silu(taps, cw_ref, cb_ref)
    rowi = lax.broadcasted_iota(jnp.int32, (CHUNK, CHUNK), 0)
    coli = lax.broadcasted_iota(jnp.int32, (CHUNK, CHUNK), 1)
    same = (rowi // seq_len) == (coli // seq_len)
    keep = same & (coli <= rowi)
    dt = jax.nn.softplus(dt_ref[...] + dtb_ref[...])
    dtt = jax.nn.softplus(dtt_ref[...] + dtbt_ref[...])
    da, cum, cum_t = _ssd_sums(dt, dtt, keep, same & (rowi <= coli), alog_ref, alogt_ref)
    whole = jnp.where(same, 1.0, 0.0).astype(BF16)
    last = _split_dot(da, jnp.concatenate([whole] * 3, axis=1), 3, left=False)
    e64 = e64_ref[...]
    ec_ref[...] = _split_dot(jnp.exp(cum), e64, 2)
    wb_e = _split_dot(dt * jnp.exp(last - cum), e64, 2)
    dec_e = _split_dot(jnp.exp(last), e64, 2)
    cum_e = _split_dot(cum, e128_ref[...], 3)
    _ssd_intra(xs, bm, cm, cum_e, cum_t, dtt, keep, dskip_ref, y_sc)
    ypre_ref[...] = y_sc[...]

    place = place_ref[...]
    dec_hi = dec_e.astype(BF16)
    dec_lo = (dec_e - dec_hi.astype(F32)).astype(BF16)
    xw = (xs * wb_e).astype(BF16)
    xw_ref[...] = _dot(place, jnp.concatenate([xw, dec_hi, dec_lo], axis=0)).astype(BF16)
    tokens = place[:, :CHUNK]
    c_ref[...] = _dot(tokens, cm.astype(BF16)).astype(BF16)
    zeros = jnp.zeros((CHUNK, D_STATE), BF16)
    b_wide = jnp.concatenate(
        [piece for g in range(SSM_GROUPS) for piece in (bm[:, g * D_STATE:(g + 1) * D_STATE].astype(BF16), zeros)], axis=1)
    n_rows = place.shape[0]
    slot = lax.broadcasted_iota(jnp.int32, (n_rows, 2 * D_STATE * SSM_GROUPS), 0) % SLOTS
    col = lax.broadcasted_iota(jnp.int32, (n_rows, 2 * D_STATE * SSM_GROUPS), 1) % (2 * D_STATE)
    ones = ((slot == seq_len) | (slot == seq_len + 1)) & (col >= D_STATE)
    bw_ref[...] = jnp.where(ones, 1.0, _dot(tokens, b_wide)).astype(BF16)


def _ssd_short(xbc, conv_rows, dt, dtt, consts, taps_sel, place, seq_len):
    rows = dt.shape[0]
    n_chunks = rows // CHUNK
    n_seq = CHUNK // seq_len
    n_slot_rows = place.shape[0]
    kern = functools.partial(_ssd_short_kernel, seq_len=seq_len)

    def out(r, n, dtype):
        return pl.BlockSpec((r, n), lambda c: (c, 0)), jax.ShapeDtypeStruct((n_chunks * r, n), dtype)

    conv_spec = pl.BlockSpec((CONV_W - 1, n_seq, CONV_DIM), lambda c: (0, c, 0))
    outs = [out(CHUNK, D_INNER, F32), out(CHUNK, D_INNER, F32), out(n_slot_rows, D_INNER, BF16),
            out(n_slot_rows, SSM_GROUPS * D_STATE, BF16), out(n_slot_rows, 2 * SSM_GROUPS * D_STATE, BF16),
            (conv_spec, jax.ShapeDtypeStruct(conv_rows.shape, F32))]
    return pl.pallas_call(
        kern,
        grid=(n_chunks,),
        in_specs=[pl.BlockSpec((CHUNK, CONV_DIM), lambda c: (c, 0)), conv_spec,
                  pl.BlockSpec((CHUNK, 128), lambda c: (c, 0)),
                  pl.BlockSpec((SSM_HEADS, CHUNK), lambda c: (0, c))]
                 + [_resident(a.shape) for a in consts] + [_resident(taps_sel.shape), _resident(place.shape)],
        out_specs=[o[0] for o in outs],
        out_shape=[o[1] for o in outs],
        scratch_shapes=[pltpu.VMEM((taps_sel.shape[1], CONV_DIM), BF16), pltpu.VMEM((CHUNK, D_INNER), F32)],
        compiler_params=_cp(("arbitrary",), 48),
        name="ssd_short",
    )(xbc, conv_rows, dt, dtt, *consts, taps_sel, place)


def _ssd_state_kernel(s0_ref, xw_ref, c_ref, bw_ref, ypre_ref, ec_ref, z_ref, ng_ref, y_ref, sout_ref, ci_sc,
                      *, nb, seq_len):
    for bb in range(nb):
        slots = slice(bb * SLOTS, (bb + 1) * SLOTS)
        toks = slice(bb * seq_len, (bb + 1) * seq_len)
        for g in range(SSM_GROUPS):
            gs = slice(g * GROUP_W, (g + 1) * GROUP_W)
            s0g = s0_ref[bb, gs, :]
            ci_sc[...] = _dot_nt(c_ref[slots, g * D_STATE:(g + 1) * D_STATE], s0g.astype(BF16))
            u = lax.dot_general(xw_ref[slots, gs], bw_ref[slots, 2 * g * D_STATE:2 * (g + 1) * D_STATE],
                                (((0,), (0,)), ((), ())), preferred_element_type=F32)
            sout_ref[bb, gs, :] = u[:, :D_STATE] + s0g * u[:, D_STATE:]
            zf = z_ref[toks, gs]
            yv = (ypre_ref[toks, gs] + ec_ref[toks, gs] * ci_sc[0:seq_len, :]) * (zf * jax.nn.sigmoid(zf))
            y_ref[toks, gs] = _rms(yv, ng_ref[:, gs], RMS_EPS)


def _ssd_state(s0, xw, cmat, bw, ypre, ec, z, ng, nb, seq_len):
    batch = s0.shape[0]
    kern = functools.partial(_ssd_state_kernel, nb=nb, seq_len=seq_len)

    def rows(r, n):
        return pl.BlockSpec((nb * r, n), lambda b: (b, 0))

    state_spec = pl.BlockSpec((nb, D_INNER, D_STATE), lambda b: (b, 0, 0))
    return pl.pallas_call(
        kern,
        grid=(batch // nb,),
        in_specs=[state_spec, rows(SLOTS, D_INNER), rows(SLOTS, SSM_GROUPS * D_STATE),
                  rows(SLOTS, 2 * SSM_GROUPS * D_STATE), rows(seq_len, D_INNER), rows(seq_len, D_INNER),
                  rows(seq_len, D_INNER), _resident(ng.shape)],
        out_specs=[rows(seq_len, D_INNER), state_spec],
        out_shape=[jax.ShapeDtypeStruct((batch * seq_len, D_INNER), F32),
                   jax.ShapeDtypeStruct((batch, D_INNER, D_STATE), F32)],
        scratch_shapes=[pltpu.VMEM((SLOTS, GROUP_W), F32)],
        compiler_params=_cp(("arbitrary",), 40),
        name="ssd_state",
    )(s0, xw, cmat, bw, ypre, ec, z, ng)


def _mem_kv_kernel(m_ref, w_ref, k_ref, v_ref):
    mb = m_ref[...].astype(BF16)
    k_ref[...] = _dot(mb, w_ref[:, :MEM_WIDTH])
    v_ref[...] = _dot(mb, w_ref[:, MEM_WIDTH:])


def _mem_kv(mem2d, w_kv, tm):
    m = mem2d.shape[0]
    return pl.pallas_call(
        _mem_kv_kernel,
        grid=(m // tm,),
        in_specs=[pl.BlockSpec((tm, D_MODEL), lambda i: (i, 0)), _resident(w_kv.shape)],
        out_specs=[pl.BlockSpec((tm, MEM_WIDTH), lambda i: (i, 0))] * 2,
        out_shape=[jax.ShapeDtypeStruct((m, MEM_WIDTH), F32)] * 2,
        compiler_params=_cp(("arbitrary",), 32),
        name="mem_kv",
    )(mem2d, w_kv)


def _mem_attn_kernel(q_ref, k_ref, v_ref, o_ref, *, nb):
    scale = MEM_HEAD_DIM ** -0.5
    for bb in range(nb):
        heads = [slice(h * MEM_HEAD_DIM, (h + 1) * MEM_HEAD_DIM) for h in range(MEM_HEADS)]
        scores = [_dot_nt(q_ref[bb, :, hs], k_ref[bb, :, hs].astype(BF16)) for hs in heads]
        for h, hs in enumerate(heads):
            vh = v_ref[bb, :, hs].astype(BF16)
            s = scores[h] * scale
            p = jnp.exp(s - jnp.max(s, axis=1, keepdims=True))
            l = jnp.sum(p, axis=1, keepdims=True)
            o_ref[bb, :, hs] = (_dot(p.astype(BF16), vh) / l).astype(BF16)


def _mem_attn(mq, mem_k, mem_v, nb, tq):
    batch, rows, _ = mq.shape
    kern = functools.partial(_mem_attn_kernel, nb=nb)
    kv_spec = pl.BlockSpec((nb, N_MEM, MEM_WIDTH), lambda b, i: (b, 0, 0))
    return pl.pallas_call(
        kern,
        grid=(batch // nb, rows // tq),
        in_specs=[pl.BlockSpec((nb, tq, MEM_WIDTH), lambda b, i: (b, i, 0)), kv_spec, kv_spec],
        out_specs=pl.BlockSpec((nb, tq, MEM_WIDTH), lambda b, i: (b, i, 0)),
        out_shape=jax.ShapeDtypeStruct((batch, rows, MEM_WIDTH), BF16),
        compiler_params=_cp(("arbitrary", "arbitrary"), 40),
        name="mem_attn",
    )(mq, mem_k, mem_v)


MEM_HALVES = MEM_HEAD_DIM // 128
MEM_ROW_GROUP = MEM_HALVES * MEM_HEADS


def _mem_attn_cache_kernel(q_ref, k_ref, v_ref, o_ref, *, nb, n_tok):
    rows = MEM_HEADS * n_tok
    n_col = N_MEM * MEM_ROW_GROUP
    col = lax.broadcasted_iota(jnp.int32, (rows, n_col), 1)
    row = lax.broadcasted_iota(jnp.int32, (rows, n_col), 0)
    own = (col % MEM_ROW_GROUP) == (row // n_tok)
    scale = MEM_HEAD_DIM ** -0.5
    for bb in range(nb):
        g = _dot_nt(q_ref[bb], k_ref[bb].astype(BF16))
        s = (g[:rows] + pltpu.roll(g[rows:], n_col - MEM_HEADS, 1)) * scale
        s = jnp.where(own, s, NEG_BIG)
        p = jnp.exp(s - jnp.max(s, axis=1, keepdims=True))
        l = jnp.sum(p, axis=1, keepdims=True)
        p2 = jnp.concatenate([p, pltpu.roll(p, MEM_HEADS, 1)], axis=0).astype(BF16)
        o = _dot(p2, v_ref[bb].astype(BF16))
        o_ref[bb] = (o / jnp.concatenate([l, l], axis=0)).astype(BF16)


def _mem_attn_cache(q, mem_k, mem_v, nb, n_tok):
    batch, rows, _ = q.shape
    kern = functools.partial(_mem_attn_cache_kernel, nb=nb, n_tok=n_tok)
    kv_spec = pl.BlockSpec((nb,) + mem_k.shape[1:], lambda b: (b, 0, 0))
    q_spec = pl.BlockSpec((nb, rows, 128), lambda b: (b, 0, 0))
    return pl.pallas_call(
        kern,
        grid=(batch // nb,),
        in_specs=[q_spec, kv_spec, kv_spec],
        out_specs=q_spec,
        out_shape=jax.ShapeDtypeStruct((batch, rows, 128), BF16),
        compiler_params=_cp(("arbitrary",), 40),
        name="mem_attn_cache",
    )(q, mem_k, mem_v)


def _merge_ffn_kernel(x_ref, oa_ref, ob_ref, om_ref, g_ref, woa_ref, wob_ref, wom_ref, wout_ref,
                      ln1g_ref, ln1b_ref, wup_ref, bup_ref, wdown_ref, bdown_ref, ln2g_ref, ln2b_ref, y_ref):
    n = x_ref.shape[0] // 2
    halves = (slice(0, n), slice(n, 2 * n))
    a = [_dot(oa_ref[r, :], woa_ref[...]) for r in halves]
    b = [_dot(ob_ref[r, :], wob_ref[...]) for r in halves]
    c = [_dot(om_ref[r, :], wom_ref[...]) for r in halves]
    m = [(g_ref[r, 0:D_MODEL].astype(F32) * a[i] + g_ref[r, D_MODEL:2 * D_MODEL].astype(F32) * b[i]
          + g_ref[r, 2 * D_MODEL:].astype(F32) * c[i]).astype(BF16) for i, r in enumerate(halves)]
    t = [_dot(mi, wout_ref[...]) for mi in m]
    x1 = [_layer_norm(ALPHA * x_ref[r, :] + t[i], ln1g_ref[...], ln1b_ref[...]) for i, r in enumerate(halves)]
    u = [_dot(xi.astype(BF16), wup_ref[...]) for xi in x1]
    h = [jnp.square(jnp.maximum(ui + bup_ref[...], 0.0)).astype(BF16) for ui in u]
    d = [_dot(hi, wdown_ref[...]) for hi in h]
    for i, r in enumerate(halves):
        y_ref[r, :] = _layer_norm(ALPHA * x1[i] + d[i] + bdown_ref[...], ln2g_ref[...], ln2b_ref[...])


def _merge_ffn(x2d, o_mla, o_ssm, o_mem, g, weights, tm):
    m = x2d.shape[0]

    def row(n):
        return pl.BlockSpec((tm, n), lambda i: (i, 0))

    return pl.pallas_call(
        _merge_ffn_kernel,
        grid=(m // tm,),
        in_specs=[row(D_MODEL), row(MLA_HEADS * V_DIM), row(D_INNER), row(MEM_WIDTH), row(N_BRANCH * D_MODEL)]
                 + [_resident(w.shape) for w in weights],
        out_specs=row(D_MODEL),
        out_shape=jax.ShapeDtypeStruct((m, D_MODEL), F32),
        compiler_params=_cp(("arbitrary",), 56),
        name="merge_ffn",
    )(x2d, o_mla, o_ssm, o_mem, g, *weights)


def _rope_table(pos):
    half = ROPE_DIM // 2
    inv = ROPE_THETA ** (-jnp.arange(half, dtype=F32) / half)
    ang = pos.astype(F32)[:, None] * inv[None, :]
    cos, sin = jnp.cos(ang), jnp.sin(ang)
    return jnp.concatenate([cos, cos, cos, cos, -sin, sin, -sin, sin], axis=1)


def _expand_matrix(width, terms):
    src = np.arange(SSM_HEADS * width) // width
    one = (np.arange(128)[:, None] == src[None, :]).astype(np.float32)
    return jnp.asarray(np.concatenate([one] * terms, axis=0), dtype=BF16)


def _shift_matrix():
    sel = np.zeros(((CONV_W - 1) * CHUNK, CARRY + CHUNK), np.float32)
    for k in range(CONV_W - 1):
        for t in range(CHUNK):
            sel[k * CHUNK + t, CARRY + t + k - (CONV_W - 1)] = 1.0
    return jnp.asarray(sel, dtype=BF16)


def _short_taps_matrix(seq_len):
    n_seq = CHUNK // seq_len
    n_old = CONV_W - 1
    n_cached = n_old * n_seq
    sel = np.zeros((n_old * CHUNK + n_cached, CHUNK + 3 * n_cached), np.float32)

    def take(out_row, s, m):
        if m >= n_old:
            sel[out_row, s * seq_len + m - n_old] = 1.0
        else:
            for term in range(3):
                sel[out_row, CHUNK + term * n_cached + m * n_seq + s] = 1.0

    for s in range(n_seq):
        for k in range(n_old):
            for i in range(seq_len):
                take(k * CHUNK + s * seq_len + i, s, i + k)
        for j in range(n_old):
            take(n_old * CHUNK + j * n_seq + s, s, seq_len + j)
    return jnp.asarray(sel, dtype=BF16)


def _placement_matrix(seq_len):
    n_seq = CHUNK // seq_len
    place = np.zeros((n_seq * SLOTS, 3 * CHUNK), np.float32)
    for s in range(n_seq):
        for j in range(seq_len):
            place[s * SLOTS + j, s * seq_len + j] = 1.0
        place[s * SLOTS + seq_len, CHUNK + s * seq_len] = 1.0
        place[s * SLOTS + seq_len + 1, 2 * CHUNK + s * seq_len] = 1.0
    return jnp.asarray(place, dtype=BF16)


def _row(v):
    return v.reshape(1, -1).astype(F32)


def kernel(x_prompt, x_sample, mem_prompt, cache_ckv, cache_krope, page_table, cache_mem_k, cache_mem_v, state_ssm, state_conv, w_in, q_norm_g, w_uq, kv_norm_g, w_uk, w_uv, conv_w, conv_b, dt_bias, a_log, d_skip, ssm_norm_g, w_mem_k, w_mem_v, b_gate, w_o_mla, w_o_ssm, w_o_mem, w_out, ln1_g, ln1_b, w_up, b_up, w_down, b_down, ln2_g, ln2_b):
    bp, seq, _ = x_prompt.shape
    bs, t_new, _ = x_sample.shape
    n_pages = page_table.shape[1]
    past = n_pages * PAGE_SIZE
    mp, ms = bp * seq, bs * t_new
    assert seq % KV_TILE == 0 and seq % CHUNK == 0 and mp % TOKEN_TILE == 0 and TOKEN_TILE % KV_TILE == 0
    assert CHUNK % t_new == 0 and ms % CHUNK == 0 and t_new + 2 <= SLOTS and MLA_HEADS * t_new % 16 == 0
    assert bs % SAMPLE_ROWS == 0 and n_pages % 2 == 0 and MEM_HALVES == 2

    o_cq, o_ckv, o_kr, o_z, o_xbc, o_dt, o_mq, o_g = np.cumsum(
        [0, Q_LORA, KV_LORA, ROPE_DIM, D_INNER, CONV_DIM, SSM_HEADS, MEM_WIDTH]).tolist()
    w_kr = w_in[:, o_kr:o_z]
    w_small = jnp.concatenate([w_in[:, o_ckv:o_kr], w_kr, w_kr, w_in[:, o_dt:o_mq],
                               jnp.zeros((D_MODEL, 128 - SSM_HEADS), F32)], axis=1)
    w_proj = tuple(w.astype(BF16) for w in (w_in[:, o_cq:o_ckv], w_small, w_in[:, o_z:o_xbc], w_in[:, o_xbc:o_dt],
                                            w_in[:, o_mq:o_g], w_in[:, o_g:]))
    half = ROPE_DIM // 2
    w_q_nope = w_uq[:, :, :NOPE_DIM].reshape(Q_LORA, MLA_HEADS * NOPE_DIM)
    w_q_rope = w_uq[:, :, NOPE_DIM:]
    w_q_swap = jnp.concatenate([w_q_rope[:, :, half:], w_q_rope[:, :, :half]], axis=-1)
    pair_w = 2 * ROPE_DIM
    w_q_pairs = jnp.concatenate([w_q_rope.reshape(Q_LORA, MLA_HEADS // 2, pair_w),
                                 w_q_swap.reshape(Q_LORA, MLA_HEADS // 2, pair_w)], axis=-1)
    wq = jnp.concatenate([w_q_nope, w_q_pairs.reshape(Q_LORA, -1)], axis=1).astype(BF16)
    wuk_t = jnp.transpose(w_uk, (1, 2, 0)).astype(BF16)
    wuk_pairs = w_uk.reshape(KV_LORA, MLA_HEADS // 2, 2 * NOPE_DIM).transpose(1, 0, 2).astype(BF16)
    wuv = jnp.transpose(w_uv, (1, 0, 2)).astype(BF16)
    wuv_t = jnp.transpose(w_uv, (1, 2, 0)).astype(BF16)
    w_mem_kv = jnp.concatenate([w_mem_k.reshape(D_MODEL, MEM_WIDTH), w_mem_v.reshape(D_MODEL, MEM_WIDTH)], axis=1).astype(BF16)
    merge_w = (w_o_mla.astype(BF16), w_o_ssm.astype(BF16), w_o_mem.astype(BF16), w_out.astype(BF16),
               _row(ln1_g), _row(ln1_b), w_up.astype(BF16), _row(b_up), w_down.astype(BF16), _row(b_down),
               _row(ln2_g), _row(ln2_b))
    gq, gkv, bg = _row(q_norm_g), _row(kv_norm_g), _row(b_gate)

    pad_heads = jnp.zeros((128 - SSM_HEADS,), F32)
    ssd_head = (conv_w.astype(F32), _row(conv_b),
                _row(jnp.concatenate([dt_bias, pad_heads])),
                jnp.broadcast_to(dt_bias.astype(F32)[:, None], (SSM_HEADS, CHUNK)),
                _row(jnp.concatenate([a_log, pad_heads])),
                jnp.broadcast_to(a_log.astype(F32)[:, None], (SSM_HEADS, CHUNK)),
                _row(jnp.repeat(d_skip, SSM_HEAD_DIM)))
    norm_g = _row(ssm_norm_g)
    expanders = (_expand_matrix(SSM_HEAD_DIM, 2), _expand_matrix(128, 3))

    n_chunks = seq // CHUNK
    cs_p = _rope_table(jnp.arange(seq))
    xp2d = x_prompt.reshape(mp, D_MODEL)
    _, ckv_p, kr_p, _, dt_p, dtt_p, z_p, xbc_p, mq_p, g_p, k_heads, vt_heads, qt_p = _in_proj(
        xp2d, w_proj, cs_p, gq, gkv, bg, tm=TOKEN_TILE,
        head_kv_weights=(wuk_pairs, wuv_t.reshape(MLA_HEADS // 2, 2 * V_DIM, KV_LORA), wq))
    o_mla_p = _prompt_attn(qt_p, k_heads, vt_heads, bp, seq)

    o_ssm_p, ssm_p, conv_p = _ssd(xbc_p, z_p, dt_p, dtt_p, ssd_head + (norm_g,) + expanders + (_shift_matrix(),),
                                  batch=bp, n_chunks=n_chunks)

    mem_k_p, mem_v_p = _mem_kv(mem_prompt.reshape(bp * N_MEM, D_MODEL), w_mem_kv, tm=min(MEM_ROWS_TILE, bp * N_MEM))
    o_mem_p = _mem_attn(mq_p.reshape(bp, seq, MEM_WIDTH), mem_k_p.reshape(bp, N_MEM, MEM_WIDTH),
                        mem_v_p.reshape(bp, N_MEM, MEM_WIDTH), nb=1, tq=min(MEM_ROWS_TILE, seq))
    y_p = _merge_ffn(xp2d, o_mla_p, o_ssm_p.reshape(mp, D_INNER), o_mem_p.reshape(mp, MEM_WIDTH), g_p, merge_w, tm=TOKEN_TILE)

    cs_s = jnp.tile(_rope_table(past + jnp.arange(t_new)), (bs, 1))
    xs2d = x_sample.reshape(ms, D_MODEL)
    tm_s = min(TOKEN_TILE, ms)
    cqn, ckv_s, kr_s, kcat, dt_s, dtt_s, z_s, xbc_s, mq_s, g_s = _in_proj(xs2d, w_proj, cs_s, gq, gkv, bg, tm=tm_s)
    q_s = _q_prep(cqn, wq, wuk_t, cs_s, tm=tm_s)
    q_s = jnp.transpose(q_s.reshape(MLA_HEADS, bs, t_new, QK_DIM), (1, 0, 2, 3)).reshape(bs, MLA_HEADS * t_new, QK_DIM)
    o_lat = _decode_attn(page_table, q_s, kcat.reshape(bs, t_new, QK_DIM).astype(F32), cache_ckv,
                         jnp.swapaxes(cache_krope, 1, 2))
    o_lat = jnp.transpose(o_lat.reshape(bs, MLA_HEADS, t_new, KV_LORA), (1, 0, 2, 3)).reshape(MLA_HEADS, ms, KV_LORA)
    o_mla_s = _uv_proj(o_lat, wuv)

    ypre_s, ec_s, xw_s, c_s, bw_s, conv_rows = _ssd_short(
        xbc_s, jnp.transpose(state_conv.astype(F32), (1, 0, 2)), dt_s, dtt_s, ssd_head + expanders,
        _short_taps_matrix(t_new), _placement_matrix(t_new), seq_len=t_new)
    conv_s = jnp.transpose(conv_rows, (1, 0, 2))
    o_ssm_s, ssm_s = _ssd_state(state_ssm.astype(F32).reshape(bs, D_INNER, D_STATE), xw_s, c_s, bw_s, ypre_s, ec_s,
                                z_s.astype(F32), norm_g, nb=SAMPLE_ROWS, seq_len=t_new)

    def cache_rows(c):
        c = c.reshape(bs, N_MEM, MEM_HEADS, MEM_HALVES, 128)
        return jnp.transpose(c, (0, 1, 3, 2, 4)).reshape(bs, N_MEM * MEM_ROW_GROUP, 128)

    mq_rows = jnp.transpose(mq_s.reshape(bs, t_new, MEM_HEADS, MEM_HALVES, 128), (0, 3, 2, 1, 4))
    o_mem_s = _mem_attn_cache(mq_rows.reshape(bs, MEM_ROW_GROUP * t_new, 128), cache_rows(cache_mem_k),
                              cache_rows(cache_mem_v), nb=SAMPLE_ROWS, n_tok=t_new)
    o_mem_s = jnp.transpose(o_mem_s.reshape(bs, MEM_HALVES, MEM_HEADS, t_new, 128), (0, 3, 2, 1, 4))
    y_s = _merge_ffn(xs2d, o_mla_s, o_ssm_s.reshape(ms, D_INNER).astype(BF16), o_mem_s.reshape(ms, MEM_WIDTH), g_s,
                     merge_w, tm=tm_s)

    return (y_p.reshape(bp, seq, D_MODEL), y_s.reshape(bs, t_new, D_MODEL),
            ckv_p.reshape(bp, seq, KV_LORA), kr_p.reshape(bp, seq, ROPE_DIM),
            mem_k_p.reshape(bp, N_MEM, MEM_HEADS, MEM_HEAD_DIM), mem_v_p.reshape(bp, N_MEM, MEM_HEADS, MEM_HEAD_DIM),
            ssm_p.reshape(bp, SSM_HEADS, SSM_HEAD_DIM, D_STATE), conv_p,
            ckv_s.reshape(bs, t_new, KV_LORA), kr_s.reshape(bs, t_new, ROPE_DIM),
            ssm_s.reshape(bs, SSM_HEADS, SSM_HEAD_DIM, D_STATE), conv_s)
```

```python
import functools

import jax
import jax.numpy as jnp
import numpy as np
from jax import lax
from jax.experimental import pallas as pl
from jax.experimental.pallas import tpu as pltpu

F32 = jnp.float32
BF16 = jnp.bfloat16

D_MODEL = 1024
MLA_HEADS = 8
Q_LORA = 384
KV_LORA = 256
NOPE_DIM = 128
ROPE_DIM = 64
V_DIM = 128
ROPE_THETA = 10000.0
QK_DIM = KV_LORA + ROPE_DIM
HEAD_QK = 256
SSM_HEADS = 32
SSM_HEAD_DIM = 64
D_INNER = SSM_HEADS * SSM_HEAD_DIM
SSM_GROUPS = 4
GROUP_W = D_INNER // SSM_GROUPS
D_STATE = 128
CONV_W = 4
CONV_DIM = D_INNER + 2 * SSM_GROUPS * D_STATE
CHUNK = 128
N_MEM = 256
MEM_HEADS = 4
MEM_HEAD_DIM = 256
MEM_WIDTH = MEM_HEADS * MEM_HEAD_DIM
D_FF = 4 * D_MODEL
N_BRANCH = 3
DEPTH = 1
ALPHA = (2 * DEPTH) ** 0.25
LN_EPS = 1e-5
RMS_EPS = 1e-6
PAGE_SIZE = 128
SCORE_SCALE = (NOPE_DIM + ROPE_DIM) ** -0.5
NEG_BIG = -1e30
KV_TILE = 256

TOKEN_TILE = 256
MEM_ROWS_TILE = 512
SAMPLE_ROWS = 4


def _cp(sem, vmem_mb):
    return pltpu.CompilerParams(dimension_semantics=sem, vmem_limit_bytes=vmem_mb << 20)


def _resident(shape):
    nd = len(shape)
    return pl.BlockSpec(shape, lambda *_: (0,) * nd, pipeline_mode=pl.Buffered(1))


def _dot(a, b):
    return jnp.dot(a, b, preferred_element_type=F32)


def _dot_nt(a, b):
    return lax.dot_general(a, b, (((1,), (1,)), ((), ())), preferred_element_type=F32)


def _rms(v, g, eps):
    return v * lax.rsqrt(jnp.mean(v * v, axis=-1, keepdims=True) + eps) * g


def _layer_norm(v, g, b):
    mu = jnp.mean(v, axis=-1, keepdims=True)
    d = v - mu
    var = jnp.mean(d * d, axis=-1, keepdims=True)
    return d * lax.rsqrt(var + LN_EPS) * g + b


def _mem_attend(q, k_ref, v_ref, o_ref):
    scale = MEM_HEAD_DIM ** -0.5
    heads = [slice(h * MEM_HEAD_DIM, (h + 1) * MEM_HEAD_DIM) for h in range(MEM_HEADS)]
    scores = [_dot_nt(q[:, hs], k_ref[:, hs].astype(BF16)) for hs in heads]
    for h, hs in enumerate(heads):
        s = scores[h] * scale
        p = jnp.exp(s - jnp.max(s, axis=1, keepdims=True))
        l = jnp.sum(p, axis=1, keepdims=True)
        o_ref[:, hs] = (_dot(p.astype(BF16), v_ref[:, hs].astype(BF16)) / l).astype(BF16)


def _in_proj_kernel(x_ref, wcq_ref, wsmall_ref, wz_ref, wxbc_ref, wmq_ref, wg_ref, cs_ref, gq_ref, gkv_ref, bg_ref,
                    *rest, per_head_kv):
    if per_head_kv:
        wuk_ref, wuvt_ref, wq_ref, memk_ref, memv_ref = rest[:5]
        rest = rest[5:]
    cqn_ref, ckv_ref, kr_ref, kcat_ref, dt_ref, dtt_ref, z_ref, xbc_ref, mq_ref, g_ref = rest[:10]
    xb = x_ref[...].astype(BF16)
    cqn = _rms(_dot(xb, wcq_ref[...]), gq_ref[...], RMS_EPS).astype(BF16)
    cqn_ref[...] = cqn

    small = _dot(xb, wsmall_ref[...])
    ckv = _rms(small[:, :KV_LORA], gkv_ref[...], RMS_EPS)
    ckv_ref[...] = ckv
    ckv_b = ckv.astype(BF16)
    kcat_ref[:, 0:KV_LORA] = ckv_b
    a = small[:, 256:384]
    b = pltpu.roll(a, ROPE_DIM // 2, 1)
    cs = cs_ref[...]
    ro = a * cs[:, :128] + b * cs[:, 128:]
    kr_ref[...] = ro[:, :ROPE_DIM]
    kcat_ref[:, KV_LORA:QK_DIM] = ro[:, :ROPE_DIM].astype(BF16)
    if per_head_kv:
        kh_ref, vt_ref, qt_ref = rest[10:]
        _write_queries(cqn, cs, wq_ref, None, qt_ref, absorb=False)
        lane = lax.broadcasted_iota(jnp.int32, ro.shape, 1)
        rope_pad = jnp.where(lane < ROPE_DIM, ro, 0.0).astype(BF16)
        ckv_t = [ckv[t * KV_TILE:(t + 1) * KV_TILE, :].T.astype(BF16) for t in range(vt_ref.shape[1])]
        for p in range(MLA_HEADS // 2):
            kk = _dot(ckv_b, wuk_ref[p]).astype(BF16)
            for i in range(2):
                kh_ref[2 * p + i, :, 0:NOPE_DIM] = kk[:, i * NOPE_DIM:(i + 1) * NOPE_DIM]
                kh_ref[2 * p + i, :, NOPE_DIM:HEAD_QK] = rope_pad
            for t, ct in enumerate(ckv_t):
                vv = _dot(wuvt_ref[p], ct).astype(BF16)
                vt_ref[2 * p, t] = vv[:V_DIM]
                vt_ref[2 * p + 1, t] = vv[V_DIM:]
    dt = small[:, 384:512]
    dt_ref[...] = dt
    dtt_ref[...] = dt.T[0:SSM_HEADS, :]

    for c in range(0, D_INNER, 1024):
        z_ref[:, c:c + 1024] = _dot(xb, wz_ref[:, c:c + 1024]).astype(BF16)
    for c in range(0, CONV_DIM, 1024):
        xbc_ref[:, c:c + 1024] = _dot(xb, wxbc_ref[:, c:c + 1024]).astype(BF16)
    mq = _dot(xb, wmq_ref[...]).astype(BF16)
    if per_head_kv:
        _mem_attend(mq, memk_ref, memv_ref, mq_ref)
    else:
        mq_ref[...] = mq
    for c in range(0, N_BRANCH * D_MODEL, 1024):
        gr = _dot(xb, wg_ref[:, c:c + 1024]) + bg_ref[:, c:c + 1024]
        g_ref[:, c:c + 1024] = jax.nn.sigmoid(gr).astype(BF16)


def _in_proj(x2d, weights, cs_tab, gq, gkv, bg, tm, head_kv_weights=(), memory=()):
    m = x2d.shape[0]
    ncs = cs_tab.shape[0] // tm
    steps_per_batch = (m // tm) // (memory[0].shape[0] // N_MEM) if memory else 0

    def row(n, dtype):
        return pl.BlockSpec((tm, n), lambda i: (i, 0)), jax.ShapeDtypeStruct((m, n), dtype)

    outs = [row(Q_LORA, BF16), row(KV_LORA, F32), row(ROPE_DIM, F32), row(QK_DIM, BF16), row(128, F32),
            (pl.BlockSpec((SSM_HEADS, tm), lambda i: (0, i)), jax.ShapeDtypeStruct((SSM_HEADS, m), F32)),
            row(D_INNER, BF16), row(CONV_DIM, BF16), row(MEM_WIDTH, BF16), row(N_BRANCH * D_MODEL, BF16)]
    if head_kv_weights:
        outs += [(pl.BlockSpec((MLA_HEADS, tm, HEAD_QK), lambda i: (0, i, 0)),
                  jax.ShapeDtypeStruct((MLA_HEADS, m, HEAD_QK), BF16)),
                 (pl.BlockSpec((MLA_HEADS, tm // KV_TILE, V_DIM, KV_TILE), lambda i: (0, i, 0, 0)),
                  jax.ShapeDtypeStruct((MLA_HEADS, m // KV_TILE, V_DIM, KV_TILE), BF16)),
                 (pl.BlockSpec((MLA_HEADS, HEAD_QK, tm), lambda i: (0, 0, i)),
                  jax.ShapeDtypeStruct((MLA_HEADS, HEAD_QK, m), BF16))]
    return pl.pallas_call(
        functools.partial(_in_proj_kernel, per_head_kv=bool(head_kv_weights)),
        grid=(m // tm,),
        in_specs=[pl.BlockSpec((tm, D_MODEL), lambda i: (i, 0))] + [_resident(w.shape) for w in weights]
                 + [pl.BlockSpec((tm, 256), lambda i: (i % ncs, 0)),
                    _resident(gq.shape), _resident(gkv.shape), _resident(bg.shape)]
                 + [_resident(w.shape) for w in head_kv_weights]
                 + [pl.BlockSpec((N_MEM, MEM_WIDTH), lambda i: (i // steps_per_batch, 0)) for _ in memory],
        out_specs=[o[0] for o in outs],
        out_shape=[o[1] for o in outs],
        compiler_params=_cp(("arbitrary",), 56),
        name="in_proj",
    )(x2d, *weights, cs_tab, gq, gkv, bg, *head_kv_weights, *memory)


def _write_queries(c, cs, wq_ref, wuk_ref, q_ref, absorb):
    qn = [_dot(c, wq_ref[:, h * NOPE_DIM:(h + 1) * NOPE_DIM]) for h in range(MLA_HEADS)]
    for h in range(MLA_HEADS):
        if absorb:
            q_ref[h, :, 0:KV_LORA] = (_dot(qn[h].astype(BF16), wuk_ref[h]) * SCORE_SCALE).astype(BF16)
        else:
            q_ref[h, 0:NOPE_DIM, :] = (qn[h] * SCORE_SCALE).T.astype(BF16)
            q_ref[h, NOPE_DIM + ROPE_DIM:HEAD_QK, :] = jnp.zeros((HEAD_QK - NOPE_DIM - ROPE_DIM, c.shape[0]), BF16)
    r0 = MLA_HEADS * NOPE_DIM
    for p in range(MLA_HEADS // 2):
        ab = _dot(c, wq_ref[:, r0 + p * 256:r0 + (p + 1) * 256])
        ro = (ab[:, :128] * cs[:, :128] + ab[:, 128:] * cs[:, 128:]) * SCORE_SCALE
        if absorb:
            q_ref[2 * p, :, KV_LORA:QK_DIM] = ro[:, :ROPE_DIM].astype(BF16)
            q_ref[2 * p + 1, :, KV_LORA:QK_DIM] = pltpu.roll(ro, ROPE_DIM, 1)[:, :ROPE_DIM].astype(BF16)
        else:
            rot = ro.T.astype(BF16)
            q_ref[2 * p, NOPE_DIM:NOPE_DIM + ROPE_DIM, :] = rot[:ROPE_DIM]
            q_ref[2 * p + 1, NOPE_DIM:NOPE_DIM + ROPE_DIM, :] = rot[ROPE_DIM:]


def _q_prep_kernel(c_ref, wq_ref, wuk_ref, cs_ref, q_ref):
    _write_queries(c_ref[...], cs_ref[...], wq_ref, wuk_ref, q_ref, absorb=True)


def _q_prep(cqn, wq, wuk_t, cs_tab, tm):
    m = cqn.shape[0]
    ncs = cs_tab.shape[0] // tm
    return pl.pallas_call(
        _q_prep_kernel,
        grid=(m // tm,),
        in_specs=[pl.BlockSpec((tm, Q_LORA), lambda i: (i, 0)), _resident(wq.shape), _resident(wuk_t.shape),
                  pl.BlockSpec((tm, 256), lambda i: (i % ncs, 0))],
        out_specs=pl.BlockSpec((MLA_HEADS, tm, QK_DIM), lambda i: (0, i, 0)),
        out_shape=jax.ShapeDtypeStruct((MLA_HEADS, m, QK_DIM), BF16),
        compiler_params=_cp(("arbitrary",), 32),
        name="q_prep",
    )(cqn, wq, wuk_t, cs_tab)


def _prompt_attn_kernel(qt_ref, k_ref, vt_ref, o_ref, m_sc, l_sc, acc_sc):
    i = pl.program_id(1)
    t = KV_TILE
    m_sc[...] = jnp.full(m_sc.shape, NEG_BIG, F32)
    l_sc[...] = jnp.zeros(l_sc.shape, F32)
    acc_sc[...] = jnp.zeros(acc_sc.shape, F32)

    def kv_block(j, masked):
        rows = pl.ds(pl.multiple_of(j * t, t), t)
        if masked:
            keep = lax.broadcasted_iota(jnp.int32, (t, t), 0) <= lax.broadcasted_iota(jnp.int32, (t, t), 1)
        pending = [_dot(k_ref[h, rows, :], qt_ref[h]) for h in range(MLA_HEADS)]
        for h in range(MLA_HEADS):
            st = pending[h]
            if masked:
                st = jnp.where(keep, st, NEG_BIG)
            m_prev = m_sc[h:h + 1, :]
            m_new = jnp.maximum(m_prev, jnp.max(st, axis=0, keepdims=True))
            alpha = jnp.exp(m_prev - m_new)
            p = jnp.exp(st - m_new)
            l_sc[h:h + 1, :] = alpha * l_sc[h:h + 1, :] + jnp.sum(p, axis=0, keepdims=True)
            m_sc[h:h + 1, :] = m_new
            acc_sc[h] = alpha * acc_sc[h] + _dot(vt_ref[h, j], p.astype(BF16))

    def body(j, carry):
        kv_block(j, False)
        return carry

    lax.fori_loop(0, i, body, 0)
    kv_block(i, True)

    for h in range(MLA_HEADS):
        o_ref[:, h * V_DIM:(h + 1) * V_DIM] = (acc_sc[h] / l_sc[h:h + 1, :]).T.astype(BF16)


def _prompt_attn(qt, k_heads, vt_heads, batch, seq):
    t = KV_TILE
    nq = seq // t
    return pl.pallas_call(
        _prompt_attn_kernel,
        grid=(batch, nq),
        in_specs=[pl.BlockSpec((MLA_HEADS, HEAD_QK, t), lambda b, i: (0, 0, b * nq + i)),
                  pl.BlockSpec((MLA_HEADS, seq, HEAD_QK), lambda b, i: (0, b, 0)),
                  pl.BlockSpec((MLA_HEADS, nq, V_DIM, t), lambda b, i: (0, b, 0, 0))],
        out_specs=pl.BlockSpec((t, MLA_HEADS * V_DIM), lambda b, i: (b * nq + i, 0)),
        out_shape=jax.ShapeDtypeStruct((batch * seq, MLA_HEADS * V_DIM), BF16),
        scratch_shapes=[pltpu.VMEM((MLA_HEADS, t), F32), pltpu.VMEM((MLA_HEADS, t), F32),
                        pltpu.VMEM((MLA_HEADS, V_DIM, t), F32)],
        compiler_params=_cp(("arbitrary", "arbitrary"), 48),
        name="prompt_attn",
    )(qt, k_heads, vt_heads)


def _decode_attn_kernel(pt_ref, q_ref, kn_ref, ckv_hbm, krt_hbm, o_ref, kbuf, rbuf, kb_sc, s_sc, sem, *, n_pages, n_new):
    b = pl.program_id(0)
    nb = pl.num_programs(0)
    slot = b % 2

    def page_copies(page, p, sl):
        rows = pl.ds(pl.multiple_of(p * PAGE_SIZE, PAGE_SIZE), PAGE_SIZE)
        return (pltpu.make_async_copy(ckv_hbm.at[page], kbuf.at[sl, rows, :], sem.at[0, sl]),
                pltpu.make_async_copy(krt_hbm.at[page], rbuf.at[sl, p], sem.at[1, sl]))

    def issue(bi, sl):
        def body(p, carry):
            for cp in page_copies(pt_ref[bi * n_pages + p], p, sl):
                cp.start()
            return carry
        lax.fori_loop(0, n_pages, body, 0, unroll=8)

    @pl.when(b == 0)
    def _():
        issue(0, 0)

    @pl.when(b + 1 < nb)
    def _():
        issue(b + 1, 1 - slot)

    def wait_body(p, carry):
        for cp in page_copies(0, p, slot):
            cp.wait()
        return carry
    lax.fori_loop(0, n_pages, wait_body, 0, unroll=8)

    q = q_ref[0]
    rows = q.shape[0]
    qc = q[:, :KV_LORA]
    qr = q[:, KV_LORA:]
    qf = q.astype(F32)
    kn = kn_ref[0]
    tok = lax.broadcasted_iota(jnp.int32, (rows, 1), 0) % n_new

    s_new = []
    for j in range(n_new):
        sj = jnp.sum(qf * kn[j:j + 1, :], axis=1, keepdims=True)
        s_new.append(jnp.where(tok >= j, sj, NEG_BIG))
    m0 = s_new[0]
    for j in range(1, n_new):
        m0 = jnp.maximum(m0, s_new[j])

    for c in range(n_pages // 2):
        r = slice(2 * c * PAGE_SIZE, 2 * (c + 1) * PAGE_SIZE)
        kc = kbuf[slot, r, :].astype(BF16)
        kb_sc[r, :] = kc
        rt = jnp.concatenate([rbuf[slot, 2 * c], rbuf[slot, 2 * c + 1]], axis=1).astype(BF16)
        s_sc[:, r] = _dot_nt(qc, kc) + _dot(qr, rt)
    s = s_sc[...]
    m = jnp.maximum(m0, jnp.max(s, axis=1, keepdims=True))
    p = jnp.exp(s - m)
    l = jnp.sum(p, axis=1, keepdims=True)
    acc = _dot(p.astype(BF16), kb_sc[...])
    for j in range(n_new):
        pj = jnp.exp(s_new[j] - m)
        l = l + pj
        acc = acc + pj * kn[j:j + 1, :KV_LORA]
    o_ref[0] = (acc / l).astype(BF16)


def _decode_attn(page_table, q, k_new, cache_ckv, cache_krope_t):
    nb, n_pages = page_table.shape
    rows = q.shape[1]
    n_new = k_new.shape[1]
    past = n_pages * PAGE_SIZE
    kern = functools.partial(_decode_attn_kernel, n_pages=n_pages, n_new=n_new)
    grid_spec = pltpu.PrefetchScalarGridSpec(
        num_scalar_prefetch=1,
        grid=(nb,),
        in_specs=[pl.BlockSpec((1, rows, QK_DIM), lambda b, pt: (b, 0, 0)),
                  pl.BlockSpec((1, n_new, QK_DIM), lambda b, pt: (b, 0, 0)),
                  pl.BlockSpec(memory_space=pl.ANY),
                  pl.BlockSpec(memory_space=pl.ANY)],
        out_specs=pl.BlockSpec((1, rows, KV_LORA), lambda b, pt: (b, 0, 0)),
        scratch_shapes=[pltpu.VMEM((2, past, KV_LORA), F32), pltpu.VMEM((2, n_pages, ROPE_DIM, PAGE_SIZE), F32),
                        pltpu.VMEM((past, KV_LORA), BF16), pltpu.VMEM((rows, past), F32),
                        pltpu.SemaphoreType.DMA((2, 2))],
    )
    return pl.pallas_call(
        kern,
        grid_spec=grid_spec,
        out_shape=jax.ShapeDtypeStruct((nb, rows, KV_LORA), BF16),
        compiler_params=_cp(("arbitrary",), 40),
        name="decode_attn",
    )(page_table.reshape(-1), q, k_new, cache_ckv, cache_krope_t)


def _uv_proj_kernel(o_ref, w_ref, out_ref):
    out_ref[...] = _dot(o_ref[0], w_ref[0]).astype(BF16)


def _uv_proj(o_lat, wuv):
    m = o_lat.shape[1]
    return pl.pallas_call(
        _uv_proj_kernel,
        grid=(MLA_HEADS,),
        in_specs=[pl.BlockSpec((1, m, KV_LORA), lambda h: (h, 0, 0)),
                  pl.BlockSpec((1, KV_LORA, V_DIM), lambda h: (h, 0, 0))],
        out_specs=pl.BlockSpec((m, V_DIM), lambda h: (0, h)),
        out_shape=jax.ShapeDtypeStruct((m, MLA_HEADS * V_DIM), BF16),
        compiler_params=_cp(("arbitrary",), 16),
        name="uv_proj",
    )(o_lat, wuv)


def _split_dot(v, e, terms, left=True):
    parts = []
    r = v
    for _ in range(terms):
        hi = r.astype(BF16)
        parts.append(hi)
        r = r - hi.astype(F32)
    if left:
        return _dot(jnp.concatenate(parts, axis=1), e)
    return _dot(e, jnp.concatenate(parts, axis=0))


def _conv_silu(taps, cw_ref, cb_ref):
    conv = cb_ref[...] + cw_ref[0:1, :] * taps[0]
    for k in range(1, CONV_W):
        conv = conv + cw_ref[k:k + 1, :] * taps[k]
    xc = conv * jax.nn.sigmoid(conv)
    n_bc = SSM_GROUPS * D_STATE
    return xc[:, :D_INNER], xc[:, D_INNER:D_INNER + n_bc], xc[:, D_INNER + n_bc:]


def _select_rows(select, rows_ref):
    n = rows_ref.shape[1]
    return jnp.concatenate([_dot(select, rows_ref[:, c:c + 1024]) for c in range(0, n, 1024)], axis=1)


def _ssd_sums(dt, dtt, keep, keep_t, alog_ref, alogt_ref):
    tri = jnp.where(keep, 1.0, 0.0).astype(BF16)
    tri_t = jnp.where(keep_t, 1.0, 0.0).astype(BF16)
    da = dt * (-jnp.exp(alog_ref[...]))
    dat = dtt * (-jnp.exp(alogt_ref[...]))
    cum = _split_dot(da, jnp.concatenate([tri] * 3, axis=1), 3, left=False)
    cum_t = _split_dot(dat, jnp.concatenate([tri_t] * 3, axis=0), 3)
    return da, cum, cum_t


def _ssd_intra(xs, bm, cm, cum_e, cum_t, dtt, keep, dskip_ref, y_sc, carried=None):
    pairs_per_group = SSM_HEADS // 2 // SSM_GROUPS
    lane = lax.broadcasted_iota(jnp.int32, (CHUNK, CHUNK), 1)
    for g in range(SSM_GROUPS):
        cg = cm[:, g * D_STATE:(g + 1) * D_STATE].astype(BF16)
        cb = _dot_nt(cg, bm[:, g * D_STATE:(g + 1) * D_STATE].astype(BF16))
        extra = None if carried is None else carried(g, cg)
        for pr in range(pairs_per_group):
            q = g * pairs_per_group + pr
            ws = []
            for h in (2 * q, 2 * q + 1):
                seg = cum_e[:, h * 128:(h + 1) * 128] - cum_t[h:h + 1, :]
                decay = jnp.exp(jnp.where(keep, seg, NEG_BIG)) * dtt[h:h + 1, :]
                ws.append((cb * decay).astype(BF16))
            w_pair = jnp.concatenate(ws, axis=1)
            xp = xs[:, q * 128:(q + 1) * 128]
            x_bd = jnp.concatenate([jnp.where(lane < SSM_HEAD_DIM, xp, 0.0).astype(BF16),
                                    jnp.where(lane >= SSM_HEAD_DIM, xp, 0.0).astype(BF16)], axis=0)
            y = _dot(w_pair, x_bd) + dskip_ref[:, q * 128:(q + 1) * 128] * xp
            if extra is not None:
                y = y + extra[:, pr * 128:(pr + 1) * 128]
            y_sc[:, q * 128:(q + 1) * 128] = y


CARRY = 16


def _ssd_kernel(xbc_ref, z_ref, dt_ref, dtt_ref, cw_ref, cb_ref, dtb_ref, dtbt_ref,
                alog_ref, alogt_ref, dskip_ref, ng_ref, e64_ref, e128_ref, shift_ref,
                y_ref, sout_ref, cout_ref, xe_sc, tail_sc, st_sc, y_sc):
    c = pl.program_id(1)
    n_pairs = SSM_HEADS // 2
    pairs_per_group = n_pairs // SSM_GROUPS

    @pl.when(c == 0)
    def _():
        xe_sc[0:CARRY, :] = jnp.zeros((CARRY, CONV_DIM), BF16)
        st_sc[...] = jnp.zeros(st_sc.shape, F32)

    xe_sc[CARRY:CARRY + CHUNK, :] = xbc_ref[...]
    shifted = _select_rows(shift_ref[...], xe_sc)
    taps = [shifted[k * CHUNK:(k + 1) * CHUNK] for k in range(CONV_W - 1)] + [xbc_ref[...].astype(F32)]
    xs, bm, cm = _conv_silu(taps, cw_ref, cb_ref)
    xe_sc[0:CARRY, :] = xe_sc[CHUNK:CHUNK + CARRY, :]

    rowi = lax.broadcasted_iota(jnp.int32, (CHUNK, CHUNK), 0)
    coli = lax.broadcasted_iota(jnp.int32, (CHUNK, CHUNK), 1)
    keep = coli <= rowi
    dt = jax.nn.softplus(dt_ref[...] + dtb_ref[...])
    dtt = jax.nn.softplus(dtt_ref[...] + dtbt_ref[...])
    _, cum, cum_t = _ssd_sums(dt, dtt, keep, rowi <= coli, alog_ref, alogt_ref)
    last = cum[CHUNK - 1:CHUNK, :]
    e64 = e64_ref[...]
    ec_e = _split_dot(jnp.exp(cum), e64, 2)
    wb_e = _split_dot(dt * jnp.exp(last - cum), e64, 2)
    dec_e = _split_dot(jnp.broadcast_to(jnp.exp(last), (16, CHUNK)), e64, 2)[0:1, :]
    cum_e = _split_dot(cum, e128_ref[...], 3)

    def carried(g, cg):
        return _dot(cg, st_sc[g].astype(BF16)) * ec_e[:, g * GROUP_W:(g + 1) * GROUP_W]

    _ssd_intra(xs, bm, cm, cum_e, cum_t, dtt, keep, dskip_ref, y_sc, carried)
    for g in range(SSM_GROUPS):
        gs = slice(g * GROUP_W, (g + 1) * GROUP_W)
        xw = (xs[:, gs] * wb_e[:, gs]).astype(BF16)
        bt = bm[:, g * D_STATE:(g + 1) * D_STATE].T.astype(BF16)
        st_sc[g] = st_sc[g] * dec_e[:, gs] + _dot(bt, xw)

    for g in range(SSM_GROUPS):
        gs = slice(g * GROUP_W, (g + 1) * GROUP_W)
        zf = z_ref[:, gs].astype(F32)
        yv = y_sc[:, gs] * (zf * jax.nn.sigmoid(zf))
        y_ref[:, gs] = _rms(yv, ng_ref[:, gs], RMS_EPS).astype(y_ref.dtype)

    @pl.when(c == pl.num_programs(1) - 1)
    def _():
        tail_sc[...] = xbc_ref[CHUNK - CARRY:CHUNK, :].astype(F32)
        cout_ref[0] = tail_sc[CARRY - (CONV_W - 1):CARRY, :]
        for q in range(n_pairs):
            g, pr = divmod(q, pairs_per_group)
            sout_ref[0, q * 128:(q + 1) * 128, :] = st_sc[g, :, pr * 128:(pr + 1) * 128].T


def _ssd(xbc, z, dt, dtt, consts, batch, n_chunks):
    def step(n):
        return pl.BlockSpec((CHUNK, n), lambda b, c: (b * n_chunks + c, 0))

    def per_batch(r, n):
        return pl.BlockSpec((1, r, n), lambda b, c: (b, 0, 0))

    return pl.pallas_call(
        _ssd_kernel,
        grid=(batch, n_chunks),
        in_specs=[step(CONV_DIM), step(D_INNER), step(128),
                  pl.BlockSpec((SSM_HEADS, CHUNK), lambda b, c: (0, b * n_chunks + c))]
                 + [_resident(a.shape) for a in consts],
        out_specs=[step(D_INNER), per_batch(D_INNER, D_STATE), per_batch(CONV_W - 1, CONV_DIM)],
        out_shape=[jax.ShapeDtypeStruct((batch * n_chunks * CHUNK, D_INNER), BF16),
                   jax.ShapeDtypeStruct((batch, D_INNER, D_STATE), F32),
                   jax.ShapeDtypeStruct((batch, CONV_W - 1, CONV_DIM), F32)],
        scratch_shapes=[pltpu.VMEM((CARRY + CHUNK, CONV_DIM), BF16),
                        pltpu.VMEM((CARRY, CONV_DIM), F32),
                        pltpu.VMEM((SSM_GROUPS, D_STATE, GROUP_W), F32),
                        pltpu.VMEM((CHUNK, D_INNER), F32)],
        compiler_params=_cp(("arbitrary", "arbitrary"), 48),
        name="ssd",
    )(xbc, z, dt, dtt, *consts)


SLOTS = 16


def _ssd_short_kernel(x_ref, sc_ref, dt_ref, dtt_ref, cw_ref, cb_ref, dtb_ref, dtbt_ref, alog_ref, alogt_ref,
                      dskip_ref, e64_ref, e128_ref, taps_sel_ref, place_ref,
                      ypre_ref, ec_ref, xw_ref, c_ref, bw_ref, tails_ref, rows_sc, y_sc, *, seq_len):
    n_cached = sc_ref.shape[0] * sc_ref.shape[1]
    rows_sc[0:CHUNK, :] = x_ref[...]
    r = sc_ref[...].reshape(n_cached, CONV_DIM)
    for t in range(3):
        hi = r.astype(BF16)
        rows_sc[CHUNK + t * n_cached:CHUNK + (t + 1) * n_cached, :] = hi
        r = r - hi.astype(F32)
    picked = _select_rows(taps_sel_ref[...], rows_sc)
    taps = [picked[k * CHUNK:(k + 1) * CHUNK] for k in range(CONV_W - 1)] + [x_ref[...].astype(F32)]
    tails_ref[...] = picked[(CONV_W - 1) * CHUNK:].reshape(tails_ref.shape)
    xs, bm, cm = _conv_silu(taps, cw_ref, cb_ref)
    rowi = lax.broadcasted_iota(jnp.int32, (CHUNK, CHUNK), 0)
    coli = lax.broadcasted_iota(jnp.int32, (CHUNK, CHUNK), 1)
    same = (rowi // seq_len) == (coli // seq_len)
    keep = same & (coli <= rowi)
    dt = jax.nn.softplus(dt_ref[...] + dtb_ref[...])
    dtt = jax.nn.softplus(dtt_ref[...] + dtbt_ref[...])
    da, cum, cum_t = _ssd_sums(dt, dtt, keep, same & (rowi <= coli), alog_ref, alogt_ref)
    whole = jnp.where(same, 1.0, 0.0).astype(BF16)
    last = _split_dot(da, jnp.concatenate([whole] * 3, axis=1), 3, left=False)
    e64 = e64_ref[...]
    ec_ref[...] = _split_dot(jnp.exp(cum), e64, 2)
    wb_e = _split_dot(dt * jnp.exp(last - cum), e64, 2)
    dec_e = _split_dot(jnp.exp(last), e64, 2)
    cum_e = _split_dot(cum, e128_ref[...], 3)
    _ssd_intra(xs, bm, cm, cum_e, cum_t, dtt, keep, dskip_ref, y_sc)
    ypre_ref[...] = y_sc[...]

    place = place_ref[...]
    dec_hi = dec_e.astype(BF16)
    dec_lo = (dec_e - dec_hi.astype(F32)).astype(BF16)
    xw = (xs * wb_e).astype(BF16)
    xw_ref[...] = _dot(place, jnp.concatenate([xw, dec_hi, dec_lo], axis=0)).astype(BF16)
    tokens = place[:, :CHUNK]
    c_ref[...] = _dot(tokens, cm.astype(BF16)).astype(BF16)
    zeros = jnp.zeros((CHUNK, D_STATE), BF16)
    b_wide = jnp.concatenate(
        [piece for g in range(SSM_GROUPS) for piece in (bm[:, g * D_STATE:(g + 1) * D_STATE].astype(BF16), zeros)], axis=1)
    n_rows = place.shape[0]
    slot = lax.broadcasted_iota(jnp.int32, (n_rows, 2 * D_STATE * SSM_GROUPS), 0) % SLOTS
    col = lax.broadcasted_iota(jnp.int32, (n_rows, 2 * D_STATE * SSM_GROUPS), 1) % (2 * D_STATE)
    ones = ((slot == seq_len) | (slot == seq_len + 1)) & (col >= D_STATE)
    bw_ref[...] = jnp.where(ones, 1.0, _dot(tokens, b_wide)).astype(BF16)


def _ssd_short(xbc, conv_rows, dt, dtt, consts, taps_sel, place, seq_len):
    rows = dt.shape[0]
    n_chunks = rows // CHUNK
    n_seq = CHUNK // seq_len
    n_slot_rows = place.shape[0]
    kern = functools.partial(_ssd_short_kernel, seq_len=seq_len)

    def out(r, n, dtype):
        return pl.BlockSpec((r, n), lambda c: (c, 0)), jax.ShapeDtypeStruct((n_chunks * r, n), dtype)

    conv_spec = pl.BlockSpec((CONV_W - 1, n_seq, CONV_DIM), lambda c: (0, c, 0))
    outs = [out(CHUNK, D_INNER, F32), out(CHUNK, D_INNER, F32), out(n_slot_rows, D_INNER, BF16),
            out(n_slot_rows, SSM_GROUPS * D_STATE, BF16), out(n_slot_rows, 2 * SSM_GROUPS * D_STATE, BF16),
            (conv_spec, jax.ShapeDtypeStruct(conv_rows.shape, F32))]
    return pl.pallas_call(
        kern,
        grid=(n_chunks,),
        in_specs=[pl.BlockSpec((CHUNK, CONV_DIM), lambda c: (c, 0)), conv_spec,
                  pl.BlockSpec((CHUNK, 128), lambda c: (c, 0)),
                  pl.BlockSpec((SSM_HEADS, CHUNK), lambda c: (0, c))]
                 + [_resident(a.shape) for a in consts] + [_resident(taps_sel.shape), _resident(place.shape)],
        out_specs=[o[0] for o in outs],
        out_shape=[o[1] for o in outs],
        scratch_shapes=[pltpu.VMEM((taps_sel.shape[1], CONV_DIM), BF16), pltpu.VMEM((CHUNK, D_INNER), F32)],
        compiler_params=_cp(("arbitrary",), 48),
        name="ssd_short",
    )(xbc, conv_rows, dt, dtt, *consts, taps_sel, place)


def _ssd_state_kernel(s0_ref, xw_ref, c_ref, bw_ref, ypre_ref, ec_ref, z_ref, ng_ref, y_ref, sout_ref, ci_sc,
                      *, nb, seq_len):
    for bb in range(nb):
        slots = slice(bb * SLOTS, (bb + 1) * SLOTS)
        toks = slice(bb * seq_len, (bb + 1) * seq_len)
        for g in range(SSM_GROUPS):
            gs = slice(g * GROUP_W, (g + 1) * GROUP_W)
            s0g = s0_ref[bb, gs, :]
            ci_sc[...] = _dot_nt(c_ref[slots, g * D_STATE:(g + 1) * D_STATE], s0g.astype(BF16))
            u = lax.dot_general(xw_ref[slots, gs], bw_ref[slots, 2 * g * D_STATE:2 * (g + 1) * D_STATE],
                                (((0,), (0,)), ((), ())), preferred_element_type=F32)
            sout_ref[bb, gs, :] = u[:, :D_STATE] + s0g * u[:, D_STATE:]
            zf = z_ref[toks, gs]
            yv = (ypre_ref[toks, gs] + ec_ref[toks, gs] * ci_sc[0:seq_len, :]) * (zf * jax.nn.sigmoid(zf))
            y_ref[toks, gs] = _rms(yv, ng_ref[:, gs], RMS_EPS)


def _ssd_state(s0, xw, cmat, bw, ypre, ec, z, ng, nb, seq_len):
    batch = s0.shape[0]
    kern = functools.partial(_ssd_state_kernel, nb=nb, seq_len=seq_len)

    def rows(r, n):
        return pl.BlockSpec((nb * r, n), lambda b: (b, 0))

    state_spec = pl.BlockSpec((nb, D_INNER, D_STATE), lambda b: (b, 0, 0))
    return pl.pallas_call(
        kern,
        grid=(batch // nb,),
        in_specs=[state_spec, rows(SLOTS, D_INNER), rows(SLOTS, SSM_GROUPS * D_STATE),
                  rows(SLOTS, 2 * SSM_GROUPS * D_STATE), rows(seq_len, D_INNER), rows(seq_len, D_INNER),
                  rows(seq_len, D_INNER), _resident(ng.shape)],
        out_specs=[rows(seq_len, D_INNER), state_spec],
        out_shape=[jax.ShapeDtypeStruct((batch * seq_len, D_INNER), F32),
                   jax.ShapeDtypeStruct((batch, D_INNER, D_STATE), F32)],
        scratch_shapes=[pltpu.VMEM((SLOTS, GROUP_W), F32)],
        compiler_params=_cp(("arbitrary",), 40),
        name="ssd_state",
    )(s0, xw, cmat, bw, ypre, ec, z, ng)


def _mem_kv_kernel(m_ref, w_ref, k_ref, v_ref):
    mb = m_ref[...].astype(BF16)
    k_ref[...] = _dot(mb, w_ref[:, :MEM_WIDTH])
    v_ref[...] = _dot(mb, w_ref[:, MEM_WIDTH:])


def _mem_kv(mem2d, w_kv, tm):
    m = mem2d.shape[0]
    return pl.pallas_call(
        _mem_kv_kernel,
        grid=(m // tm,),
        in_specs=[pl.BlockSpec((tm, D_MODEL), lambda i: (i, 0)), _resident(w_kv.shape)],
        out_specs=[pl.BlockSpec((tm, MEM_WIDTH), lambda i: (i, 0))] * 2,
        out_shape=[jax.ShapeDtypeStruct((m, MEM_WIDTH), F32)] * 2,
        compiler_params=_cp(("arbitrary",), 32),
        name="mem_kv",
    )(mem2d, w_kv)


MEM_HALVES = MEM_HEAD_DIM // 128
MEM_ROW_GROUP = MEM_HALVES * MEM_HEADS


def _mem_attn_cache_kernel(q_ref, k_ref, v_ref, o_ref, *, nb, n_tok):
    rows = MEM_HEADS * n_tok
    n_col = N_MEM * MEM_ROW_GROUP
    col = lax.broadcasted_iota(jnp.int32, (rows, n_col), 1)
    row = lax.broadcasted_iota(jnp.int32, (rows, n_col), 0)
    own = (col % MEM_ROW_GROUP) == (row // n_tok)
    scale = MEM_HEAD_DIM ** -0.5
    for bb in range(nb):
        g = _dot_nt(q_ref[bb], k_ref[bb].astype(BF16))
        s = (g[:rows] + pltpu.roll(g[rows:], n_col - MEM_HEADS, 1)) * scale
        s = jnp.where(own, s, NEG_BIG)
        p = jnp.exp(s - jnp.max(s, axis=1, keepdims=True))
        l = jnp.sum(p, axis=1, keepdims=True)
        p2 = jnp.concatenate([p, pltpu.roll(p, MEM_HEADS, 1)], axis=0).astype(BF16)
        o = _dot(p2, v_ref[bb].astype(BF16))
        o_ref[bb] = (o / jnp.concatenate([l, l], axis=0)).astype(BF16)


def _mem_attn_cache(q, mem_k, mem_v, nb, n_tok):
    batch, rows, _ = q.shape
    kern = functools.partial(_mem_attn_cache_kernel, nb=nb, n_tok=n_tok)
    kv_spec = pl.BlockSpec((nb,) + mem_k.shape[1:], lambda b: (b, 0, 0))
    q_spec = pl.BlockSpec((nb, rows, 128), lambda b: (b, 0, 0))
    return pl.pallas_call(
        kern,
        grid=(batch // nb,),
        in_specs=[q_spec, kv_spec, kv_spec],
        out_specs=q_spec,
        out_shape=jax.ShapeDtypeStruct((batch, rows, 128), BF16),
        compiler_params=_cp(("arbitrary",), 40),
        name="mem_attn_cache",
    )(q, mem_k, mem_v)


def _merge_ffn_kernel(x_ref, oa_ref, ob_ref, om_ref, g_ref, woa_ref, wob_ref, wom_ref, wout_ref,
                      ln1g_ref, ln1b_ref, wup_ref, bup_ref, wdown_ref, bdown_ref, ln2g_ref, ln2b_ref, y_ref):
    n = x_ref.shape[0] // 2
    halves = (slice(0, n), slice(n, 2 * n))
    a = [_dot(oa_ref[r, :], woa_ref[...]) for r in halves]
    b = [_dot(ob_ref[r, :], wob_ref[...]) for r in halves]
    c = [_dot(om_ref[r, :], wom_ref[...]) for r in halves]
    m = [(g_ref[r, 0:D_MODEL].astype(F32) * a[i] + g_ref[r, D_MODEL:2 * D_MODEL].astype(F32) * b[i]
          + g_ref[r, 2 * D_MODEL:].astype(F32) * c[i]).astype(BF16) for i, r in enumerate(halves)]
    t = [_dot(mi, wout_ref[...]) for mi in m]
    x1 = [_layer_norm(ALPHA * x_ref[r, :] + t[i], ln1g_ref[...], ln1b_ref[...]) for i, r in enumerate(halves)]
    u = [_dot(xi.astype(BF16), wup_ref[...]) for xi in x1]
    h = [jnp.square(jnp.maximum(ui + bup_ref[...], 0.0)).astype(BF16) for ui in u]
    d = [_dot(hi, wdown_ref[...]) for hi in h]
    for i, r in enumerate(halves):
        y_ref[r, :] = _layer_norm(ALPHA * x1[i] + d[i] + bdown_ref[...], ln2g_ref[...], ln2b_ref[...])


def _merge_ffn(x2d, o_mla, o_ssm, o_mem, g, weights, tm):
    m = x2d.shape[0]

    def row(n):
        return pl.BlockSpec((tm, n), lambda i: (i, 0))

    return pl.pallas_call(
        _merge_ffn_kernel,
        grid=(m // tm,),
        in_specs=[row(D_MODEL), row(MLA_HEADS * V_DIM), row(D_INNER), row(MEM_WIDTH), row(N_BRANCH * D_MODEL)]
                 + [_resident(w.shape) for w in weights],
        out_specs=row(D_MODEL),
        out_shape=jax.ShapeDtypeStruct((m, D_MODEL), F32),
        compiler_params=_cp(("arbitrary",), 56),
        name="merge_ffn",
    )(x2d, o_mla, o_ssm, o_mem, g, *weights)


def _rope_table(pos):
    half = ROPE_DIM // 2
    inv = ROPE_THETA ** (-jnp.arange(half, dtype=F32) / half)
    ang = pos.astype(F32)[:, None] * inv[None, :]
    cos, sin = jnp.cos(ang), jnp.sin(ang)
    return jnp.concatenate([cos, cos, cos, cos, -sin, sin, -sin, sin], axis=1)


def _expand_matrix(width, terms):
    src = np.arange(SSM_HEADS * width) // width
    one = (np.arange(128)[:, None] == src[None, :]).astype(np.float32)
    return jnp.asarray(np.concatenate([one] * terms, axis=0), dtype=BF16)


def _shift_matrix():
    sel = np.zeros(((CONV_W - 1) * CHUNK, CARRY + CHUNK), np.float32)
    for k in range(CONV_W - 1):
        for t in range(CHUNK):
            sel[k * CHUNK + t, CARRY + t + k - (CONV_W - 1)] = 1.0
    return jnp.asarray(sel, dtype=BF16)


def _short_taps_matrix(seq_len):
    n_seq = CHUNK // seq_len
    n_old = CONV_W - 1
    n_cached = n_old * n_seq
    sel = np.zeros((n_old * CHUNK + n_cached, CHUNK + 3 * n_cached), np.float32)

    def take(out_row, s, m):
        if m >= n_old:
            sel[out_row, s * seq_len + m - n_old] = 1.0
        else:
            for term in range(3):
                sel[out_row, CHUNK + term * n_cached + m * n_seq + s] = 1.0

    for s in range(n_seq):
        for k in range(n_old):
            for i in range(seq_len):
                take(k * CHUNK + s * seq_len + i, s, i + k)
        for j in range(n_old):
            take(n_old * CHUNK + j * n_seq + s, s, seq_len + j)
    return jnp.asarray(sel, dtype=BF16)


def _placement_matrix(seq_len):
    n_seq = CHUNK // seq_len
    place = np.zeros((n_seq * SLOTS, 3 * CHUNK), np.float32)
    for s in range(n_seq):
        for j in range(seq_len):
            place[s * SLOTS + j, s * seq_len + j] = 1.0
        place[s * SLOTS + seq_len, CHUNK + s * seq_len] = 1.0
        place[s * SLOTS + seq_len + 1, 2 * CHUNK + s * seq_len] = 1.0
    return jnp.asarray(place, dtype=BF16)


def _row(v):
    return v.reshape(1, -1).astype(F32)


def kernel(x_prompt, x_sample, mem_prompt, cache_ckv, cache_krope, page_table, cache_mem_k, cache_mem_v, state_ssm, state_conv, w_in, q_norm_g, w_uq, kv_norm_g, w_uk, w_uv, conv_w, conv_b, dt_bias, a_log, d_skip, ssm_norm_g, w_mem_k, w_mem_v, b_gate, w_o_mla, w_o_ssm, w_o_mem, w_out, ln1_g, ln1_b, w_up, b_up, w_down, b_down, ln2_g, ln2_b):
    bp, seq, _ = x_prompt.shape
    bs, t_new, _ = x_sample.shape
    n_pages = page_table.shape[1]
    past = n_pages * PAGE_SIZE
    mp, ms = bp * seq, bs * t_new
    assert seq % KV_TILE == 0 and seq % CHUNK == 0 and mp % TOKEN_TILE == 0 and TOKEN_TILE % KV_TILE == 0
    assert CHUNK % t_new == 0 and ms % CHUNK == 0 and t_new + 2 <= SLOTS and MLA_HEADS * t_new % 16 == 0
    assert bs % SAMPLE_ROWS == 0 and n_pages % 2 == 0 and MEM_HALVES == 2

    o_cq, o_ckv, o_kr, o_z, o_xbc, o_dt, o_mq, o_g = np.cumsum(
        [0, Q_LORA, KV_LORA, ROPE_DIM, D_INNER, CONV_DIM, SSM_HEADS, MEM_WIDTH]).tolist()
    w_kr = w_in[:, o_kr:o_z]
    w_small = jnp.concatenate([w_in[:, o_ckv:o_kr], w_kr, w_kr, w_in[:, o_dt:o_mq],
                               jnp.zeros((D_MODEL, 128 - SSM_HEADS), F32)], axis=1)
    w_proj = tuple(w.astype(BF16) for w in (w_in[:, o_cq:o_ckv], w_small, w_in[:, o_z:o_xbc], w_in[:, o_xbc:o_dt],
                                            w_in[:, o_mq:o_g], w_in[:, o_g:]))
    half = ROPE_DIM // 2
    w_q_nope = w_uq[:, :, :NOPE_DIM].reshape(Q_LORA, MLA_HEADS * NOPE_DIM)
    w_q_rope = w_uq[:, :, NOPE_DIM:]
    w_q_swap = jnp.concatenate([w_q_rope[:, :, half:], w_q_rope[:, :, :half]], axis=-1)
    pair_w = 2 * ROPE_DIM
    w_q_pairs = jnp.concatenate([w_q_rope.reshape(Q_LORA, MLA_HEADS // 2, pair_w),
                                 w_q_swap.reshape(Q_LORA, MLA_HEADS // 2, pair_w)], axis=-1)
    wq = jnp.concatenate([w_q_nope, w_q_pairs.reshape(Q_LORA, -1)], axis=1).astype(BF16)
    wuk_t = jnp.transpose(w_uk, (1, 2, 0)).astype(BF16)
    wuk_pairs = w_uk.reshape(KV_LORA, MLA_HEADS // 2, 2 * NOPE_DIM).transpose(1, 0, 2).astype(BF16)
    wuv = jnp.transpose(w_uv, (1, 0, 2)).astype(BF16)
    wuv_t = jnp.transpose(w_uv, (1, 2, 0)).astype(BF16)
    w_mem_kv = jnp.concatenate([w_mem_k.reshape(D_MODEL, MEM_WIDTH), w_mem_v.reshape(D_MODEL, MEM_WIDTH)], axis=1).astype(BF16)
    merge_w = (w_o_mla.astype(BF16), w_o_ssm.astype(BF16), w_o_mem.astype(BF16), w_out.astype(BF16),
               _row(ln1_g), _row(ln1_b), w_up.astype(BF16), _row(b_up), w_down.astype(BF16), _row(b_down),
               _row(ln2_g), _row(ln2_b))
    gq, gkv, bg = _row(q_norm_g), _row(kv_norm_g), _row(b_gate)

    pad_heads = jnp.zeros((128 - SSM_HEADS,), F32)
    ssd_head = (conv_w.astype(F32), _row(conv_b),
                _row(jnp.concatenate([dt_bias, pad_heads])),
                jnp.broadcast_to(dt_bias.astype(F32)[:, None], (SSM_HEADS, CHUNK)),
                _row(jnp.concatenate([a_log, pad_heads])),
                jnp.broadcast_to(a_log.astype(F32)[:, None], (SSM_HEADS, CHUNK)),
                _row(jnp.repeat(d_skip, SSM_HEAD_DIM)))
    norm_g = _row(ssm_norm_g)
    expanders = (_expand_matrix(SSM_HEAD_DIM, 2), _expand_matrix(128, 3))

    n_chunks = seq // CHUNK
    cs_p = _rope_table(jnp.arange(seq))
    xp2d = x_prompt.reshape(mp, D_MODEL)
    mem_k_p, mem_v_p = _mem_kv(mem_prompt.reshape(bp * N_MEM, D_MODEL), w_mem_kv, tm=min(MEM_ROWS_TILE, bp * N_MEM))
    _, ckv_p, kr_p, _, dt_p, dtt_p, z_p, xbc_p, o_mem_p, g_p, k_heads, vt_heads, qt_p = _in_proj(
        xp2d, w_proj, cs_p, gq, gkv, bg, tm=TOKEN_TILE,
        head_kv_weights=(wuk_pairs, wuv_t.reshape(MLA_HEADS // 2, 2 * V_DIM, KV_LORA), wq),
        memory=(mem_k_p, mem_v_p))
    o_mla_p = _prompt_attn(qt_p, k_heads, vt_heads, bp, seq)

    o_ssm_p, ssm_p, conv_p = _ssd(xbc_p, z_p, dt_p, dtt_p, ssd_head + (norm_g,) + expanders + (_shift_matrix(),),
                                  batch=bp, n_chunks=n_chunks)

    y_p = _merge_ffn(xp2d, o_mla_p, o_ssm_p, o_mem_p, g_p, merge_w, tm=TOKEN_TILE)

    cs_s = jnp.tile(_rope_table(past + jnp.arange(t_new)), (bs, 1))
    xs2d = x_sample.reshape(ms, D_MODEL)
    tm_s = min(TOKEN_TILE, ms)
    cqn, ckv_s, kr_s, kcat, dt_s, dtt_s, z_s, xbc_s, mq_s, g_s = _in_proj(xs2d, w_proj, cs_s, gq, gkv, bg, tm=tm_s)
    q_s = _q_prep(cqn, wq, wuk_t, cs_s, tm=tm_s)
    q_s = jnp.transpose(q_s.reshape(MLA_HEADS, bs, t_new, QK_DIM), (1, 0, 2, 3)).reshape(bs, MLA_HEADS * t_new, QK_DIM)
    o_lat = _decode_attn(page_table, q_s, kcat.reshape(bs, t_new, QK_DIM).astype(F32), cache_ckv,
                         jnp.swapaxes(cache_krope, 1, 2))
    o_lat = jnp.transpose(o_lat.reshape(bs, MLA_HEADS, t_new, KV_LORA), (1, 0, 2, 3)).reshape(MLA_HEADS, ms, KV_LORA)
    o_mla_s = _uv_proj(o_lat, wuv)

    ypre_s, ec_s, xw_s, c_s, bw_s, conv_rows = _ssd_short(
        xbc_s, jnp.transpose(state_conv.astype(F32), (1, 0, 2)), dt_s, dtt_s, ssd_head + expanders,
        _short_taps_matrix(t_new), _placement_matrix(t_new), seq_len=t_new)
    conv_s = jnp.transpose(conv_rows, (1, 0, 2))
    o_ssm_s, ssm_s = _ssd_state(state_ssm.astype(F32).reshape(bs, D_INNER, D_STATE), xw_s, c_s, bw_s, ypre_s, ec_s,
                                z_s.astype(F32), norm_g, nb=SAMPLE_ROWS, seq_len=t_new)

    def cache_rows(c):
        c = c.reshape(bs, N_MEM, MEM_HEADS, MEM_HALVES, 128)
        return jnp.transpose(c, (0, 1, 3, 2, 4)).reshape(bs, N_MEM * MEM_ROW_GROUP, 128)

    mq_rows = jnp.transpose(mq_s.reshape(bs, t_new, MEM_HEADS, MEM_HALVES, 128), (0, 3, 2, 1, 4))
    o_mem_s = _mem_attn_cache(mq_rows.reshape(bs, MEM_ROW_GROUP * t_new, 128), cache_rows(cache_mem_k),
                              cache_rows(cache_mem_v), nb=SAMPLE_ROWS, n_tok=t_new)
    o_mem_s = jnp.transpose(o_mem_s.reshape(bs, MEM_HALVES, MEM_HEADS, t_new, 128), (0, 3, 2, 1, 4))
    y_s = _merge_ffn(xs2d, o_mla_s, o_ssm_s.reshape(ms, D_INNER).astype(BF16), o_mem_s.reshape(ms, MEM_WIDTH), g_s,
                     merge_w, tm=tm_s)

    return (y_p.reshape(bp, seq, D_MODEL), y_s.reshape(bs, t_new, D_MODEL),
            ckv_p.reshape(bp, seq, KV_LORA), kr_p.reshape(bp, seq, ROPE_DIM),
            mem_k_p.reshape(bp, N_MEM, MEM_HEADS, MEM_HEAD_DIM), mem_v_p.reshape(bp, N_MEM, MEM_HEADS, MEM_HEAD_DIM),
            ssm_p.reshape(bp, SSM_HEADS, SSM_HEAD_DIM, D_STATE), conv_p,
            ckv_s.reshape(bs, t_new, KV_LORA), kr_s.reshape(bs, t_new, ROPE_DIM),
            ssm_s.reshape(bs, SSM_HEADS, SSM_HEAD_DIM, D_STATE), conv_s)
```

```python
import functools

import jax
import jax.numpy as jnp
import numpy as np
from jax import lax
from jax.experimental import pallas as pl
from jax.experimental.pallas import tpu as pltpu

F32 = jnp.float32
BF16 = jnp.bfloat16

D_MODEL = 1024
MLA_HEADS = 8
Q_LORA = 384
KV_LORA = 256
NOPE_DIM = 128
ROPE_DIM = 64
V_DIM = 128
ROPE_THETA = 10000.0
QK_DIM = KV_LORA + ROPE_DIM
HEAD_QK = 256
SSM_HEADS = 32
SSM_HEAD_DIM = 64
D_INNER = SSM_HEADS * SSM_HEAD_DIM
SSM_GROUPS = 4
GROUP_W = D_INNER // SSM_GROUPS
D_STATE = 128
CONV_W = 4
CONV_DIM = D_INNER + 2 * SSM_GROUPS * D_STATE
CHUNK = 128
N_MEM = 256
MEM_HEADS = 4
MEM_HEAD_DIM = 256
MEM_WIDTH = MEM_HEADS * MEM_HEAD_DIM
D_FF = 4 * D_MODEL
N_BRANCH = 3
DEPTH = 1
ALPHA = (2 * DEPTH) ** 0.25
LN_EPS = 1e-5
RMS_EPS = 1e-6
PAGE_SIZE = 128
SCORE_SCALE = (NOPE_DIM + ROPE_DIM) ** -0.5
NEG_BIG = -1e30
KV_TILE = 256

TOKEN_TILE = 256
MEM_ROWS_TILE = 512
SAMPLE_ROWS = 4


def _cp(sem, vmem_mb):
    return pltpu.CompilerParams(dimension_semantics=sem, vmem_limit_bytes=vmem_mb << 20)


def _resident(shape):
    nd = len(shape)
    return pl.BlockSpec(shape, lambda *_: (0,) * nd, pipeline_mode=pl.Buffered(1))


def _dot(a, b):
    return jnp.dot(a, b, preferred_element_type=F32)


def _dot_nt(a, b):
    return lax.dot_general(a, b, (((1,), (1,)), ((), ())), preferred_element_type=F32)


def _rms(v, g, eps):
    return v * lax.rsqrt(jnp.mean(v * v, axis=-1, keepdims=True) + eps) * g


def _layer_norm(v, g, b):
    mu = jnp.mean(v, axis=-1, keepdims=True)
    d = v - mu
    var = jnp.mean(d * d, axis=-1, keepdims=True)
    return d * lax.rsqrt(var + LN_EPS) * g + b


def _mem_attend(q, k_ref, v_ref, o_ref):
    scale = MEM_HEAD_DIM ** -0.5
    heads = [slice(h * MEM_HEAD_DIM, (h + 1) * MEM_HEAD_DIM) for h in range(MEM_HEADS)]
    scores = [_dot_nt(q[:, hs], k_ref[:, hs].astype(BF16)) for hs in heads]
    for h, hs in enumerate(heads):
        s = scores[h] * scale
        p = jnp.exp(s - jnp.max(s, axis=1, keepdims=True))
        l = jnp.sum(p, axis=1, keepdims=True)
        o_ref[:, hs] = (_dot(p.astype(BF16), v_ref[:, hs].astype(BF16)) / l).astype(BF16)


def _in_proj_kernel(x_ref, wcq_ref, wsmall_ref, wz_ref, wxbc_ref, wmq_ref, wg_ref, cs_ref, gq_ref, gkv_ref, bg_ref,
                    *rest, per_head_kv):
    if per_head_kv:
        wuk_ref, wuvt_ref, wq_ref, memk_ref, memv_ref = rest[:5]
        rest = rest[5:]
    cqn_ref, ckv_ref, kr_ref, kcat_ref, dt_ref, dtt_ref, z_ref, xbc_ref, mq_ref, g_ref = rest[:10]
    xb = x_ref[...].astype(BF16)
    cqn = _rms(_dot(xb, wcq_ref[...]), gq_ref[...], RMS_EPS).astype(BF16)
    cqn_ref[...] = cqn

    small = _dot(xb, wsmall_ref[...])
    ckv = _rms(small[:, :KV_LORA], gkv_ref[...], RMS_EPS)
    ckv_ref[...] = ckv
    ckv_b = ckv.astype(BF16)
    kcat_ref[:, 0:KV_LORA] = ckv_b
    a = small[:, 256:384]
    b = pltpu.roll(a, ROPE_DIM // 2, 1)
    cs = cs_ref[...]
    ro = a * cs[:, :128] + b * cs[:, 128:]
    kr_ref[...] = ro[:, :ROPE_DIM]
    kcat_ref[:, KV_LORA:QK_DIM] = ro[:, :ROPE_DIM].astype(BF16)
    if per_head_kv:
        kh_ref, vt_ref, qt_ref = rest[10:]
        _write_queries(cqn, cs, wq_ref, None, qt_ref, absorb=False)
        lane = lax.broadcasted_iota(jnp.int32, ro.shape, 1)
        rope_pad = jnp.where(lane < ROPE_DIM, ro, 0.0).astype(BF16)
        ckv_t = [ckv[t * KV_TILE:(t + 1) * KV_TILE, :].T.astype(BF16) for t in range(vt_ref.shape[1])]
        for p in range(MLA_HEADS // 2):
            kk = _dot(ckv_b, wuk_ref[p]).astype(BF16)
            for i in range(2):
                kh_ref[2 * p + i, :, 0:NOPE_DIM] = kk[:, i * NOPE_DIM:(i + 1) * NOPE_DIM]
                kh_ref[2 * p + i, :, NOPE_DIM:HEAD_QK] = rope_pad
            for t, ct in enumerate(ckv_t):
                vv = _dot(wuvt_ref[p], ct).astype(BF16)
                vt_ref[2 * p, t] = vv[:V_DIM]
                vt_ref[2 * p + 1, t] = vv[V_DIM:]
    dt = small[:, 384:512]
    dt_ref[...] = dt
    dtt_ref[...] = dt.T[0:SSM_HEADS, :]

    for c in range(0, D_INNER, 1024):
        z_ref[:, c:c + 1024] = _dot(xb, wz_ref[:, c:c + 1024]).astype(BF16)
    for c in range(0, CONV_DIM, 1024):
        xbc_ref[:, c:c + 1024] = _dot(xb, wxbc_ref[:, c:c + 1024]).astype(BF16)
    mq = _dot(xb, wmq_ref[...]).astype(BF16)
    if per_head_kv:
        _mem_attend(mq, memk_ref, memv_ref, mq_ref)
    else:
        mq_ref[...] = mq
    for c in range(0, N_BRANCH * D_MODEL, 1024):
        gr = _dot(xb, wg_ref[:, c:c + 1024]) + bg_ref[:, c:c + 1024]
        g_ref[:, c:c + 1024] = jax.nn.sigmoid(gr).astype(BF16)


def _in_proj(x2d, weights, cs_tab, gq, gkv, bg, tm, head_kv_weights=(), memory=()):
    m = x2d.shape[0]
    ncs = cs_tab.shape[0] // tm
    steps_per_batch = (m // tm) // (memory[0].shape[0] // N_MEM) if memory else 0

    def row(n, dtype):
        return pl.BlockSpec((tm, n), lambda i: (i, 0)), jax.ShapeDtypeStruct((m, n), dtype)

    outs = [row(Q_LORA, BF16), row(KV_LORA, F32), row(ROPE_DIM, F32), row(QK_DIM, BF16), row(128, F32),
            (pl.BlockSpec((SSM_HEADS, tm), lambda i: (0, i)), jax.ShapeDtypeStruct((SSM_HEADS, m), F32)),
            row(D_INNER, BF16), row(CONV_DIM, BF16), row(MEM_WIDTH, BF16), row(N_BRANCH * D_MODEL, BF16)]
    if head_kv_weights:
        outs += [(pl.BlockSpec((MLA_HEADS, tm, HEAD_QK), lambda i: (0, i, 0)),
                  jax.ShapeDtypeStruct((MLA_HEADS, m, HEAD_QK), BF16)),
                 (pl.BlockSpec((MLA_HEADS, tm // KV_TILE, V_DIM, KV_TILE), lambda i: (0, i, 0, 0)),
                  jax.ShapeDtypeStruct((MLA_HEADS, m // KV_TILE, V_DIM, KV_TILE), BF16)),
                 (pl.BlockSpec((MLA_HEADS, HEAD_QK, tm), lambda i: (0, 0, i)),
                  jax.ShapeDtypeStruct((MLA_HEADS, HEAD_QK, m), BF16))]
    return pl.pallas_call(
        functools.partial(_in_proj_kernel, per_head_kv=bool(head_kv_weights)),
        grid=(m // tm,),
        in_specs=[pl.BlockSpec((tm, D_MODEL), lambda i: (i, 0))] + [_resident(w.shape) for w in weights]
                 + [pl.BlockSpec((tm, 256), lambda i: (i % ncs, 0)),
                    _resident(gq.shape), _resident(gkv.shape), _resident(bg.shape)]
                 + [_resident(w.shape) for w in head_kv_weights]
                 + [pl.BlockSpec((N_MEM, MEM_WIDTH), lambda i: (i // steps_per_batch, 0)) for _ in memory],
        out_specs=[o[0] for o in outs],
        out_shape=[o[1] for o in outs],
        compiler_params=_cp(("arbitrary",), 56),
        name="in_proj",
    )(x2d, *weights, cs_tab, gq, gkv, bg, *head_kv_weights, *memory)


def _write_queries(c, cs, wq_ref, wuk_ref, q_ref, absorb):
    qn = [_dot(c, wq_ref[:, h * NOPE_DIM:(h + 1) * NOPE_DIM]) for h in range(MLA_HEADS)]
    for h in range(MLA_HEADS):
        if absorb:
            q_ref[h, :, 0:KV_LORA] = (_dot(qn[h].astype(BF16), wuk_ref[h]) * SCORE_SCALE).astype(BF16)
        else:
            q_ref[h, 0:NOPE_DIM, :] = (qn[h] * SCORE_SCALE).T.astype(BF16)
            q_ref[h, NOPE_DIM + ROPE_DIM:HEAD_QK, :] = jnp.zeros((HEAD_QK - NOPE_DIM - ROPE_DIM, c.shape[0]), BF16)
    r0 = MLA_HEADS * NOPE_DIM
    for p in range(MLA_HEADS // 2):
        ab = _dot(c, wq_ref[:, r0 + p * 256:r0 + (p + 1) * 256])
        ro = (ab[:, :128] * cs[:, :128] + ab[:, 128:] * cs[:, 128:]) * SCORE_SCALE
        if absorb:
            q_ref[2 * p, :, KV_LORA:QK_DIM] = ro[:, :ROPE_DIM].astype(BF16)
            q_ref[2 * p + 1, :, KV_LORA:QK_DIM] = pltpu.roll(ro, ROPE_DIM, 1)[:, :ROPE_DIM].astype(BF16)
        else:
            rot = ro.T.astype(BF16)
            q_ref[2 * p, NOPE_DIM:NOPE_DIM + ROPE_DIM, :] = rot[:ROPE_DIM]
            q_ref[2 * p + 1, NOPE_DIM:NOPE_DIM + ROPE_DIM, :] = rot[ROPE_DIM:]


def _q_prep_kernel(c_ref, wq_ref, wuk_ref, cs_ref, q_ref):
    _write_queries(c_ref[...], cs_ref[...], wq_ref, wuk_ref, q_ref, absorb=True)


def _q_prep(cqn, wq, wuk_t, cs_tab, tm):
    m = cqn.shape[0]
    ncs = cs_tab.shape[0] // tm
    return pl.pallas_call(
        _q_prep_kernel,
        grid=(m // tm,),
        in_specs=[pl.BlockSpec((tm, Q_LORA), lambda i: (i, 0)), _resident(wq.shape), _resident(wuk_t.shape),
                  pl.BlockSpec((tm, 256), lambda i: (i % ncs, 0))],
        out_specs=pl.BlockSpec((MLA_HEADS, tm, QK_DIM), lambda i: (0, i, 0)),
        out_shape=jax.ShapeDtypeStruct((MLA_HEADS, m, QK_DIM), BF16),
        compiler_params=_cp(("arbitrary",), 32),
        name="q_prep",
    )(cqn, wq, wuk_t, cs_tab)


def _prompt_attn_kernel(qt_ref, k_ref, vt_ref, o_ref, m_sc, l_sc, acc_sc):
    i = pl.program_id(1)
    t = KV_TILE
    m_sc[...] = jnp.full(m_sc.shape, NEG_BIG, F32)
    l_sc[...] = jnp.zeros(l_sc.shape, F32)
    acc_sc[...] = jnp.zeros(acc_sc.shape, F32)

    def scores(j):
        rows = pl.ds(pl.multiple_of(j * t, t), t)
        return [_dot(k_ref[h, rows, :], qt_ref[h]) for h in range(MLA_HEADS)]

    def kv_block(j, masked, pending):
        if masked:
            keep = lax.broadcasted_iota(jnp.int32, (t, t), 0) <= lax.broadcasted_iota(jnp.int32, (t, t), 1)
        for h in range(MLA_HEADS):
            st = pending[h]
            if masked:
                st = jnp.where(keep, st, NEG_BIG)
            m_prev = m_sc[h:h + 1, :]
            m_new = jnp.maximum(m_prev, jnp.max(st, axis=0, keepdims=True))
            alpha = jnp.exp(m_prev - m_new)
            p = jnp.exp(st - m_new)
            l_sc[h:h + 1, :] = alpha * l_sc[h:h + 1, :] + jnp.sum(p, axis=0, keepdims=True)
            m_sc[h:h + 1, :] = m_new
            acc_sc[h] = alpha * acc_sc[h] + _dot(vt_ref[h, j], p.astype(BF16))

    def two_tiles(jj, carry):
        first, second = scores(2 * jj), scores(2 * jj + 1)
        kv_block(2 * jj, False, first)
        kv_block(2 * jj + 1, False, second)
        return carry

    lax.fori_loop(0, i // 2, two_tiles, 0)

    @pl.when(i % 2 == 1)
    def _():
        kv_block(i - 1, False, scores(i - 1))

    kv_block(i, True, scores(i))

    for h in range(MLA_HEADS):
        o_ref[:, h * V_DIM:(h + 1) * V_DIM] = (acc_sc[h] / l_sc[h:h + 1, :]).T.astype(BF16)


def _prompt_attn(qt, k_heads, vt_heads, batch, seq):
    t = KV_TILE
    nq = seq // t
    return pl.pallas_call(
        _prompt_attn_kernel,
        grid=(batch, nq),
        in_specs=[pl.BlockSpec((MLA_HEADS, HEAD_QK, t), lambda b, i: (0, 0, b * nq + i)),
                  pl.BlockSpec((MLA_HEADS, seq, HEAD_QK), lambda b, i: (0, b, 0)),
                  pl.BlockSpec((MLA_HEADS, nq, V_DIM, t), lambda b, i: (0, b, 0, 0))],
        out_specs=pl.BlockSpec((t, MLA_HEADS * V_DIM), lambda b, i: (b * nq + i, 0)),
        out_shape=jax.ShapeDtypeStruct((batch * seq, MLA_HEADS * V_DIM), BF16),
        scratch_shapes=[pltpu.VMEM((MLA_HEADS, t), F32), pltpu.VMEM((MLA_HEADS, t), F32),
                        pltpu.VMEM((MLA_HEADS, V_DIM, t), F32)],
        compiler_params=_cp(("arbitrary", "arbitrary"), 48),
        name="prompt_attn",
    )(qt, k_heads, vt_heads)


def _decode_attn_kernel(pt_ref, q_ref, kn_ref, ckv_hbm, krt_hbm, o_ref, kbuf, rbuf, kb_sc, s_sc, sem, *, n_pages, n_new):
    b = pl.program_id(0)
    nb = pl.num_programs(0)
    slot = b % 2

    def page_copies(page, p, sl):
        rows = pl.ds(pl.multiple_of(p * PAGE_SIZE, PAGE_SIZE), PAGE_SIZE)
        return (pltpu.make_async_copy(ckv_hbm.at[page], kbuf.at[sl, rows, :], sem.at[0, sl]),
                pltpu.make_async_copy(krt_hbm.at[page], rbuf.at[sl, p], sem.at[1, sl]))

    def issue(bi, sl):
        def body(p, carry):
            for cp in page_copies(pt_ref[bi * n_pages + p], p, sl):
                cp.start()
            return carry
        lax.fori_loop(0, n_pages, body, 0, unroll=8)

    @pl.when(b == 0)
    def _():
        issue(0, 0)

    @pl.when(b + 1 < nb)
    def _():
        issue(b + 1, 1 - slot)

    def wait_body(p, carry):
        for cp in page_copies(0, p, slot):
            cp.wait()
        return carry
    lax.fori_loop(0, n_pages, wait_body, 0, unroll=8)

    q = q_ref[0]
    rows = q.shape[0]
    qc = q[:, :KV_LORA]
    qr = q[:, KV_LORA:]
    qf = q.astype(F32)
    kn = kn_ref[0]
    tok = lax.broadcasted_iota(jnp.int32, (rows, 1), 0) % n_new

    s_new = []
    for j in range(n_new):
        sj = jnp.sum(qf * kn[j:j + 1, :], axis=1, keepdims=True)
        s_new.append(jnp.where(tok >= j, sj, NEG_BIG))
    m0 = s_new[0]
    for j in range(1, n_new):
        m0 = jnp.maximum(m0, s_new[j])

    for c in range(n_pages // 2):
        r = slice(2 * c * PAGE_SIZE, 2 * (c + 1) * PAGE_SIZE)
        kc = kbuf[slot, r, :].astype(BF16)
        kb_sc[r, :] = kc
        rt = jnp.concatenate([rbuf[slot, 2 * c], rbuf[slot, 2 * c + 1]], axis=1).astype(BF16)
        s_sc[:, r] = _dot_nt(qc, kc) + _dot(qr, rt)
    s = s_sc[...]
    m = jnp.maximum(m0, jnp.max(s, axis=1, keepdims=True))
    p = jnp.exp(s - m)
    l = jnp.sum(p, axis=1, keepdims=True)
    acc = _dot(p.astype(BF16), kb_sc[...])
    for j in range(n_new):
        pj = jnp.exp(s_new[j] - m)
        l = l + pj
        acc = acc + pj * kn[j:j + 1, :KV_LORA]
    o_ref[0] = (acc / l).astype(BF16)


def _decode_attn(page_table, q, k_new, cache_ckv, cache_krope_t):
    nb, n_pages = page_table.shape
    rows = q.shape[1]
    n_new = k_new.shape[1]
    past = n_pages * PAGE_SIZE
    kern = functools.partial(_decode_attn_kernel, n_pages=n_pages, n_new=n_new)
    grid_spec = pltpu.PrefetchScalarGridSpec(
        num_scalar_prefetch=1,
        grid=(nb,),
        in_specs=[pl.BlockSpec((1, rows, QK_DIM), lambda b, pt: (b, 0, 0)),
                  pl.BlockSpec((1, n_new, QK_DIM), lambda b, pt: (b, 0, 0)),
                  pl.BlockSpec(memory_space=pl.ANY),
                  pl.BlockSpec(memory_space=pl.ANY)],
        out_specs=pl.BlockSpec((1, rows, KV_LORA), lambda b, pt: (b, 0, 0)),
        scratch_shapes=[pltpu.VMEM((2, past, KV_LORA), F32), pltpu.VMEM((2, n_pages, ROPE_DIM, PAGE_SIZE), F32),
                        pltpu.VMEM((past, KV_LORA), BF16), pltpu.VMEM((rows, past), F32),
                        pltpu.SemaphoreType.DMA((2, 2))],
    )
    return pl.pallas_call(
        kern,
        grid_spec=grid_spec,
        out_shape=jax.ShapeDtypeStruct((nb, rows, KV_LORA), BF16),
        compiler_params=_cp(("arbitrary",), 40),
        name="decode_attn",
    )(page_table.reshape(-1), q, k_new, cache_ckv, cache_krope_t)


def _uv_proj_kernel(o_ref, w_ref, out_ref):
    out_ref[...] = _dot(o_ref[0], w_ref[0]).astype(BF16)


def _uv_proj(o_lat, wuv):
    m = o_lat.shape[1]
    return pl.pallas_call(
        _uv_proj_kernel,
        grid=(MLA_HEADS,),
        in_specs=[pl.BlockSpec((1, m, KV_LORA), lambda h: (h, 0, 0)),
                  pl.BlockSpec((1, KV_LORA, V_DIM), lambda h: (h, 0, 0))],
        out_specs=pl.BlockSpec((m, V_DIM), lambda h: (0, h)),
        out_shape=jax.ShapeDtypeStruct((m, MLA_HEADS * V_DIM), BF16),
        compiler_params=_cp(("arbitrary",), 16),
        name="uv_proj",
    )(o_lat, wuv)


def _split_dot(v, e, terms, left=True):
    parts = []
    r = v
    for _ in range(terms):
        hi = r.astype(BF16)
        parts.append(hi)
        r = r - hi.astype(F32)
    if left:
        return _dot(jnp.concatenate(parts, axis=1), e)
    return _dot(e, jnp.concatenate(parts, axis=0))


def _conv_silu(taps, cw_ref, cb_ref):
    conv = cb_ref[...] + cw_ref[0:1, :] * taps[0]
    for k in range(1, CONV_W):
        conv = conv + cw_ref[k:k + 1, :] * taps[k]
    xc = conv * jax.nn.sigmoid(conv)
    n_bc = SSM_GROUPS * D_STATE
    return xc[:, :D_INNER], xc[:, D_INNER:D_INNER + n_bc], xc[:, D_INNER + n_bc:]


def _select_rows(select, rows_ref):
    n = rows_ref.shape[1]
    return jnp.concatenate([_dot(select, rows_ref[:, c:c + 1024]) for c in range(0, n, 1024)], axis=1)


def _ssd_sums(dt, dtt, keep, keep_t, alog_ref, alogt_ref):
    tri = jnp.where(keep, 1.0, 0.0).astype(BF16)
    tri_t = jnp.where(keep_t, 1.0, 0.0).astype(BF16)
    da = dt * (-jnp.exp(alog_ref[...]))
    dat = dtt * (-jnp.exp(alogt_ref[...]))
    cum = _split_dot(da, jnp.concatenate([tri] * 3, axis=1), 3, left=False)
    cum_t = _split_dot(dat, jnp.concatenate([tri_t] * 3, axis=0), 3)
    return da, cum, cum_t


def _ssd_intra(xs, bm, cm, cum_e, cum_t, dtt, keep, dskip_ref, y_sc, carried=None):
    pairs_per_group = SSM_HEADS // 2 // SSM_GROUPS
    lane = lax.broadcasted_iota(jnp.int32, (CHUNK, CHUNK), 1)
    for g in range(SSM_GROUPS):
        cg = cm[:, g * D_STATE:(g + 1) * D_STATE].astype(BF16)
        cb = _dot_nt(cg, bm[:, g * D_STATE:(g + 1) * D_STATE].astype(BF16))
        extra = None if carried is None else carried(g, cg)
        for pr in range(pairs_per_group):
            q = g * pairs_per_group + pr
            ws = []
            for h in (2 * q, 2 * q + 1):
                seg = cum_e[:, h * 128:(h + 1) * 128] - cum_t[h:h + 1, :]
                decay = jnp.exp(jnp.where(keep, seg, NEG_BIG)) * dtt[h:h + 1, :]
                ws.append((cb * decay).astype(BF16))
            w_pair = jnp.concatenate(ws, axis=1)
            xp = xs[:, q * 128:(q + 1) * 128]
            x_bd = jnp.concatenate([jnp.where(lane < SSM_HEAD_DIM, xp, 0.0).astype(BF16),
                                    jnp.where(lane >= SSM_HEAD_DIM, xp, 0.0).astype(BF16)], axis=0)
            y = _dot(w_pair, x_bd) + dskip_ref[:, q * 128:(q + 1) * 128] * xp
            if extra is not None:
                y = y + extra[:, pr * 128:(pr + 1) * 128]
            y_sc[:, q * 128:(q + 1) * 128] = y


CARRY = 16


def _ssd_kernel(xbc_ref, z_ref, dt_ref, dtt_ref, cw_ref, cb_ref, dtb_ref, dtbt_ref,
                alog_ref, alogt_ref, dskip_ref, ng_ref, e64_ref, e128_ref, shift_ref,
                y_ref, sout_ref, cout_ref, xe_sc, tail_sc, st_sc, y_sc):
    c = pl.program_id(1)
    n_pairs = SSM_HEADS // 2
    pairs_per_group = n_pairs // SSM_GROUPS

    @pl.when(c == 0)
    def _():
        xe_sc[0:CARRY, :] = jnp.zeros((CARRY, CONV_DIM), BF16)
        st_sc[...] = jnp.zeros(st_sc.shape, F32)

    xe_sc[CARRY:CARRY + CHUNK, :] = xbc_ref[...]
    shifted = _select_rows(shift_ref[...], xe_sc)
    taps = [shifted[k * CHUNK:(k + 1) * CHUNK] for k in range(CONV_W - 1)] + [xbc_ref[...].astype(F32)]
    xs, bm, cm = _conv_silu(taps, cw_ref, cb_ref)
    xe_sc[0:CARRY, :] = xe_sc[CHUNK:CHUNK + CARRY, :]

    rowi = lax.broadcasted_iota(jnp.int32, (CHUNK, CHUNK), 0)
    coli = lax.broadcasted_iota(jnp.int32, (CHUNK, CHUNK), 1)
    keep = coli <= rowi
    dt = jax.nn.softplus(dt_ref[...] + dtb_ref[...])
    dtt = jax.nn.softplus(dtt_ref[...] + dtbt_ref[...])
    _, cum, cum_t = _ssd_sums(dt, dtt, keep, rowi <= coli, alog_ref, alogt_ref)
    last = cum[CHUNK - 1:CHUNK, :]
    e64 = e64_ref[...]
    ec_e = _split_dot(jnp.exp(cum), e64, 2)
    wb_e = _split_dot(dt * jnp.exp(last - cum), e64, 2)
    dec_e = _split_dot(jnp.broadcast_to(jnp.exp(last), (16, CHUNK)), e64, 2)[0:1, :]
    cum_e = _split_dot(cum, e128_ref[...], 3)

    def carried(g, cg):
        return _dot(cg, st_sc[g].astype(BF16)) * ec_e[:, g * GROUP_W:(g + 1) * GROUP_W]

    _ssd_intra(xs, bm, cm, cum_e, cum_t, dtt, keep, dskip_ref, y_sc, carried)
    for g in range(SSM_GROUPS):
        gs = slice(g * GROUP_W, (g + 1) * GROUP_W)
        xw = (xs[:, gs] * wb_e[:, gs]).astype(BF16)
        bt = bm[:, g * D_STATE:(g + 1) * D_STATE].T.astype(BF16)
        st_sc[g] = st_sc[g] * dec_e[:, gs] + _dot(bt, xw)

    for g in range(SSM_GROUPS):
        gs = slice(g * GROUP_W, (g + 1) * GROUP_W)
        zf = z_ref[:, gs].astype(F32)
        yv = y_sc[:, gs] * (zf * jax.nn.sigmoid(zf))
        y_ref[:, gs] = _rms(yv, ng_ref[:, gs], RMS_EPS).astype(y_ref.dtype)

    @pl.when(c == pl.num_programs(1) - 1)
    def _():
        tail_sc[...] = xbc_ref[CHUNK - CARRY:CHUNK, :].astype(F32)
        cout_ref[0] = tail_sc[CARRY - (CONV_W - 1):CARRY, :]
        for q in range(n_pairs):
            g, pr = divmod(q, pairs_per_group)
            sout_ref[0, q * 128:(q + 1) * 128, :] = st_sc[g, :, pr * 128:(pr + 1) * 128].T


def _ssd(xbc, z, dt, dtt, consts, batch, n_chunks):
    def step(n):
        return pl.BlockSpec((CHUNK, n), lambda b, c: (b * n_chunks + c, 0))

    def per_batch(r, n):
        return pl.BlockSpec((1, r, n), lambda b, c: (b, 0, 0))

    return pl.pallas_call(
        _ssd_kernel,
        grid=(batch, n_chunks),
        in_specs=[step(CONV_DIM), step(D_INNER), step(128),
                  pl.BlockSpec((SSM_HEADS, CHUNK), lambda b, c: (0, b * n_chunks + c))]
                 + [_resident(a.shape) for a in consts],
        out_specs=[step(D_INNER), per_batch(D_INNER, D_STATE), per_batch(CONV_W - 1, CONV_DIM)],
        out_shape=[jax.ShapeDtypeStruct((batch * n_chunks * CHUNK, D_INNER), BF16),
                   jax.ShapeDtypeStruct((batch, D_INNER, D_STATE), F32),
                   jax.ShapeDtypeStruct((batch, CONV_W - 1, CONV_DIM), F32)],
        scratch_shapes=[pltpu.VMEM((CARRY + CHUNK, CONV_DIM), BF16),
                        pltpu.VMEM((CARRY, CONV_DIM), F32),
                        pltpu.VMEM((SSM_GROUPS, D_STATE, GROUP_W), F32),
                        pltpu.VMEM((CHUNK, D_INNER), F32)],
        compiler_params=_cp(("arbitrary", "arbitrary"), 48),
        name="ssd",
    )(xbc, z, dt, dtt, *consts)


SLOTS = 16


def _ssd_short_kernel(x_ref, sc_ref, dt_ref, dtt_ref, cw_ref, cb_ref, dtb_ref, dtbt_ref, alog_ref, alogt_ref,
                      dskip_ref, e64_ref, e128_ref, taps_sel_ref, place_ref,
                      ypre_ref, ec_ref, xw_ref, c_ref, bw_ref, tails_ref, rows_sc, y_sc, *, seq_len):
    n_cached = sc_ref.shape[0] * sc_ref.shape[1]
    rows_sc[0:CHUNK, :] = x_ref[...]
    r = sc_ref[...].reshape(n_cached, CONV_DIM)
    for t in range(3):
        hi = r.astype(BF16)
        rows_sc[CHUNK + t * n_cached:CHUNK + (t + 1) * n_cached, :] = hi
        r = r - hi.astype(F32)
    picked = _select_rows(taps_sel_ref[...], rows_sc)
    taps = [picked[k * CHUNK:(k + 1) * CHUNK] for k in range(CONV_W - 1)] + [x_ref[...].astype(F32)]
    tails_ref[...] = picked[(CONV_W - 1) * CHUNK:].reshape(tails_ref.shape)
    xs, bm, cm = _conv_silu(taps, cw_ref, cb_ref)
    rowi = lax.broadcasted_iota(jnp.int32, (CHUNK, CHUNK), 0)
    coli = lax.broadcasted_iota(jnp.int32, (CHUNK, CHUNK), 1)
    same = (rowi // seq_len) == (coli // seq_len)
    keep = same & (coli <= rowi)
    dt = jax.nn.softplus(dt_ref[...] + dtb_ref[...])
    dtt = jax.nn.softplus(dtt_ref[...] + dtbt_ref[...])
    da, cum, cum_t = _ssd_sums(dt, dtt, keep, same & (rowi <= coli), alog_ref, alogt_ref)
    whole = jnp.where(same, 1.0, 0.0).astype(BF16)
    last = _split_dot(da, jnp.concatenate([whole] * 3, axis=1), 3, left=False)
    e64 = e64_ref[...]
    ec_ref[...] = _split_dot(jnp.exp(cum), e64, 2)
    wb_e = _split_dot(dt * jnp.exp(last - cum), e64, 2)
    dec_e = _split_dot(jnp.exp(last), e64, 2)
    cum_e = _split_dot(cum, e128_ref[...], 3)
    _ssd_intra(xs, bm, cm, cum_e, cum_t, dtt, keep, dskip_ref, y_sc)
    ypre_ref[...] = y_sc[...]

    place = place_ref[...]
    dec_hi = dec_e.astype(BF16)
    dec_lo = (dec_e - dec_hi.astype(F32)).astype(BF16)
    xw = (xs * wb_e).astype(BF16)
    xw_ref[...] = _dot(place, jnp.concatenate([xw, dec_hi, dec_lo], axis=0)).astype(BF16)
    tokens = place[:, :CHUNK]
    c_ref[...] = _dot(tokens, cm.astype(BF16)).astype(BF16)
    zeros = jnp.zeros((CHUNK, D_STATE), BF16)
    b_wide = jnp.concatenate(
        [piece for g in range(SSM_GROUPS) for piece in (bm[:, g * D_STATE:(g + 1) * D_STATE].astype(BF16), zeros)], axis=1)
    n_rows = place.shape[0]
    slot = lax.broadcasted_iota(jnp.int32, (n_rows, 2 * D_STATE * SSM_GROUPS), 0) % SLOTS
    col = lax.broadcasted_iota(jnp.int32, (n_rows, 2 * D_STATE * SSM_GROUPS), 1) % (2 * D_STATE)
    ones = ((slot == seq_len) | (slot == seq_len + 1)) & (col >= D_STATE)
    bw_ref[...] = jnp.where(ones, 1.0, _dot(tokens, b_wide)).astype(BF16)


def _ssd_short(xbc, conv_rows, dt, dtt, consts, taps_sel, place, seq_len):
    rows = dt.shape[0]
    n_chunks = rows // CHUNK
    n_seq = CHUNK // seq_len
    n_slot_rows = place.shape[0]
    kern = functools.partial(_ssd_short_kernel, seq_len=seq_len)

    def out(r, n, dtype):
        return pl.BlockSpec((r, n), lambda c: (c, 0)), jax.ShapeDtypeStruct((n_chunks * r, n), dtype)

    conv_spec = pl.BlockSpec((CONV_W - 1, n_seq, CONV_DIM), lambda c: (0, c, 0))
    outs = [out(CHUNK, D_INNER, F32), out(CHUNK, D_INNER, F32), out(n_slot_rows, D_INNER, BF16),
            out(n_slot_rows, SSM_GROUPS * D_STATE, BF16), out(n_slot_rows, 2 * SSM_GROUPS * D_STATE, BF16),
            (conv_spec, jax.ShapeDtypeStruct(conv_rows.shape, F32))]
    return pl.pallas_call(
        kern,
        grid=(n_chunks,),
        in_specs=[pl.BlockSpec((CHUNK, CONV_DIM), lambda c: (c, 0)), conv_spec,
                  pl.BlockSpec((CHUNK, 128), lambda c: (c, 0)),
                  pl.BlockSpec((SSM_HEADS, CHUNK), lambda c: (0, c))]
                 + [_resident(a.shape) for a in consts] + [_resident(taps_sel.shape), _resident(place.shape)],
        out_specs=[o[0] for o in outs],
        out_shape=[o[1] for o in outs],
        scratch_shapes=[pltpu.VMEM((taps_sel.shape[1], CONV_DIM), BF16), pltpu.VMEM((CHUNK, D_INNER), F32)],
        compiler_params=_cp(("arbitrary",), 48),
        name="ssd_short",
    )(xbc, conv_rows, dt, dtt, *consts, taps_sel, place)


def _ssd_state_kernel(s0_ref, xw_ref, c_ref, bw_ref, ypre_ref, ec_ref, z_ref, ng_ref, y_ref, sout_ref, ci_sc,
                      *, nb, seq_len):
    for bb in range(nb):
        slots = slice(bb * SLOTS, (bb + 1) * SLOTS)
        toks = slice(bb * seq_len, (bb + 1) * seq_len)
        for g in range(SSM_GROUPS):
            gs = slice(g * GROUP_W, (g + 1) * GROUP_W)
            s0g = s0_ref[bb, gs, :]
            ci_sc[...] = _dot_nt(c_ref[slots, g * D_STATE:(g + 1) * D_STATE], s0g.astype(BF16))
            u = lax.dot_general(xw_ref[slots, gs], bw_ref[slots, 2 * g * D_STATE:2 * (g + 1) * D_STATE],
                                (((0,), (0,)), ((), ())), preferred_element_type=F32)
            sout_ref[bb, gs, :] = u[:, :D_STATE] + s0g * u[:, D_STATE:]
            zf = z_ref[toks, gs]
            yv = (ypre_ref[toks, gs] + ec_ref[toks, gs] * ci_sc[0:seq_len, :]) * (zf * jax.nn.sigmoid(zf))
            y_ref[toks, gs] = _rms(yv, ng_ref[:, gs], RMS_EPS)


def _ssd_state(s0, xw, cmat, bw, ypre, ec, z, ng, nb, seq_len):
    batch = s0.shape[0]
    kern = functools.partial(_ssd_state_kernel, nb=nb, seq_len=seq_len)

    def rows(r, n):
        return pl.BlockSpec((nb * r, n), lambda b: (b, 0))

    state_spec = pl.BlockSpec((nb, D_INNER, D_STATE), lambda b: (b, 0, 0))
    return pl.pallas_call(
        kern,
        grid=(batch // nb,),
        in_specs=[state_spec, rows(SLOTS, D_INNER), rows(SLOTS, SSM_GROUPS * D_STATE),
                  rows(SLOTS, 2 * SSM_GROUPS * D_STATE), rows(seq_len, D_INNER), rows(seq_len, D_INNER),
                  rows(seq_len, D_INNER), _resident(ng.shape)],
        out_specs=[rows(seq_len, D_INNER), state_spec],
        out_shape=[jax.ShapeDtypeStruct((batch * seq_len, D_INNER), F32),
                   jax.ShapeDtypeStruct((batch, D_INNER, D_STATE), F32)],
        scratch_shapes=[pltpu.VMEM((SLOTS, GROUP_W), F32)],
        compiler_params=_cp(("arbitrary",), 40),
        name="ssd_state",
    )(s0, xw, cmat, bw, ypre, ec, z, ng)


def _mem_kv_kernel(m_ref, w_ref, k_ref, v_ref):
    mb = m_ref[...].astype(BF16)
    k_ref[...] = _dot(mb, w_ref[:, :MEM_WIDTH])
    v_ref[...] = _dot(mb, w_ref[:, MEM_WIDTH:])


def _mem_kv(mem2d, w_kv, tm):
    m = mem2d.shape[0]
    return pl.pallas_call(
        _mem_kv_kernel,
        grid=(m // tm,),
        in_specs=[pl.BlockSpec((tm, D_MODEL), lambda i: (i, 0)), _resident(w_kv.shape)],
        out_specs=[pl.BlockSpec((tm, MEM_WIDTH), lambda i: (i, 0))] * 2,
        out_shape=[jax.ShapeDtypeStruct((m, MEM_WIDTH), F32)] * 2,
        compiler_params=_cp(("arbitrary",), 32),
        name="mem_kv",
    )(mem2d, w_kv)


MEM_HALVES = MEM_HEAD_DIM // 128
MEM_ROW_GROUP = MEM_HALVES * MEM_HEADS


def _mem_attn_cache_kernel(q_ref, k_ref, v_ref, o_ref, *, nb, n_tok):
    rows = MEM_HEADS * n_tok
    n_col = N_MEM * MEM_ROW_GROUP
    col = lax.broadcasted_iota(jnp.int32, (rows, n_col), 1)
    row = lax.broadcasted_iota(jnp.int32, (rows, n_col), 0)
    own = (col % MEM_ROW_GROUP) == (row // n_tok)
    scale = MEM_HEAD_DIM ** -0.5
    for bb in range(nb):
        g = _dot_nt(q_ref[bb], k_ref[bb].astype(BF16))
        s = (g[:rows] + pltpu.roll(g[rows:], n_col - MEM_HEADS, 1)) * scale
        s = jnp.where(own, s, NEG_BIG)
        p = jnp.exp(s - jnp.max(s, axis=1, keepdims=True))
        l = jnp.sum(p, axis=1, keepdims=True)
        p2 = jnp.concatenate([p, pltpu.roll(p, MEM_HEADS, 1)], axis=0).astype(BF16)
        o = _dot(p2, v_ref[bb].astype(BF16))
        o_ref[bb] = (o / jnp.concatenate([l, l], axis=0)).astype(BF16)


def _mem_attn_cache(q, mem_k, mem_v, nb, n_tok):
    batch, rows, _ = q.shape
    kern = functools.partial(_mem_attn_cache_kernel, nb=nb, n_tok=n_tok)
    kv_spec = pl.BlockSpec((nb,) + mem_k.shape[1:], lambda b: (b, 0, 0))
    q_spec = pl.BlockSpec((nb, rows, 128), lambda b: (b, 0, 0))
    return pl.pallas_call(
        kern,
        grid=(batch // nb,),
        in_specs=[q_spec, kv_spec, kv_spec],
        out_specs=q_spec,
        out_shape=jax.ShapeDtypeStruct((batch, rows, 128), BF16),
        compiler_params=_cp(("arbitrary",), 40),
        name="mem_attn_cache",
    )(q, mem_k, mem_v)


def _merge_ffn_kernel(x_ref, oa_ref, ob_ref, om_ref, g_ref, woa_ref, wob_ref, wom_ref, wout_ref,
                      ln1g_ref, ln1b_ref, wup_ref, bup_ref, wdown_ref, bdown_ref, ln2g_ref, ln2b_ref, y_ref):
    n = x_ref.shape[0] // 2
    halves = (slice(0, n), slice(n, 2 * n))
    a = [_dot(oa_ref[r, :], woa_ref[...]) for r in halves]
    b = [_dot(ob_ref[r, :], wob_ref[...]) for r in halves]
    c = [_dot(om_ref[r, :], wom_ref[...]) for r in halves]
    m = [(g_ref[r, 0:D_MODEL].astype(F32) * a[i] + g_ref[r, D_MODEL:2 * D_MODEL].astype(F32) * b[i]
          + g_ref[r, 2 * D_MODEL:].astype(F32) * c[i]).astype(BF16) for i, r in enumerate(halves)]
    t = [_dot(mi, wout_ref[...]) for mi in m]
    x1 = [_layer_norm(ALPHA * x_ref[r, :] + t[i], ln1g_ref[...], ln1b_ref[...]) for i, r in enumerate(halves)]
    u = [_dot(xi.astype(BF16), wup_ref[...]) for xi in x1]
    h = [jnp.square(jnp.maximum(ui + bup_ref[...], 0.0)).astype(BF16) for ui in u]
    d = [_dot(hi, wdown_ref[...]) for hi in h]
    for i, r in enumerate(halves):
        y_ref[r, :] = _layer_norm(ALPHA * x1[i] + d[i] + bdown_ref[...], ln2g_ref[...], ln2b_ref[...])


def _merge_ffn(x2d, o_mla, o_ssm, o_mem, g, weights, tm):
    m = x2d.shape[0]

    def row(n):
        return pl.BlockSpec((tm, n), lambda i: (i, 0))

    return pl.pallas_call(
        _merge_ffn_kernel,
        grid=(m // tm,),
        in_specs=[row(D_MODEL), row(MLA_HEADS * V_DIM), row(D_INNER), row(MEM_WIDTH), row(N_BRANCH * D_MODEL)]
                 + [_resident(w.shape) for w in weights],
        out_specs=row(D_MODEL),
        out_shape=jax.ShapeDtypeStruct((m, D_MODEL), F32),
        compiler_params=_cp(("arbitrary",), 56),
        name="merge_ffn",
    )(x2d, o_mla, o_ssm, o_mem, g, *weights)


def _rope_table(pos):
    half = ROPE_DIM // 2
    inv = ROPE_THETA ** (-jnp.arange(half, dtype=F32) / half)
    ang = pos.astype(F32)[:, None] * inv[None, :]
    cos, sin = jnp.cos(ang), jnp.sin(ang)
    return jnp.concatenate([cos, cos, cos, cos, -sin, sin, -sin, sin], axis=1)


def _expand_matrix(width, terms):
    src = np.arange(SSM_HEADS * width) // width
    one = (np.arange(128)[:, None] == src[None, :]).astype(np.float32)
    return jnp.asarray(np.concatenate([one] * terms, axis=0), dtype=BF16)


def _shift_matrix():
    sel = np.zeros(((CONV_W - 1) * CHUNK, CARRY + CHUNK), np.float32)
    for k in range(CONV_W - 1):
        for t in range(CHUNK):
            sel[k * CHUNK + t, CARRY + t + k - (CONV_W - 1)] = 1.0
    return jnp.asarray(sel, dtype=BF16)


def _short_taps_matrix(seq_len):
    n_seq = CHUNK // seq_len
    n_old = CONV_W - 1
    n_cached = n_old * n_seq
    sel = np.zeros((n_old * CHUNK + n_cached, CHUNK + 3 * n_cached), np.float32)

    def take(out_row, s, m):
        if m >= n_old:
            sel[out_row, s * seq_len + m - n_old] = 1.0
        else:
            for term in range(3):
                sel[out_row, CHUNK + term * n_cached + m * n_seq + s] = 1.0

    for s in range(n_seq):
        for k in range(n_old):
            for i in range(seq_len):
                take(k * CHUNK + s * seq_len + i, s, i + k)
        for j in range(n_old):
            take(n_old * CHUNK + j * n_seq + s, s, seq_len + j)
    return jnp.asarray(sel, dtype=BF16)


def _placement_matrix(seq_len):
    n_seq = CHUNK // seq_len
    place = np.zeros((n_seq * SLOTS, 3 * CHUNK), np.float32)
    for s in range(n_seq):
        for j in range(seq_len):
            place[s * SLOTS + j, s * seq_len + j] = 1.0
        place[s * SLOTS + seq_len, CHUNK + s * seq_len] = 1.0
        place[s * SLOTS + seq_len + 1, 2 * CHUNK + s * seq_len] = 1.0
    return jnp.asarray(place, dtype=BF16)


def _row(v):
    return v.reshape(1, -1).astype(F32)


def kernel(x_prompt, x_sample, mem_prompt, cache_ckv, cache_krope, page_table, cache_mem_k, cache_mem_v, state_ssm, state_conv, w_in, q_norm_g, w_uq, kv_norm_g, w_uk, w_uv, conv_w, conv_b, dt_bias, a_log, d_skip, ssm_norm_g, w_mem_k, w_mem_v, b_gate, w_o_mla, w_o_ssm, w_o_mem, w_out, ln1_g, ln1_b, w_up, b_up, w_down, b_down, ln2_g, ln2_b):
    bp, seq, _ = x_prompt.shape
    bs, t_new, _ = x_sample.shape
    n_pages = page_table.shape[1]
    past = n_pages * PAGE_SIZE
    mp, ms = bp * seq, bs * t_new
    assert seq % KV_TILE == 0 and seq % CHUNK == 0 and mp % TOKEN_TILE == 0 and TOKEN_TILE % KV_TILE == 0
    assert CHUNK % t_new == 0 and ms % CHUNK == 0 and t_new + 2 <= SLOTS and MLA_HEADS * t_new % 16 == 0
    assert bs % SAMPLE_ROWS == 0 and n_pages % 2 == 0 and MEM_HALVES == 2

    o_cq, o_ckv, o_kr, o_z, o_xbc, o_dt, o_mq, o_g = np.cumsum(
        [0, Q_LORA, KV_LORA, ROPE_DIM, D_INNER, CONV_DIM, SSM_HEADS, MEM_WIDTH]).tolist()
    w_kr = w_in[:, o_kr:o_z]
    w_small = jnp.concatenate([w_in[:, o_ckv:o_kr], w_kr, w_kr, w_in[:, o_dt:o_mq],
                               jnp.zeros((D_MODEL, 128 - SSM_HEADS), F32)], axis=1)
    w_proj = tuple(w.astype(BF16) for w in (w_in[:, o_cq:o_ckv], w_small, w_in[:, o_z:o_xbc], w_in[:, o_xbc:o_dt],
                                            w_in[:, o_mq:o_g], w_in[:, o_g:]))
    half = ROPE_DIM // 2
    w_q_nope = w_uq[:, :, :NOPE_DIM].reshape(Q_LORA, MLA_HEADS * NOPE_DIM)
    w_q_rope = w_uq[:, :, NOPE_DIM:]
    w_q_swap = jnp.concatenate([w_q_rope[:, :, half:], w_q_rope[:, :, :half]], axis=-1)
    pair_w = 2 * ROPE_DIM
    w_q_pairs = jnp.concatenate([w_q_rope.reshape(Q_LORA, MLA_HEADS // 2, pair_w),
                                 w_q_swap.reshape(Q_LORA, MLA_HEADS // 2, pair_w)], axis=-1)
    wq = jnp.concatenate([w_q_nope, w_q_pairs.reshape(Q_LORA, -1)], axis=1).astype(BF16)
    wuk_t = jnp.transpose(w_uk, (1, 2, 0)).astype(BF16)
    wuk_pairs = w_uk.reshape(KV_LORA, MLA_HEADS // 2, 2 * NOPE_DIM).transpose(1, 0, 2).astype(BF16)
    wuv = jnp.transpose(w_uv, (1, 0, 2)).astype(BF16)
    wuv_t = jnp.transpose(w_uv, (1, 2, 0)).astype(BF16)
    w_mem_kv = jnp.concatenate([w_mem_k.reshape(D_MODEL, MEM_WIDTH), w_mem_v.reshape(D_MODEL, MEM_WIDTH)], axis=1).astype(BF16)
    merge_w = (w_o_mla.astype(BF16), w_o_ssm.astype(BF16), w_o_mem.astype(BF16), w_out.astype(BF16),
               _row(ln1_g), _row(ln1_b), w_up.astype(BF16), _row(b_up), w_down.astype(BF16), _row(b_down),
               _row(ln2_g), _row(ln2_b))
    gq, gkv, bg = _row(q_norm_g), _row(kv_norm_g), _row(b_gate)

    pad_heads = jnp.zeros((128 - SSM_HEADS,), F32)
    ssd_head = (conv_w.astype(F32), _row(conv_b),
                _row(jnp.concatenate([dt_bias, pad_heads])),
                jnp.broadcast_to(dt_bias.astype(F32)[:, None], (SSM_HEADS, CHUNK)),
                _row(jnp.concatenate([a_log, pad_heads])),
                jnp.broadcast_to(a_log.astype(F32)[:, None], (SSM_HEADS, CHUNK)),
                _row(jnp.repeat(d_skip, SSM_HEAD_DIM)))
    norm_g = _row(ssm_norm_g)
    expanders = (_expand_matrix(SSM_HEAD_DIM, 2), _expand_matrix(128, 3))

    n_chunks = seq // CHUNK
    cs_p = _rope_table(jnp.arange(seq))
    xp2d = x_prompt.reshape(mp, D_MODEL)
    mem_k_p, mem_v_p = _mem_kv(mem_prompt.reshape(bp * N_MEM, D_MODEL), w_mem_kv, tm=min(MEM_ROWS_TILE, bp * N_MEM))
    _, ckv_p, kr_p, _, dt_p, dtt_p, z_p, xbc_p, o_mem_p, g_p, k_heads, vt_heads, qt_p = _in_proj(
        xp2d, w_proj, cs_p, gq, gkv, bg, tm=TOKEN_TILE,
        head_kv_weights=(wuk_pairs, wuv_t.reshape(MLA_HEADS // 2, 2 * V_DIM, KV_LORA), wq),
        memory=(mem_k_p, mem_v_p))
    o_mla_p = _prompt_attn(qt_p, k_heads, vt_heads, bp, seq)

    o_ssm_p, ssm_p, conv_p = _ssd(xbc_p, z_p, dt_p, dtt_p, ssd_head + (norm_g,) + expanders + (_shift_matrix(),),
                                  batch=bp, n_chunks=n_chunks)

    y_p = _merge_ffn(xp2d, o_mla_p, o_ssm_p, o_mem_p, g_p, merge_w, tm=TOKEN_TILE)

    cs_s = jnp.tile(_rope_table(past + jnp.arange(t_new)), (bs, 1))
    xs2d = x_sample.reshape(ms, D_MODEL)
    tm_s = min(TOKEN_TILE, ms)
    cqn, ckv_s, kr_s, kcat, dt_s, dtt_s, z_s, xbc_s, mq_s, g_s = _in_proj(xs2d, w_proj, cs_s, gq, gkv, bg, tm=tm_s)
    q_s = _q_prep(cqn, wq, wuk_t, cs_s, tm=tm_s)
    q_s = jnp.transpose(q_s.reshape(MLA_HEADS, bs, t_new, QK_DIM), (1, 0, 2, 3)).reshape(bs, MLA_HEADS * t_new, QK_DIM)
    o_lat = _decode_attn(page_table, q_s, kcat.reshape(bs, t_new, QK_DIM).astype(F32), cache_ckv,
                         jnp.swapaxes(cache_krope, 1, 2))
    o_lat = jnp.transpose(o_lat.reshape(bs, MLA_HEADS, t_new, KV_LORA), (1, 0, 2, 3)).reshape(MLA_HEADS, ms, KV_LORA)
    o_mla_s = _uv_proj(o_lat, wuv)

    ypre_s, ec_s, xw_s, c_s, bw_s, conv_rows = _ssd_short(
        xbc_s, jnp.transpose(state_conv.astype(F32), (1, 0, 2)), dt_s, dtt_s, ssd_head + expanders,
        _short_taps_matrix(t_new), _placement_matrix(t_new), seq_len=t_new)
    conv_s = jnp.transpose(conv_rows, (1, 0, 2))
    o_ssm_s, ssm_s = _ssd_state(state_ssm.astype(F32).reshape(bs, D_INNER, D_STATE), xw_s, c_s, bw_s, ypre_s, ec_s,
                                z_s.astype(F32), norm_g, nb=SAMPLE_ROWS, seq_len=t_new)

    def cache_rows(c):
        c = c.reshape(bs, N_MEM, MEM_HEADS, MEM_HALVES, 128)
        return jnp.transpose(c, (0, 1, 3, 2, 4)).reshape(bs, N_MEM * MEM_ROW_GROUP, 128)

    mq_rows = jnp.transpose(mq_s.reshape(bs, t_new, MEM_HEADS, MEM_HALVES, 128), (0, 3, 2, 1, 4))
    o_mem_s = _mem_attn_cache(mq_rows.reshape(bs, MEM_ROW_GROUP * t_new, 128), cache_rows(cache_mem_k),
                              cache_rows(cache_mem_v), nb=SAMPLE_ROWS, n_tok=t_new)
    o_mem_s = jnp.transpose(o_mem_s.reshape(bs, MEM_HALVES, MEM_HEADS, t_new, 128), (0, 3, 2, 1, 4))
    y_s = _merge_ffn(xs2d, o_mla_s, o_ssm_s.reshape(ms, D_INNER).astype(BF16), o_mem_s.reshape(ms, MEM_WIDTH), g_s,
                     merge_w, tm=tm_s)

    return (y_p.reshape(bp, seq, D_MODEL), y_s.reshape(bs, t_new, D_MODEL),
            ckv_p.reshape(bp, seq, KV_LORA), kr_p.reshape(bp, seq, ROPE_DIM),
            mem_k_p.reshape(bp, N_MEM, MEM_HEADS, MEM_HEAD_DIM), mem_v_p.reshape(bp, N_MEM, MEM_HEADS, MEM_HEAD_DIM),
            ssm_p.reshape(bp, SSM_HEADS, SSM_HEAD_DIM, D_STATE), conv_p,
            ckv_s.reshape(bs, t_new, KV_LORA), kr_s.reshape(bs, t_new, ROPE_DIM),
            ssm_s.reshape(bs, SSM_HEADS, SSM_HEAD_DIM, D_STATE), conv_s)
```

```python
import functools

import jax
import jax.numpy as jnp
import numpy as np
from jax import lax
from jax.experimental import pallas as pl
from jax.experimental.pallas import tpu as pltpu

F32 = jnp.float32
BF16 = jnp.bfloat16

D_MODEL = 1024
MLA_HEADS = 8
Q_LORA = 384
KV_LORA = 256
NOPE_DIM = 128
ROPE_DIM = 64
V_DIM = 128
ROPE_THETA = 10000.0
QK_DIM = KV_LORA + ROPE_DIM
HEAD_QK = 256
SSM_HEADS = 32
SSM_HEAD_DIM = 64
D_INNER = SSM_HEADS * SSM_HEAD_DIM
SSM_GROUPS = 4
GROUP_W = D_INNER // SSM_GROUPS
D_STATE = 128
CONV_W = 4
CONV_DIM = D_INNER + 2 * SSM_GROUPS * D_STATE
CHUNK = 128
N_MEM = 256
MEM_HEADS = 4
MEM_HEAD_DIM = 256
MEM_WIDTH = MEM_HEADS * MEM_HEAD_DIM
D_FF = 4 * D_MODEL
N_BRANCH = 3
DEPTH = 1
ALPHA = (2 * DEPTH) ** 0.25
LN_EPS = 1e-5
RMS_EPS = 1e-6
PAGE_SIZE = 128
SCORE_SCALE = (NOPE_DIM + ROPE_DIM) ** -0.5
NEG_BIG = -1e30
KV_TILE = 256

TOKEN_TILE = 256
MEM_ROWS_TILE = 512
SAMPLE_ROWS = 4


def _cp(sem, vmem_mb):
    return pltpu.CompilerParams(dimension_semantics=sem, vmem_limit_bytes=vmem_mb << 20)


def _resident(shape):
    nd = len(shape)
    return pl.BlockSpec(shape, lambda *_: (0,) * nd, pipeline_mode=pl.Buffered(1))


def _dot(a, b):
    return jnp.dot(a, b, preferred_element_type=F32)


def _dot_nt(a, b):
    return lax.dot_general(a, b, (((1,), (1,)), ((), ())), preferred_element_type=F32)


def _rms(v, g, eps):
    return v * lax.rsqrt(jnp.mean(v * v, axis=-1, keepdims=True) + eps) * g


def _layer_norm(v, g, b):
    mu = jnp.mean(v, axis=-1, keepdims=True)
    d = v - mu
    var = jnp.mean(d * d, axis=-1, keepdims=True)
    return d * lax.rsqrt(var + LN_EPS) * g + b


def _mem_attend(q, k_ref, v_ref, o_ref):
    scale = MEM_HEAD_DIM ** -0.5
    heads = [slice(h * MEM_HEAD_DIM, (h + 1) * MEM_HEAD_DIM) for h in range(MEM_HEADS)]
    scores = [_dot_nt(q[:, hs], k_ref[:, hs].astype(BF16)) for hs in heads]
    for h, hs in enumerate(heads):
        s = scores[h] * scale
        p = jnp.exp(s - jnp.max(s, axis=1, keepdims=True))
        l = jnp.sum(p, axis=1, keepdims=True)
        o_ref[:, hs] = (_dot(p.astype(BF16), v_ref[:, hs].astype(BF16)) / l).astype(BF16)


def _in_proj_kernel(x_ref, wcq_ref, wsmall_ref, wz_ref, wxbc_ref, wmq_ref, wg_ref, cs_ref, gq_ref, gkv_ref, bg_ref,
                    *rest, per_head_kv):
    if per_head_kv:
        wuk_ref, wuvt_ref, wq_ref, memk_ref, memv_ref = rest[:5]
        rest = rest[5:]
    cqn_ref, ckv_ref, kr_ref, kcat_ref, dt_ref, dtt_ref, z_ref, xbc_ref, mq_ref, g_ref = rest[:10]
    xb = x_ref[...].astype(BF16)
    cqn = _rms(_dot(xb, wcq_ref[...]), gq_ref[...], RMS_EPS).astype(BF16)
    cqn_ref[...] = cqn

    small = _dot(xb, wsmall_ref[...])
    ckv = _rms(small[:, :KV_LORA], gkv_ref[...], RMS_EPS)
    ckv_ref[...] = ckv
    ckv_b = ckv.astype(BF16)
    kcat_ref[:, 0:KV_LORA] = ckv_b
    a = small[:, 256:384]
    b = pltpu.roll(a, ROPE_DIM // 2, 1)
    cs = cs_ref[...]
    ro = a * cs[:, :128] + b * cs[:, 128:]
    kr_ref[...] = ro[:, :ROPE_DIM]
    kcat_ref[:, KV_LORA:QK_DIM] = ro[:, :ROPE_DIM].astype(BF16)
    if per_head_kv:
        kh_ref, vt_ref, qt_ref = rest[10:]
        _write_queries(cqn, cs, wq_ref, None, qt_ref, absorb=False)
        lane = lax.broadcasted_iota(jnp.int32, ro.shape, 1)
        rope_pad = jnp.where(lane < ROPE_DIM, ro, 0.0).astype(BF16)
        ckv_t = [ckv[t * KV_TILE:(t + 1) * KV_TILE, :].T.astype(BF16) for t in range(vt_ref.shape[1])]
        for p in range(MLA_HEADS // 2):
            kk = _dot(ckv_b, wuk_ref[p]).astype(BF16)
            for i in range(2):
                kh_ref[2 * p + i, :, 0:NOPE_DIM] = kk[:, i * NOPE_DIM:(i + 1) * NOPE_DIM]
                kh_ref[2 * p + i, :, NOPE_DIM:HEAD_QK] = rope_pad
            for t, ct in enumerate(ckv_t):
                vv = _dot(wuvt_ref[p], ct).astype(BF16)
                vt_ref[2 * p, t] = vv[:V_DIM]
                vt_ref[2 * p + 1, t] = vv[V_DIM:]
    dt = small[:, 384:512]
    dt_ref[...] = dt
    dtt_ref[...] = dt.T[0:SSM_HEADS, :]

    for c in range(0, D_INNER, 1024):
        z_ref[:, c:c + 1024] = _dot(xb, wz_ref[:, c:c + 1024]).astype(BF16)
    for c in range(0, CONV_DIM, 1024):
        xbc_ref[:, c:c + 1024] = _dot(xb, wxbc_ref[:, c:c + 1024]).astype(BF16)
    mq = _dot(xb, wmq_ref[...]).astype(BF16)
    if per_head_kv:
        _mem_attend(mq, memk_ref, memv_ref, mq_ref)
    else:
        mq_ref[...] = mq
    for c in range(0, N_BRANCH * D_MODEL, 1024):
        gr = _dot(xb, wg_ref[:, c:c + 1024]) + bg_ref[:, c:c + 1024]
        g_ref[:, c:c + 1024] = jax.nn.sigmoid(gr).astype(BF16)


def _in_proj(x2d, weights, cs_tab, gq, gkv, bg, tm, head_kv_weights=(), memory=()):
    m = x2d.shape[0]
    ncs = cs_tab.shape[0] // tm
    steps_per_batch = (m // tm) // (memory[0].shape[0] // N_MEM) if memory else 0

    def row(n, dtype):
        return pl.BlockSpec((tm, n), lambda i: (i, 0)), jax.ShapeDtypeStruct((m, n), dtype)

    outs = [row(Q_LORA, BF16), row(KV_LORA, F32), row(ROPE_DIM, F32), row(QK_DIM, BF16), row(128, F32),
            (pl.BlockSpec((SSM_HEADS, tm), lambda i: (0, i)), jax.ShapeDtypeStruct((SSM_HEADS, m), F32)),
            row(D_INNER, BF16), row(CONV_DIM, BF16), row(MEM_WIDTH, BF16), row(N_BRANCH * D_MODEL, BF16)]
    if head_kv_weights:
        outs += [(pl.BlockSpec((MLA_HEADS, tm, HEAD_QK), lambda i: (0, i, 0)),
                  jax.ShapeDtypeStruct((MLA_HEADS, m, HEAD_QK), BF16)),
                 (pl.BlockSpec((MLA_HEADS, tm // KV_TILE, V_DIM, KV_TILE), lambda i: (0, i, 0, 0)),
                  jax.ShapeDtypeStruct((MLA_HEADS, m // KV_TILE, V_DIM, KV_TILE), BF16)),
                 (pl.BlockSpec((MLA_HEADS, HEAD_QK, tm), lambda i: (0, 0, i)),
                  jax.ShapeDtypeStruct((MLA_HEADS, HEAD_QK, m), BF16))]
    return pl.pallas_call(
        functools.partial(_in_proj_kernel, per_head_kv=bool(head_kv_weights)),
        grid=(m // tm,),
        in_specs=[pl.BlockSpec((tm, D_MODEL), lambda i: (i, 0))] + [_resident(w.shape) for w in weights]
                 + [pl.BlockSpec((tm, 256), lambda i: (i % ncs, 0)),
                    _resident(gq.shape), _resident(gkv.shape), _resident(bg.shape)]
                 + [_resident(w.shape) for w in head_kv_weights]
                 + [pl.BlockSpec((N_MEM, MEM_WIDTH), lambda i: (i // steps_per_batch, 0)) for _ in memory],
        out_specs=[o[0] for o in outs],
        out_shape=[o[1] for o in outs],
        compiler_params=_cp(("arbitrary",), 56),
        name="in_proj",
    )(x2d, *weights, cs_tab, gq, gkv, bg, *head_kv_weights, *memory)


def _write_queries(c, cs, wq_ref, wuk_ref, q_ref, absorb):
    qn = [_dot(c, wq_ref[:, h * NOPE_DIM:(h + 1) * NOPE_DIM]) for h in range(MLA_HEADS)]
    for h in range(MLA_HEADS):
        if absorb:
            q_ref[h, :, 0:KV_LORA] = (_dot(qn[h].astype(BF16), wuk_ref[h]) * SCORE_SCALE).astype(BF16)
        else:
            q_ref[h, 0:NOPE_DIM, :] = (qn[h] * SCORE_SCALE).T.astype(BF16)
            q_ref[h, NOPE_DIM + ROPE_DIM:HEAD_QK, :] = jnp.zeros((HEAD_QK - NOPE_DIM - ROPE_DIM, c.shape[0]), BF16)
    r0 = MLA_HEADS * NOPE_DIM
    for p in range(MLA_HEADS // 2):
        ab = _dot(c, wq_ref[:, r0 + p * 256:r0 + (p + 1) * 256])
        ro = (ab[:, :128] * cs[:, :128] + ab[:, 128:] * cs[:, 128:]) * SCORE_SCALE
        if absorb:
            q_ref[2 * p, :, KV_LORA:QK_DIM] = ro[:, :ROPE_DIM].astype(BF16)
            q_ref[2 * p + 1, :, KV_LORA:QK_DIM] = pltpu.roll(ro, ROPE_DIM, 1)[:, :ROPE_DIM].astype(BF16)
        else:
            rot = ro.T.astype(BF16)
            q_ref[2 * p, NOPE_DIM:NOPE_DIM + ROPE_DIM, :] = rot[:ROPE_DIM]
            q_ref[2 * p + 1, NOPE_DIM:NOPE_DIM + ROPE_DIM, :] = rot[ROPE_DIM:]


def _q_prep_kernel(c_ref, wq_ref, wuk_ref, cs_ref, q_ref):
    _write_queries(c_ref[...], cs_ref[...], wq_ref, wuk_ref, q_ref, absorb=True)


def _q_prep(cqn, wq, wuk_t, cs_tab, tm):
    m = cqn.shape[0]
    ncs = cs_tab.shape[0] // tm
    return pl.pallas_call(
        _q_prep_kernel,
        grid=(m // tm,),
        in_specs=[pl.BlockSpec((tm, Q_LORA), lambda i: (i, 0)), _resident(wq.shape), _resident(wuk_t.shape),
                  pl.BlockSpec((tm, 256), lambda i: (i % ncs, 0))],
        out_specs=pl.BlockSpec((MLA_HEADS, tm, QK_DIM), lambda i: (0, i, 0)),
        out_shape=jax.ShapeDtypeStruct((MLA_HEADS, m, QK_DIM), BF16),
        compiler_params=_cp(("arbitrary",), 32),
        name="q_prep",
    )(cqn, wq, wuk_t, cs_tab)


def _prompt_attn_kernel(qt_ref, k_ref, vt_ref, o_ref, m_sc, l_sc, acc_sc):
    i = pl.program_id(1)
    t = KV_TILE
    m_sc[...] = jnp.full(m_sc.shape, NEG_BIG, F32)
    l_sc[...] = jnp.zeros(l_sc.shape, F32)
    acc_sc[...] = jnp.zeros(acc_sc.shape, F32)

    def scores(j):
        rows = pl.ds(pl.multiple_of(j * t, t), t)
        return [_dot(k_ref[h, rows, :], qt_ref[h]) for h in range(MLA_HEADS)]

    def kv_block(j, masked, pending):
        if masked:
            keep = lax.broadcasted_iota(jnp.int32, (t, t), 0) <= lax.broadcasted_iota(jnp.int32, (t, t), 1)
        for h in range(MLA_HEADS):
            st = pending[h]
            if masked:
                st = jnp.where(keep, st, NEG_BIG)
            m_prev = m_sc[h:h + 1, :]
            m_new = jnp.maximum(m_prev, jnp.max(st, axis=0, keepdims=True))
            alpha = jnp.exp(m_prev - m_new)
            p = jnp.exp(st - m_new)
            l_sc[h:h + 1, :] = alpha * l_sc[h:h + 1, :] + jnp.sum(p, axis=0, keepdims=True)
            m_sc[h:h + 1, :] = m_new
            acc_sc[h] = alpha * acc_sc[h] + _dot(vt_ref[h, j], p.astype(BF16))

    def two_tiles(jj, carry):
        first, second = scores(2 * jj), scores(2 * jj + 1)
        kv_block(2 * jj, False, first)
        kv_block(2 * jj + 1, False, second)
        return carry

    lax.fori_loop(0, i // 2, two_tiles, 0)

    @pl.when(i % 2 == 1)
    def _():
        kv_block(i - 1, False, scores(i - 1))

    kv_block(i, True, scores(i))

    for h in range(MLA_HEADS):
        o_ref[:, h * V_DIM:(h + 1) * V_DIM] = (acc_sc[h] / l_sc[h:h + 1, :]).T.astype(BF16)


def _prompt_attn(qt, k_heads, vt_heads, batch, seq):
    t = KV_TILE
    nq = seq // t
    return pl.pallas_call(
        _prompt_attn_kernel,
        grid=(batch, nq),
        in_specs=[pl.BlockSpec((MLA_HEADS, HEAD_QK, t), lambda b, i: (0, 0, b * nq + i)),
                  pl.BlockSpec((MLA_HEADS, seq, HEAD_QK), lambda b, i: (0, b, 0)),
                  pl.BlockSpec((MLA_HEADS, nq, V_DIM, t), lambda b, i: (0, b, 0, 0))],
        out_specs=pl.BlockSpec((t, MLA_HEADS * V_DIM), lambda b, i: (b * nq + i, 0)),
        out_shape=jax.ShapeDtypeStruct((batch * seq, MLA_HEADS * V_DIM), BF16),
        scratch_shapes=[pltpu.VMEM((MLA_HEADS, t), F32), pltpu.VMEM((MLA_HEADS, t), F32),
                        pltpu.VMEM((MLA_HEADS, V_DIM, t), F32)],
        compiler_params=_cp(("arbitrary", "arbitrary"), 48),
        name="prompt_attn",
    )(qt, k_heads, vt_heads)


def _decode_attn_kernel(pt_ref, q_ref, kn_ref, ckv_hbm, krt_hbm, o_ref, kbuf, rbuf, kb_sc, s_sc, sem, *, n_pages, n_new):
    b = pl.program_id(0)
    nb = pl.num_programs(0)
    slot = b % 2

    def page_copies(page, p, sl):
        rows = pl.ds(pl.multiple_of(p * PAGE_SIZE, PAGE_SIZE), PAGE_SIZE)
        return (pltpu.make_async_copy(ckv_hbm.at[page], kbuf.at[sl, rows, :], sem.at[0, sl]),
                pltpu.make_async_copy(krt_hbm.at[page], rbuf.at[sl, p], sem.at[1, sl]))

    def issue(bi, sl):
        def body(p, carry):
            for cp in page_copies(pt_ref[bi * n_pages + p], p, sl):
                cp.start()
            return carry
        lax.fori_loop(0, n_pages, body, 0, unroll=8)

    @pl.when(b == 0)
    def _():
        issue(0, 0)

    @pl.when(b + 1 < nb)
    def _():
        issue(b + 1, 1 - slot)

    def wait_body(p, carry):
        for cp in page_copies(0, p, slot):
            cp.wait()
        return carry
    lax.fori_loop(0, n_pages, wait_body, 0, unroll=8)

    q = q_ref[0]
    rows = q.shape[0]
    qc = q[:, :KV_LORA]
    qr = q[:, KV_LORA:]
    qf = q.astype(F32)
    kn = kn_ref[0]
    tok = lax.broadcasted_iota(jnp.int32, (rows, 1), 0) % n_new

    s_new = []
    for j in range(n_new):
        sj = jnp.sum(qf * kn[j:j + 1, :], axis=1, keepdims=True)
        s_new.append(jnp.where(tok >= j, sj, NEG_BIG))
    m0 = s_new[0]
    for j in range(1, n_new):
        m0 = jnp.maximum(m0, s_new[j])

    for c in range(n_pages // 2):
        r = slice(2 * c * PAGE_SIZE, 2 * (c + 1) * PAGE_SIZE)
        kc = kbuf[slot, r, :].astype(BF16)
        kb_sc[r, :] = kc
        rt = jnp.concatenate([rbuf[slot, 2 * c], rbuf[slot, 2 * c + 1]], axis=1).astype(BF16)
        s_sc[:, r] = _dot_nt(qc, kc) + _dot(qr, rt)
    s = s_sc[...]
    m = jnp.maximum(m0, jnp.max(s, axis=1, keepdims=True))
    p = jnp.exp(s - m)
    l = jnp.sum(p, axis=1, keepdims=True)
    acc = _dot(p.astype(BF16), kb_sc[...])
    for j in range(n_new):
        pj = jnp.exp(s_new[j] - m)
        l = l + pj
        acc = acc + pj * kn[j:j + 1, :KV_LORA]
    o_ref[0] = (acc / l).astype(BF16)


def _decode_attn(page_table, q, k_new, cache_ckv, cache_krope_t):
    nb, n_pages = page_table.shape
    rows = q.shape[1]
    n_new = k_new.shape[1]
    past = n_pages * PAGE_SIZE
    kern = functools.partial(_decode_attn_kernel, n_pages=n_pages, n_new=n_new)
    grid_spec = pltpu.PrefetchScalarGridSpec(
        num_scalar_prefetch=1,
        grid=(nb,),
        in_specs=[pl.BlockSpec((1, rows, QK_DIM), lambda b, pt: (b, 0, 0)),
                  pl.BlockSpec((1, n_new, QK_DIM), lambda b, pt: (b, 0, 0)),
                  pl.BlockSpec(memory_space=pl.ANY),
                  pl.BlockSpec(memory_space=pl.ANY)],
        out_specs=pl.BlockSpec((1, rows, KV_LORA), lambda b, pt: (b, 0, 0)),
        scratch_shapes=[pltpu.VMEM((2, past, KV_LORA), F32), pltpu.VMEM((2, n_pages, ROPE_DIM, PAGE_SIZE), F32),
                        pltpu.VMEM((past, KV_LORA), BF16), pltpu.VMEM((rows, past), F32),
                        pltpu.SemaphoreType.DMA((2, 2))],
    )
    return pl.pallas_call(
        kern,
        grid_spec=grid_spec,
        out_shape=jax.ShapeDtypeStruct((nb, rows, KV_LORA), BF16),
        compiler_params=_cp(("arbitrary",), 40),
        name="decode_attn",
    )(page_table.reshape(-1), q, k_new, cache_ckv, cache_krope_t)


def _uv_proj_kernel(o_ref, w_ref, out_ref):
    out_ref[...] = _dot(o_ref[0], w_ref[0]).astype(BF16)


def _uv_proj(o_lat, wuv):
    m = o_lat.shape[1]
    return pl.pallas_call(
        _uv_proj_kernel,
        grid=(MLA_HEADS,),
        in_specs=[pl.BlockSpec((1, m, KV_LORA), lambda h: (h, 0, 0)),
                  pl.BlockSpec((1, KV_LORA, V_DIM), lambda h: (h, 0, 0))],
        out_specs=pl.BlockSpec((m, V_DIM), lambda h: (0, h)),
        out_shape=jax.ShapeDtypeStruct((m, MLA_HEADS * V_DIM), BF16),
        compiler_params=_cp(("arbitrary",), 16),
        name="uv_proj",
    )(o_lat, wuv)


def _split_dot(v, e, terms, left=True):
    parts = []
    r = v
    for _ in range(terms):
        hi = r.astype(BF16)
        parts.append(hi)
        r = r - hi.astype(F32)
    if left:
        return _dot(jnp.concatenate(parts, axis=1), e)
    return _dot(e, jnp.concatenate(parts, axis=0))


def _conv_silu(taps, cw_ref, cb_ref):
    conv = cb_ref[...] + cw_ref[0:1, :] * taps[0]
    for k in range(1, CONV_W):
        conv = conv + cw_ref[k:k + 1, :] * taps[k]
    xc = conv * jax.nn.sigmoid(conv)
    n_bc = SSM_GROUPS * D_STATE
    return xc[:, :D_INNER], xc[:, D_INNER:D_INNER + n_bc], xc[:, D_INNER + n_bc:]


def _select_rows(select, rows_ref):
    n = rows_ref.shape[1]
    return jnp.concatenate([_dot(select, rows_ref[:, c:c + 1024]) for c in range(0, n, 1024)], axis=1)


def _ssd_sums(dt, dtt, keep, keep_t, alog_ref, alogt_ref):
    tri = jnp.where(keep, 1.0, 0.0).astype(BF16)
    tri_t = jnp.where(keep_t, 1.0, 0.0).astype(BF16)
    da = dt * (-jnp.exp(alog_ref[...]))
    dat = dtt * (-jnp.exp(alogt_ref[...]))
    cum = _split_dot(da, jnp.concatenate([tri] * 3, axis=1), 3, left=False)
    cum_t = _split_dot(dat, jnp.concatenate([tri_t] * 3, axis=0), 3)
    return da, cum, cum_t


def _ssd_intra(xs, bm, cm, cum_e, cum_t, dtt, keep, dskip_ref, y_sc, carried=None):
    pairs_per_group = SSM_HEADS // 2 // SSM_GROUPS
    lane = lax.broadcasted_iota(jnp.int32, (CHUNK, CHUNK), 1)
    for g in range(SSM_GROUPS):
        cg = cm[:, g * D_STATE:(g + 1) * D_STATE].astype(BF16)
        cb = _dot_nt(cg, bm[:, g * D_STATE:(g + 1) * D_STATE].astype(BF16))
        extra = None if carried is None else carried(g, cg)
        for pr in range(pairs_per_group):
            q = g * pairs_per_group + pr
            ws = []
            for h in (2 * q, 2 * q + 1):
                seg = cum_e[:, h * 128:(h + 1) * 128] - cum_t[h:h + 1, :]
                decay = jnp.exp(jnp.where(keep, seg, NEG_BIG)) * dtt[h:h + 1, :]
                ws.append((cb * decay).astype(BF16))
            w_pair = jnp.concatenate(ws, axis=1)
            xp = xs[:, q * 128:(q + 1) * 128]
            x_bd = jnp.concatenate([jnp.where(lane < SSM_HEAD_DIM, xp, 0.0).astype(BF16),
                                    jnp.where(lane >= SSM_HEAD_DIM, xp, 0.0).astype(BF16)], axis=0)
            y = _dot(w_pair, x_bd) + dskip_ref[:, q * 128:(q + 1) * 128] * xp
            if extra is not None:
                y = y + extra[:, pr * 128:(pr + 1) * 128]
            y_sc[:, q * 128:(q + 1) * 128] = y


CHUNKS_PER_STEP = 2
CARRY = 16


def _ssd_kernel(xbc_ref, z_ref, dt_ref, dtt_ref, cw_ref, cb_ref, dtb_ref, dtbt_ref,
                alog_ref, alogt_ref, dskip_ref, ng_ref, e64_ref, e128_ref, shift_ref,
                y_ref, sout_ref, cout_ref, xe_sc, tail_sc, st_sc, y_sc):
    c = pl.program_id(1)
    n_pairs = SSM_HEADS // 2
    pairs_per_group = n_pairs // SSM_GROUPS
    rows_per_step = CHUNKS_PER_STEP * CHUNK

    @pl.when(c == 0)
    def _():
        xe_sc[0:CARRY, :] = jnp.zeros((CARRY, CONV_DIM), BF16)
        st_sc[...] = jnp.zeros(st_sc.shape, F32)

    xe_sc[CARRY:CARRY + rows_per_step, :] = xbc_ref[...]
    rowi = lax.broadcasted_iota(jnp.int32, (CHUNK, CHUNK), 0)
    coli = lax.broadcasted_iota(jnp.int32, (CHUNK, CHUNK), 1)
    keep = coli <= rowi
    e64 = e64_ref[...]

    def prepare(k):
        r = slice(k * CHUNK, (k + 1) * CHUNK)
        shifted = _select_rows(shift_ref[...], xe_sc.at[pl.ds(k * CHUNK, CARRY + CHUNK)])
        taps = [shifted[j * CHUNK:(j + 1) * CHUNK] for j in range(CONV_W - 1)] + [xbc_ref[r, :].astype(F32)]
        xs, bm, cm = _conv_silu(taps, cw_ref, cb_ref)
        dt = jax.nn.softplus(dt_ref[r, :] + dtb_ref[...])
        dtt = jax.nn.softplus(dtt_ref[:, r] + dtbt_ref[...])
        _, cum, cum_t = _ssd_sums(dt, dtt, keep, rowi <= coli, alog_ref, alogt_ref)
        last = cum[CHUNK - 1:CHUNK, :]
        ec_e = _split_dot(jnp.exp(cum), e64, 2)
        wb_e = _split_dot(dt * jnp.exp(last - cum), e64, 2)
        dec_e = _split_dot(jnp.broadcast_to(jnp.exp(last), (16, CHUNK)), e64, 2)[0:1, :]
        cum_e = _split_dot(cum, e128_ref[...], 3)
        return xs, bm, cm, dtt, cum_t, ec_e, wb_e, dec_e, cum_e

    prepared = [prepare(k) for k in range(CHUNKS_PER_STEP)]
    xe_sc[0:CARRY, :] = xe_sc[rows_per_step:rows_per_step + CARRY, :]

    for k, (xs, bm, cm, dtt, cum_t, ec_e, wb_e, dec_e, cum_e) in enumerate(prepared):
        def carried(g, cg, ec_e=ec_e):
            return _dot(cg, st_sc[g].astype(BF16)) * ec_e[:, g * GROUP_W:(g + 1) * GROUP_W]

        _ssd_intra(xs, bm, cm, cum_e, cum_t, dtt, keep, dskip_ref, y_sc.at[pl.ds(k * CHUNK, CHUNK)], carried)
        for g in range(SSM_GROUPS):
            gs = slice(g * GROUP_W, (g + 1) * GROUP_W)
            xw = (xs[:, gs] * wb_e[:, gs]).astype(BF16)
            bt = bm[:, g * D_STATE:(g + 1) * D_STATE].T.astype(BF16)
            st_sc[g] = st_sc[g] * dec_e[:, gs] + _dot(bt, xw)

    for g in range(SSM_GROUPS):
        gs = slice(g * GROUP_W, (g + 1) * GROUP_W)
        zf = z_ref[:, gs].astype(F32)
        yv = y_sc[:, gs] * (zf * jax.nn.sigmoid(zf))
        y_ref[:, gs] = _rms(yv, ng_ref[:, gs], RMS_EPS).astype(y_ref.dtype)

    @pl.when(c == pl.num_programs(1) - 1)
    def _():
        tail_sc[...] = xbc_ref[rows_per_step - CARRY:rows_per_step, :].astype(F32)
        cout_ref[0] = tail_sc[CARRY - (CONV_W - 1):CARRY, :]
        for q in range(n_pairs):
            g, pr = divmod(q, pairs_per_group)
            sout_ref[0, q * 128:(q + 1) * 128, :] = st_sc[g, :, pr * 128:(pr + 1) * 128].T


def _ssd(xbc, z, dt, dtt, consts, batch, n_chunks):
    n_steps = n_chunks // CHUNKS_PER_STEP
    rows = CHUNKS_PER_STEP * CHUNK

    def step(n):
        return pl.BlockSpec((rows, n), lambda b, c: (b * n_steps + c, 0))

    def per_batch(r, n):
        return pl.BlockSpec((1, r, n), lambda b, c: (b, 0, 0))

    return pl.pallas_call(
        _ssd_kernel,
        grid=(batch, n_steps),
        in_specs=[step(CONV_DIM), step(D_INNER), step(128),
                  pl.BlockSpec((SSM_HEADS, rows), lambda b, c: (0, b * n_steps + c))]
                 + [_resident(a.shape) for a in consts],
        out_specs=[step(D_INNER), per_batch(D_INNER, D_STATE), per_batch(CONV_W - 1, CONV_DIM)],
        out_shape=[jax.ShapeDtypeStruct((batch * n_chunks * CHUNK, D_INNER), BF16),
                   jax.ShapeDtypeStruct((batch, D_INNER, D_STATE), F32),
                   jax.ShapeDtypeStruct((batch, CONV_W - 1, CONV_DIM), F32)],
        scratch_shapes=[pltpu.VMEM((CARRY + rows, CONV_DIM), BF16),
                        pltpu.VMEM((CARRY, CONV_DIM), F32),
                        pltpu.VMEM((SSM_GROUPS, D_STATE, GROUP_W), F32),
                        pltpu.VMEM((rows, D_INNER), F32)],
        compiler_params=_cp(("arbitrary", "arbitrary"), 56),
        name="ssd",
    )(xbc, z, dt, dtt, *consts)


SLOTS = 16


def _ssd_short_kernel(x_ref, sc_ref, dt_ref, dtt_ref, cw_ref, cb_ref, dtb_ref, dtbt_ref, alog_ref, alogt_ref,
                      dskip_ref, e64_ref, e128_ref, taps_sel_ref, place_ref,
                      ypre_ref, ec_ref, xw_ref, c_ref, bw_ref, tails_ref, rows_sc, y_sc, *, seq_len):
    n_cached = sc_ref.shape[0] * sc_ref.shape[1]
    rows_sc[0:CHUNK, :] = x_ref[...]
    r = sc_ref[...].reshape(n_cached, CONV_DIM)
    for t in range(3):
        hi = r.astype(BF16)
        rows_sc[CHUNK + t * n_cached:CHUNK + (t + 1) * n_cached, :] = hi
        r = r - hi.astype(F32)
    picked = _select_rows(taps_sel_ref[...], rows_sc)
    taps = [picked[k * CHUNK:(k + 1) * CHUNK] for k in range(CONV_W - 1)] + [x_ref[...].astype(F32)]
    tails_ref[...] = picked[(CONV_W - 1) * CHUNK:].reshape(tails_ref.shape)
    xs, bm, cm = _conv_silu(taps, cw_ref, cb_ref)
    rowi = lax.broadcasted_iota(jnp.int32, (CHUNK, CHUNK), 0)
    coli = lax.broadcasted_iota(jnp.int32, (CHUNK, CHUNK), 1)
    same = (rowi // seq_len) == (coli // seq_len)
    keep = same & (coli <= rowi)
    dt = jax.nn.softplus(dt_ref[...] + dtb_ref[...])
    dtt = jax.nn.softplus(dtt_ref[...] + dtbt_ref[...])
    da, cum, cum_t = _ssd_sums(dt, dtt, keep, same & (rowi <= coli), alog_ref, alogt_ref)
    whole = jnp.where(same, 1.0, 0.0).astype(BF16)
    last = _split_dot(da, jnp.concatenate([whole] * 3, axis=1), 3, left=False)
    e64 = e64_ref[...]
    ec_ref[...] = _split_dot(jnp.exp(cum), e64, 2)
    wb_e = _split_dot(dt * jnp.exp(last - cum), e64, 2)
    dec_e = _split_dot(jnp.exp(last), e64, 2)
    cum_e = _split_dot(cum, e128_ref[...], 3)
    _ssd_intra(xs, bm, cm, cum_e, cum_t, dtt, keep, dskip_ref, y_sc)
    ypre_ref[...] = y_sc[...]

    place = place_ref[...]
    dec_hi = dec_e.astype(BF16)
    dec_lo = (dec_e - dec_hi.astype(F32)).astype(BF16)
    xw = (xs * wb_e).astype(BF16)
    xw_ref[...] = _dot(place, jnp.concatenate([xw, dec_hi, dec_lo], axis=0)).astype(BF16)
    tokens = place[:, :CHUNK]
    c_ref[...] = _dot(tokens, cm.astype(BF16)).astype(BF16)
    zeros = jnp.zeros((CHUNK, D_STATE), BF16)
    b_wide = jnp.concatenate(
        [piece for g in range(SSM_GROUPS) for piece in (bm[:, g * D_STATE:(g + 1) * D_STATE].astype(BF16), zeros)], axis=1)
    n_rows = place.shape[0]
    slot = lax.broadcasted_iota(jnp.int32, (n_rows, 2 * D_STATE * SSM_GROUPS), 0) % SLOTS
    col = lax.broadcasted_iota(jnp.int32, (n_rows, 2 * D_STATE * SSM_GROUPS), 1) % (2 * D_STATE)
    ones = ((slot == seq_len) | (slot == seq_len + 1)) & (col >= D_STATE)
    bw_ref[...] = jnp.where(ones, 1.0, _dot(tokens, b_wide)).astype(BF16)


def _ssd_short(xbc, conv_rows, dt, dtt, consts, taps_sel, place, seq_len):
    rows = dt.shape[0]
    n_chunks = rows // CHUNK
    n_seq = CHUNK // seq_len
    n_slot_rows = place.shape[0]
    kern = functools.partial(_ssd_short_kernel, seq_len=seq_len)

    def out(r, n, dtype):
        return pl.BlockSpec((r, n), lambda c: (c, 0)), jax.ShapeDtypeStruct((n_chunks * r, n), dtype)

    conv_spec = pl.BlockSpec((CONV_W - 1, n_seq, CONV_DIM), lambda c: (0, c, 0))
    outs = [out(CHUNK, D_INNER, F32), out(CHUNK, D_INNER, F32), out(n_slot_rows, D_INNER, BF16),
            out(n_slot_rows, SSM_GROUPS * D_STATE, BF16), out(n_slot_rows, 2 * SSM_GROUPS * D_STATE, BF16),
            (conv_spec, jax.ShapeDtypeStruct(conv_rows.shape, F32))]
    return pl.pallas_call(
        kern,
        grid=(n_chunks,),
        in_specs=[pl.BlockSpec((CHUNK, CONV_DIM), lambda c: (c, 0)), conv_spec,
                  pl.BlockSpec((CHUNK, 128), lambda c: (c, 0)),
                  pl.BlockSpec((SSM_HEADS, CHUNK), lambda c: (0, c))]
                 + [_resident(a.shape) for a in consts] + [_resident(taps_sel.shape), _resident(place.shape)],
        out_specs=[o[0] for o in outs],
        out_shape=[o[1] for o in outs],
        scratch_shapes=[pltpu.VMEM((taps_sel.shape[1], CONV_DIM), BF16), pltpu.VMEM((CHUNK, D_INNER), F32)],
        compiler_params=_cp(("arbitrary",), 48),
        name="ssd_short",
    )(xbc, conv_rows, dt, dtt, *consts, taps_sel, place)


def _ssd_state_kernel(s0_ref, xw_ref, c_ref, bw_ref, ypre_ref, ec_ref, z_ref, ng_ref, y_ref, sout_ref, ci_sc,
                      *, nb, seq_len):
    for bb in range(nb):
        slots = slice(bb * SLOTS, (bb + 1) * SLOTS)
        toks = slice(bb * seq_len, (bb + 1) * seq_len)
        for g in range(SSM_GROUPS):
            gs = slice(g * GROUP_W, (g + 1) * GROUP_W)
            s0g = s0_ref[bb, gs, :]
            ci_sc[...] = _dot_nt(c_ref[slots, g * D_STATE:(g + 1) * D_STATE], s0g.astype(BF16))
            u = lax.dot_general(xw_ref[slots, gs], bw_ref[slots, 2 * g * D_STATE:2 * (g + 1) * D_STATE],
                                (((0,), (0,)), ((), ())), preferred_element_type=F32)
            sout_ref[bb, gs, :] = u[:, :D_STATE] + s0g * u[:, D_STATE:]
            zf = z_ref[toks, gs]
            yv = (ypre_ref[toks, gs] + ec_ref[toks, gs] * ci_sc[0:seq_len, :]) * (zf * jax.nn.sigmoid(zf))
            y_ref[toks, gs] = _rms(yv, ng_ref[:, gs], RMS_EPS)


def _ssd_state(s0, xw, cmat, bw, ypre, ec, z, ng, nb, seq_len):
    batch = s0.shape[0]
    kern = functools.partial(_ssd_state_kernel, nb=nb, seq_len=seq_len)

    def rows(r, n):
        return pl.BlockSpec((nb * r, n), lambda b: (b, 0))

    state_spec = pl.BlockSpec((nb, D_INNER, D_STATE), lambda b: (b, 0, 0))
    return pl.pallas_call(
        kern,
        grid=(batch // nb,),
        in_specs=[state_spec, rows(SLOTS, D_INNER), rows(SLOTS, SSM_GROUPS * D_STATE),
                  rows(SLOTS, 2 * SSM_GROUPS * D_STATE), rows(seq_len, D_INNER), rows(seq_len, D_INNER),
                  rows(seq_len, D_INNER), _resident(ng.shape)],
        out_specs=[rows(seq_len, D_INNER), state_spec],
        out_shape=[jax.ShapeDtypeStruct((batch * seq_len, D_INNER), F32),
                   jax.ShapeDtypeStruct((batch, D_INNER, D_STATE), F32)],
        scratch_shapes=[pltpu.VMEM((SLOTS, GROUP_W), F32)],
        compiler_params=_cp(("arbitrary",), 40),
        name="ssd_state",
    )(s0, xw, cmat, bw, ypre, ec, z, ng)


def _mem_kv_kernel(m_ref, w_ref, k_ref, v_ref):
    mb = m_ref[...].astype(BF16)
    k_ref[...] = _dot(mb, w_ref[:, :MEM_WIDTH])
    v_ref[...] = _dot(mb, w_ref[:, MEM_WIDTH:])


def _mem_kv(mem2d, w_kv, tm):
    m = mem2d.shape[0]
    return pl.pallas_call(
        _mem_kv_kernel,
        grid=(m // tm,),
        in_specs=[pl.BlockSpec((tm, D_MODEL), lambda i: (i, 0)), _resident(w_kv.shape)],
        out_specs=[pl.BlockSpec((tm, MEM_WIDTH), lambda i: (i, 0))] * 2,
        out_shape=[jax.ShapeDtypeStruct((m, MEM_WIDTH), F32)] * 2,
        compiler_params=_cp(("arbitrary",), 32),
        name="mem_kv",
    )(mem2d, w_kv)


MEM_HALVES = MEM_HEAD_DIM // 128
MEM_ROW_GROUP = MEM_HALVES * MEM_HEADS


def _mem_attn_cache_kernel(q_ref, k_ref, v_ref, o_ref, *, nb, n_tok):
    rows = MEM_HEADS * n_tok
    n_col = N_MEM * MEM_ROW_GROUP
    col = lax.broadcasted_iota(jnp.int32, (rows, n_col), 1)
    row = lax.broadcasted_iota(jnp.int32, (rows, n_col), 0)
    own = (col % MEM_ROW_GROUP) == (row // n_tok)
    scale = MEM_HEAD_DIM ** -0.5
    for bb in range(nb):
        g = _dot_nt(q_ref[bb], k_ref[bb].astype(BF16))
        s = (g[:rows] + pltpu.roll(g[rows:], n_col - MEM_HEADS, 1)) * scale
        s = jnp.where(own, s, NEG_BIG)
        p = jnp.exp(s - jnp.max(s, axis=1, keepdims=True))
        l = jnp.sum(p, axis=1, keepdims=True)
        p2 = jnp.concatenate([p, pltpu.roll(p, MEM_HEADS, 1)], axis=0).astype(BF16)
        o = _dot(p2, v_ref[bb].astype(BF16))
        o_ref[bb] = (o / jnp.concatenate([l, l], axis=0)).astype(BF16)


def _mem_attn_cache(q, mem_k, mem_v, nb, n_tok):
    batch, rows, _ = q.shape
    kern = functools.partial(_mem_attn_cache_kernel, nb=nb, n_tok=n_tok)
    kv_spec = pl.BlockSpec((nb,) + mem_k.shape[1:], lambda b: (b, 0, 0))
    q_spec = pl.BlockSpec((nb, rows, 128), lambda b: (b, 0, 0))
    return pl.pallas_call(
        kern,
        grid=(batch // nb,),
        in_specs=[q_spec, kv_spec, kv_spec],
        out_specs=q_spec,
        out_shape=jax.ShapeDtypeStruct((batch, rows, 128), BF16),
        compiler_params=_cp(("arbitrary",), 40),
        name="mem_attn_cache",
    )(q, mem_k, mem_v)


def _merge_ffn_kernel(x_ref, oa_ref, ob_ref, om_ref, g_ref, woa_ref, wob_ref, wom_ref, wout_ref,
                      ln1g_ref, ln1b_ref, wup_ref, bup_ref, wdown_ref, bdown_ref, ln2g_ref, ln2b_ref, y_ref):
    n = x_ref.shape[0] // 2
    halves = (slice(0, n), slice(n, 2 * n))
    a = [_dot(oa_ref[r, :], woa_ref[...]) for r in halves]
    b = [_dot(ob_ref[r, :], wob_ref[...]) for r in halves]
    c = [_dot(om_ref[r, :], wom_ref[...]) for r in halves]
    m = [(g_ref[r, 0:D_MODEL].astype(F32) * a[i] + g_ref[r, D_MODEL:2 * D_MODEL].astype(F32) * b[i]
          + g_ref[r, 2 * D_MODEL:].astype(F32) * c[i]).astype(BF16) for i, r in enumerate(halves)]
    t = [_dot(mi, wout_ref[...]) for mi in m]
    x1 = [_layer_norm(ALPHA * x_ref[r, :] + t[i], ln1g_ref[...], ln1b_ref[...]) for i, r in enumerate(halves)]
    u = [_dot(xi.astype(BF16), wup_ref[...]) for xi in x1]
    h = [jnp.square(jnp.maximum(ui + bup_ref[...], 0.0)).astype(BF16) for ui in u]
    d = [_dot(hi, wdown_ref[...]) for hi in h]
    for i, r in enumerate(halves):
        y_ref[r, :] = _layer_norm(ALPHA * x1[i] + d[i] + bdown_ref[...], ln2g_ref[...], ln2b_ref[...])


def _merge_ffn(x2d, o_mla, o_ssm, o_mem, g, weights, tm):
    m = x2d.shape[0]

    def row(n):
        return pl.BlockSpec((tm, n), lambda i: (i, 0))

    return pl.pallas_call(
        _merge_ffn_kernel,
        grid=(m // tm,),
        in_specs=[row(D_MODEL), row(MLA_HEADS * V_DIM), row(D_INNER), row(MEM_WIDTH), row(N_BRANCH * D_MODEL)]
                 + [_resident(w.shape) for w in weights],
        out_specs=row(D_MODEL),
        out_shape=jax.ShapeDtypeStruct((m, D_MODEL), F32),
        compiler_params=_cp(("arbitrary",), 56),
        name="merge_ffn",
    )(x2d, o_mla, o_ssm, o_mem, g, *weights)


def _rope_table(pos):
    half = ROPE_DIM // 2
    inv = ROPE_THETA ** (-jnp.arange(half, dtype=F32) / half)
    ang = pos.astype(F32)[:, None] * inv[None, :]
    cos, sin = jnp.cos(ang), jnp.sin(ang)
    return jnp.concatenate([cos, cos, cos, cos, -sin, sin, -sin, sin], axis=1)


def _expand_matrix(width, terms):
    src = np.arange(SSM_HEADS * width) // width
    one = (np.arange(128)[:, None] == src[None, :]).astype(np.float32)
    return jnp.asarray(np.concatenate([one] * terms, axis=0), dtype=BF16)


def _shift_matrix():
    sel = np.zeros(((CONV_W - 1) * CHUNK, CARRY + CHUNK), np.float32)
    for k in range(CONV_W - 1):
        for t in range(CHUNK):
            sel[k * CHUNK + t, CARRY + t + k - (CONV_W - 1)] = 1.0
    return jnp.asarray(sel, dtype=BF16)


def _short_taps_matrix(seq_len):
    n_seq = CHUNK // seq_len
    n_old = CONV_W - 1
    n_cached = n_old * n_seq
    sel = np.zeros((n_old * CHUNK + n_cached, CHUNK + 3 * n_cached), np.float32)

    def take(out_row, s, m):
        if m >= n_old:
            sel[out_row, s * seq_len + m - n_old] = 1.0
        else:
            for term in range(3):
                sel[out_row, CHUNK + term * n_cached + m * n_seq + s] = 1.0

    for s in range(n_seq):
        for k in range(n_old):
            for i in range(seq_len):
                take(k * CHUNK + s * seq_len + i, s, i + k)
        for j in range(n_old):
            take(n_old * CHUNK + j * n_seq + s, s, seq_len + j)
    return jnp.asarray(sel, dtype=BF16)


def _placement_matrix(seq_len):
    n_seq = CHUNK // seq_len
    place = np.zeros((n_seq * SLOTS, 3 * CHUNK), np.float32)
    for s in range(n_seq):
        for j in range(seq_len):
            place[s * SLOTS + j, s * seq_len + j] = 1.0
        place[s * SLOTS + seq_len, CHUNK + s * seq_len] = 1.0
        place[s * SLOTS + seq_len + 1, 2 * CHUNK + s * seq_len] = 1.0
    return jnp.asarray(place, dtype=BF16)


def _row(v):
    return v.reshape(1, -1).astype(F32)


def kernel(x_prompt, x_sample, mem_prompt, cache_ckv, cache_krope, page_table, cache_mem_k, cache_mem_v, state_ssm, state_conv, w_in, q_norm_g, w_uq, kv_norm_g, w_uk, w_uv, conv_w, conv_b, dt_bias, a_log, d_skip, ssm_norm_g, w_mem_k, w_mem_v, b_gate, w_o_mla, w_o_ssm, w_o_mem, w_out, ln1_g, ln1_b, w_up, b_up, w_down, b_down, ln2_g, ln2_b):
    bp, seq, _ = x_prompt.shape
    bs, t_new, _ = x_sample.shape
    n_pages = page_table.shape[1]
    past = n_pages * PAGE_SIZE
    mp, ms = bp * seq, bs * t_new
    assert seq % KV_TILE == 0 and seq % (CHUNKS_PER_STEP * CHUNK) == 0 and mp % TOKEN_TILE == 0
    assert TOKEN_TILE % KV_TILE == 0
    assert CHUNK % t_new == 0 and ms % CHUNK == 0 and t_new + 2 <= SLOTS and MLA_HEADS * t_new % 16 == 0
    assert bs % SAMPLE_ROWS == 0 and n_pages % 2 == 0 and MEM_HALVES == 2

    o_cq, o_ckv, o_kr, o_z, o_xbc, o_dt, o_mq, o_g = np.cumsum(
        [0, Q_LORA, KV_LORA, ROPE_DIM, D_INNER, CONV_DIM, SSM_HEADS, MEM_WIDTH]).tolist()
    w_kr = w_in[:, o_kr:o_z]
    w_small = jnp.concatenate([w_in[:, o_ckv:o_kr], w_kr, w_kr, w_in[:, o_dt:o_mq],
                               jnp.zeros((D_MODEL, 128 - SSM_HEADS), F32)], axis=1)
    w_proj = tuple(w.astype(BF16) for w in (w_in[:, o_cq:o_ckv], w_small, w_in[:, o_z:o_xbc], w_in[:, o_xbc:o_dt],
                                            w_in[:, o_mq:o_g], w_in[:, o_g:]))
    half = ROPE_DIM // 2
    w_q_nope = w_uq[:, :, :NOPE_DIM].reshape(Q_LORA, MLA_HEADS * NOPE_DIM)
    w_q_rope = w_uq[:, :, NOPE_DIM:]
    w_q_swap = jnp.concatenate([w_q_rope[:, :, half:], w_q_rope[:, :, :half]], axis=-1)
    pair_w = 2 * ROPE_DIM
    w_q_pairs = jnp.concatenate([w_q_rope.reshape(Q_LORA, MLA_HEADS // 2, pair_w),
                                 w_q_swap.reshape(Q_LORA, MLA_HEADS // 2, pair_w)], axis=-1)
    wq = jnp.concatenate([w_q_nope, w_q_pairs.reshape(Q_LORA, -1)], axis=1).astype(BF16)
    wuk_t = jnp.transpose(w_uk, (1, 2, 0)).astype(BF16)
    wuk_pairs = w_uk.reshape(KV_LORA, MLA_HEADS // 2, 2 * NOPE_DIM).transpose(1, 0, 2).astype(BF16)
    wuv = jnp.transpose(w_uv, (1, 0, 2)).astype(BF16)
    wuv_t = jnp.transpose(w_uv, (1, 2, 0)).astype(BF16)
    w_mem_kv = jnp.concatenate([w_mem_k.reshape(D_MODEL, MEM_WIDTH), w_mem_v.reshape(D_MODEL, MEM_WIDTH)], axis=1).astype(BF16)
    merge_w = (w_o_mla.astype(BF16), w_o_ssm.astype(BF16), w_o_mem.astype(BF16), w_out.astype(BF16),
               _row(ln1_g), _row(ln1_b), w_up.astype(BF16), _row(b_up), w_down.astype(BF16), _row(b_down),
               _row(ln2_g), _row(ln2_b))
    gq, gkv, bg = _row(q_norm_g), _row(kv_norm_g), _row(b_gate)

    pad_heads = jnp.zeros((128 - SSM_HEADS,), F32)
    ssd_head = (conv_w.astype(F32), _row(conv_b),
                _row(jnp.concatenate([dt_bias, pad_heads])),
                jnp.broadcast_to(dt_bias.astype(F32)[:, None], (SSM_HEADS, CHUNK)),
                _row(jnp.concatenate([a_log, pad_heads])),
                jnp.broadcast_to(a_log.astype(F32)[:, None], (SSM_HEADS, CHUNK)),
                _row(jnp.repeat(d_skip, SSM_HEAD_DIM)))
    norm_g = _row(ssm_norm_g)
    expanders = (_expand_matrix(SSM_HEAD_DIM, 2), _expand_matrix(128, 3))

    n_chunks = seq // CHUNK
    cs_p = _rope_table(jnp.arange(seq))
    xp2d = x_prompt.reshape(mp, D_MODEL)
    mem_k_p, mem_v_p = _mem_kv(mem_prompt.reshape(bp * N_MEM, D_MODEL), w_mem_kv, tm=min(MEM_ROWS_TILE, bp * N_MEM))
    _, ckv_p, kr_p, _, dt_p, dtt_p, z_p, xbc_p, o_mem_p, g_p, k_heads, vt_heads, qt_p = _in_proj(
        xp2d, w_proj, cs_p, gq, gkv, bg, tm=TOKEN_TILE,
        head_kv_weights=(wuk_pairs, wuv_t.reshape(MLA_HEADS // 2, 2 * V_DIM, KV_LORA), wq),
        memory=(mem_k_p, mem_v_p))
    o_mla_p = _prompt_attn(qt_p, k_heads, vt_heads, bp, seq)

    o_ssm_p, ssm_p, conv_p = _ssd(xbc_p, z_p, dt_p, dtt_p, ssd_head + (norm_g,) + expanders + (_shift_matrix(),),
                                  batch=bp, n_chunks=n_chunks)

    y_p = _merge_ffn(xp2d, o_mla_p, o_ssm_p, o_mem_p, g_p, merge_w, tm=TOKEN_TILE)

    cs_s = jnp.tile(_rope_table(past + jnp.arange(t_new)), (bs, 1))
    xs2d = x_sample.reshape(ms, D_MODEL)
    tm_s = min(TOKEN_TILE, ms)
    cqn, ckv_s, kr_s, kcat, dt_s, dtt_s, z_s, xbc_s, mq_s, g_s = _in_proj(xs2d, w_proj, cs_s, gq, gkv, bg, tm=tm_s)
    q_s = _q_prep(cqn, wq, wuk_t, cs_s, tm=tm_s)
    q_s = jnp.transpose(q_s.reshape(MLA_HEADS, bs, t_new, QK_DIM), (1, 0, 2, 3)).reshape(bs, MLA_HEADS * t_new, QK_DIM)
    o_lat = _decode_attn(page_table, q_s, kcat.reshape(bs, t_new, QK_DIM).astype(F32), cache_ckv,
                         jnp.swapaxes(cache_krope, 1, 2))
    o_lat = jnp.transpose(o_lat.reshape(bs, MLA_HEADS, t_new, KV_LORA), (1, 0, 2, 3)).reshape(MLA_HEADS, ms, KV_LORA)
    o_mla_s = _uv_proj(o_lat, wuv)

    ypre_s, ec_s, xw_s, c_s, bw_s, conv_rows = _ssd_short(
        xbc_s, jnp.transpose(state_conv.astype(F32), (1, 0, 2)), dt_s, dtt_s, ssd_head + expanders,
        _short_taps_matrix(t_new), _placement_matrix(t_new), seq_len=t_new)
    conv_s = jnp.transpose(conv_rows, (1, 0, 2))
    o_ssm_s, ssm_s = _ssd_state(state_ssm.astype(F32).reshape(bs, D_INNER, D_STATE), xw_s, c_s, bw_s, ypre_s, ec_s,
                                z_s.astype(F32), norm_g, nb=SAMPLE_ROWS, seq_len=t_new)

    def cache_rows(c):
        c = c.reshape(bs, N_MEM, MEM_HEADS, MEM_HALVES, 128)
        return jnp.transpose(c, (0, 1, 3, 2, 4)).reshape(bs, N_MEM * MEM_ROW_GROUP, 128)

    mq_rows = jnp.transpose(mq_s.reshape(bs, t_new, MEM_HEADS, MEM_HALVES, 128), (0, 3, 2, 1, 4))
    o_mem_s = _mem_attn_cache(mq_rows.reshape(bs, MEM_ROW_GROUP * t_new, 128), cache_rows(cache_mem_k),
                              cache_rows(cache_mem_v), nb=SAMPLE_ROWS, n_tok=t_new)
    o_mem_s = jnp.transpose(o_mem_s.reshape(bs, MEM_HALVES, MEM_HEADS, t_new, 128), (0, 3, 2, 1, 4))
    y_s = _merge_ffn(xs2d, o_mla_s, o_ssm_s.reshape(ms, D_INNER).astype(BF16), o_mem_s.reshape(ms, MEM_WIDTH), g_s,
                     merge_w, tm=tm_s)

    return (y_p.reshape(bp, seq, D_MODEL), y_s.reshape(bs, t_new, D_MODEL),
            ckv_p.reshape(bp, seq, KV_LORA), kr_p.reshape(bp, seq, ROPE_DIM),
            mem_k_p.reshape(bp, N_MEM, MEM_HEADS, MEM_HEAD_DIM), mem_v_p.reshape(bp, N_MEM, MEM_HEADS, MEM_HEAD_DIM),
            ssm_p.reshape(bp, SSM_HEADS, SSM_HEAD_DIM, D_STATE), conv_p,
            ckv_s.reshape(bs, t_new, KV_LORA), kr_s.reshape(bs, t_new, ROPE_DIM),
            ssm_s.reshape(bs, SSM_HEADS, SSM_HEAD_DIM, D_STATE), conv_s)
```

```python
import functools

import jax
import jax.numpy as jnp
import numpy as np
from jax import lax
from jax.experimental import pallas as pl
from jax.experimental.pallas import tpu as pltpu

F32 = jnp.float32
BF16 = jnp.bfloat16

D_MODEL = 1024
MLA_HEADS = 8
Q_LORA = 384
KV_LORA = 256
NOPE_DIM = 128
ROPE_DIM = 64
V_DIM = 128
ROPE_THETA = 10000.0
QK_DIM = KV_LORA + ROPE_DIM
HEAD_QK = 256
SSM_HEADS = 32
SSM_HEAD_DIM = 64
D_INNER = SSM_HEADS * SSM_HEAD_DIM
SSM_GROUPS = 4
GROUP_W = D_INNER // SSM_GROUPS
D_STATE = 128
CONV_W = 4
CONV_DIM = D_INNER + 2 * SSM_GROUPS * D_STATE
CHUNK = 128
N_MEM = 256
MEM_HEADS = 4
MEM_HEAD_DIM = 256
MEM_WIDTH = MEM_HEADS * MEM_HEAD_DIM
D_FF = 4 * D_MODEL
N_BRANCH = 3
DEPTH = 1
ALPHA = (2 * DEPTH) ** 0.25
LN_EPS = 1e-5
RMS_EPS = 1e-6
PAGE_SIZE = 128
SCORE_SCALE = (NOPE_DIM + ROPE_DIM) ** -0.5
NEG_BIG = -1e30
KV_TILE = 256

TOKEN_TILE = 256
MEM_ROWS_TILE = 512
SAMPLE_ROWS = 4


def _cp(sem, vmem_mb):
    return pltpu.CompilerParams(dimension_semantics=sem, vmem_limit_bytes=vmem_mb << 20)


def _resident(shape):
    nd = len(shape)
    return pl.BlockSpec(shape, lambda *_: (0,) * nd, pipeline_mode=pl.Buffered(1))


def _dot(a, b):
    return jnp.dot(a, b, preferred_element_type=F32)


def _dot_nt(a, b):
    return lax.dot_general(a, b, (((1,), (1,)), ((), ())), preferred_element_type=F32)


def _rms(v, g, eps):
    return v * lax.rsqrt(jnp.mean(v * v, axis=-1, keepdims=True) + eps) * g


def _layer_norm(v, g, b):
    mu = jnp.mean(v, axis=-1, keepdims=True)
    d = v - mu
    var = jnp.mean(d * d, axis=-1, keepdims=True)
    return d * lax.rsqrt(var + LN_EPS) * g + b


def _mem_attend(q, k_ref, v_ref, o_ref):
    scale = MEM_HEAD_DIM ** -0.5
    heads = [slice(h * MEM_HEAD_DIM, (h + 1) * MEM_HEAD_DIM) for h in range(MEM_HEADS)]
    scores = [_dot_nt(q[:, hs], k_ref[:, hs].astype(BF16)) for hs in heads]
    for h, hs in enumerate(heads):
        s = scores[h] * scale
        p = jnp.exp(s - jnp.max(s, axis=1, keepdims=True))
        l = jnp.sum(p, axis=1, keepdims=True)
        o_ref[:, hs] = (_dot(p.astype(BF16), v_ref[:, hs].astype(BF16)) / l).astype(BF16)


def _in_proj_kernel(x_ref, wcq_ref, wsmall_ref, wz_ref, wxbc_ref, wmq_ref, wg_ref, cs_ref, gq_ref, gkv_ref, bg_ref,
                    *rest, per_head_kv):
    if per_head_kv:
        wuk_ref, wuvt_ref, wq_ref, memk_ref, memv_ref = rest[:5]
        rest = rest[5:]
    cqn_ref, ckv_ref, kr_ref, kcat_ref, dt_ref, dtt_ref, z_ref, xbc_ref, mq_ref, g_ref = rest[:10]
    xb = x_ref[...].astype(BF16)
    cqn = _rms(_dot(xb, wcq_ref[...]), gq_ref[...], RMS_EPS).astype(BF16)
    cqn_ref[...] = cqn

    small = _dot(xb, wsmall_ref[...])
    ckv = _rms(small[:, :KV_LORA], gkv_ref[...], RMS_EPS)
    ckv_ref[...] = ckv
    ckv_b = ckv.astype(BF16)
    kcat_ref[:, 0:KV_LORA] = ckv_b
    a = small[:, 256:384]
    b = pltpu.roll(a, ROPE_DIM // 2, 1)
    cs = cs_ref[...]
    ro = a * cs[:, :128] + b * cs[:, 128:]
    kr_ref[...] = ro[:, :ROPE_DIM]
    kcat_ref[:, KV_LORA:QK_DIM] = ro[:, :ROPE_DIM].astype(BF16)
    if per_head_kv:
        kh_ref, vt_ref, qt_ref = rest[10:]
        _write_queries(cqn, cs, wq_ref, None, qt_ref, absorb=False)
        lane = lax.broadcasted_iota(jnp.int32, ro.shape, 1)
        rope_pad = jnp.where(lane < ROPE_DIM, ro, 0.0).astype(BF16)
        ckv_t = [ckv[t * KV_TILE:(t + 1) * KV_TILE, :].T.astype(BF16) for t in range(vt_ref.shape[1])]
        for p in range(MLA_HEADS // 2):
            kk = _dot(ckv_b, wuk_ref[p]).astype(BF16)
            for i in range(2):
                kh_ref[2 * p + i, :, 0:NOPE_DIM] = kk[:, i * NOPE_DIM:(i + 1) * NOPE_DIM]
                kh_ref[2 * p + i, :, NOPE_DIM:HEAD_QK] = rope_pad
            for t, ct in enumerate(ckv_t):
                vv = _dot(wuvt_ref[p], ct).astype(BF16)
                vt_ref[2 * p, t] = vv[:V_DIM]
                vt_ref[2 * p + 1, t] = vv[V_DIM:]
    dt = small[:, 384:512]
    dt_ref[...] = dt
    dtt_ref[...] = dt.T[0:SSM_HEADS, :]

    for c in range(0, D_INNER, 1024):
        z_ref[:, c:c + 1024] = _dot(xb, wz_ref[:, c:c + 1024]).astype(BF16)
    for c in range(0, CONV_DIM, 1024):
        xbc_ref[:, c:c + 1024] = _dot(xb, wxbc_ref[:, c:c + 1024]).astype(BF16)
    mq = _dot(xb, wmq_ref[...]).astype(BF16)
    if per_head_kv:
        _mem_attend(mq, memk_ref, memv_ref, mq_ref)
    else:
        mq_ref[...] = mq
    for c in range(0, N_BRANCH * D_MODEL, 1024):
        gr = _dot(xb, wg_ref[:, c:c + 1024]) + bg_ref[:, c:c + 1024]
        g_ref[:, c:c + 1024] = jax.nn.sigmoid(gr).astype(BF16)


def _in_proj(x2d, weights, cs_tab, gq, gkv, bg, tm, head_kv_weights=(), memory=()):
    m = x2d.shape[0]
    ncs = cs_tab.shape[0] // tm
    steps_per_batch = (m // tm) // (memory[0].shape[0] // N_MEM) if memory else 0

    def row(n, dtype):
        return pl.BlockSpec((tm, n), lambda i: (i, 0)), jax.ShapeDtypeStruct((m, n), dtype)

    outs = [row(Q_LORA, BF16), row(KV_LORA, F32), row(ROPE_DIM, F32), row(QK_DIM, BF16), row(128, F32),
            (pl.BlockSpec((SSM_HEADS, tm), lambda i: (0, i)), jax.ShapeDtypeStruct((SSM_HEADS, m), F32)),
            row(D_INNER, BF16), row(CONV_DIM, BF16), row(MEM_WIDTH, BF16), row(N_BRANCH * D_MODEL, BF16)]
    if head_kv_weights:
        outs += [(pl.BlockSpec((MLA_HEADS, tm, HEAD_QK), lambda i: (0, i, 0)),
                  jax.ShapeDtypeStruct((MLA_HEADS, m, HEAD_QK), BF16)),
                 (pl.BlockSpec((MLA_HEADS, tm // KV_TILE, V_DIM, KV_TILE), lambda i: (0, i, 0, 0)),
                  jax.ShapeDtypeStruct((MLA_HEADS, m // KV_TILE, V_DIM, KV_TILE), BF16)),
                 (pl.BlockSpec((MLA_HEADS, HEAD_QK, tm), lambda i: (0, 0, i)),
                  jax.ShapeDtypeStruct((MLA_HEADS, HEAD_QK, m), BF16))]
    return pl.pallas_call(
        functools.partial(_in_proj_kernel, per_head_kv=bool(head_kv_weights)),
        grid=(m // tm,),
        in_specs=[pl.BlockSpec((tm, D_MODEL), lambda i: (i, 0))] + [_resident(w.shape) for w in weights]
                 + [pl.BlockSpec((tm, 256), lambda i: (i % ncs, 0)),
                    _resident(gq.shape), _resident(gkv.shape), _resident(bg.shape)]
                 + [_resident(w.shape) for w in head_kv_weights]
                 + [pl.BlockSpec((N_MEM, MEM_WIDTH), lambda i: (i // steps_per_batch, 0)) for _ in memory],
        out_specs=[o[0] for o in outs],
        out_shape=[o[1] for o in outs],
        compiler_params=_cp(("arbitrary",), 56),
        name="in_proj",
    )(x2d, *weights, cs_tab, gq, gkv, bg, *head_kv_weights, *memory)


def _write_queries(c, cs, wq_ref, wuk_ref, q_ref, absorb):
    qn = [_dot(c, wq_ref[:, h * NOPE_DIM:(h + 1) * NOPE_DIM]) for h in range(MLA_HEADS)]
    for h in range(MLA_HEADS):
        if absorb:
            q_ref[h, :, 0:KV_LORA] = (_dot(qn[h].astype(BF16), wuk_ref[h]) * SCORE_SCALE).astype(BF16)
        else:
            q_ref[h, 0:NOPE_DIM, :] = (qn[h] * SCORE_SCALE).T.astype(BF16)
            q_ref[h, NOPE_DIM + ROPE_DIM:HEAD_QK, :] = jnp.zeros((HEAD_QK - NOPE_DIM - ROPE_DIM, c.shape[0]), BF16)
    r0 = MLA_HEADS * NOPE_DIM
    for p in range(MLA_HEADS // 2):
        ab = _dot(c, wq_ref[:, r0 + p * 256:r0 + (p + 1) * 256])
        ro = (ab[:, :128] * cs[:, :128] + ab[:, 128:] * cs[:, 128:]) * SCORE_SCALE
        if absorb:
            q_ref[2 * p, :, KV_LORA:QK_DIM] = ro[:, :ROPE_DIM].astype(BF16)
            q_ref[2 * p + 1, :, KV_LORA:QK_DIM] = pltpu.roll(ro, ROPE_DIM, 1)[:, :ROPE_DIM].astype(BF16)
        else:
            rot = ro.T.astype(BF16)
            q_ref[2 * p, NOPE_DIM:NOPE_DIM + ROPE_DIM, :] = rot[:ROPE_DIM]
            q_ref[2 * p + 1, NOPE_DIM:NOPE_DIM + ROPE_DIM, :] = rot[ROPE_DIM:]


def _q_prep_kernel(c_ref, wq_ref, wuk_ref, cs_ref, q_ref):
    _write_queries(c_ref[...], cs_ref[...], wq_ref, wuk_ref, q_ref, absorb=True)


def _q_prep(cqn, wq, wuk_t, cs_tab, tm):
    m = cqn.shape[0]
    ncs = cs_tab.shape[0] // tm
    return pl.pallas_call(
        _q_prep_kernel,
        grid=(m // tm,),
        in_specs=[pl.BlockSpec((tm, Q_LORA), lambda i: (i, 0)), _resident(wq.shape), _resident(wuk_t.shape),
                  pl.BlockSpec((tm, 256), lambda i: (i % ncs, 0))],
        out_specs=pl.BlockSpec((MLA_HEADS, tm, QK_DIM), lambda i: (0, i, 0)),
        out_shape=jax.ShapeDtypeStruct((MLA_HEADS, m, QK_DIM), BF16),
        compiler_params=_cp(("arbitrary",), 32),
        name="q_prep",
    )(cqn, wq, wuk_t, cs_tab)


def _prompt_attn_kernel(qt_ref, k_ref, vt_ref, o_ref, m_sc, l_sc, acc_sc):
    i = pl.program_id(1)
    t = KV_TILE
    m_sc[...] = jnp.full(m_sc.shape, NEG_BIG, F32)
    l_sc[...] = jnp.zeros(l_sc.shape, F32)
    acc_sc[...] = jnp.zeros(acc_sc.shape, F32)

    def scores(j):
        rows = pl.ds(pl.multiple_of(j * t, t), t)
        return [_dot(k_ref[h, rows, :], qt_ref[h]) for h in range(MLA_HEADS)]

    def kv_block(j, masked, pending):
        if masked:
            keep = lax.broadcasted_iota(jnp.int32, (t, t), 0) <= lax.broadcasted_iota(jnp.int32, (t, t), 1)
        for h in range(MLA_HEADS):
            st = pending[h]
            if masked:
                st = jnp.where(keep, st, NEG_BIG)
            m_prev = m_sc[h:h + 1, :]
            m_new = jnp.maximum(m_prev, jnp.max(st, axis=0, keepdims=True))
            alpha = jnp.exp(m_prev - m_new)
            p = jnp.exp(st - m_new)
            l_sc[h:h + 1, :] = alpha * l_sc[h:h + 1, :] + jnp.sum(p, axis=0, keepdims=True)
            m_sc[h:h + 1, :] = m_new
            acc_sc[h] = alpha * acc_sc[h] + _dot(vt_ref[h, j], p.astype(BF16))

    def two_tiles(jj, carry):
        first, second = scores(2 * jj), scores(2 * jj + 1)
        kv_block(2 * jj, False, first)
        kv_block(2 * jj + 1, False, second)
        return carry

    lax.fori_loop(0, i // 2, two_tiles, 0)

    @pl.when(i % 2 == 1)
    def _():
        kv_block(i - 1, False, scores(i - 1))

    kv_block(i, True, scores(i))

    for h in range(MLA_HEADS):
        o_ref[:, h * V_DIM:(h + 1) * V_DIM] = (acc_sc[h] / l_sc[h:h + 1, :]).T.astype(BF16)


def _prompt_attn(qt, k_heads, vt_heads, batch, seq):
    t = KV_TILE
    nq = seq // t
    return pl.pallas_call(
        _prompt_attn_kernel,
        grid=(batch, nq),
        in_specs=[pl.BlockSpec((MLA_HEADS, HEAD_QK, t), lambda b, i: (0, 0, b * nq + i)),
                  pl.BlockSpec((MLA_HEADS, seq, HEAD_QK), lambda b, i: (0, b, 0)),
                  pl.BlockSpec((MLA_HEADS, nq, V_DIM, t), lambda b, i: (0, b, 0, 0))],
        out_specs=pl.BlockSpec((t, MLA_HEADS * V_DIM), lambda b, i: (b * nq + i, 0)),
        out_shape=jax.ShapeDtypeStruct((batch * seq, MLA_HEADS * V_DIM), BF16),
        scratch_shapes=[pltpu.VMEM((MLA_HEADS, t), F32), pltpu.VMEM((MLA_HEADS, t), F32),
                        pltpu.VMEM((MLA_HEADS, V_DIM, t), F32)],
        compiler_params=_cp(("arbitrary", "arbitrary"), 48),
        name="prompt_attn",
    )(qt, k_heads, vt_heads)


def _decode_attn_kernel(pt_ref, q_ref, kn_ref, ckv_hbm, krt_hbm, o_ref, kbuf, rbuf, kb_sc, s_sc, sem, *, n_pages, n_new):
    b = pl.program_id(0)
    nb = pl.num_programs(0)
    slot = b % 2

    def page_copies(page, p, sl):
        rows = pl.ds(pl.multiple_of(p * PAGE_SIZE, PAGE_SIZE), PAGE_SIZE)
        return (pltpu.make_async_copy(ckv_hbm.at[page], kbuf.at[sl, rows, :], sem.at[0, sl]),
                pltpu.make_async_copy(krt_hbm.at[page], rbuf.at[sl, p], sem.at[1, sl]))

    def issue(bi, sl):
        def body(pp, carry):
            for k in range(2):
                p = 2 * pp + k
                latent, rotary = page_copies(pt_ref[bi * n_pages + p], p, sl)
                latent.start(priority=k)
                rotary.start(priority=1 - k)
            return carry
        lax.fori_loop(0, n_pages // 2, body, 0, unroll=4)

    @pl.when(b == 0)
    def _():
        issue(0, 0)

    @pl.when(b + 1 < nb)
    def _():
        issue(b + 1, 1 - slot)

    def wait_body(p, carry):
        for cp in page_copies(0, p, slot):
            cp.wait()
        return carry
    lax.fori_loop(0, n_pages, wait_body, 0, unroll=8)

    q = q_ref[0]
    rows = q.shape[0]
    qc = q[:, :KV_LORA]
    qr = q[:, KV_LORA:]
    qf = q.astype(F32)
    kn = kn_ref[0]
    tok = lax.broadcasted_iota(jnp.int32, (rows, 1), 0) % n_new

    s_new = []
    for j in range(n_new):
        sj = jnp.sum(qf * kn[j:j + 1, :], axis=1, keepdims=True)
        s_new.append(jnp.where(tok >= j, sj, NEG_BIG))
    m0 = s_new[0]
    for j in range(1, n_new):
        m0 = jnp.maximum(m0, s_new[j])

    for c in range(n_pages // 2):
        r = slice(2 * c * PAGE_SIZE, 2 * (c + 1) * PAGE_SIZE)
        kc = kbuf[slot, r, :].astype(BF16)
        kb_sc[r, :] = kc
        rt = jnp.concatenate([rbuf[slot, 2 * c], rbuf[slot, 2 * c + 1]], axis=1).astype(BF16)
        s_sc[:, r] = _dot_nt(qc, kc) + _dot(qr, rt)
    s = s_sc[...]
    m = jnp.maximum(m0, jnp.max(s, axis=1, keepdims=True))
    p = jnp.exp(s - m)
    l = jnp.sum(p, axis=1, keepdims=True)
    acc = _dot(p.astype(BF16), kb_sc[...])
    for j in range(n_new):
        pj = jnp.exp(s_new[j] - m)
        l = l + pj
        acc = acc + pj * kn[j:j + 1, :KV_LORA]
    o_ref[0] = (acc / l).astype(BF16)


def _decode_attn(page_table, q, k_new, cache_ckv, cache_krope_t):
    nb, n_pages = page_table.shape
    rows = q.shape[1]
    n_new = k_new.shape[1]
    past = n_pages * PAGE_SIZE
    kern = functools.partial(_decode_attn_kernel, n_pages=n_pages, n_new=n_new)
    grid_spec = pltpu.PrefetchScalarGridSpec(
        num_scalar_prefetch=1,
        grid=(nb,),
        in_specs=[pl.BlockSpec((1, rows, QK_DIM), lambda b, pt: (b, 0, 0)),
                  pl.BlockSpec((1, n_new, QK_DIM), lambda b, pt: (b, 0, 0)),
                  pl.BlockSpec(memory_space=pl.ANY),
                  pl.BlockSpec(memory_space=pl.ANY)],
        out_specs=pl.BlockSpec((1, rows, KV_LORA), lambda b, pt: (b, 0, 0)),
        scratch_shapes=[pltpu.VMEM((2, past, KV_LORA), F32), pltpu.VMEM((2, n_pages, ROPE_DIM, PAGE_SIZE), F32),
                        pltpu.VMEM((past, KV_LORA), BF16), pltpu.VMEM((rows, past), F32),
                        pltpu.SemaphoreType.DMA((2, 2))],
    )
    return pl.pallas_call(
        kern,
        grid_spec=grid_spec,
        out_shape=jax.ShapeDtypeStruct((nb, rows, KV_LORA), BF16),
        compiler_params=_cp(("arbitrary",), 40),
        name="decode_attn",
    )(page_table.reshape(-1), q, k_new, cache_ckv, cache_krope_t)


def _uv_proj_kernel(o_ref, w_ref, out_ref):
    out_ref[...] = _dot(o_ref[0], w_ref[0]).astype(BF16)


def _uv_proj(o_lat, wuv):
    m = o_lat.shape[1]
    return pl.pallas_call(
        _uv_proj_kernel,
        grid=(MLA_HEADS,),
        in_specs=[pl.BlockSpec((1, m, KV_LORA), lambda h: (h, 0, 0)),
                  pl.BlockSpec((1, KV_LORA, V_DIM), lambda h: (h, 0, 0))],
        out_specs=pl.BlockSpec((m, V_DIM), lambda h: (0, h)),
        out_shape=jax.ShapeDtypeStruct((m, MLA_HEADS * V_DIM), BF16),
        compiler_params=_cp(("arbitrary",), 16),
        name="uv_proj",
    )(o_lat, wuv)


def _split_dot(v, e, terms, left=True):
    parts = []
    r = v
    for _ in range(terms):
        hi = r.astype(BF16)
        parts.append(hi)
        r = r - hi.astype(F32)
    if left:
        return _dot(jnp.concatenate(parts, axis=1), e)
    return _dot(e, jnp.concatenate(parts, axis=0))


def _conv_silu(taps, cw_ref, cb_ref):
    conv = cb_ref[...] + cw_ref[0:1, :] * taps[0]
    for k in range(1, CONV_W):
        conv = conv + cw_ref[k:k + 1, :] * taps[k]
    xc = conv * jax.nn.sigmoid(conv)
    n_bc = SSM_GROUPS * D_STATE
    return xc[:, :D_INNER], xc[:, D_INNER:D_INNER + n_bc], xc[:, D_INNER + n_bc:]


def _select_rows(select, rows_ref):
    n = rows_ref.shape[1]
    return jnp.concatenate([_dot(select, rows_ref[:, c:c + 1024]) for c in range(0, n, 1024)], axis=1)


def _ssd_sums(dt, dtt, keep, keep_t, alog_ref, alogt_ref):
    tri = jnp.where(keep, 1.0, 0.0).astype(BF16)
    tri_t = jnp.where(keep_t, 1.0, 0.0).astype(BF16)
    da = dt * (-jnp.exp(alog_ref[...]))
    dat = dtt * (-jnp.exp(alogt_ref[...]))
    cum = _split_dot(da, jnp.concatenate([tri] * 3, axis=1), 3, left=False)
    cum_t = _split_dot(dat, jnp.concatenate([tri_t] * 3, axis=0), 3)
    return da, cum, cum_t


def _ssd_intra(xs, bm, cm, cum_e, cum_t, dtt, keep, dskip_ref, y_sc, carried=None):
    pairs_per_group = SSM_HEADS // 2 // SSM_GROUPS
    lane = lax.broadcasted_iota(jnp.int32, (CHUNK, CHUNK), 1)
    for g in range(SSM_GROUPS):
        cg = cm[:, g * D_STATE:(g + 1) * D_STATE].astype(BF16)
        cb = _dot_nt(cg, bm[:, g * D_STATE:(g + 1) * D_STATE].astype(BF16))
        extra = None if carried is None else carried(g, cg)
        for pr in range(pairs_per_group):
            q = g * pairs_per_group + pr
            ws = []
            for h in (2 * q, 2 * q + 1):
                seg = cum_e[:, h * 128:(h + 1) * 128] - cum_t[h:h + 1, :]
                decay = jnp.exp(jnp.where(keep, seg, NEG_BIG)) * dtt[h:h + 1, :]
                ws.append((cb * decay).astype(BF16))
            w_pair = jnp.concatenate(ws, axis=1)
            xp = xs[:, q * 128:(q + 1) * 128]
            x_bd = jnp.concatenate([jnp.where(lane < SSM_HEAD_DIM, xp, 0.0).astype(BF16),
                                    jnp.where(lane >= SSM_HEAD_DIM, xp, 0.0).astype(BF16)], axis=0)
            y = _dot(w_pair, x_bd) + dskip_ref[:, q * 128:(q + 1) * 128] * xp
            if extra is not None:
                y = y + extra[:, pr * 128:(pr + 1) * 128]
            y_sc[:, q * 128:(q + 1) * 128] = y


CARRY = 16


def _ssd_kernel(xbc_ref, z_ref, dt_ref, dtt_ref, cw_ref, cb_ref, dtb_ref, dtbt_ref,
                alog_ref, alogt_ref, dskip_ref, ng_ref, e64_ref, e128_ref, shift_ref,
                y_ref, sout_ref, cout_ref, xe_sc, tail_sc, st_sc, y_sc):
    c = pl.program_id(1)
    n_pairs = SSM_HEADS // 2
    pairs_per_group = n_pairs // SSM_GROUPS

    @pl.when(c == 0)
    def _():
        xe_sc[0:CARRY, :] = jnp.zeros((CARRY, CONV_DIM), BF16)
        st_sc[...] = jnp.zeros(st_sc.shape, F32)

    xe_sc[CARRY:CARRY + CHUNK, :] = xbc_ref[...]
    shifted = _select_rows(shift_ref[...], xe_sc)
    taps = [shifted[k * CHUNK:(k + 1) * CHUNK] for k in range(CONV_W - 1)] + [xbc_ref[...].astype(F32)]
    xs, bm, cm = _conv_silu(taps, cw_ref, cb_ref)
    xe_sc[0:CARRY, :] = xe_sc[CHUNK:CHUNK + CARRY, :]

    rowi = lax.broadcasted_iota(jnp.int32, (CHUNK, CHUNK), 0)
    coli = lax.broadcasted_iota(jnp.int32, (CHUNK, CHUNK), 1)
    keep = coli <= rowi
    dt = jax.nn.softplus(dt_ref[...] + dtb_ref[...])
    dtt = jax.nn.softplus(dtt_ref[...] + dtbt_ref[...])
    _, cum, cum_t = _ssd_sums(dt, dtt, keep, rowi <= coli, alog_ref, alogt_ref)
    last = cum[CHUNK - 1:CHUNK, :]
    e64 = e64_ref[...]
    ec_e = _split_dot(jnp.exp(cum), e64, 2)
    wb_e = _split_dot(dt * jnp.exp(last - cum), e64, 2)
    dec_e = _split_dot(jnp.broadcast_to(jnp.exp(last), (16, CHUNK)), e64, 2)[0:1, :]
    cum_e = _split_dot(cum, e128_ref[...], 3)

    def carried(g, cg):
        return _dot(cg, st_sc[g].astype(BF16)) * ec_e[:, g * GROUP_W:(g + 1) * GROUP_W]

    _ssd_intra(xs, bm, cm, cum_e, cum_t, dtt, keep, dskip_ref, y_sc, carried)
    for g in range(SSM_GROUPS):
        gs = slice(g * GROUP_W, (g + 1) * GROUP_W)
        xw = (xs[:, gs] * wb_e[:, gs]).astype(BF16)
        bt = bm[:, g * D_STATE:(g + 1) * D_STATE].T.astype(BF16)
        st_sc[g] = st_sc[g] * dec_e[:, gs] + _dot(bt, xw)

    for g in range(SSM_GROUPS):
        gs = slice(g * GROUP_W, (g + 1) * GROUP_W)
        zf = z_ref[:, gs].astype(F32)
        yv = y_sc[:, gs] * (zf * jax.nn.sigmoid(zf))
        y_ref[:, gs] = _rms(yv, ng_ref[:, gs], RMS_EPS).astype(y_ref.dtype)

    @pl.when(c == pl.num_programs(1) - 1)
    def _():
        tail_sc[...] = xbc_ref[CHUNK - CARRY:CHUNK, :].astype(F32)
        cout_ref[0] = tail_sc[CARRY - (CONV_W - 1):CARRY, :]
        for q in range(n_pairs):
            g, pr = divmod(q, pairs_per_group)
            sout_ref[0, q * 128:(q + 1) * 128, :] = st_sc[g, :, pr * 128:(pr + 1) * 128].T


def _ssd(xbc, z, dt, dtt, consts, batch, n_chunks):
    def step(n):
        return pl.BlockSpec((CHUNK, n), lambda b, c: (b * n_chunks + c, 0))

    def per_batch(r, n):
        return pl.BlockSpec((1, r, n), lambda b, c: (b, 0, 0))

    return pl.pallas_call(
        _ssd_kernel,
        grid=(batch, n_chunks),
        in_specs=[step(CONV_DIM), step(D_INNER), step(128),
                  pl.BlockSpec((SSM_HEADS, CHUNK), lambda b, c: (0, b * n_chunks + c))]
                 + [_resident(a.shape) for a in consts],
        out_specs=[step(D_INNER), per_batch(D_INNER, D_STATE), per_batch(CONV_W - 1, CONV_DIM)],
        out_shape=[jax.ShapeDtypeStruct((batch * n_chunks * CHUNK, D_INNER), BF16),
                   jax.ShapeDtypeStruct((batch, D_INNER, D_STATE), F32),
                   jax.ShapeDtypeStruct((batch, CONV_W - 1, CONV_DIM), F32)],
        scratch_shapes=[pltpu.VMEM((CARRY + CHUNK, CONV_DIM), BF16),
                        pltpu.VMEM((CARRY, CONV_DIM), F32),
                        pltpu.VMEM((SSM_GROUPS, D_STATE, GROUP_W), F32),
                        pltpu.VMEM((CHUNK, D_INNER), F32)],
        compiler_params=_cp(("arbitrary", "arbitrary"), 48),
        name="ssd",
    )(xbc, z, dt, dtt, *consts)


SLOTS = 16


def _ssd_short_kernel(x_ref, sc_ref, dt_ref, dtt_ref, cw_ref, cb_ref, dtb_ref, dtbt_ref, alog_ref, alogt_ref,
                      dskip_ref, e64_ref, e128_ref, taps_sel_ref, place_ref,
                      ypre_ref, ec_ref, xw_ref, c_ref, bw_ref, tails_ref, rows_sc, y_sc, *, seq_len):
    n_cached = sc_ref.shape[0] * sc_ref.shape[1]
    rows_sc[0:CHUNK, :] = x_ref[...]
    r = sc_ref[...].reshape(n_cached, CONV_DIM)
    for t in range(3):
        hi = r.astype(BF16)
        rows_sc[CHUNK + t * n_cached:CHUNK + (t + 1) * n_cached, :] = hi
        r = r - hi.astype(F32)
    picked = _select_rows(taps_sel_ref[...], rows_sc)
    taps = [picked[k * CHUNK:(k + 1) * CHUNK] for k in range(CONV_W - 1)] + [x_ref[...].astype(F32)]
    tails_ref[...] = picked[(CONV_W - 1) * CHUNK:].reshape(tails_ref.shape)
    xs, bm, cm = _conv_silu(taps, cw_ref, cb_ref)
    rowi = lax.broadcasted_iota(jnp.int32, (CHUNK, CHUNK), 0)
    coli = lax.broadcasted_iota(jnp.int32, (CHUNK, CHUNK), 1)
    same = (rowi // seq_len) == (coli // seq_len)
    keep = same & (coli <= rowi)
    dt = jax.nn.softplus(dt_ref[...] + dtb_ref[...])
    dtt = jax.nn.softplus(dtt_ref[...] + dtbt_ref[...])
    da, cum, cum_t = _ssd_sums(dt, dtt, keep, same & (rowi <= coli), alog_ref, alogt_ref)
    whole = jnp.where(same, 1.0, 0.0).astype(BF16)
    last = _split_dot(da, jnp.concatenate([whole] * 3, axis=1), 3, left=False)
    e64 = e64_ref[...]
    ec_ref[...] = _split_dot(jnp.exp(cum), e64, 2)
    wb_e = _split_dot(dt * jnp.exp(last - cum), e64, 2)
    dec_e = _split_dot(jnp.exp(last), e64, 2)
    cum_e = _split_dot(cum, e128_ref[...], 3)
    _ssd_intra(xs, bm, cm, cum_e, cum_t, dtt, keep, dskip_ref, y_sc)
    ypre_ref[...] = y_sc[...]

    place = place_ref[...]
    dec_hi = dec_e.astype(BF16)
    dec_lo = (dec_e - dec_hi.astype(F32)).astype(BF16)
    xw = (xs * wb_e).astype(BF16)
    xw_ref[...] = _dot(place, jnp.concatenate([xw, dec_hi, dec_lo], axis=0)).astype(BF16)
    tokens = place[:, :CHUNK]
    c_ref[...] = _dot(tokens, cm.astype(BF16)).astype(BF16)
    zeros = jnp.zeros((CHUNK, D_STATE), BF16)
    b_wide = jnp.concatenate(
        [piece for g in range(SSM_GROUPS) for piece in (bm[:, g * D_STATE:(g + 1) * D_STATE].astype(BF16), zeros)], axis=1)
    n_rows = place.shape[0]
    slot = lax.broadcasted_iota(jnp.int32, (n_rows, 2 * D_STATE * SSM_GROUPS), 0) % SLOTS
    col = lax.broadcasted_iota(jnp.int32, (n_rows, 2 * D_STATE * SSM_GROUPS), 1) % (2 * D_STATE)
    ones = ((slot == seq_len) | (slot == seq_len + 1)) & (col >= D_STATE)
    bw_ref[...] = jnp.where(ones, 1.0, _dot(tokens, b_wide)).astype(BF16)


def _ssd_short(xbc, conv_rows, dt, dtt, consts, taps_sel, place, seq_len):
    rows = dt.shape[0]
    n_chunks = rows // CHUNK
    n_seq = CHUNK // seq_len
    n_slot_rows = place.shape[0]
    kern = functools.partial(_ssd_short_kernel, seq_len=seq_len)

    def out(r, n, dtype):
        return pl.BlockSpec((r, n), lambda c: (c, 0)), jax.ShapeDtypeStruct((n_chunks * r, n), dtype)

    conv_spec = pl.BlockSpec((CONV_W - 1, n_seq, CONV_DIM), lambda c: (0, c, 0))
    outs = [out(CHUNK, D_INNER, F32), out(CHUNK, D_INNER, F32), out(n_slot_rows, D_INNER, BF16),
            out(n_slot_rows, SSM_GROUPS * D_STATE, BF16), out(n_slot_rows, 2 * SSM_GROUPS * D_STATE, BF16),
            (conv_spec, jax.ShapeDtypeStruct(conv_rows.shape, F32))]
    return pl.pallas_call(
        kern,
        grid=(n_chunks,),
        in_specs=[pl.BlockSpec((CHUNK, CONV_DIM), lambda c: (c, 0)), conv_spec,
                  pl.BlockSpec((CHUNK, 128), lambda c: (c, 0)),
                  pl.BlockSpec((SSM_HEADS, CHUNK), lambda c: (0, c))]
                 + [_resident(a.shape) for a in consts] + [_resident(taps_sel.shape), _resident(place.shape)],
        out_specs=[o[0] for o in outs],
        out_shape=[o[1] for o in outs],
        scratch_shapes=[pltpu.VMEM((taps_sel.shape[1], CONV_DIM), BF16), pltpu.VMEM((CHUNK, D_INNER), F32)],
        compiler_params=_cp(("arbitrary",), 48),
        name="ssd_short",
    )(xbc, conv_rows, dt, dtt, *consts, taps_sel, place)


def _ssd_state_kernel(s0_ref, xw_ref, c_ref, bw_ref, ypre_ref, ec_ref, z_ref, ng_ref, y_ref, sout_ref, ci_sc,
                      *, nb, seq_len):
    for bb in range(nb):
        slots = slice(bb * SLOTS, (bb + 1) * SLOTS)
        toks = slice(bb * seq_len, (bb + 1) * seq_len)
        for g in range(SSM_GROUPS):
            gs = slice(g * GROUP_W, (g + 1) * GROUP_W)
            s0g = s0_ref[bb, gs, :]
            ci_sc[...] = _dot_nt(c_ref[slots, g * D_STATE:(g + 1) * D_STATE], s0g.astype(BF16))
            u = lax.dot_general(xw_ref[slots, gs], bw_ref[slots, 2 * g * D_STATE:2 * (g + 1) * D_STATE],
                                (((0,), (0,)), ((), ())), preferred_element_type=F32)
            sout_ref[bb, gs, :] = u[:, :D_STATE] + s0g * u[:, D_STATE:]
            zf = z_ref[toks, gs]
            yv = (ypre_ref[toks, gs] + ec_ref[toks, gs] * ci_sc[0:seq_len, :]) * (zf * jax.nn.sigmoid(zf))
            y_ref[toks, gs] = _rms(yv, ng_ref[:, gs], RMS_EPS)


def _ssd_state(s0, xw, cmat, bw, ypre, ec, z, ng, nb, seq_len):
    batch = s0.shape[0]
    kern = functools.partial(_ssd_state_kernel, nb=nb, seq_len=seq_len)

    def rows(r, n):
        return pl.BlockSpec((nb * r, n), lambda b: (b, 0))

    state_spec = pl.BlockSpec((nb, D_INNER, D_STATE), lambda b: (b, 0, 0))
    return pl.pallas_call(
        kern,
        grid=(batch // nb,),
        in_specs=[state_spec, rows(SLOTS, D_INNER), rows(SLOTS, SSM_GROUPS * D_STATE),
                  rows(SLOTS, 2 * SSM_GROUPS * D_STATE), rows(seq_len, D_INNER), rows(seq_len, D_INNER),
                  rows(seq_len, D_INNER), _resident(ng.shape)],
        out_specs=[rows(seq_len, D_INNER), state_spec],
        out_shape=[jax.ShapeDtypeStruct((batch * seq_len, D_INNER), F32),
                   jax.ShapeDtypeStruct((batch, D_INNER, D_STATE), F32)],
        scratch_shapes=[pltpu.VMEM((SLOTS, GROUP_W), F32)],
        compiler_params=_cp(("arbitrary",), 40),
        name="ssd_state",
    )(s0, xw, cmat, bw, ypre, ec, z, ng)


def _mem_kv_kernel(m_ref, w_ref, k_ref, v_ref):
    mb = m_ref[...].astype(BF16)
    k_ref[...] = _dot(mb, w_ref[:, :MEM_WIDTH])
    v_ref[...] = _dot(mb, w_ref[:, MEM_WIDTH:])


def _mem_kv(mem2d, w_kv, tm):
    m = mem2d.shape[0]
    return pl.pallas_call(
        _mem_kv_kernel,
        grid=(m // tm,),
        in_specs=[pl.BlockSpec((tm, D_MODEL), lambda i: (i, 0)), _resident(w_kv.shape)],
        out_specs=[pl.BlockSpec((tm, MEM_WIDTH), lambda i: (i, 0))] * 2,
        out_shape=[jax.ShapeDtypeStruct((m, MEM_WIDTH), F32)] * 2,
        compiler_params=_cp(("arbitrary",), 32),
        name="mem_kv",
    )(mem2d, w_kv)


MEM_HALVES = MEM_HEAD_DIM // 128
MEM_ROW_GROUP = MEM_HALVES * MEM_HEADS


def _mem_attn_cache_kernel(q_ref, k_ref, v_ref, o_ref, *, nb, n_tok):
    rows = MEM_HEADS * n_tok
    n_col = N_MEM * MEM_ROW_GROUP
    col = lax.broadcasted_iota(jnp.int32, (rows, n_col), 1)
    row = lax.broadcasted_iota(jnp.int32, (rows, n_col), 0)
    own = (col % MEM_ROW_GROUP) == (row // n_tok)
    scale = MEM_HEAD_DIM ** -0.5
    for bb in range(nb):
        g = _dot_nt(q_ref[bb], k_ref[bb].astype(BF16))
        s = (g[:rows] + pltpu.roll(g[rows:], n_col - MEM_HEADS, 1)) * scale
        s = jnp.where(own, s, NEG_BIG)
        p = jnp.exp(s - jnp.max(s, axis=1, keepdims=True))
        l = jnp.sum(p, axis=1, keepdims=True)
        p2 = jnp.concatenate([p, pltpu.roll(p, MEM_HEADS, 1)], axis=0).astype(BF16)
        o = _dot(p2, v_ref[bb].astype(BF16))
        o_ref[bb] = (o / jnp.concatenate([l, l], axis=0)).astype(BF16)


def _mem_attn_cache(q, mem_k, mem_v, nb, n_tok):
    batch, rows, _ = q.shape
    kern = functools.partial(_mem_attn_cache_kernel, nb=nb, n_tok=n_tok)
    kv_spec = pl.BlockSpec((nb,) + mem_k.shape[1:], lambda b: (b, 0, 0))
    q_spec = pl.BlockSpec((nb, rows, 128), lambda b: (b, 0, 0))
    return pl.pallas_call(
        kern,
        grid=(batch // nb,),
        in_specs=[q_spec, kv_spec, kv_spec],
        out_specs=q_spec,
        out_shape=jax.ShapeDtypeStruct((batch, rows, 128), BF16),
        compiler_params=_cp(("arbitrary",), 40),
        name="mem_attn_cache",
    )(q, mem_k, mem_v)


def _merge_ffn_kernel(x_ref, oa_ref, ob_ref, om_ref, g_ref, woa_ref, wob_ref, wom_ref, wout_ref,
                      ln1g_ref, ln1b_ref, wup_ref, bup_ref, wdown_ref, bdown_ref, ln2g_ref, ln2b_ref, y_ref):
    n = x_ref.shape[0] // 2
    halves = (slice(0, n), slice(n, 2 * n))
    a = [_dot(oa_ref[r, :], woa_ref[...]) for r in halves]
    b = [_dot(ob_ref[r, :], wob_ref[...]) for r in halves]
    c = [_dot(om_ref[r, :], wom_ref[...]) for r in halves]
    m = [(g_ref[r, 0:D_MODEL].astype(F32) * a[i] + g_ref[r, D_MODEL:2 * D_MODEL].astype(F32) * b[i]
          + g_ref[r, 2 * D_MODEL:].astype(F32) * c[i]).astype(BF16) for i, r in enumerate(halves)]
    t = [_dot(mi, wout_ref[...]) for mi in m]
    x1 = [_layer_norm(ALPHA * x_ref[r, :] + t[i], ln1g_ref[...], ln1b_ref[...]) for i, r in enumerate(halves)]
    u = [_dot(xi.astype(BF16), wup_ref[...]) for xi in x1]
    h = [jnp.square(jnp.maximum(ui + bup_ref[...], 0.0)).astype(BF16) for ui in u]
    d = [_dot(hi, wdown_ref[...]) for hi in h]
    for i, r in enumerate(halves):
        y_ref[r, :] = _layer_norm(ALPHA * x1[i] + d[i] + bdown_ref[...], ln2g_ref[...], ln2b_ref[...])


def _merge_ffn(x2d, o_mla, o_ssm, o_mem, g, weights, tm):
    m = x2d.shape[0]

    def row(n):
        return pl.BlockSpec((tm, n), lambda i: (i, 0))

    return pl.pallas_call(
        _merge_ffn_kernel,
        grid=(m // tm,),
        in_specs=[row(D_MODEL), row(MLA_HEADS * V_DIM), row(D_INNER), row(MEM_WIDTH), row(N_BRANCH * D_MODEL)]
                 + [_resident(w.shape) for w in weights],
        out_specs=row(D_MODEL),
        out_shape=jax.ShapeDtypeStruct((m, D_MODEL), F32),
        compiler_params=_cp(("arbitrary",), 56),
        name="merge_ffn",
    )(x2d, o_mla, o_ssm, o_mem, g, *weights)


def _rope_table(pos):
    half = ROPE_DIM // 2
    inv = ROPE_THETA ** (-jnp.arange(half, dtype=F32) / half)
    ang = pos.astype(F32)[:, None] * inv[None, :]
    cos, sin = jnp.cos(ang), jnp.sin(ang)
    return jnp.concatenate([cos, cos, cos, cos, -sin, sin, -sin, sin], axis=1)


def _expand_matrix(width, terms):
    src = np.arange(SSM_HEADS * width) // width
    one = (np.arange(128)[:, None] == src[None, :]).astype(np.float32)
    return jnp.asarray(np.concatenate([one] * terms, axis=0), dtype=BF16)


def _shift_matrix():
    sel = np.zeros(((CONV_W - 1) * CHUNK, CARRY + CHUNK), np.float32)
    for k in range(CONV_W - 1):
        for t in range(CHUNK):
            sel[k * CHUNK + t, CARRY + t + k - (CONV_W - 1)] = 1.0
    return jnp.asarray(sel, dtype=BF16)


def _short_taps_matrix(seq_len):
    n_seq = CHUNK // seq_len
    n_old = CONV_W - 1
    n_cached = n_old * n_seq
    sel = np.zeros((n_old * CHUNK + n_cached, CHUNK + 3 * n_cached), np.float32)

    def take(out_row, s, m):
        if m >= n_old:
            sel[out_row, s * seq_len + m - n_old] = 1.0
        else:
            for term in range(3):
                sel[out_row, CHUNK + term * n_cached + m * n_seq + s] = 1.0

    for s in range(n_seq):
        for k in range(n_old):
            for i in range(seq_len):
                take(k * CHUNK + s * seq_len + i, s, i + k)
        for j in range(n_old):
            take(n_old * CHUNK + j * n_seq + s, s, seq_len + j)
    return jnp.asarray(sel, dtype=BF16)


def _placement_matrix(seq_len):
    n_seq = CHUNK // seq_len
    place = np.zeros((n_seq * SLOTS, 3 * CHUNK), np.float32)
    for s in range(n_seq):
        for j in range(seq_len):
            place[s * SLOTS + j, s * seq_len + j] = 1.0
        place[s * SLOTS + seq_len, CHUNK + s * seq_len] = 1.0
        place[s * SLOTS + seq_len + 1, 2 * CHUNK + s * seq_len] = 1.0
    return jnp.asarray(place, dtype=BF16)


def _row(v):
    return v.reshape(1, -1).astype(F32)


def kernel(x_prompt, x_sample, mem_prompt, cache_ckv, cache_krope, page_table, cache_mem_k, cache_mem_v, state_ssm, state_conv, w_in, q_norm_g, w_uq, kv_norm_g, w_uk, w_uv, conv_w, conv_b, dt_bias, a_log, d_skip, ssm_norm_g, w_mem_k, w_mem_v, b_gate, w_o_mla, w_o_ssm, w_o_mem, w_out, ln1_g, ln1_b, w_up, b_up, w_down, b_down, ln2_g, ln2_b):
    bp, seq, _ = x_prompt.shape
    bs, t_new, _ = x_sample.shape
    n_pages = page_table.shape[1]
    past = n_pages * PAGE_SIZE
    mp, ms = bp * seq, bs * t_new
    assert seq % KV_TILE == 0 and seq % CHUNK == 0 and mp % TOKEN_TILE == 0 and TOKEN_TILE % KV_TILE == 0
    assert CHUNK % t_new == 0 and ms % CHUNK == 0 and t_new + 2 <= SLOTS and MLA_HEADS * t_new % 16 == 0
    assert bs % SAMPLE_ROWS == 0 and n_pages % 2 == 0 and MEM_HALVES == 2

    o_cq, o_ckv, o_kr, o_z, o_xbc, o_dt, o_mq, o_g = np.cumsum(
        [0, Q_LORA, KV_LORA, ROPE_DIM, D_INNER, CONV_DIM, SSM_HEADS, MEM_WIDTH]).tolist()
    w_kr = w_in[:, o_kr:o_z]
    w_small = jnp.concatenate([w_in[:, o_ckv:o_kr], w_kr, w_kr, w_in[:, o_dt:o_mq],
                               jnp.zeros((D_MODEL, 128 - SSM_HEADS), F32)], axis=1)
    w_proj = tuple(w.astype(BF16) for w in (w_in[:, o_cq:o_ckv], w_small, w_in[:, o_z:o_xbc], w_in[:, o_xbc:o_dt],
                                            w_in[:, o_mq:o_g], w_in[:, o_g:]))
    half = ROPE_DIM // 2
    w_q_nope = w_uq[:, :, :NOPE_DIM].reshape(Q_LORA, MLA_HEADS * NOPE_DIM)
    w_q_rope = w_uq[:, :, NOPE_DIM:]
    w_q_swap = jnp.concatenate([w_q_rope[:, :, half:], w_q_rope[:, :, :half]], axis=-1)
    pair_w = 2 * ROPE_DIM
    w_q_pairs = jnp.concatenate([w_q_rope.reshape(Q_LORA, MLA_HEADS // 2, pair_w),
                                 w_q_swap.reshape(Q_LORA, MLA_HEADS // 2, pair_w)], axis=-1)
    wq = jnp.concatenate([w_q_nope, w_q_pairs.reshape(Q_LORA, -1)], axis=1).astype(BF16)
    wuk_t = jnp.transpose(w_uk, (1, 2, 0)).astype(BF16)
    wuk_pairs = w_uk.reshape(KV_LORA, MLA_HEADS // 2, 2 * NOPE_DIM).transpose(1, 0, 2).astype(BF16)
    wuv = jnp.transpose(w_uv, (1, 0, 2)).astype(BF16)
    wuv_t = jnp.transpose(w_uv, (1, 2, 0)).astype(BF16)
    w_mem_kv = jnp.concatenate([w_mem_k.reshape(D_MODEL, MEM_WIDTH), w_mem_v.reshape(D_MODEL, MEM_WIDTH)], axis=1).astype(BF16)
    merge_w = (w_o_mla.astype(BF16), w_o_ssm.astype(BF16), w_o_mem.astype(BF16), w_out.astype(BF16),
               _row(ln1_g), _row(ln1_b), w_up.astype(BF16), _row(b_up), w_down.astype(BF16), _row(b_down),
               _row(ln2_g), _row(ln2_b))
    gq, gkv, bg = _row(q_norm_g), _row(kv_norm_g), _row(b_gate)

    pad_heads = jnp.zeros((128 - SSM_HEADS,), F32)
    ssd_head = (conv_w.astype(F32), _row(conv_b),
                _row(jnp.concatenate([dt_bias, pad_heads])),
                jnp.broadcast_to(dt_bias.astype(F32)[:, None], (SSM_HEADS, CHUNK)),
                _row(jnp.concatenate([a_log, pad_heads])),
                jnp.broadcast_to(a_log.astype(F32)[:, None], (SSM_HEADS, CHUNK)),
                _row(jnp.repeat(d_skip, SSM_HEAD_DIM)))
    norm_g = _row(ssm_norm_g)
    expanders = (_expand_matrix(SSM_HEAD_DIM, 2), _expand_matrix(128, 3))

    n_chunks = seq // CHUNK
    cs_p = _rope_table(jnp.arange(seq))
    xp2d = x_prompt.reshape(mp, D_MODEL)
    mem_k_p, mem_v_p = _mem_kv(mem_prompt.reshape(bp * N_MEM, D_MODEL), w_mem_kv, tm=min(MEM_ROWS_TILE, bp * N_MEM))
    _, ckv_p, kr_p, _, dt_p, dtt_p, z_p, xbc_p, o_mem_p, g_p, k_heads, vt_heads, qt_p = _in_proj(
        xp2d, w_proj, cs_p, gq, gkv, bg, tm=TOKEN_TILE,
        head_kv_weights=(wuk_pairs, wuv_t.reshape(MLA_HEADS // 2, 2 * V_DIM, KV_LORA), wq),
        memory=(mem_k_p, mem_v_p))
    o_mla_p = _prompt_attn(qt_p, k_heads, vt_heads, bp, seq)

    o_ssm_p, ssm_p, conv_p = _ssd(xbc_p, z_p, dt_p, dtt_p, ssd_head + (norm_g,) + expanders + (_shift_matrix(),),
                                  batch=bp, n_chunks=n_chunks)

    y_p = _merge_ffn(xp2d, o_mla_p, o_ssm_p, o_mem_p, g_p, merge_w, tm=TOKEN_TILE)

    cs_s = jnp.tile(_rope_table(past + jnp.arange(t_new)), (bs, 1))
    xs2d = x_sample.reshape(ms, D_MODEL)
    tm_s = min(TOKEN_TILE, ms)
    cqn, ckv_s, kr_s, kcat, dt_s, dtt_s, z_s, xbc_s, mq_s, g_s = _in_proj(xs2d, w_proj, cs_s, gq, gkv, bg, tm=tm_s)
    q_s = _q_prep(cqn, wq, wuk_t, cs_s, tm=tm_s)
    q_s = jnp.transpose(q_s.reshape(MLA_HEADS, bs, t_new, QK_DIM), (1, 0, 2, 3)).reshape(bs, MLA_HEADS * t_new, QK_DIM)
    o_lat = _decode_attn(page_table, q_s, kcat.reshape(bs, t_new, QK_DIM).astype(F32), cache_ckv,
                         jnp.swapaxes(cache_krope, 1, 2))
    o_lat = jnp.transpose(o_lat.reshape(bs, MLA_HEADS, t_new, KV_LORA), (1, 0, 2, 3)).reshape(MLA_HEADS, ms, KV_LORA)
    o_mla_s = _uv_proj(o_lat, wuv)

    ypre_s, ec_s, xw_s, c_s, bw_s, conv_rows = _ssd_short(
        xbc_s, jnp.transpose(state_conv.astype(F32), (1, 0, 2)), dt_s, dtt_s, ssd_head + expanders,
        _short_taps_matrix(t_new), _placement_matrix(t_new), seq_len=t_new)
    conv_s = jnp.transpose(conv_rows, (1, 0, 2))
    o_ssm_s, ssm_s = _ssd_state(state_ssm.astype(F32).reshape(bs, D_INNER, D_STATE), xw_s, c_s, bw_s, ypre_s, ec_s,
                                z_s.astype(F32), norm_g, nb=SAMPLE_ROWS, seq_len=t_new)

    def cache_rows(c):
        c = c.reshape(bs, N_MEM, MEM_HEADS, MEM_HALVES, 128)
        return jnp.transpose(c, (0, 1, 3, 2, 4)).reshape(bs, N_MEM * MEM_ROW_GROUP, 128)

    mq_rows = jnp.transpose(mq_s.reshape(bs, t_new, MEM_HEADS, MEM_HALVES, 128), (0, 3, 2, 1, 4))
    o_mem_s = _mem_attn_cache(mq_rows.reshape(bs, MEM_ROW_GROUP * t_new, 128), cache_rows(cache_mem_k),
                              cache_rows(cache_mem_v), nb=SAMPLE_ROWS, n_tok=t_new)
    o_mem_s = jnp.transpose(o_mem_s.reshape(bs, MEM_HALVES, MEM_HEADS, t_new, 128), (0, 3, 2, 1, 4))
    y_s = _merge_ffn(xs2d, o_mla_s, o_ssm_s.reshape(ms, D_INNER).astype(BF16), o_mem_s.reshape(ms, MEM_WIDTH), g_s,
                     merge_w, tm=tm_s)

    return (y_p.reshape(bp, seq, D_MODEL), y_s.reshape(bs, t_new, D_MODEL),
            ckv_p.reshape(bp, seq, KV_LORA), kr_p.reshape(bp, seq, ROPE_DIM),
            mem_k_p.reshape(bp, N_MEM, MEM_HEADS, MEM_HEAD_DIM), mem_v_p.reshape(bp, N_MEM, MEM_HEADS, MEM_HEAD_DIM),
            ssm_p.reshape(bp, SSM_HEADS, SSM_HEAD_DIM, D_STATE), conv_p,
            ckv_s.reshape(bs, t_new, KV_LORA), kr_s.reshape(bs, t_new, ROPE_DIM),
            ssm_s.reshape(bs, SSM_HEADS, SSM_HEAD_DIM, D_STATE), conv_s)
```
